```python
import jax, jax.numpy as jnp
from jax import lax
import numpy as np

D_MODEL = 1024
BATCH = 8
SEQ = 2048
DEPTH = 1
DEC_BATCH = 128
DEC_SEQ = 8
PAST_LEN = 16384
PAGE_SIZE = 128

SSD_EXPAND = 2
D_INNER = SSD_EXPAND * D_MODEL
SSD_HEAD_DIM = 64
SSD_HEADS = D_INNER // SSD_HEAD_DIM
SSD_GROUPS = 4
HEADS_PER_GROUP = SSD_HEADS // SSD_GROUPS
D_STATE = 128
CONV_W = 4
CONV_DIM = D_INNER + 2 * SSD_GROUPS * D_STATE
SSD_CHUNK = 128
D_GMLP = D_MODEL
GMLP_HEADS = 8
GMLP_HEAD_DIM = D_GMLP // GMLP_HEADS
GMLP_CHUNK = 128
N_BRANCHES = 2
N_EXPERTS = 32
TOP_K = 4
D_FF = D_MODEL
SWIGLU_ALPHA = 1.702
SWIGLU_LIMIT = 7.0
MOE_BLOCK = 128
EPS = 1e-5
SPLIT_Z = D_INNER
SPLIT_XBC = SPLIT_Z + CONV_DIM
SPLIT_DT = SPLIT_XBC + SSD_HEADS
SPLIT_UV = SPLIT_DT + 2 * D_GMLP
D_IN_PROJ = SPLIT_UV + N_BRANCHES * D_MODEL

kernel_name = "hybrid_ssd_gmlp_moe_step"


def rmsnorm(x, w):
    xf = x.astype(jnp.float32)
    y = xf * lax.rsqrt(jnp.mean(xf * xf, axis=-1, keepdims=True) + EPS)
    return (y * w.astype(jnp.float32)).astype(x.dtype)


def layernorm(x, w, b):
    xf = x.astype(jnp.float32)
    mu = jnp.mean(xf, axis=-1, keepdims=True)
    xc = xf - mu
    y = xc * lax.rsqrt(jnp.mean(xc * xc, axis=-1, keepdims=True) + EPS)
    return (y * w.astype(jnp.float32) + b.astype(jnp.float32)).astype(x.dtype)


def ssd_chunked(x, dt, a_log, b_in, c_in, init_state):
    f32 = jnp.float32
    bsz, l = x.shape[0], x.shape[1]
    q = SSD_CHUNK if l % SSD_CHUNK == 0 else l
    nc = l // q
    G, R, P, N = SSD_GROUPS, HEADS_PER_GROUP, SSD_HEAD_DIM, D_STATE
    a = dt * (-jnp.exp(a_log.astype(f32)))
    a_cum = jnp.cumsum(a.reshape(bsz, nc, q, SSD_HEADS), axis=2)
    xdt = (x.astype(f32) * dt[..., None]).reshape(bsz, nc, q, G, R, P)
    bc = b_in.astype(f32).reshape(bsz, nc, q, G, N)
    cc = c_in.astype(f32).reshape(bsz, nc, q, G, N)
    causal = jnp.tril(jnp.ones((q, q), bool))[None, None, :, :, None]
    seg = a_cum[:, :, :, None, :] - a_cum[:, :, None, :, :]
    decay = jnp.exp(jnp.where(causal, seg, -jnp.inf)).reshape(bsz, nc, q, q, G, R)
    cb = jnp.einsum('bclgn,bcsgn->bclsg', cc, bc)
    y_diag = jnp.einsum('bclsgr,bcsgrp->bclgrp', cb[..., None] * decay, xdt)
    decay_end = jnp.exp(a_cum[:, :, -1:, :] - a_cum).reshape(bsz, nc, q, G, R)
    chunk_states = jnp.einsum('bclgn,bclgrp->bcgrpn', bc, xdt * decay_end[..., None])
    chunk_decay = jnp.exp(a_cum[:, :, -1, :]).reshape(bsz, nc, G, R)

    def carry_step(s, inp):
        st, dec = inp
        return s * dec[..., None, None] + st, s

    s0 = init_state.astype(f32).reshape(bsz, G, R, P, N)
    s_final, s_start = lax.scan(carry_step, s0,
                                (jnp.moveaxis(chunk_states, 1, 0), jnp.moveaxis(chunk_decay, 1, 0)))
    s_start = jnp.moveaxis(s_start, 0, 1)
    y_off = jnp.einsum('bclgn,bcgrpn->bclgrp', cc, s_start) * jnp.exp(a_cum).reshape(bsz, nc, q, G, R)[..., None]
    y = (y_diag + y_off).reshape(bsz, l, SSD_HEADS, P)
    return y, s_final.reshape(bsz, SSD_HEADS, P, N)


def chunk_gmlp(u, v, w_spatial, b_spatial):
    bsz, l, _ = v.shape
    lp = -(-l // GMLP_CHUNK) * GMLP_CHUNK
    vp = jnp.pad(v, ((0, 0), (0, lp - l), (0, 0))).reshape(bsz, lp // GMLP_CHUNK, GMLP_CHUNK, GMLP_HEADS, GMLP_HEAD_DIM)
    ws = jnp.where(jnp.tril(jnp.ones((GMLP_CHUNK, GMLP_CHUNK), bool))[None], w_spatial, 0).astype(v.dtype)
    mixed = jnp.einsum('gts,bcsgd->bctgd', ws, vp) + b_spatial.T.astype(v.dtype)[None, None, :, :, None]
    mixed = mixed.reshape(bsz, lp, D_GMLP)[:, :l]
    return u * mixed


def moe_ffn(h, w_router, b_router, w_gu, b_gu, w_down, b_down):
    f32 = jnp.float32
    t = h.shape[0]
    logits = jnp.dot(h.astype(f32), w_router.astype(f32)) + b_router.astype(f32)
    top_logit, top_e = lax.top_k(logits, TOP_K)
    gate = jax.nn.softmax(top_logit, axis=-1)
    n_assign = t * TOP_K
    flat_e = top_e.reshape(-1)
    order = jnp.argsort(flat_e)
    sorted_e = flat_e[order]
    counts = jnp.bincount(flat_e, length=N_EXPERTS)
    padded = (counts + MOE_BLOCK - 1) // MOE_BLOCK * MOE_BLOCK
    padded_end = jnp.cumsum(padded)
    padded_start = padded_end - padded
    start = jnp.cumsum(counts) - counts
    slot = padded_start[sorted_e] + jnp.arange(n_assign) - start[sorted_e]
    n_blocks = -(-n_assign // MOE_BLOCK) + N_EXPERTS
    n_slots = n_blocks * MOE_BLOCK
    slot_tok = jnp.zeros((n_slots,), jnp.int32).at[slot].set((order // TOP_K).astype(jnp.int32))
    slot_gate = jnp.zeros((n_slots,), f32).at[slot].set(gate.reshape(-1)[order])
    block_e = jnp.minimum(jnp.searchsorted(padded_end, jnp.arange(n_blocks) * MOE_BLOCK, side='right'), N_EXPERTS - 1)
    xs = h[slot_tok].reshape(n_blocks, MOE_BLOCK, D_MODEL)

    def expert_block(args):
        xb, e = args
        gu = xb @ w_gu[e] + b_gu[e]
        glu = jnp.minimum(gu[:, 0::2], SWIGLU_LIMIT)
        lin = jnp.clip(gu[:, 1::2], -SWIGLU_LIMIT, SWIGLU_LIMIT)
        act = glu * jax.nn.sigmoid(SWIGLU_ALPHA * glu) * (lin + 1)
        return act @ w_down[e] + b_down[e]

    ys = lax.map(expert_block, (xs, block_e)).reshape(n_slots, D_MODEL)
    ys = (ys.astype(f32) * slot_gate[:, None]).astype(h.dtype)
    return jnp.zeros_like(h).at[slot_tok].add(ys)


def hybrid_layer(x, conv_prefix, ssm_init, norm_mix, w_in, conv_w, conv_b, dt_bias, a_log, d_skip, gnorm_w,
                 v_norm_w, v_norm_b, w_spatial, b_spatial, w_branch_ssd, w_branch_mlp, w_out,
                 norm_ffn, w_router, b_router, w_gu, b_gu, w_down, b_down):
    bsz, l, _ = x.shape
    hn = rmsnorm(x, norm_mix)
    proj = hn @ w_in
    z, xbc, dt_raw, uv, gates = jnp.split(proj, [SPLIT_Z, SPLIT_XBC, SPLIT_DT, SPLIT_UV], axis=-1)
    xpad = jnp.concatenate([conv_prefix.astype(xbc.dtype), xbc], axis=1)
    new_conv = xpad[:, -(CONV_W - 1):]
    conv = conv_b
    for k in range(CONV_W):
        conv = conv + xpad[:, k:k + l] * conv_w[k]
    xbc_act = jax.nn.silu(conv)
    xs, b_ssm, c_ssm = jnp.split(xbc_act, [D_INNER, D_INNER + SSD_GROUPS * D_STATE], axis=-1)
    xs = xs.reshape(bsz, l, SSD_HEADS, SSD_HEAD_DIM)
    dt = jax.nn.softplus(dt_raw.astype(jnp.float32) + dt_bias.astype(jnp.float32))
    y_ssd, new_ssm = ssd_chunked(xs, dt, a_log, b_ssm.reshape(bsz, l, SSD_GROUPS, D_STATE),
                                 c_ssm.reshape(bsz, l, SSD_GROUPS, D_STATE), ssm_init)
    y_ssd = y_ssd + d_skip.astype(jnp.float32)[:, None] * xs.astype(jnp.float32)
    y_ssd = y_ssd.reshape(bsz, l, D_INNER).astype(x.dtype)
    y_ssd = rmsnorm(y_ssd * jax.nn.silu(z), gnorm_w)
    uv = jax.nn.gelu(uv, approximate=False)
    u, v = jnp.split(uv, 2, axis=-1)
    v = layernorm(v, v_norm_w, v_norm_b)
    y_mlp = chunk_gmlp(u, v, w_spatial, b_spatial)
    g_ssd, g_mlp = jnp.split(jax.nn.sigmoid(gates), 2, axis=-1)
    merged = g_ssd * (y_ssd @ w_branch_ssd) + g_mlp * (y_mlp @ w_branch_mlp)
    h = x + merged @ w_out
    hf = rmsnorm(h, norm_ffn).reshape(bsz * l, D_MODEL)
    h = h + moe_ffn(hf, w_router, b_router, w_gu, b_gu, w_down, b_down).reshape(bsz, l, D_MODEL)
    return h, new_conv, new_ssm.astype(ssm_init.dtype), v


def setup_inputs(seed: int = 0) -> dict:
    key = jax.random.key(seed)
    ks = jax.random.split(key, 32)
    f32 = jnp.float32
    nrm = lambda k, shape, s: jax.random.normal(k, shape, f32) * s
    dt0 = jnp.exp(jax.random.uniform(ks[7], (DEPTH, SSD_HEADS), f32, np.log(1e-3), np.log(1e-1)))
    return {
        "x_prompt": nrm(ks[0], (BATCH, SEQ, D_MODEL), 1.0),
        "x_sample": nrm(ks[1], (DEC_BATCH, DEC_SEQ, D_MODEL), 1.0),
        "state_conv": nrm(ks[2], (DEPTH, DEC_BATCH, CONV_W - 1, CONV_DIM), 1.0),
        "state_ssm": nrm(ks[3], (DEPTH, DEC_BATCH, SSD_HEADS, SSD_HEAD_DIM, D_STATE), 0.5),
        "norm_mix": 1.0 + nrm(ks[4], (DEPTH, D_MODEL), 0.05),
        "w_in": nrm(ks[5], (DEPTH, D_MODEL, D_IN_PROJ), D_MODEL ** -0.5),
        "conv_w": nrm(ks[6], (DEPTH, CONV_W, CONV_DIM), CONV_W ** -0.5),
        "conv_b": nrm(ks[8], (DEPTH, CONV_DIM), 0.02),
        "dt_bias": dt0 + jnp.log(-jnp.expm1(-dt0)),
        "a_log": jnp.log(jax.random.uniform(ks[9], (DEPTH, SSD_HEADS), f32, 1.0, 16.0)),
        "d_skip": 1.0 + nrm(ks[10], (DEPTH, SSD_HEADS), 0.05),
        "gnorm_w": 1.0 + nrm(ks[11], (DEPTH, D_INNER), 0.05),
        "v_norm_w": 1.0 + nrm(ks[12], (DEPTH, D_GMLP), 0.05),
        "v_norm_b": nrm(ks[13], (DEPTH, D_GMLP), 0.02),
        "w_spatial": nrm(ks[14], (DEPTH, GMLP_HEADS, GMLP_CHUNK, GMLP_CHUNK), GMLP_CHUNK ** -0.5),
        "b_spatial": 1.0 + nrm(ks[15], (DEPTH, GMLP_HEADS, GMLP_CHUNK), 0.1),
        "w_branch_ssd": nrm(ks[16], (DEPTH, D_INNER, D_MODEL), D_INNER ** -0.5),
        "w_branch_mlp": nrm(ks[17], (DEPTH, D_GMLP, D_MODEL), D_GMLP ** -0.5),
        "w_out": nrm(ks[18], (DEPTH, D_MODEL, D_MODEL), D_MODEL ** -0.5),
        "norm_ffn": 1.0 + nrm(ks[19], (DEPTH, D_MODEL), 0.05),
        "w_router": nrm(ks[20], (DEPTH, D_MODEL, N_EXPERTS), D_MODEL ** -0.5),
        "b_router": nrm(ks[21], (DEPTH, N_EXPERTS), 0.01),
        "w_gu": nrm(ks[22], (DEPTH, N_EXPERTS, D_MODEL, 2 * D_FF), D_MODEL ** -0.5),
        "b_gu": nrm(ks[23], (DEPTH, N_EXPERTS, 2 * D_FF), 0.01),
        "w_down": nrm(ks[24], (DEPTH, N_EXPERTS, D_FF, D_MODEL), D_FF ** -0.5),
        "b_down": nrm(ks[25], (DEPTH, N_EXPERTS, D_MODEL), 0.01),
        "norm_final": 1.0 + nrm(ks[26], (D_MODEL,), 0.05),
    }


def reference(x_prompt, x_sample, state_conv, state_ssm, norm_mix, w_in, conv_w, conv_b, dt_bias, a_log,
              d_skip, gnorm_w, v_norm_w, v_norm_b, w_spatial, b_spatial, w_branch_ssd, w_branch_mlp, w_out,
              norm_ffn, w_router, b_router, w_gu, b_gu, w_down, b_down, norm_final):
    hp, hs = x_prompt, x_sample
    conv_p, ssm_p, conv_s, ssm_s, v_s = [], [], [], [], []
    for i in range(DEPTH):
        params = (norm_mix[i], w_in[i], conv_w[i], conv_b[i], dt_bias[i], a_log[i], d_skip[i], gnorm_w[i],
                  v_norm_w[i], v_norm_b[i], w_spatial[i], b_spatial[i], w_branch_ssd[i], w_branch_mlp[i],
                  w_out[i], norm_ffn[i], w_router[i], b_router[i], w_gu[i], b_gu[i], w_down[i], b_down[i])
        zero_conv = jnp.zeros((hp.shape[0], CONV_W - 1, CONV_DIM), hp.dtype)
        zero_ssm = jnp.zeros((hp.shape[0], SSD_HEADS, SSD_HEAD_DIM, D_STATE), state_ssm.dtype)
        hp, cp, sp, _ = hybrid_layer(hp, zero_conv, zero_ssm, *params)
        hs, cs, ss, vs = hybrid_layer(hs, state_conv[i], state_ssm[i], *params)
        conv_p.append(cp); ssm_p.append(sp); conv_s.append(cs); ssm_s.append(ss); v_s.append(vs)
    y_prompt = rmsnorm(hp, norm_final)
    y_sample = rmsnorm(hs, norm_final)
    return (y_prompt, y_sample, jnp.stack(conv_p), jnp.stack(ssm_p), jnp.stack(conv_s), jnp.stack(ssm_s), jnp.stack(v_s))
```

```python
import functools

import numpy as np
import jax
import jax.numpy as jnp
from jax import lax
from jax.experimental import pallas as pl
from jax.experimental.pallas import tpu as pltpu

f32, bf16, i32 = jnp.float32, jnp.bfloat16, jnp.int32

D_MODEL = 1024
D_INNER = 2048
HEAD_DIM = 64
HEADS = 32
GROUPS = 4
HEADS_PER_GROUP = 8
GROUP_WIDTH = HEADS_PER_GROUP * HEAD_DIM
D_STATE = 128
CONV_W = 4
CONV_DIM = D_INNER + 2 * GROUPS * D_STATE
CHUNK = 128
D_GMLP = 1024
GMLP_HEADS = 8
N_EXPERTS = 32
TOP_K = 4
D_FF = 1024
SWIGLU_ALPHA = 1.702
SWIGLU_LIMIT = 7.0
EPS = 1e-5
MOE_BLOCK = 256
D_MAIN = 3 * D_INNER + CONV_DIM
VMEM_LIMIT = 56 * 1024 * 1024

_NT = (((1,), (1,)), ((), ()))


def _params(*sem):
    return pltpu.CompilerParams(dimension_semantics=sem, vmem_limit_bytes=VMEM_LIMIT)


def _split3(x):
    hi = x.astype(bf16)
    r = x - hi.astype(f32)
    mid = r.astype(bf16)
    lo = (r - mid.astype(f32)).astype(bf16)
    return hi, mid, lo


def _dot_sel_rhs(x, sel):
    return sum(jnp.dot(p, sel, preferred_element_type=f32) for p in _split3(x))


def _dot_sel_lhs(sel, x):
    return sum(jnp.dot(sel, p, preferred_element_type=f32) for p in _split3(x))


def _silu(x):
    return x * jax.nn.sigmoid(x)


def _in_proj_body(x_ref, nw_ref, w_ref, wdt_ref, wdtT_ref, o_ref, dt_ref, dtT_ref, hn_ref):
    @pl.when(pl.program_id(1) == 0)
    def _():
        x = x_ref[...]
        ms = jnp.mean(x * x, axis=-1, keepdims=True)
        hn = (x * lax.rsqrt(ms + EPS) * nw_ref[...]).astype(bf16)
        hn_ref[...] = hn
        dt_ref[...] = jnp.dot(hn, wdt_ref[...], preferred_element_type=f32)
        dtT_ref[...] = lax.dot_general(wdtT_ref[...], hn, _NT, preferred_element_type=f32)

    o_ref[...] = jnp.dot(hn_ref[...], w_ref[...], preferred_element_type=f32)


def _in_proj(x2d, norm_w, w_main, w_dt, w_dtT):
    t = x2d.shape[0]
    tm = min(1024, t)
    tn = 512
    return pl.pallas_call(
        _in_proj_body,
        grid=(t // tm, D_MAIN // tn),
        in_specs=[
            pl.BlockSpec((tm, D_MODEL), lambda i, j: (i, 0)),
            pl.BlockSpec((1, D_MODEL), lambda i, j: (0, 0)),
            pl.BlockSpec((D_MODEL, tn), lambda i, j: (0, j)),
            pl.BlockSpec((D_MODEL, HEADS), lambda i, j: (0, 0)),
            pl.BlockSpec((HEADS, D_MODEL), lambda i, j: (0, 0)),
        ],
        out_specs=[
            pl.BlockSpec((tm, tn), lambda i, j: (i, j)),
            pl.BlockSpec((tm, HEADS), lambda i, j: (i, 0)),
            pl.BlockSpec((HEADS, tm), lambda i, j: (0, i)),
        ],
        out_shape=[
            jax.ShapeDtypeStruct((t, D_MAIN), f32),
            jax.ShapeDtypeStruct((t, HEADS), f32),
            jax.ShapeDtypeStruct((HEADS, t), f32),
        ],
        scratch_shapes=[pltpu.VMEM((tm, D_MODEL), bf16)],
        compiler_params=_params("arbitrary", "arbitrary"),
        name="in_proj",
    )(x2d, norm_w, w_main, w_dt, w_dtT)


def _conv_silu(cur, prev, cw_ref, cb_ref, seq_len, prev_shift):
    n = cur.shape[0]
    pos = lax.broadcasted_iota(i32, (n, 1), 0) % seq_len
    acc = cb_ref[...]
    for k in range(CONV_W):
        j = CONV_W - 1 - k
        if j == 0:
            xj = cur
        else:
            xj = jnp.where(pos >= j, pltpu.roll(cur, j, 0), pltpu.roll(prev, (j + prev_shift) % n, 0))
        acc = acc + xj * cw_ref[k:k + 1, :]
    return _silu(acc)


def _ssd_block(xa, dt_raw, dtT_raw, c):
    tril = c["tril"][...]
    dt = jax.nn.softplus(dt_raw + c["dtb_row"][...])
    dtT = jax.nn.softplus(dtT_raw + c["dtb_col"][...])
    a = dt * c["aneg_row"][...]
    aT = dtT * c["aneg_col"][...]
    a_cum = _dot_sel_lhs(tril, a)
    a_cumT = _dot_sel_rhs(aT, c["triu"][...])
    a_tot = _dot_sel_lhs(c["same"][...], a)
    stack = jnp.concatenate([dt, jnp.exp(a_cum), jnp.exp(a_tot - a_cum)], axis=0)
    ex = _dot_sel_rhs(stack, c["expand"][...])
    n = xa.shape[0]
    dtx, eax, dex = ex[:n], ex[n:2 * n], ex[2 * n:]

    xs = xa[:, :D_INNER]
    bm = xa[:, D_INNER:D_INNER + GROUPS * D_STATE]
    cm = xa[:, D_INNER + GROUPS * D_STATE:]
    xdt = xs * dtx
    xdt_bf = xdt.astype(bf16)
    mask = tril > 0
    lane = lax.broadcasted_iota(i32, (1, 2 * HEAD_DIM), 1)
    ys = []
    for g in range(GROUPS):
        cg = cm[:, g * D_STATE:(g + 1) * D_STATE].astype(bf16)
        bg = bm[:, g * D_STATE:(g + 1) * D_STATE].astype(bf16)
        cb = lax.dot_general(cg, bg, _NT, preferred_element_type=f32)
        for pair in range(HEADS_PER_GROUP // 2):
            halves = []
            for h in (g * HEADS_PER_GROUP + 2 * pair, g * HEADS_PER_GROUP + 2 * pair + 1):
                seg = a_cum[:, h:h + 1] - a_cumT[h:h + 1, :]
                decay = jnp.exp(jnp.where(mask, seg, -jnp.inf))
                m = (cb * decay).astype(bf16)
                col = (h // 2) * 2 * HEAD_DIM
                halves.append(jnp.dot(m, xdt_bf[:, col:col + 2 * HEAD_DIM], preferred_element_type=f32))
            ys.append(jnp.where(lane < HEAD_DIM, halves[0], halves[1]))
    y_diag = jnp.concatenate(ys, axis=1)
    return dict(xs=xs, bm=bm, cm=cm, xdt=xdt, xd=xdt * dex, eax=eax, aT=aT, y_diag=y_diag)


def _gated_norm(y, z, gw):
    g = y * _silu(z)
    ms = jnp.mean(g * g, axis=-1, keepdims=True)
    return (g * lax.rsqrt(ms + EPS) * gw).astype(bf16)


_SSD_CONST_NAMES = ("cw", "cb", "dtb_row", "dtb_col", "aneg_row", "aneg_col", "dskipx", "gw",
                    "tril", "triu", "same", "expand")


def _ssd_consts(conv_w, conv_b, dt_bias, a_log, d_skip, gnorm_w, seq_len):
    r = np.arange(CHUNK)
    same = (r[:, None] // seq_len) == (r[None, :] // seq_len)
    tril = same & (r[None, :] <= r[:, None])
    expand = np.repeat(np.eye(HEADS, dtype=np.float32), HEAD_DIM, axis=1)
    aneg = -jnp.exp(a_log.astype(f32))
    return dict(
        cw=conv_w, cb=conv_b.reshape(1, CONV_DIM),
        dtb_row=dt_bias.reshape(1, HEADS), dtb_col=dt_bias.reshape(HEADS, 1),
        aneg_row=aneg.reshape(1, HEADS), aneg_col=aneg.reshape(HEADS, 1),
        dskipx=jnp.repeat(d_skip.astype(f32), HEAD_DIM).reshape(1, D_INNER),
        gw=gnorm_w.reshape(1, D_INNER),
        tril=jnp.asarray(tril, bf16), triu=jnp.asarray(tril.T, bf16), same=jnp.asarray(same, bf16),
        expand=jnp.asarray(expand, bf16),
    )


def _const_specs(consts, ngrid):
    zero = (lambda *_: (0, 0))
    return [pl.BlockSpec(consts[k].shape, zero) for k in _SSD_CONST_NAMES]


def _ssd_prompt_body(z_ref, xbc_ref, dt_ref, dtT_ref, *rest):
    nc = len(_SSD_CONST_NAMES)
    c = dict(zip(_SSD_CONST_NAMES, rest[:nc]))
    y_ref, state_ref, prev_ref, st_ref = rest[nc:]
    ci = pl.program_id(1)

    @pl.when(ci == 0)
    def _():
        prev_ref[...] = jnp.zeros_like(prev_ref)
        st_ref[...] = jnp.zeros_like(st_ref)

    cur = xbc_ref[...]
    xa = _conv_silu(cur, prev_ref[...], c["cw"], c["cb"], CHUNK, 0)
    prev_ref[...] = cur
    b = _ssd_block(xa, dt_ref[...], dtT_ref[...], c)

    y_off = []
    for g in range(GROUPS):
        sl = slice(g * GROUP_WIDTH, (g + 1) * GROUP_WIDTH)
        st = st_ref[g]
        cg = b["cm"][:, g * D_STATE:(g + 1) * D_STATE].astype(bf16)
        y_off.append(jnp.dot(cg, st.astype(bf16), preferred_element_type=f32))
        bgT = b["bm"][:, g * D_STATE:(g + 1) * D_STATE].T.astype(bf16)
        upd = jnp.dot(bgT, b["xd"][:, sl].astype(bf16), preferred_element_type=f32)
        st_ref[g] = st * b["eax"][CHUNK - 1:CHUNK, sl] + upd
    y = b["y_diag"] + jnp.concatenate(y_off, axis=1) * b["eax"] + c["dskipx"][...] * b["xs"]
    y_ref[...] = _gated_norm(y, z_ref[...], c["gw"][...])

    @pl.when(ci == pl.num_programs(1) - 1)
    def _():
        for g in range(GROUPS):
            state_ref[0, g * GROUP_WIDTH:(g + 1) * GROUP_WIDTH, :] = st_ref[g].T


def _ssd_prompt(proj, dt, dtT, consts, batch, seq):
    nchunk = seq // CHUNK
    t = batch * seq
    row = lambda b, ci: b * nchunk + ci
    return pl.pallas_call(
        _ssd_prompt_body,
        grid=(batch, nchunk),
        in_specs=[
            pl.BlockSpec((CHUNK, D_INNER), lambda b, ci: (row(b, ci), 0)),
            pl.BlockSpec((CHUNK, CONV_DIM), lambda b, ci: (row(b, ci), 3 * D_INNER // CONV_DIM)),
            pl.BlockSpec((CHUNK, HEADS), lambda b, ci: (row(b, ci), 0)),
            pl.BlockSpec((HEADS, CHUNK), lambda b, ci: (0, row(b, ci))),
        ] + _const_specs(consts, 2),
        out_specs=[
            pl.BlockSpec((CHUNK, D_INNER), lambda b, ci: (row(b, ci), 0)),
            pl.BlockSpec((1, D_INNER, D_STATE), lambda b, ci: (b, 0, 0)),
        ],
        out_shape=[
            jax.ShapeDtypeStruct((t, D_INNER), bf16),
            jax.ShapeDtypeStruct((batch, D_INNER, D_STATE), f32),
        ],
        scratch_shapes=[pltpu.VMEM((CHUNK, CONV_DIM), f32), pltpu.VMEM((GROUPS, D_STATE, GROUP_WIDTH), f32)],
        compiler_params=_params("arbitrary", "arbitrary"),
        name="ssd_prompt",
    )(proj, proj, dt, dtT, *[consts[k] for k in _SSD_CONST_NAMES])


def _ssd_sample_body(seq_len, z_ref, xbc_ref, pre_ref, dt_ref, dtT_ref, state_ref, *rest):
    nc = len(_SSD_CONST_NAMES)
    c = dict(zip(_SSD_CONST_NAMES, rest[:nc]))
    y_ref, state_out_ref, c_ref, b_ref, xdT_ref, yacc_ref, eax_ref, eatT_ref = rest[nc:]
    s = pl.program_id(1)

    @pl.when(s == 0)
    def _():
        xa = _conv_silu(xbc_ref[...], pre_ref[...], c["cw"], c["cb"], seq_len, CHUNK - seq_len)
        b = _ssd_block(xa, dt_ref[...], dtT_ref[...], c)
        c_ref[...] = b["cm"]
        b_ref[...] = b["bm"]
        xdT_ref[...] = b["xd"].T.astype(bf16)
        yacc_ref[...] = b["y_diag"] + c["dskipx"][...] * b["xs"]
        eax_ref[...] = b["eax"]
        eatT_ref[...] = jnp.exp(_dot_sel_rhs(b["aT"], c["same"][...]))

    r0 = pl.multiple_of(s * seq_len, seq_len)
    rows = pl.ds(r0, seq_len)
    lane = lax.broadcasted_iota(i32, (1, CHUNK), 1)
    arep = jnp.broadcast_to(jnp.sum(jnp.where(lane == r0, eatT_ref[...], 0.0), axis=1, keepdims=True), (HEADS, D_STATE))
    rmask = (lax.broadcasted_iota(i32, (CHUNK, 1), 0) // seq_len) == s
    for g in range(GROUPS):
        gs = slice(g * GROUP_WIDTH, (g + 1) * GROUP_WIDTH)
        ns = slice(g * D_STATE, (g + 1) * D_STATE)
        s0 = state_ref[0, gs, :]
        cg = c_ref[rows, ns].astype(bf16)
        yo = lax.dot_general(cg, s0.astype(bf16), _NT, preferred_element_type=f32)
        yacc_ref[rows, gs] = yacc_ref[rows, gs] + yo * eax_ref[rows, gs]
        bmask = jnp.where(rmask, b_ref[:, ns], 0.0).astype(bf16)
        upd = jnp.dot(xdT_ref[gs, :], bmask, preferred_element_type=f32)
        for r in range(HEADS_PER_GROUP):
            h = g * HEADS_PER_GROUP + r
            hs = slice(r * HEAD_DIM, (r + 1) * HEAD_DIM)
            state_out_ref[0, h * HEAD_DIM:(h + 1) * HEAD_DIM, :] = s0[hs] * arep[h:h + 1, :] + upd[hs]

    @pl.when(s == pl.num_programs(1) - 1)
    def _():
        y_ref[...] = _gated_norm(yacc_ref[...], z_ref[...], c["gw"][...])


def _ssd_sample(proj, pre, dt, dtT, state, consts, nseq, seq_len):
    per = CHUNK // seq_len
    nblk = nseq // per
    t = nseq * seq_len
    return pl.pallas_call(
        functools.partial(_ssd_sample_body, seq_len),
        grid=(nblk, per),
        in_specs=[
            pl.BlockSpec((CHUNK, D_INNER), lambda i, s: (i, 0)),
            pl.BlockSpec((CHUNK, CONV_DIM), lambda i, s: (i, 3 * D_INNER // CONV_DIM)),
            pl.BlockSpec((CHUNK, CONV_DIM), lambda i, s: (i, 0)),
            pl.BlockSpec((CHUNK, HEADS), lambda i, s: (i, 0)),
            pl.BlockSpec((HEADS, CHUNK), lambda i, s: (0, i)),
            pl.BlockSpec((1, D_INNER, D_STATE), lambda i, s: (i * per + s, 0, 0)),
        ] + _const_specs(consts, 2),
        out_specs=[
            pl.BlockSpec((CHUNK, D_INNER), lambda i, s: (i, 0)),
            pl.BlockSpec((1, D_INNER, D_STATE), lambda i, s: (i * per + s, 0, 0)),
        ],
        out_shape=[
            jax.ShapeDtypeStruct((t, D_INNER), bf16),
            jax.ShapeDtypeStruct((nseq, D_INNER, D_STATE), f32),
        ],
        scratch_shapes=[
            pltpu.VMEM((CHUNK, GROUPS * D_STATE), f32),
            pltpu.VMEM((CHUNK, GROUPS * D_STATE), f32),
            pltpu.VMEM((D_INNER, CHUNK), bf16),
            pltpu.VMEM((CHUNK, D_INNER), f32),
            pltpu.VMEM((CHUNK, D_INNER), f32),
            pltpu.VMEM((HEADS, CHUNK), f32),
        ],
        compiler_params=_params("arbitrary", "arbitrary"),
        name="ssd_sample",
    )(proj, proj, pre, dt, dtT, state, *[consts[k] for k in _SSD_CONST_NAMES])


def _mix_body(uv_ref, gates_ref, yssd_ref, x_ref, ws_ref, bs_ref, vnw_ref, vnb_ref, wbs_ref, wbm_ref, wo_ref,
              nffn_ref, wrT_ref, br_ref, tris_ref,
              h_ref, hf_ref, v_ref, te_ref, gate_ref, rank_ref, cnt_ref, cnt_acc):
    tm = uv_ref.shape[0]

    @pl.when(pl.program_id(0) == 0)
    def _():
        cnt_acc[...] = jnp.zeros_like(cnt_acc)

    uv = uv_ref[...]
    uv = 0.5 * uv * (1.0 + lax.erf(uv * np.float32(np.sqrt(0.5))))
    u, v = uv[:, :D_GMLP], uv[:, D_GMLP:]
    mu = jnp.mean(v, axis=-1, keepdims=True)
    vc = v - mu
    vn = vc * lax.rsqrt(jnp.mean(vc * vc, axis=-1, keepdims=True) + EPS) * vnw_ref[...] + vnb_ref[...]
    v_ref[...] = vn
    vn_bf = vn.astype(bf16)
    gd = D_GMLP // GMLP_HEADS
    rows = []
    for ck in range(tm // CHUNK):
        rs = slice(ck * CHUNK, (ck + 1) * CHUNK)
        heads = []
        for g in range(GMLP_HEADS):
            mixed = jnp.dot(ws_ref[g], vn_bf[rs, g * gd:(g + 1) * gd], preferred_element_type=f32)
            heads.append(mixed + bs_ref[:, g:g + 1])
        rows.append(jnp.concatenate(heads, axis=1))
    y_mlp = u * jnp.concatenate(rows, axis=0)

    a = jnp.dot(yssd_ref[...], wbs_ref[...], preferred_element_type=f32)
    b = jnp.dot(y_mlp.astype(bf16), wbm_ref[...], preferred_element_type=f32)
    gs = jax.nn.sigmoid(gates_ref[...])
    merged = gs[:, :D_MODEL] * a + gs[:, D_MODEL:] * b
    h = x_ref[...] + jnp.dot(merged.astype(bf16), wo_ref[...], preferred_element_type=f32)
    h_ref[...] = h
    hf = h * lax.rsqrt(jnp.mean(h * h, axis=-1, keepdims=True) + EPS) * nffn_ref[...]
    hf_ref[...] = hf

    lg = lax.dot_general(wrT_ref[...], hf, _NT, precision=lax.Precision.HIGHEST, preferred_element_type=f32) + br_ref[...]
    sub = lax.broadcasted_iota(i32, lg.shape, 0)
    idxs, vals = [], []
    for _ in range(TOP_K):
        m = jnp.max(lg, axis=0, keepdims=True)
        idx = jnp.min(jnp.where(lg == m, sub, N_EXPERTS), axis=0, keepdims=True)
        idxs.append(idx)
        vals.append(m)
        lg = jnp.where(sub == idx, -jnp.inf, lg)
    p = jnp.exp(jnp.concatenate(vals, axis=0) - vals[0])
    gate_ref[...] = p / jnp.sum(p, axis=0, keepdims=True)
    te_ref[...] = jnp.concatenate(idxs, axis=0)

    onehots = [sub == idx for idx in idxs]
    member = functools.reduce(jnp.logical_or, onehots).astype(f32)
    before = jnp.dot(member.astype(bf16), tris_ref[...], preferred_element_type=f32) + cnt_acc[:, 0:1]
    rank_ref[...] = jnp.concatenate(
        [jnp.sum(jnp.where(oh, before, 0.0), axis=0, keepdims=True) for oh in onehots], axis=0).astype(i32)
    cnt_acc[...] = cnt_acc[...] + jnp.sum(member, axis=1, keepdims=True)
    cnt_ref[...] = cnt_acc[...].astype(i32)


def _mix(proj, yssd, x2d, mc, tm):
    t = x2d.shape[0]
    const = lambda a: pl.BlockSpec(a.shape, lambda i: (0,) * a.ndim)
    names = ("ws", "bs", "vnw", "vnb", "wbs", "wbm", "wo", "nffn", "wrT", "br", "tris")
    return pl.pallas_call(
        _mix_body,
        grid=(t // tm,),
        in_specs=[
            pl.BlockSpec((tm, D_INNER), lambda i: (i, 1)),
            pl.BlockSpec((tm, D_INNER), lambda i: (i, 2)),
            pl.BlockSpec((tm, D_INNER), lambda i: (i, 0)),
            pl.BlockSpec((tm, D_MODEL), lambda i: (i, 0)),
        ] + [const(mc[k]) for k in names],
        out_specs=[
            pl.BlockSpec((tm, D_MODEL), lambda i: (i, 0)),
            pl.BlockSpec((tm, D_MODEL), lambda i: (i, 0)),
            pl.BlockSpec((tm, D_GMLP), lambda i: (i, 0)),
            pl.BlockSpec((TOP_K, tm), lambda i: (0, i)),
            pl.BlockSpec((TOP_K, tm), lambda i: (0, i)),
            pl.BlockSpec((TOP_K, tm), lambda i: (0, i)),
            pl.BlockSpec((N_EXPERTS, 128), lambda i: (0, 0)),
        ],
        out_shape=[
            jax.ShapeDtypeStruct((t, D_MODEL), f32),
            jax.ShapeDtypeStruct((t, D_MODEL), f32),
            jax.ShapeDtypeStruct((t, D_GMLP), f32),
            jax.ShapeDtypeStruct((TOP_K, t), i32),
            jax.ShapeDtypeStruct((TOP_K, t), f32),
            jax.ShapeDtypeStruct((TOP_K, t), i32),
            jax.ShapeDtypeStruct((N_EXPERTS, 128), i32),
        ],
        scratch_shapes=[pltpu.VMEM((N_EXPERTS, 128), f32)],
        compiler_params=_params("arbitrary"),
        name="mix_route",
    )(proj, proj, yssd, x2d, *[mc[k] for k in names])


def _dispatch_body(tm, n_p_tiles, slot_ref, pend_ref, padded_ref, hfp_ref, hfs_ref, xs_ref, zero_ref, sem):
    i = pl.program_id(0)

    def tail_copy(e):
        start = pl.multiple_of(pend_ref[e] - MOE_BLOCK, MOE_BLOCK)
        return pltpu.make_async_copy(zero_ref, xs_ref.at[pl.ds(start, MOE_BLOCK), :], sem)

    def unused_copy(j):
        return pltpu.make_async_copy(zero_ref, xs_ref.at[pl.ds(j * MOE_BLOCK, MOE_BLOCK), :], sem)

    @pl.when(i == 0)
    def _():
        zero_ref[...] = jnp.zeros_like(zero_ref)
        n_blocks = xs_ref.shape[0] // MOE_BLOCK
        for start_or_wait in ("start", "wait"):
            for e in range(N_EXPERTS):
                @pl.when(padded_ref[e] > 0)
                def _():
                    getattr(tail_copy(e), start_or_wait)()
            for j in range(n_blocks - N_EXPERTS, n_blocks):
                @pl.when(j * MOE_BLOCK >= pend_ref[N_EXPERTS - 1])
                def _():
                    getattr(unused_copy(j), start_or_wait)()

    def scatter(src_ref, tile):
        def row_copy(r, k):
            tok = tile * tm + r
            slot = slot_ref[(i * tm + r) * TOP_K + k]
            return pltpu.make_async_copy(src_ref.at[tok], xs_ref.at[slot], sem)

        def start(r, carry):
            for k in range(TOP_K):
                row_copy(r, k).start()
            return carry

        def wait(r, carry):
            for k in range(TOP_K):
                row_copy(r, k).wait()
            return carry

        lax.fori_loop(0, tm, start, 0)
        lax.fori_loop(0, tm, wait, 0)

    @pl.when(i < n_p_tiles)
    def _():
        scatter(hfp_ref, i)

    @pl.when(i >= n_p_tiles)
    def _():
        scatter(hfs_ref, i - n_p_tiles)


def _dispatch(slot_flat, pend, padded, hf_p, hf_s, n_slots, tm):
    n_p, n_s = hf_p.shape[0] // tm, hf_s.shape[0] // tm
    any_spec = pl.BlockSpec(memory_space=pl.ANY)
    return pl.pallas_call(
        functools.partial(_dispatch_body, tm, n_p),
        grid_spec=pltpu.PrefetchScalarGridSpec(
            num_scalar_prefetch=3,
            grid=(n_p + n_s,),
            in_specs=[any_spec, any_spec],
            out_specs=any_spec,
            scratch_shapes=[pltpu.VMEM((MOE_BLOCK, D_MODEL), f32), pltpu.SemaphoreType.DMA],
        ),
        out_shape=jax.ShapeDtypeStruct((n_slots, D_MODEL), f32),
        compiler_params=_params("arbitrary"),
        name="moe_dispatch",
    )(slot_flat, pend, padded, hf_p, hf_s)


def _experts_body(be_ref, nused_ref, xs_ref, wgu_ref, bgu_ref, wd_ref, bd_ref, ys_ref, wd_bf):
    i = pl.program_id(0)

    @pl.when(i < nused_ref[0])
    def _():
        @pl.when((i == 0) | (be_ref[i] != be_ref[jnp.maximum(i - 1, 0)]))
        def _():
            wd_bf[...] = wd_ref[0].astype(bf16)

        gu = jnp.dot(xs_ref[...].astype(bf16), wgu_ref[0], preferred_element_type=f32) + bgu_ref[0]
        glu = jnp.minimum(gu[:, :D_FF], SWIGLU_LIMIT)
        lin = jnp.clip(gu[:, D_FF:], -SWIGLU_LIMIT, SWIGLU_LIMIT)
        act = glu * jax.nn.sigmoid(SWIGLU_ALPHA * glu) * (lin + 1.0)
        ys_ref[...] = jnp.dot(act.astype(bf16), wd_bf[...], preferred_element_type=f32) + bd_ref[0]

    @pl.when(i >= nused_ref[0])
    def _():
        ys_ref[...] = jnp.zeros_like(ys_ref)


def _experts(block_e, n_used, xs, wgu, bgu, w_down, b_down):
    n_blocks = xs.shape[0] // MOE_BLOCK
    blk = lambda i, be, nu: (jnp.minimum(i, nu[0] - 1), 0)
    exp3 = lambda i, be, nu: (be[i], 0, 0)
    return pl.pallas_call(
        _experts_body,
        grid_spec=pltpu.PrefetchScalarGridSpec(
            num_scalar_prefetch=2,
            grid=(n_blocks,),
            in_specs=[
                pl.BlockSpec((MOE_BLOCK, D_MODEL), blk),
                pl.BlockSpec((1, D_MODEL, 2 * D_FF), exp3),
                pl.BlockSpec((1, 1, 2 * D_FF), exp3),
                pl.BlockSpec((1, D_FF, D_MODEL), exp3),
                pl.BlockSpec((1, 1, D_MODEL), exp3),
            ],
            out_specs=pl.BlockSpec((MOE_BLOCK, D_MODEL), lambda i, be, nu: (i, 0)),
            scratch_shapes=[pltpu.VMEM((D_FF, D_MODEL), bf16)],
        ),
        out_shape=jax.ShapeDtypeStruct(xs.shape, f32),
        compiler_params=_params("arbitrary"),
        name="moe_experts",
    )(block_e, n_used, xs, wgu, bgu, w_down, b_down)


def _combine_body(tm, slot_ref, h_ref, gate_ref, nf_ref, ys_ref, o_ref, rows_ref, sem):
    i = pl.program_id(0)

    def row_copy(r, k):
        slot = slot_ref[(i * tm + r) * TOP_K + k]
        return pltpu.make_async_copy(ys_ref.at[slot], rows_ref.at[k, r], sem)

    def start(r, carry):
        for k in range(TOP_K):
            row_copy(r, k).start()
        return carry

    def wait(r, carry):
        for k in range(TOP_K):
            row_copy(r, k).wait()
        return carry

    lax.fori_loop(0, tm, start, 0)
    lax.fori_loop(0, tm, wait, 0)
    g = gate_ref[...]
    moe = g[:, 0:1] * rows_ref[0]
    for k in range(1, TOP_K):
        moe = moe + g[:, k:k + 1] * rows_ref[k]
    h = h_ref[...] + moe
    o_ref[...] = h * lax.rsqrt(jnp.mean(h * h, axis=-1, keepdims=True) + EPS) * nf_ref[...]


def _combine(slot_flat, h, gate_t, norm_final, ys, tm):
    t = h.shape[0]
    return pl.pallas_call(
        functools.partial(_combine_body, tm),
        grid_spec=pltpu.PrefetchScalarGridSpec(
            num_scalar_prefetch=1,
            grid=(t // tm,),
            in_specs=[
                pl.BlockSpec((tm, D_MODEL), lambda i, s: (i, 0)),
                pl.BlockSpec((tm, TOP_K), lambda i, s: (i, 0)),
                pl.BlockSpec((1, D_MODEL), lambda i, s: (0, 0)),
                pl.BlockSpec(memory_space=pl.ANY),
            ],
            out_specs=pl.BlockSpec((tm, D_MODEL), lambda i, s: (i, 0)),
            scratch_shapes=[pltpu.VMEM((TOP_K, tm, D_MODEL), f32), pltpu.SemaphoreType.DMA],
        ),
        out_shape=jax.ShapeDtypeStruct((t, D_MODEL), f32),
        compiler_params=_params("arbitrary"),
        name="moe_combine",
    )(slot_flat, h, gate_t, norm_final, ys)


def kernel(x_prompt, x_sample, state_conv, state_ssm, norm_mix, w_in, conv_w, conv_b, dt_bias, a_log, d_skip, gnorm_w, v_norm_w, v_norm_b, w_spatial, b_spatial, w_branch_ssd, w_branch_mlp, w_out, norm_ffn, w_router, b_router, w_gu, b_gu, w_down, b_down, norm_final):
    assert w_in.shape[0] == 1, "single-layer trunk"
    batch, seq, _ = x_prompt.shape
    nseq, dec_seq, _ = x_sample.shape
    assert seq % CHUNK == 0 and CHUNK % dec_seq == 0 and nseq % (CHUNK // dec_seq) == 0 and dec_seq >= CONV_W - 1
    t_p, t_s = batch * seq, nseq * dec_seq
    tm = 256
    assert t_p % tm == 0 and t_s % tm == 0

    wi = w_in[0]
    z0, x0, d0, u0 = D_INNER, D_INNER + CONV_DIM, D_INNER + CONV_DIM + HEADS, D_INNER + CONV_DIM + HEADS + 2 * D_GMLP
    w_main = jnp.concatenate([wi[:, :z0], wi[:, d0:u0], wi[:, u0:], wi[:, z0:x0]], axis=1).astype(bf16)
    w_dt = wi[:, x0:d0].astype(bf16)
    w_dtT = w_dt.T
    nm = norm_mix[0].reshape(1, D_MODEL)
    sc = lambda L: _ssd_consts(conv_w[0], conv_b[0], dt_bias[0], a_log[0], d_skip[0], gnorm_w[0], L)
    xp2, xs2 = x_prompt.reshape(t_p, D_MODEL), x_sample.reshape(t_s, D_MODEL)

    tril = np.tril(np.ones((CHUNK, CHUNK), bool))
    ws_p = jnp.where(tril[None], w_spatial[0], 0).astype(bf16)
    per = CHUNK // dec_seq
    blockdiag = (np.arange(CHUNK)[:, None] // dec_seq) == (np.arange(CHUNK)[None, :] // dec_seq)
    ws_s = jnp.where((tril & blockdiag)[None], jnp.tile(w_spatial[0][:, :dec_seq, :dec_seq], (1, per, per)), 0).astype(bf16)
    bs_p = b_spatial[0].T
    bs_s = jnp.tile(b_spatial[0][:, :dec_seq], (1, per)).T
    mc = dict(
        vnw=v_norm_w[0].reshape(1, D_GMLP), vnb=v_norm_b[0].reshape(1, D_GMLP),
        wbs=w_branch_ssd[0].astype(bf16), wbm=w_branch_mlp[0].astype(bf16), wo=w_out[0].astype(bf16),
        nffn=norm_ffn[0].reshape(1, D_MODEL), wrT=w_router[0].T, br=b_router[0].reshape(N_EXPERTS, 1),
        tris=jnp.asarray(np.triu(np.ones((tm, tm), np.float32), 1), bf16),
    )

    proj_p, dt_p, dtT_p = _in_proj(xp2, nm, w_main, w_dt, w_dtT)
    yssd_p, ssm_p = _ssd_prompt(proj_p, dt_p, dtT_p, sc(CHUNK), batch, seq)
    h_p, hf_p, _, te_p, gate_p, rank_p, cnt_p = _mix(proj_p, yssd_p, xp2, dict(mc, ws=ws_p, bs=bs_p), tm)

    proj_s, dt_s, dtT_s = _in_proj(xs2, nm, w_main, w_dt, w_dtT)
    pre = jnp.pad(state_conv[0], ((0, 0), (dec_seq - (CONV_W - 1), 0), (0, 0))).reshape(t_s, CONV_DIM)
    yssd_s, ssm_s = _ssd_sample(proj_s, pre, dt_s, dtT_s, state_ssm[0].reshape(nseq, D_INNER, D_STATE),
                                sc(dec_seq), nseq, dec_seq)
    h_s, hf_s, v_s, te_s, gate_s, rank_s, cnt_s = _mix(proj_s, yssd_s, xs2, dict(mc, ws=ws_s, bs=bs_s), tm)

    cp, cs = cnt_p[:, 0], cnt_s[:, 0]
    padded = (cp + cs + MOE_BLOCK - 1) // MOE_BLOCK * MOE_BLOCK
    pend = jnp.cumsum(padded)
    pstart = pend - padded
    slot_p = (pstart[te_p] + rank_p).T.reshape(-1)
    slot_s = (pstart[te_s] + cp[te_s] + rank_s).T.reshape(-1)
    slot_all = jnp.concatenate([slot_p, slot_s]).astype(i32)
    n_blocks = (t_p + t_s) * TOP_K // MOE_BLOCK + N_EXPERTS
    n_used = (pend[-1] // MOE_BLOCK).astype(i32).reshape(1)
    block_e = jnp.minimum(jnp.searchsorted(pend, jnp.arange(n_blocks) * MOE_BLOCK, side="right"), N_EXPERTS - 1).astype(i32)

    xs_sorted = _dispatch(slot_all, pend.astype(i32), padded.astype(i32), hf_p, hf_s, n_blocks * MOE_BLOCK, tm)
    wgu = jnp.concatenate([w_gu[0][..., 0::2], w_gu[0][..., 1::2]], axis=-1).astype(bf16)
    bgu = jnp.concatenate([b_gu[0][..., 0::2], b_gu[0][..., 1::2]], axis=-1).reshape(N_EXPERTS, 1, 2 * D_FF)
    ys = _experts(block_e, n_used, xs_sorted, wgu, bgu, w_down[0], b_down[0].reshape(N_EXPERTS, 1, D_MODEL))

    nf = norm_final.reshape(1, D_MODEL)
    y_p = _combine(slot_p.astype(i32), h_p, gate_p.T, nf, ys, tm)
    y_s = _combine(slot_s.astype(i32), h_s, gate_s.T, nf, ys, tm)

    xbc_p = proj_p[:, 3 * D_INNER:].reshape(batch, seq, CONV_DIM)
    xbc_s = proj_s[:, 3 * D_INNER:].reshape(nseq, dec_seq, CONV_DIM)
    st_shape = (HEADS, HEAD_DIM, D_STATE)
    return (
        y_p.reshape(batch, seq, D_MODEL),
        y_s.reshape(nseq, dec_seq, D_MODEL),
        xbc_p[None, :, seq - (CONV_W - 1):],
        ssm_p.reshape(1, batch, *st_shape),
        xbc_s[None, :, dec_seq - (CONV_W - 1):],
        ssm_s.reshape(1, nseq, *st_shape),
        v_s.reshape(1, nseq, dec_seq, D_GMLP),
    )
```

```python
import functools

import numpy as np
import jax
import jax.numpy as jnp
from jax import lax
from jax.experimental import pallas as pl
from jax.experimental.pallas import tpu as pltpu

f32, bf16, i32 = jnp.float32, jnp.bfloat16, jnp.int32

D_MODEL = 1024
D_INNER = 2048
HEAD_DIM = 64
HEADS = 32
GROUPS = 4
HEADS_PER_GROUP = 8
GROUP_WIDTH = HEADS_PER_GROUP * HEAD_DIM
D_STATE = 128
CONV_W = 4
CONV_DIM = D_INNER + 2 * GROUPS * D_STATE
CHUNK = 128
D_GMLP = 1024
GMLP_HEADS = 8
N_EXPERTS = 32
TOP_K = 4
D_FF = 1024
SWIGLU_ALPHA = 1.702
SWIGLU_LIMIT = 7.0
EPS = 1e-5
MOE_BLOCK = 256
DEINT = 256
ROWS_PER_ITER = 8
D_MAIN = 3 * D_INNER + CONV_DIM
VMEM_LIMIT = 56 * 1024 * 1024

_NT = (((1,), (1,)), ((), ()))


def _params(*sem):
    return pltpu.CompilerParams(dimension_semantics=sem, vmem_limit_bytes=VMEM_LIMIT)


def _split3(x):
    hi = x.astype(bf16)
    r = x - hi.astype(f32)
    mid = r.astype(bf16)
    lo = (r - mid.astype(f32)).astype(bf16)
    return hi, mid, lo


def _dot_sel_rhs(x, sel):
    return sum(jnp.dot(p, sel, preferred_element_type=f32) for p in _split3(x))


def _dot_sel_lhs(sel, x):
    return sum(jnp.dot(sel, p, preferred_element_type=f32) for p in _split3(x))


def _silu(x):
    return x * jax.nn.sigmoid(x)


def _in_proj_body(x_ref, nw_ref, w_ref, wdt_ref, wdtT_ref, o_ref, dt_ref, dtT_ref, hn_ref):
    @pl.when(pl.program_id(1) == 0)
    def _():
        x = x_ref[...]
        ms = jnp.mean(x * x, axis=-1, keepdims=True)
        hn = (x * lax.rsqrt(ms + EPS) * nw_ref[...]).astype(bf16)
        hn_ref[...] = hn
        dt_ref[...] = jnp.dot(hn, wdt_ref[...], preferred_element_type=f32)
        dtT_ref[...] = lax.dot_general(wdtT_ref[...], hn, _NT, preferred_element_type=f32)

    o_ref[...] = jnp.dot(hn_ref[...], w_ref[...], preferred_element_type=f32)


def _in_proj(x2d, norm_w, w_main, w_dt, w_dtT):
    t = x2d.shape[0]
    tm = min(1024, t)
    tn = 512
    return pl.pallas_call(
        _in_proj_body,
        grid=(t // tm, D_MAIN // tn),
        in_specs=[
            pl.BlockSpec((tm, D_MODEL), lambda i, j: (i, 0)),
            pl.BlockSpec((1, D_MODEL), lambda i, j: (0, 0)),
            pl.BlockSpec((D_MODEL, tn), lambda i, j: (0, j)),
            pl.BlockSpec((D_MODEL, HEADS), lambda i, j: (0, 0)),
            pl.BlockSpec((HEADS, D_MODEL), lambda i, j: (0, 0)),
        ],
        out_specs=[
            pl.BlockSpec((tm, tn), lambda i, j: (i, j)),
            pl.BlockSpec((tm, HEADS), lambda i, j: (i, 0)),
            pl.BlockSpec((HEADS, tm), lambda i, j: (0, i)),
        ],
        out_shape=[
            jax.ShapeDtypeStruct((t, D_MAIN), f32),
            jax.ShapeDtypeStruct((t, HEADS), f32),
            jax.ShapeDtypeStruct((HEADS, t), f32),
        ],
        scratch_shapes=[pltpu.VMEM((tm, D_MODEL), bf16)],
        compiler_params=_params("arbitrary", "arbitrary"),
        name="in_proj",
    )(x2d, norm_w, w_main, w_dt, w_dtT)


def _conv_silu(cur, prev, cw_ref, cb_ref, seq_len, prev_shift):
    n = cur.shape[0]
    pos = lax.broadcasted_iota(i32, (n, 1), 0) % seq_len
    acc = cb_ref[...]
    for k in range(CONV_W):
        j = CONV_W - 1 - k
        if j == 0:
            xj = cur
        else:
            xj = jnp.where(pos >= j, pltpu.roll(cur, j, 0), pltpu.roll(prev, (j + prev_shift) % n, 0))
        acc = acc + xj * cw_ref[k:k + 1, :]
    return _silu(acc)


def _ssd_block(xa, dt_raw, dtT_raw, c):
    tril = c["tril"][...]
    dt = jax.nn.softplus(dt_raw + c["dtb_row"][...])
    dtT = jax.nn.softplus(dtT_raw + c["dtb_col"][...])
    a = dt * c["aneg_row"][...]
    aT = dtT * c["aneg_col"][...]
    a_cum = _dot_sel_lhs(tril, a)
    a_cumT = _dot_sel_rhs(aT, c["triu"][...])
    a_tot = _dot_sel_lhs(c["same"][...], a)
    stack = jnp.concatenate([dt, jnp.exp(a_cum), jnp.exp(a_tot - a_cum)], axis=0)
    ex = _dot_sel_rhs(stack, c["expand"][...])
    n = xa.shape[0]
    dtx, eax, dex = ex[:n], ex[n:2 * n], ex[2 * n:]

    xs = xa[:, :D_INNER]
    bm = xa[:, D_INNER:D_INNER + GROUPS * D_STATE]
    cm = xa[:, D_INNER + GROUPS * D_STATE:]
    xdt = xs * dtx
    xdt_bf = xdt.astype(bf16)
    mask = tril > 0
    lane = lax.broadcasted_iota(i32, (1, 2 * HEAD_DIM), 1)
    ys = []
    for g in range(GROUPS):
        cg = cm[:, g * D_STATE:(g + 1) * D_STATE].astype(bf16)
        bg = bm[:, g * D_STATE:(g + 1) * D_STATE].astype(bf16)
        cb = lax.dot_general(cg, bg, _NT, preferred_element_type=f32)
        for pair in range(HEADS_PER_GROUP // 2):
            halves = []
            for h in (g * HEADS_PER_GROUP + 2 * pair, g * HEADS_PER_GROUP + 2 * pair + 1):
                seg = a_cum[:, h:h + 1] - a_cumT[h:h + 1, :]
                decay = jnp.exp(jnp.where(mask, seg, -jnp.inf))
                m = (cb * decay).astype(bf16)
                col = (h // 2) * 2 * HEAD_DIM
                halves.append(jnp.dot(m, xdt_bf[:, col:col + 2 * HEAD_DIM], preferred_element_type=f32))
            ys.append(jnp.where(lane < HEAD_DIM, halves[0], halves[1]))
    y_diag = jnp.concatenate(ys, axis=1)
    return dict(xs=xs, bm=bm, cm=cm, xdt=xdt, xd=xdt * dex, eax=eax, aT=aT, y_diag=y_diag)


def _gated_norm(y, z, gw):
    g = y * _silu(z)
    ms = jnp.mean(g * g, axis=-1, keepdims=True)
    return (g * lax.rsqrt(ms + EPS) * gw).astype(bf16)


_SSD_CONST_NAMES = ("cw", "cb", "dtb_row", "dtb_col", "aneg_row", "aneg_col", "dskipx", "gw",
                    "tril", "triu", "same", "expand")


def _ssd_consts(conv_w, conv_b, dt_bias, a_log, d_skip, gnorm_w, seq_len):
    r = np.arange(CHUNK)
    same = (r[:, None] // seq_len) == (r[None, :] // seq_len)
    tril = same & (r[None, :] <= r[:, None])
    expand = np.repeat(np.eye(HEADS, dtype=np.float32), HEAD_DIM, axis=1)
    aneg = -jnp.exp(a_log.astype(f32))
    return dict(
        cw=conv_w, cb=conv_b.reshape(1, CONV_DIM),
        dtb_row=dt_bias.reshape(1, HEADS), dtb_col=dt_bias.reshape(HEADS, 1),
        aneg_row=aneg.reshape(1, HEADS), aneg_col=aneg.reshape(HEADS, 1),
        dskipx=jnp.repeat(d_skip.astype(f32), HEAD_DIM).reshape(1, D_INNER),
        gw=gnorm_w.reshape(1, D_INNER),
        tril=jnp.asarray(tril, bf16), triu=jnp.asarray(tril.T, bf16), same=jnp.asarray(same, bf16),
        expand=jnp.asarray(expand, bf16),
    )


def _const_specs(consts, ngrid):
    zero = (lambda *_: (0, 0))
    return [pl.BlockSpec(consts[k].shape, zero) for k in _SSD_CONST_NAMES]


def _ssd_prompt_body(z_ref, xbc_ref, dt_ref, dtT_ref, *rest):
    nc = len(_SSD_CONST_NAMES)
    c = dict(zip(_SSD_CONST_NAMES, rest[:nc]))
    y_ref, state_ref, prev_ref, st_ref = rest[nc:]
    ci = pl.program_id(1)

    @pl.when(ci == 0)
    def _():
        prev_ref[...] = jnp.zeros_like(prev_ref)
        st_ref[...] = jnp.zeros_like(st_ref)

    cur = xbc_ref[...]
    xa = _conv_silu(cur, prev_ref[...], c["cw"], c["cb"], CHUNK, 0)
    prev_ref[...] = cur
    b = _ssd_block(xa, dt_ref[...], dtT_ref[...], c)

    y_off = []
    for g in range(GROUPS):
        sl = slice(g * GROUP_WIDTH, (g + 1) * GROUP_WIDTH)
        st = st_ref[g]
        cg = b["cm"][:, g * D_STATE:(g + 1) * D_STATE].astype(bf16)
        y_off.append(jnp.dot(cg, st.astype(bf16), preferred_element_type=f32))
        bgT = b["bm"][:, g * D_STATE:(g + 1) * D_STATE].T.astype(bf16)
        upd = jnp.dot(bgT, b["xd"][:, sl].astype(bf16), preferred_element_type=f32)
        st_ref[g] = st * b["eax"][CHUNK - 1:CHUNK, sl] + upd
    y = b["y_diag"] + jnp.concatenate(y_off, axis=1) * b["eax"] + c["dskipx"][...] * b["xs"]
    y_ref[...] = _gated_norm(y, z_ref[...], c["gw"][...])

    @pl.when(ci == pl.num_programs(1) - 1)
    def _():
        for g in range(GROUPS):
            state_ref[0, g * GROUP_WIDTH:(g + 1) * GROUP_WIDTH, :] = st_ref[g].T


def _ssd_prompt(proj, dt, dtT, consts, batch, seq):
    nchunk = seq // CHUNK
    t = batch * seq
    row = lambda b, ci: b * nchunk + ci
    return pl.pallas_call(
        _ssd_prompt_body,
        grid=(batch, nchunk),
        in_specs=[
            pl.BlockSpec((CHUNK, D_INNER), lambda b, ci: (row(b, ci), 0)),
            pl.BlockSpec((CHUNK, CONV_DIM), lambda b, ci: (row(b, ci), 3 * D_INNER // CONV_DIM)),
            pl.BlockSpec((CHUNK, HEADS), lambda b, ci: (row(b, ci), 0)),
            pl.BlockSpec((HEADS, CHUNK), lambda b, ci: (0, row(b, ci))),
        ] + _const_specs(consts, 2),
        out_specs=[
            pl.BlockSpec((CHUNK, D_INNER), lambda b, ci: (row(b, ci), 0)),
            pl.BlockSpec((1, D_INNER, D_STATE), lambda b, ci: (b, 0, 0)),
        ],
        out_shape=[
            jax.ShapeDtypeStruct((t, D_INNER), bf16),
            jax.ShapeDtypeStruct((batch, D_INNER, D_STATE), f32),
        ],
        scratch_shapes=[pltpu.VMEM((CHUNK, CONV_DIM), f32), pltpu.VMEM((GROUPS, D_STATE, GROUP_WIDTH), f32)],
        compiler_params=_params("arbitrary", "arbitrary"),
        name="ssd_prompt",
    )(proj, proj, dt, dtT, *[consts[k] for k in _SSD_CONST_NAMES])


def _ssd_sample_body(seq_len, z_ref, xbc_ref, pre_ref, dt_ref, dtT_ref, state_ref, *rest):
    nc = len(_SSD_CONST_NAMES)
    c = dict(zip(_SSD_CONST_NAMES, rest[:nc]))
    y_ref, state_out_ref, c_ref, b_ref, xdT_ref, yacc_ref, eax_ref, eatT_ref = rest[nc:]
    s = pl.program_id(1)

    @pl.when(s == 0)
    def _():
        xa = _conv_silu(xbc_ref[...], pre_ref[...], c["cw"], c["cb"], seq_len, CHUNK - seq_len)
        b = _ssd_block(xa, dt_ref[...], dtT_ref[...], c)
        c_ref[...] = b["cm"]
        b_ref[...] = b["bm"]
        xdT_ref[...] = b["xd"].T.astype(bf16)
        yacc_ref[...] = b["y_diag"] + c["dskipx"][...] * b["xs"]
        eax_ref[...] = b["eax"]
        eatT_ref[...] = jnp.exp(_dot_sel_rhs(b["aT"], c["same"][...]))

    r0 = pl.multiple_of(s * seq_len, seq_len)
    rows = pl.ds(r0, seq_len)
    lane = lax.broadcasted_iota(i32, (1, CHUNK), 1)
    arep = jnp.broadcast_to(jnp.sum(jnp.where(lane == r0, eatT_ref[...], 0.0), axis=1, keepdims=True), (HEADS, D_STATE))
    rmask = (lax.broadcasted_iota(i32, (CHUNK, 1), 0) // seq_len) == s
    for g in range(GROUPS):
        gs = slice(g * GROUP_WIDTH, (g + 1) * GROUP_WIDTH)
        ns = slice(g * D_STATE, (g + 1) * D_STATE)
        s0 = state_ref[0, gs, :]
        cg = c_ref[rows, ns].astype(bf16)
        yo = lax.dot_general(cg, s0.astype(bf16), _NT, preferred_element_type=f32)
        yacc_ref[rows, gs] = yacc_ref[rows, gs] + yo * eax_ref[rows, gs]
        bmask = jnp.where(rmask, b_ref[:, ns], 0.0).astype(bf16)
        upd = jnp.dot(xdT_ref[gs, :], bmask, preferred_element_type=f32)
        for r in range(HEADS_PER_GROUP):
            h = g * HEADS_PER_GROUP + r
            hs = slice(r * HEAD_DIM, (r + 1) * HEAD_DIM)
            state_out_ref[0, h * HEAD_DIM:(h + 1) * HEAD_DIM, :] = s0[hs] * arep[h:h + 1, :] + upd[hs]

    @pl.when(s == pl.num_programs(1) - 1)
    def _():
        y_ref[...] = _gated_norm(yacc_ref[...], z_ref[...], c["gw"][...])


def _ssd_sample(proj, pre, dt, dtT, state, consts, nseq, seq_len):
    per = CHUNK // seq_len
    nblk = nseq // per
    t = nseq * seq_len
    return pl.pallas_call(
        functools.partial(_ssd_sample_body, seq_len),
        grid=(nblk, per),
        in_specs=[
            pl.BlockSpec((CHUNK, D_INNER), lambda i, s: (i, 0)),
            pl.BlockSpec((CHUNK, CONV_DIM), lambda i, s: (i, 3 * D_INNER // CONV_DIM)),
            pl.BlockSpec((CHUNK, CONV_DIM), lambda i, s: (i, 0)),
            pl.BlockSpec((CHUNK, HEADS), lambda i, s: (i, 0)),
            pl.BlockSpec((HEADS, CHUNK), lambda i, s: (0, i)),
            pl.BlockSpec((1, D_INNER, D_STATE), lambda i, s: (i * per + s, 0, 0)),
        ] + _const_specs(consts, 2),
        out_specs=[
            pl.BlockSpec((CHUNK, D_INNER), lambda i, s: (i, 0)),
            pl.BlockSpec((1, D_INNER, D_STATE), lambda i, s: (i * per + s, 0, 0)),
        ],
        out_shape=[
            jax.ShapeDtypeStruct((t, D_INNER), bf16),
            jax.ShapeDtypeStruct((nseq, D_INNER, D_STATE), f32),
        ],
        scratch_shapes=[
            pltpu.VMEM((CHUNK, GROUPS * D_STATE), f32),
            pltpu.VMEM((CHUNK, GROUPS * D_STATE), f32),
            pltpu.VMEM((D_INNER, CHUNK), bf16),
            pltpu.VMEM((CHUNK, D_INNER), f32),
            pltpu.VMEM((CHUNK, D_INNER), f32),
            pltpu.VMEM((HEADS, CHUNK), f32),
        ],
        compiler_params=_params("arbitrary", "arbitrary"),
        name="ssd_sample",
    )(proj, proj, pre, dt, dtT, state, *[consts[k] for k in _SSD_CONST_NAMES])


def _mix_body(uv_ref, gates_ref, yssd_ref, x_ref, ws_ref, bs_ref, vnw_ref, vnb_ref, wbs_ref, wbm_ref, wo_ref,
              nffn_ref, wrT_ref, br_ref, tris_ref,
              h_ref, hf_ref, v_ref, te_ref, gate_ref, rank_ref, cnt_ref, cnt_acc):
    tm = uv_ref.shape[0]

    @pl.when(pl.program_id(0) == 0)
    def _():
        cnt_acc[...] = jnp.zeros_like(cnt_acc)

    uv = uv_ref[...]
    uv = 0.5 * uv * (1.0 + lax.erf(uv * np.float32(np.sqrt(0.5))))
    u, v = uv[:, :D_GMLP], uv[:, D_GMLP:]
    mu = jnp.mean(v, axis=-1, keepdims=True)
    vc = v - mu
    vn = vc * lax.rsqrt(jnp.mean(vc * vc, axis=-1, keepdims=True) + EPS) * vnw_ref[...] + vnb_ref[...]
    v_ref[...] = vn
    vn_bf = vn.astype(bf16)
    gd = D_GMLP // GMLP_HEADS
    rows = []
    for ck in range(tm // CHUNK):
        rs = slice(ck * CHUNK, (ck + 1) * CHUNK)
        heads = []
        for g in range(GMLP_HEADS):
            mixed = jnp.dot(ws_ref[g], vn_bf[rs, g * gd:(g + 1) * gd], preferred_element_type=f32)
            heads.append(mixed + bs_ref[:, g:g + 1])
        rows.append(jnp.concatenate(heads, axis=1))
    y_mlp = u * jnp.concatenate(rows, axis=0)

    a = jnp.dot(yssd_ref[...], wbs_ref[...], preferred_element_type=f32)
    b = jnp.dot(y_mlp.astype(bf16), wbm_ref[...], preferred_element_type=f32)
    gs = jax.nn.sigmoid(gates_ref[...])
    merged = gs[:, :D_MODEL] * a + gs[:, D_MODEL:] * b
    h = x_ref[...] + jnp.dot(merged.astype(bf16), wo_ref[...], preferred_element_type=f32)
    h_ref[...] = h
    hf = h * lax.rsqrt(jnp.mean(h * h, axis=-1, keepdims=True) + EPS) * nffn_ref[...]
    hf_ref[...] = hf

    lg = lax.dot_general(wrT_ref[...], hf, _NT, precision=lax.Precision.HIGHEST, preferred_element_type=f32) + br_ref[...]
    sub = lax.broadcasted_iota(i32, lg.shape, 0)
    idxs, vals = [], []
    for _ in range(TOP_K):
        m = jnp.max(lg, axis=0, keepdims=True)
        idx = jnp.min(jnp.where(lg == m, sub, N_EXPERTS), axis=0, keepdims=True)
        idxs.append(idx)
        vals.append(m)
        lg = jnp.where(sub == idx, -jnp.inf, lg)
    p = jnp.exp(jnp.concatenate(vals, axis=0) - vals[0])
    gate_ref[...] = p / jnp.sum(p, axis=0, keepdims=True)
    te_ref[...] = jnp.concatenate(idxs, axis=0)

    onehots = [sub == idx for idx in idxs]
    member = functools.reduce(jnp.logical_or, onehots).astype(f32)
    before = jnp.dot(member.astype(bf16), tris_ref[...], preferred_element_type=f32) + cnt_acc[:, 0:1]
    rank_ref[...] = jnp.concatenate(
        [jnp.sum(jnp.where(oh, before, 0.0), axis=0, keepdims=True) for oh in onehots], axis=0).astype(i32)
    cnt_acc[...] = cnt_acc[...] + jnp.sum(member, axis=1, keepdims=True)
    cnt_ref[...] = cnt_acc[...].astype(i32)


def _mix(proj, yssd, x2d, mc, tm):
    t = x2d.shape[0]
    const = lambda a: pl.BlockSpec(a.shape, lambda i: (0,) * a.ndim)
    names = ("ws", "bs", "vnw", "vnb", "wbs", "wbm", "wo", "nffn", "wrT", "br", "tris")
    return pl.pallas_call(
        _mix_body,
        grid=(t // tm,),
        in_specs=[
            pl.BlockSpec((tm, D_INNER), lambda i: (i, 1)),
            pl.BlockSpec((tm, D_INNER), lambda i: (i, 2)),
            pl.BlockSpec((tm, D_INNER), lambda i: (i, 0)),
            pl.BlockSpec((tm, D_MODEL), lambda i: (i, 0)),
        ] + [const(mc[k]) for k in names],
        out_specs=[
            pl.BlockSpec((tm, D_MODEL), lambda i: (i, 0)),
            pl.BlockSpec((tm, D_MODEL), lambda i: (i, 0)),
            pl.BlockSpec((tm, D_GMLP), lambda i: (i, 0)),
            pl.BlockSpec((TOP_K, tm), lambda i: (0, i)),
            pl.BlockSpec((TOP_K, tm), lambda i: (0, i)),
            pl.BlockSpec((TOP_K, tm), lambda i: (0, i)),
            pl.BlockSpec((N_EXPERTS, 128), lambda i: (0, 0)),
        ],
        out_shape=[
            jax.ShapeDtypeStruct((t, D_MODEL), f32),
            jax.ShapeDtypeStruct((t, D_MODEL), f32),
            jax.ShapeDtypeStruct((t, D_GMLP), f32),
            jax.ShapeDtypeStruct((TOP_K, t), i32),
            jax.ShapeDtypeStruct((TOP_K, t), f32),
            jax.ShapeDtypeStruct((TOP_K, t), i32),
            jax.ShapeDtypeStruct((N_EXPERTS, 128), i32),
        ],
        scratch_shapes=[pltpu.VMEM((N_EXPERTS, 128), f32)],
        compiler_params=_params("arbitrary"),
        name="mix_route",
    )(proj, proj, yssd, x2d, *[mc[k] for k in names])


def _dispatch_body(tm, n_p_tiles, slot_ref, pend_ref, padded_ref, hfp_ref, hfs_ref, xs_ref, zero_ref, sem):
    i = pl.program_id(0)

    def tail_copy(e):
        start = pl.multiple_of(pend_ref[e] - MOE_BLOCK, MOE_BLOCK)
        return pltpu.make_async_copy(zero_ref, xs_ref.at[pl.ds(start, MOE_BLOCK), :], sem)

    def unused_copy(j):
        return pltpu.make_async_copy(zero_ref, xs_ref.at[pl.ds(j * MOE_BLOCK, MOE_BLOCK), :], sem)

    @pl.when(i == 0)
    def _():
        zero_ref[...] = jnp.zeros_like(zero_ref)
        n_blocks = xs_ref.shape[0] // MOE_BLOCK
        for start_or_wait in ("start", "wait"):
            for e in range(N_EXPERTS):
                @pl.when(padded_ref[e] > 0)
                def _():
                    getattr(tail_copy(e), start_or_wait)()
            for j in range(n_blocks - N_EXPERTS, n_blocks):
                @pl.when(j * MOE_BLOCK >= pend_ref[N_EXPERTS - 1])
                def _():
                    getattr(unused_copy(j), start_or_wait)()

    def scatter(src_ref):
        def row_copy(r, k):
            slot = slot_ref[(i * tm + r) * TOP_K + k]
            return pltpu.make_async_copy(src_ref.at[r], xs_ref.at[slot], sem)

        def start(r8, carry):
            for r in range(ROWS_PER_ITER):
                for k in range(TOP_K):
                    row_copy(r8 * ROWS_PER_ITER + r, k).start(priority=k % 2)
            return carry

        def wait(r8, carry):
            for r in range(ROWS_PER_ITER):
                for k in range(TOP_K):
                    row_copy(r8 * ROWS_PER_ITER + r, k).wait()
            return carry

        lax.fori_loop(0, tm // ROWS_PER_ITER, start, 0)
        lax.fori_loop(0, tm // ROWS_PER_ITER, wait, 0)

    @pl.when(i < n_p_tiles)
    def _():
        scatter(hfp_ref)

    @pl.when(i >= n_p_tiles)
    def _():
        scatter(hfs_ref)


def _dispatch(slot_flat, pend, padded, hf_p, hf_s, n_slots, tm):
    n_p, n_s = hf_p.shape[0] // tm, hf_s.shape[0] // tm
    return pl.pallas_call(
        functools.partial(_dispatch_body, tm, n_p),
        grid_spec=pltpu.PrefetchScalarGridSpec(
            num_scalar_prefetch=3,
            grid=(n_p + n_s,),
            in_specs=[
                pl.BlockSpec((tm, D_MODEL), lambda i, *_: (jnp.minimum(i, n_p - 1), 0)),
                pl.BlockSpec((tm, D_MODEL), lambda i, *_: (jnp.maximum(i - n_p, 0), 0)),
            ],
            out_specs=pl.BlockSpec(memory_space=pl.ANY),
            scratch_shapes=[pltpu.VMEM((MOE_BLOCK, D_MODEL), f32), pltpu.SemaphoreType.DMA],
        ),
        out_shape=jax.ShapeDtypeStruct((n_slots, D_MODEL), f32),
        compiler_params=_params("arbitrary"),
        name="moe_dispatch",
    )(slot_flat, pend, padded, hf_p, hf_s)


def _experts_body(be_ref, nused_ref, xs_ref, wgu_ref, bgu_ref, wd_ref, bd_ref, perm_ref, ys_ref, wgu_bf, wd_bf):
    i = pl.program_id(0)

    @pl.when(i < nused_ref[0])
    def _():
        @pl.when((i == 0) | (be_ref[i] != be_ref[jnp.maximum(i - 1, 0)]))
        def _():
            wd_bf[...] = wd_ref[0].astype(bf16)
            half = DEINT // 2
            for j in range(2 * D_FF // DEINT):
                w = jnp.dot(wgu_ref[0, :, j * DEINT:(j + 1) * DEINT].astype(bf16), perm_ref[...],
                            preferred_element_type=f32).astype(bf16)
                wgu_bf[:, j * half:(j + 1) * half] = w[:, :half]
                wgu_bf[:, D_FF + j * half:D_FF + (j + 1) * half] = w[:, half:]

        gu = jnp.dot(xs_ref[...].astype(bf16), wgu_bf[...], preferred_element_type=f32) + bgu_ref[0]
        glu = jnp.minimum(gu[:, :D_FF], SWIGLU_LIMIT)
        lin = jnp.clip(gu[:, D_FF:], -SWIGLU_LIMIT, SWIGLU_LIMIT)
        act = glu * jax.nn.sigmoid(SWIGLU_ALPHA * glu) * (lin + 1.0)
        ys_ref[...] = jnp.dot(act.astype(bf16), wd_bf[...], preferred_element_type=f32) + bd_ref[0]

    @pl.when(i >= nused_ref[0])
    def _():
        ys_ref[...] = jnp.zeros_like(ys_ref)


def _experts(block_e, n_used, xs, wgu, bgu, w_down, b_down):
    n_blocks = xs.shape[0] // MOE_BLOCK
    blk = lambda i, be, nu: (jnp.minimum(i, nu[0] - 1), 0)
    exp3 = lambda i, be, nu: (be[i], 0, 0)
    c = np.arange(DEINT)
    src = np.where(c < DEINT // 2, 2 * c, 2 * (c - DEINT // 2) + 1)
    perm = jnp.asarray(np.arange(DEINT)[:, None] == src[None, :], bf16)
    return pl.pallas_call(
        _experts_body,
        grid_spec=pltpu.PrefetchScalarGridSpec(
            num_scalar_prefetch=2,
            grid=(n_blocks,),
            in_specs=[
                pl.BlockSpec((MOE_BLOCK, D_MODEL), blk),
                pl.BlockSpec((1, D_MODEL, 2 * D_FF), exp3),
                pl.BlockSpec((1, 1, 2 * D_FF), exp3),
                pl.BlockSpec((1, D_FF, D_MODEL), exp3),
                pl.BlockSpec((1, 1, D_MODEL), exp3),
                pl.BlockSpec((DEINT, DEINT), lambda i, be, nu: (0, 0)),
            ],
            out_specs=pl.BlockSpec((MOE_BLOCK, D_MODEL), lambda i, be, nu: (i, 0)),
            scratch_shapes=[pltpu.VMEM((D_MODEL, 2 * D_FF), bf16), pltpu.VMEM((D_FF, D_MODEL), bf16)],
        ),
        out_shape=jax.ShapeDtypeStruct(xs.shape, f32),
        compiler_params=_params("arbitrary"),
        name="moe_experts",
    )(block_e, n_used, xs, wgu, bgu, w_down, b_down, perm)


def _combine_body(tm, slot_ref, h_ref, gate_ref, nf_ref, ys_ref, o_ref, rows_ref, sem):
    i = pl.program_id(0)

    def row_copy(r, k):
        slot = slot_ref[(i * tm + r) * TOP_K + k]
        return pltpu.make_async_copy(ys_ref.at[slot], rows_ref.at[k, r], sem)

    def start(r8, carry):
        for r in range(ROWS_PER_ITER):
            for k in range(TOP_K):
                row_copy(r8 * ROWS_PER_ITER + r, k).start(priority=k % 2)
        return carry

    def wait(r8, carry):
        for r in range(ROWS_PER_ITER):
            for k in range(TOP_K):
                row_copy(r8 * ROWS_PER_ITER + r, k).wait()
        return carry

    lax.fori_loop(0, tm // ROWS_PER_ITER, start, 0)
    lax.fori_loop(0, tm // ROWS_PER_ITER, wait, 0)
    g = gate_ref[...]
    moe = g[:, 0:1] * rows_ref[0]
    for k in range(1, TOP_K):
        moe = moe + g[:, k:k + 1] * rows_ref[k]
    h = h_ref[...] + moe
    o_ref[...] = h * lax.rsqrt(jnp.mean(h * h, axis=-1, keepdims=True) + EPS) * nf_ref[...]


def _combine(slot_flat, h, gate_t, norm_final, ys, tm):
    t = h.shape[0]
    return pl.pallas_call(
        functools.partial(_combine_body, tm),
        grid_spec=pltpu.PrefetchScalarGridSpec(
            num_scalar_prefetch=1,
            grid=(t // tm,),
            in_specs=[
                pl.BlockSpec((tm, D_MODEL), lambda i, s: (i, 0)),
                pl.BlockSpec((tm, TOP_K), lambda i, s: (i, 0)),
                pl.BlockSpec((1, D_MODEL), lambda i, s: (0, 0)),
                pl.BlockSpec(memory_space=pl.ANY),
            ],
            out_specs=pl.BlockSpec((tm, D_MODEL), lambda i, s: (i, 0)),
            scratch_shapes=[pltpu.VMEM((TOP_K, tm, D_MODEL), f32), pltpu.SemaphoreType.DMA],
        ),
        out_shape=jax.ShapeDtypeStruct((t, D_MODEL), f32),
        compiler_params=_params("arbitrary"),
        name="moe_combine",
    )(slot_flat, h, gate_t, norm_final, ys)


def kernel(x_prompt, x_sample, state_conv, state_ssm, norm_mix, w_in, conv_w, conv_b, dt_bias, a_log, d_skip, gnorm_w, v_norm_w, v_norm_b, w_spatial, b_spatial, w_branch_ssd, w_branch_mlp, w_out, norm_ffn, w_router, b_router, w_gu, b_gu, w_down, b_down, norm_final):
    assert w_in.shape[0] == 1, "single-layer trunk"
    batch, seq, _ = x_prompt.shape
    nseq, dec_seq, _ = x_sample.shape
    assert seq % CHUNK == 0 and CHUNK % dec_seq == 0 and nseq % (CHUNK // dec_seq) == 0 and dec_seq >= CONV_W - 1
    t_p, t_s = batch * seq, nseq * dec_seq
    tm = 256
    assert t_p % tm == 0 and t_s % tm == 0

    wi = w_in[0]
    z0, x0, d0, u0 = D_INNER, D_INNER + CONV_DIM, D_INNER + CONV_DIM + HEADS, D_INNER + CONV_DIM + HEADS + 2 * D_GMLP
    w_main = jnp.concatenate([wi[:, :z0], wi[:, d0:u0], wi[:, u0:], wi[:, z0:x0]], axis=1).astype(bf16)
    w_dt = wi[:, x0:d0].astype(bf16)
    w_dtT = w_dt.T
    nm = norm_mix[0].reshape(1, D_MODEL)
    sc = lambda L: _ssd_consts(conv_w[0], conv_b[0], dt_bias[0], a_log[0], d_skip[0], gnorm_w[0], L)
    xp2, xs2 = x_prompt.reshape(t_p, D_MODEL), x_sample.reshape(t_s, D_MODEL)

    tril = np.tril(np.ones((CHUNK, CHUNK), bool))
    ws_p = jnp.where(tril[None], w_spatial[0], 0).astype(bf16)
    per = CHUNK // dec_seq
    blockdiag = (np.arange(CHUNK)[:, None] // dec_seq) == (np.arange(CHUNK)[None, :] // dec_seq)
    ws_s = jnp.where((tril & blockdiag)[None], jnp.tile(w_spatial[0][:, :dec_seq, :dec_seq], (1, per, per)), 0).astype(bf16)
    bs_p = b_spatial[0].T
    bs_s = jnp.tile(b_spatial[0][:, :dec_seq], (1, per)).T
    mc = dict(
        vnw=v_norm_w[0].reshape(1, D_GMLP), vnb=v_norm_b[0].reshape(1, D_GMLP),
        wbs=w_branch_ssd[0].astype(bf16), wbm=w_branch_mlp[0].astype(bf16), wo=w_out[0].astype(bf16),
        nffn=norm_ffn[0].reshape(1, D_MODEL), wrT=w_router[0].T, br=b_router[0].reshape(N_EXPERTS, 1),
        tris=jnp.asarray(np.triu(np.ones((tm, tm), np.float32), 1), bf16),
    )

    proj_p, dt_p, dtT_p = _in_proj(xp2, nm, w_main, w_dt, w_dtT)
    yssd_p, ssm_p = _ssd_prompt(proj_p, dt_p, dtT_p, sc(CHUNK), batch, seq)
    h_p, hf_p, _, te_p, gate_p, rank_p, cnt_p = _mix(proj_p, yssd_p, xp2, dict(mc, ws=ws_p, bs=bs_p), tm)

    proj_s, dt_s, dtT_s = _in_proj(xs2, nm, w_main, w_dt, w_dtT)
    pre = jnp.pad(state_conv[0], ((0, 0), (dec_seq - (CONV_W - 1), 0), (0, 0))).reshape(t_s, CONV_DIM)
    yssd_s, ssm_s = _ssd_sample(proj_s, pre, dt_s, dtT_s, state_ssm[0].reshape(nseq, D_INNER, D_STATE),
                                sc(dec_seq), nseq, dec_seq)
    h_s, hf_s, v_s, te_s, gate_s, rank_s, cnt_s = _mix(proj_s, yssd_s, xs2, dict(mc, ws=ws_s, bs=bs_s), tm)

    cp, cs = cnt_p[:, 0], cnt_s[:, 0]
    padded = (cp + cs + MOE_BLOCK - 1) // MOE_BLOCK * MOE_BLOCK
    pend = jnp.cumsum(padded)
    pstart = pend - padded

    def lookup(table, te):
        eid = jnp.arange(N_EXPERTS, dtype=i32)[:, None, None]
        return jnp.sum(jnp.where(te[None] == eid, table[:, None, None], 0), axis=0)

    slot_p = (lookup(pstart, te_p) + rank_p).T.reshape(-1).astype(i32)
    slot_s = (lookup(pstart + cp, te_s) + rank_s).T.reshape(-1).astype(i32)
    slot_all = jnp.concatenate([slot_p, slot_s])
    n_blocks = (t_p + t_s) * TOP_K // MOE_BLOCK + N_EXPERTS
    n_used = (pend[-1] // MOE_BLOCK).astype(i32).reshape(1)
    first_row = jnp.arange(n_blocks, dtype=i32) * MOE_BLOCK
    block_e = jnp.minimum(jnp.sum(pend[None, :] <= first_row[:, None], axis=1), N_EXPERTS - 1).astype(i32)

    xs_sorted = _dispatch(slot_all, pend.astype(i32), padded.astype(i32), hf_p, hf_s, n_blocks * MOE_BLOCK, tm)
    bgu = jnp.concatenate([b_gu[0][..., 0::2], b_gu[0][..., 1::2]], axis=-1).reshape(N_EXPERTS, 1, 2 * D_FF)
    ys = _experts(block_e, n_used, xs_sorted, w_gu[0], bgu, w_down[0], b_down[0].reshape(N_EXPERTS, 1, D_MODEL))

    nf = norm_final.reshape(1, D_MODEL)
    y_p = _combine(slot_p, h_p, gate_p.T, nf, ys, tm)
    y_s = _combine(slot_s, h_s, gate_s.T, nf, ys, tm)

    conv_p = proj_p.reshape(batch, seq, D_MAIN)[:, seq - (CONV_W - 1):, 3 * D_INNER:]
    conv_s = proj_s.reshape(nseq, dec_seq, D_MAIN)[:, dec_seq - (CONV_W - 1):, 3 * D_INNER:]
    st_shape = (HEADS, HEAD_DIM, D_STATE)
    return (
        y_p.reshape(batch, seq, D_MODEL),
        y_s.reshape(nseq, dec_seq, D_MODEL),
        conv_p[None],
        ssm_p.reshape(1, batch, *st_shape),
        conv_s[None],
        ssm_s.reshape(1, nseq, *st_shape),
        v_s.reshape(1, nseq, dec_seq, D_GMLP),
    )
```

```python
import functools

import numpy as np
import jax
import jax.numpy as jnp
from jax import lax
from jax.experimental import pallas as pl
from jax.experimental.pallas import tpu as pltpu

f32, bf16, i32, u32 = jnp.float32, jnp.bfloat16, jnp.int32, jnp.uint32

D_MODEL = 1024
D_INNER = 2048
HEAD_DIM = 64
HEADS = 32
GROUPS = 4
HEADS_PER_GROUP = 8
GROUP_WIDTH = HEADS_PER_GROUP * HEAD_DIM
D_STATE = 128
CONV_W = 4
CONV_DIM = D_INNER + 2 * GROUPS * D_STATE
CHUNK = 128
SUBLANES = 8
D_GMLP = 1024
GMLP_HEADS = 8
N_EXPERTS = 32
TOP_K = 4
D_FF = 1024
SWIGLU_ALPHA = 1.702
SWIGLU_LIMIT = 7.0
EPS = 1e-5
MOE_BLOCK = 256
DEINT = 256
ROWS_PER_ITER = 8
D_MAIN = 3 * D_INNER + CONV_DIM
PROJ_TN = 512
D_PACK = D_MODEL // 2
VMEM_LIMIT = 56 * 1024 * 1024

_NT = (((1,), (1,)), ((), ()))
_HI16 = np.uint32(0xFFFF0000)


def _params(*sem):
    return pltpu.CompilerParams(dimension_semantics=sem, vmem_limit_bytes=VMEM_LIMIT)


def _split(x, parts):
    out = []
    for _ in range(parts - 1):
        p = x.astype(bf16)
        out.append(p)
        x = x - p.astype(f32)
    out.append(x.astype(bf16))
    return out


def _dot_sel_rhs(x, sel, parts=3):
    return sum(jnp.dot(p, sel, preferred_element_type=f32) for p in _split(x, parts))


def _dot_sel_lhs(sel, x, parts=3):
    return sum(jnp.dot(sel, p, preferred_element_type=f32) for p in _split(x, parts))


def _silu(x):
    return x * jax.nn.sigmoid(x)


def _pack_bf16_pair(x):
    n = x.shape[1] // 2
    lo = lax.bitcast_convert_type(x[:, :n].astype(bf16).astype(f32), u32) >> 16
    hi = lax.bitcast_convert_type(x[:, n:].astype(bf16).astype(f32), u32) & _HI16
    return lo | hi


def _unpack_bf16_pair(p):
    lo = lax.bitcast_convert_type(p << 16, f32).astype(bf16)
    hi = lax.bitcast_convert_type(p & _HI16, f32).astype(bf16)
    return jnp.concatenate([lo, hi], axis=1)


_PROJ_SEGMENTS = ((0, D_INNER), (D_INNER, D_INNER), (2 * D_INNER, D_INNER), (3 * D_INNER, CONV_DIM))


def _in_proj_body(x_ref, nw_ref, w_ref, wdt_ref, wdtT_ref, z_ref, uv_ref, g_ref, xbc_ref, dt_ref, dtT_ref):
    x = x_ref[...]
    ms = jnp.mean(x * x, axis=-1, keepdims=True)
    hn = (x * lax.rsqrt(ms + EPS) * nw_ref[...]).astype(bf16)
    dt_ref[...] = jnp.dot(hn, wdt_ref[...], preferred_element_type=f32)
    dtT_ref[...] = lax.dot_general(wdtT_ref[...], hn, _NT, preferred_element_type=f32)
    for o_ref, (col0, width) in zip((z_ref, uv_ref, g_ref, xbc_ref), _PROJ_SEGMENTS):
        for c in range(0, width, PROJ_TN):
            acc = jnp.dot(hn, w_ref[:, col0 + c:col0 + c + PROJ_TN], preferred_element_type=f32)
            o_ref[:, c:c + PROJ_TN] = acc.astype(o_ref.dtype)


def _in_proj(x2d, norm_w, w_main, w_dt, w_dtT):
    t = x2d.shape[0]
    tm = min(512, t)
    row = lambda i: (i, 0)
    fixed = lambda i: (0, 0)
    return pl.pallas_call(
        _in_proj_body,
        grid=(t // tm,),
        in_specs=[
            pl.BlockSpec((tm, D_MODEL), row),
            pl.BlockSpec((1, D_MODEL), fixed),
            pl.BlockSpec((D_MODEL, D_MAIN), fixed, pipeline_mode=pl.Buffered(1)),
            pl.BlockSpec((D_MODEL, HEADS), fixed),
            pl.BlockSpec((HEADS, D_MODEL), fixed),
        ],
        out_specs=[
            pl.BlockSpec((tm, D_INNER), row),
            pl.BlockSpec((tm, D_INNER), row),
            pl.BlockSpec((tm, D_INNER), row),
            pl.BlockSpec((tm, CONV_DIM), row),
            pl.BlockSpec((tm, HEADS), row),
            pl.BlockSpec((HEADS, tm), lambda i: (0, i)),
        ],
        out_shape=[
            jax.ShapeDtypeStruct((t, D_INNER), bf16),
            jax.ShapeDtypeStruct((t, D_INNER), bf16),
            jax.ShapeDtypeStruct((t, D_INNER), bf16),
            jax.ShapeDtypeStruct((t, CONV_DIM), f32),
            jax.ShapeDtypeStruct((t, HEADS), f32),
            jax.ShapeDtypeStruct((HEADS, t), f32),
        ],
        compiler_params=_params("arbitrary"),
        name="in_proj",
    )(x2d, norm_w, w_main, w_dt, w_dtT)


def _conv_taps(shifted, cw_ref, cb_ref):
    acc = cb_ref[...]
    for k in range(CONV_W):
        acc = acc + shifted(CONV_W - 1 - k) * cw_ref[k:k + 1, :]
    return _silu(acc)


def _conv_silu_one_seq(cur, prev8, cw_ref, cb_ref):
    row8 = lax.broadcasted_iota(i32, (SUBLANES, 1), 0)

    def shifted(j):
        if j == 0:
            return cur
        xr = pltpu.roll(cur, j, 0)
        top = jnp.where(row8 >= j, xr[:SUBLANES], pltpu.roll(prev8, j, 0))
        return jnp.concatenate([top, xr[SUBLANES:]], axis=0)

    return _conv_taps(shifted, cw_ref, cb_ref)


def _conv_silu_many_seq(cur, pre, cw_ref, cb_ref, seq_len):
    n = cur.shape[0]
    pos = lax.broadcasted_iota(i32, (n, 1), 0) % seq_len

    def shifted(j):
        if j == 0:
            return cur
        return jnp.where(pos >= j, pltpu.roll(cur, j, 0), pltpu.roll(pre, (j + n - seq_len) % n, 0))

    return _conv_taps(shifted, cw_ref, cb_ref)


def _ssd_block(xa, dt_raw, dtT_raw, c):
    tril = c["tril"][...]
    dt = jax.nn.softplus(dt_raw + c["dtb_row"][...])
    dtT = jax.nn.softplus(dtT_raw + c["dtb_col"][...])
    a = dt * c["aneg_row"][...]
    aT = dtT * c["aneg_col"][...]
    a_cum = _dot_sel_lhs(tril, a)
    a_cumT = _dot_sel_rhs(aT, c["triu"][...])
    a_tot = _dot_sel_lhs(c["same"][...], a)
    stack = jnp.concatenate([dt, jnp.exp(a_cum), jnp.exp(a_tot - a_cum)], axis=0)
    ex = _dot_sel_rhs(stack, c["expand"][...], parts=2)
    n = xa.shape[0]
    dtx, eax, dex = ex[:n], ex[n:2 * n], ex[2 * n:]

    xs = xa[:, :D_INNER]
    bm = xa[:, D_INNER:D_INNER + GROUPS * D_STATE]
    cm = xa[:, D_INNER + GROUPS * D_STATE:]
    xdt = xs * dtx
    xdt_bf = xdt.astype(bf16)
    mask = tril > 0
    lane = lax.broadcasted_iota(i32, (1, 2 * HEAD_DIM), 1)
    ys = []
    for g in range(GROUPS):
        cg = cm[:, g * D_STATE:(g + 1) * D_STATE].astype(bf16)
        bg = bm[:, g * D_STATE:(g + 1) * D_STATE].astype(bf16)
        cb = lax.dot_general(cg, bg, _NT, preferred_element_type=f32)
        for pair in range(HEADS_PER_GROUP // 2):
            halves = []
            for h in (g * HEADS_PER_GROUP + 2 * pair, g * HEADS_PER_GROUP + 2 * pair + 1):
                seg = a_cum[:, h:h + 1] - a_cumT[h:h + 1, :]
                decay = jnp.exp(jnp.where(mask, seg, -jnp.inf))
                m = (cb * decay).astype(bf16)
                col = (h // 2) * 2 * HEAD_DIM
                halves.append(jnp.dot(m, xdt_bf[:, col:col + 2 * HEAD_DIM], preferred_element_type=f32))
            ys.append(jnp.where(lane < HEAD_DIM, halves[0], halves[1]))
    y_diag = jnp.concatenate(ys, axis=1)
    return dict(xs=xs, bm=bm, cm=cm, xdt=xdt, xd=xdt * dex, eax=eax, aT=aT, y_diag=y_diag)


def _gated_norm(y, z, gw):
    g = y * _silu(z.astype(f32))
    ms = jnp.mean(g * g, axis=-1, keepdims=True)
    return (g * lax.rsqrt(ms + EPS) * gw).astype(bf16)


_SSD_CONST_NAMES = ("cw", "cb", "dtb_row", "dtb_col", "aneg_row", "aneg_col", "dskipx", "gw",
                    "tril", "triu", "same", "expand")


def _ssd_consts(conv_w, conv_b, dt_bias, a_log, d_skip, gnorm_w, seq_len):
    r = np.arange(CHUNK)
    same = (r[:, None] // seq_len) == (r[None, :] // seq_len)
    tril = same & (r[None, :] <= r[:, None])
    expand = np.repeat(np.eye(HEADS, dtype=np.float32), HEAD_DIM, axis=1)
    aneg = -jnp.exp(a_log.astype(f32))
    return dict(
        cw=conv_w, cb=conv_b.reshape(1, CONV_DIM),
        dtb_row=dt_bias.reshape(1, HEADS), dtb_col=dt_bias.reshape(HEADS, 1),
        aneg_row=aneg.reshape(1, HEADS), aneg_col=aneg.reshape(HEADS, 1),
        dskipx=jnp.repeat(d_skip.astype(f32), HEAD_DIM).reshape(1, D_INNER),
        gw=gnorm_w.reshape(1, D_INNER),
        tril=jnp.asarray(tril, bf16), triu=jnp.asarray(tril.T, bf16), same=jnp.asarray(same, bf16),
        expand=jnp.asarray(expand, bf16),
    )


def _const_specs(consts):
    zero = (lambda *_: (0, 0))
    return [pl.BlockSpec(consts[k].shape, zero) for k in _SSD_CONST_NAMES]


def _ssd_prompt_body(z_ref, xbc_ref, dt_ref, dtT_ref, *rest):
    nc = len(_SSD_CONST_NAMES)
    c = dict(zip(_SSD_CONST_NAMES, rest[:nc]))
    y_ref, state_ref, prev_ref, st_ref = rest[nc:]
    ci = pl.program_id(1)

    @pl.when(ci == 0)
    def _():
        prev_ref[...] = jnp.zeros_like(prev_ref)
        st_ref[...] = jnp.zeros_like(st_ref)

    cur = xbc_ref[...]
    xa = _conv_silu_one_seq(cur, prev_ref[...], c["cw"], c["cb"])
    prev_ref[...] = cur[CHUNK - SUBLANES:]
    b = _ssd_block(xa, dt_ref[...], dtT_ref[...], c)

    y_off = []
    for g in range(GROUPS):
        sl = slice(g * GROUP_WIDTH, (g + 1) * GROUP_WIDTH)
        st = st_ref[g]
        cg = b["cm"][:, g * D_STATE:(g + 1) * D_STATE].astype(bf16)
        y_off.append(jnp.dot(cg, st.astype(bf16), preferred_element_type=f32))
        bgT = b["bm"][:, g * D_STATE:(g + 1) * D_STATE].T.astype(bf16)
        upd = jnp.dot(bgT, b["xd"][:, sl].astype(bf16), preferred_element_type=f32)
        st_ref[g] = st * b["eax"][CHUNK - 1:CHUNK, sl] + upd
    y = b["y_diag"] + jnp.concatenate(y_off, axis=1) * b["eax"] + c["dskipx"][...] * b["xs"]
    y_ref[...] = _gated_norm(y, z_ref[...], c["gw"][...])

    @pl.when(ci == pl.num_programs(1) - 1)
    def _():
        for g in range(GROUPS):
            state_ref[0, g * GROUP_WIDTH:(g + 1) * GROUP_WIDTH, :] = st_ref[g].T


def _ssd_prompt(z, xbc, dt, dtT, consts, batch, seq):
    nchunk = seq // CHUNK
    t = batch * seq
    row = lambda b, ci: (b * nchunk + ci, 0)
    return pl.pallas_call(
        _ssd_prompt_body,
        grid=(batch, nchunk),
        in_specs=[
            pl.BlockSpec((CHUNK, D_INNER), row),
            pl.BlockSpec((CHUNK, CONV_DIM), row),
            pl.BlockSpec((CHUNK, HEADS), row),
            pl.BlockSpec((HEADS, CHUNK), lambda b, ci: (0, b * nchunk + ci)),
        ] + _const_specs(consts),
        out_specs=[
            pl.BlockSpec((CHUNK, D_INNER), row),
            pl.BlockSpec((1, D_INNER, D_STATE), lambda b, ci: (b, 0, 0)),
        ],
        out_shape=[
            jax.ShapeDtypeStruct((t, D_INNER), bf16),
            jax.ShapeDtypeStruct((batch, D_INNER, D_STATE), f32),
        ],
        scratch_shapes=[pltpu.VMEM((SUBLANES, CONV_DIM), f32), pltpu.VMEM((GROUPS, D_STATE, GROUP_WIDTH), f32)],
        compiler_params=_params("arbitrary", "arbitrary"),
        name="ssd_prompt",
    )(z, xbc, dt, dtT, *[consts[k] for k in _SSD_CONST_NAMES])


def _ssd_sample_body(seq_len, z_ref, xbc_ref, pre_ref, dt_ref, dtT_ref, state_ref, *rest):
    nc = len(_SSD_CONST_NAMES)
    c = dict(zip(_SSD_CONST_NAMES, rest[:nc]))
    y_ref, state_out_ref, c_ref, b_ref, xdT_ref, yacc_ref, eax_ref, eatT_ref = rest[nc:]
    s = pl.program_id(1)

    @pl.when(s == 0)
    def _():
        xa = _conv_silu_many_seq(xbc_ref[...], pre_ref[...], c["cw"], c["cb"], seq_len)
        b = _ssd_block(xa, dt_ref[...], dtT_ref[...], c)
        c_ref[...] = b["cm"]
        b_ref[...] = b["bm"]
        xdT_ref[...] = b["xd"].T.astype(bf16)
        yacc_ref[...] = b["y_diag"] + c["dskipx"][...] * b["xs"]
        eax_ref[...] = b["eax"]
        eatT_ref[...] = jnp.exp(_dot_sel_rhs(b["aT"], c["same"][...]))

    r0 = pl.multiple_of(s * seq_len, seq_len)
    rows = pl.ds(r0, seq_len)
    lane = lax.broadcasted_iota(i32, (1, CHUNK), 1)
    arep = jnp.broadcast_to(jnp.sum(jnp.where(lane == r0, eatT_ref[...], 0.0), axis=1, keepdims=True), (HEADS, D_STATE))
    rmask = (lax.broadcasted_iota(i32, (CHUNK, 1), 0) // seq_len) == s
    for g in range(GROUPS):
        gs = slice(g * GROUP_WIDTH, (g + 1) * GROUP_WIDTH)
        ns = slice(g * D_STATE, (g + 1) * D_STATE)
        s0 = state_ref[0, gs, :]
        cg = c_ref[rows, ns].astype(bf16)
        yo = lax.dot_general(cg, s0.astype(bf16), _NT, preferred_element_type=f32)
        yacc_ref[rows, gs] = yacc_ref[rows, gs] + yo * eax_ref[rows, gs]
        bmask = jnp.where(rmask, b_ref[:, ns], 0.0).astype(bf16)
        upd = jnp.dot(xdT_ref[gs, :], bmask, preferred_element_type=f32)
        for r in range(HEADS_PER_GROUP):
            h = g * HEADS_PER_GROUP + r
            hs = slice(r * HEAD_DIM, (r + 1) * HEAD_DIM)
            state_out_ref[0, h * HEAD_DIM:(h + 1) * HEAD_DIM, :] = s0[hs] * arep[h:h + 1, :] + upd[hs]

    @pl.when(s == pl.num_programs(1) - 1)
    def _():
        y_ref[...] = _gated_norm(yacc_ref[...], z_ref[...], c["gw"][...])


def _ssd_sample(z, xbc, pre, dt, dtT, state, consts, nseq, seq_len):
    per = CHUNK // seq_len
    nblk = nseq // per
    t = nseq * seq_len
    blk = lambda i, s: (i, 0)
    seq3 = lambda i, s: (i * per + s, 0, 0)
    return pl.pallas_call(
        functools.partial(_ssd_sample_body, seq_len),
        grid=(nblk, per),
        in_specs=[
            pl.BlockSpec((CHUNK, D_INNER), blk),
            pl.BlockSpec((CHUNK, CONV_DIM), blk),
            pl.BlockSpec((CHUNK, CONV_DIM), blk),
            pl.BlockSpec((CHUNK, HEADS), blk),
            pl.BlockSpec((HEADS, CHUNK), lambda i, s: (0, i)),
            pl.BlockSpec((1, D_INNER, D_STATE), seq3),
        ] + _const_specs(consts),
        out_specs=[
            pl.BlockSpec((CHUNK, D_INNER), blk),
            pl.BlockSpec((1, D_INNER, D_STATE), seq3),
        ],
        out_shape=[
            jax.ShapeDtypeStruct((t, D_INNER), bf16),
            jax.ShapeDtypeStruct((nseq, D_INNER, D_STATE), f32),
        ],
        scratch_shapes=[
            pltpu.VMEM((CHUNK, GROUPS * D_STATE), f32),
            pltpu.VMEM((CHUNK, GROUPS * D_STATE), f32),
            pltpu.VMEM((D_INNER, CHUNK), bf16),
            pltpu.VMEM((CHUNK, D_INNER), f32),
            pltpu.VMEM((CHUNK, D_INNER), f32),
            pltpu.VMEM((HEADS, CHUNK), f32),
        ],
        compiler_params=_params("arbitrary", "arbitrary"),
        name="ssd_sample",
    )(z, xbc, pre, dt, dtT, state, *[consts[k] for k in _SSD_CONST_NAMES])


_MIX_CONST_NAMES = ("ws", "bs", "vnw", "vnb", "wbs", "wbm", "wo", "nffn", "wrT", "br", "tris")


def _mix_body(emit_v, uv_ref, gates_ref, yssd_ref, x_ref, *rest):
    nc = len(_MIX_CONST_NAMES)
    c = dict(zip(_MIX_CONST_NAMES, rest[:nc]))
    outs = list(rest[nc:])
    h_ref, hfp_ref = outs[:2]
    v_ref = outs[2] if emit_v else None
    te_ref, gate_ref, rank_ref, cnt_ref, cnt_acc = outs[-5:]
    tm = uv_ref.shape[0]

    @pl.when(pl.program_id(0) == 0)
    def _():
        cnt_acc[...] = jnp.zeros_like(cnt_acc)

    uv = uv_ref[...].astype(f32)
    uv = 0.5 * uv * (1.0 + lax.erf(uv * np.float32(np.sqrt(0.5))))
    u, v = uv[:, :D_GMLP], uv[:, D_GMLP:]
    mu = jnp.mean(v, axis=-1, keepdims=True)
    vc = v - mu
    vn = vc * lax.rsqrt(jnp.mean(vc * vc, axis=-1, keepdims=True) + EPS) * c["vnw"][...] + c["vnb"][...]
    if emit_v:
        v_ref[...] = vn
    vn_bf = vn.astype(bf16)
    gd = D_GMLP // GMLP_HEADS
    rows = []
    for ck in range(tm // CHUNK):
        rs = slice(ck * CHUNK, (ck + 1) * CHUNK)
        heads = []
        for g in range(GMLP_HEADS):
            mixed = jnp.dot(c["ws"][g], vn_bf[rs, g * gd:(g + 1) * gd], preferred_element_type=f32)
            heads.append(mixed + c["bs"][:, g:g + 1])
        rows.append(jnp.concatenate(heads, axis=1))
    y_mlp = u * jnp.concatenate(rows, axis=0)

    a = jnp.dot(yssd_ref[...], c["wbs"][...], preferred_element_type=f32)
    b = jnp.dot(y_mlp.astype(bf16), c["wbm"][...], preferred_element_type=f32)
    gs = jax.nn.sigmoid(gates_ref[...].astype(f32))
    merged = gs[:, :D_MODEL] * a + gs[:, D_MODEL:] * b
    h = x_ref[...] + jnp.dot(merged.astype(bf16), c["wo"][...], preferred_element_type=f32)
    h_ref[...] = h
    hf = h * lax.rsqrt(jnp.mean(h * h, axis=-1, keepdims=True) + EPS) * c["nffn"][...]
    hfp_ref[...] = _pack_bf16_pair(hf)

    lg = lax.dot_general(c["wrT"][...], hf, _NT, precision=lax.Precision.HIGHEST, preferred_element_type=f32) + c["br"][...]
    sub = lax.broadcasted_iota(i32, lg.shape, 0)
    idxs, vals = [], []
    for _ in range(TOP_K):
        m = jnp.max(lg, axis=0, keepdims=True)
        idx = jnp.min(jnp.where(lg == m, sub, N_EXPERTS), axis=0, keepdims=True)
        idxs.append(idx)
        vals.append(m)
        lg = jnp.where(sub == idx, -jnp.inf, lg)
    p = jnp.exp(jnp.concatenate(vals, axis=0) - vals[0])
    gate_ref[...] = p / jnp.sum(p, axis=0, keepdims=True)
    te_ref[...] = jnp.concatenate(idxs, axis=0)

    onehots = [sub == idx for idx in idxs]
    member = functools.reduce(jnp.logical_or, onehots).astype(f32)
    before = jnp.dot(member.astype(bf16), c["tris"][...], preferred_element_type=f32) + cnt_acc[:, 0:1]
    rank_ref[...] = jnp.concatenate(
        [jnp.sum(jnp.where(oh, before, 0.0), axis=0, keepdims=True) for oh in onehots], axis=0).astype(i32)
    cnt_acc[...] = cnt_acc[...] + jnp.sum(member, axis=1, keepdims=True)
    cnt_ref[...] = cnt_acc[...].astype(i32)


def _mix(uv, gates, yssd, x2d, mc, tm, emit_v):
    t = x2d.shape[0]
    row = lambda i: (i, 0)
    col = lambda i: (0, i)
    const = lambda a: pl.BlockSpec(a.shape, lambda i: (0,) * a.ndim)
    v_spec = [pl.BlockSpec((tm, D_GMLP), row)] if emit_v else []
    v_shape = [jax.ShapeDtypeStruct((t, D_GMLP), f32)] if emit_v else []
    return pl.pallas_call(
        functools.partial(_mix_body, emit_v),
        grid=(t // tm,),
        in_specs=[
            pl.BlockSpec((tm, D_INNER), row),
            pl.BlockSpec((tm, D_INNER), row),
            pl.BlockSpec((tm, D_INNER), row),
            pl.BlockSpec((tm, D_MODEL), row),
        ] + [const(mc[k]) for k in _MIX_CONST_NAMES],
        out_specs=[pl.BlockSpec((tm, D_MODEL), row), pl.BlockSpec((tm, D_PACK), row)] + v_spec + [
            pl.BlockSpec((TOP_K, tm), col),
            pl.BlockSpec((TOP_K, tm), col),
            pl.BlockSpec((TOP_K, tm), col),
            pl.BlockSpec((N_EXPERTS, 128), lambda i: (0, 0)),
        ],
        out_shape=[
            jax.ShapeDtypeStruct((t, D_MODEL), f32),
            jax.ShapeDtypeStruct((t, D_PACK), u32),
        ] + v_shape + [
            jax.ShapeDtypeStruct((TOP_K, t), i32),
            jax.ShapeDtypeStruct((TOP_K, t), f32),
            jax.ShapeDtypeStruct((TOP_K, t), i32),
            jax.ShapeDtypeStruct((N_EXPERTS, 128), i32),
        ],
        scratch_shapes=[pltpu.VMEM((N_EXPERTS, 128), f32)],
        compiler_params=_params("arbitrary"),
        name="mix_route",
    )(uv, gates, yssd, x2d, *[mc[k] for k in _MIX_CONST_NAMES])


def _dispatch_body(tm, n_p_tiles, slot_ref, pend_ref, padded_ref, hfp_ref, hfs_ref, xs_ref, zero_ref, sem):
    i = pl.program_id(0)

    def tail_copy(e):
        start = pl.multiple_of(pend_ref[e] - MOE_BLOCK, MOE_BLOCK)
        return pltpu.make_async_copy(zero_ref, xs_ref.at[pl.ds(start, MOE_BLOCK), :], sem)

    def unused_copy(j):
        return pltpu.make_async_copy(zero_ref, xs_ref.at[pl.ds(j * MOE_BLOCK, MOE_BLOCK), :], sem)

    @pl.when(i == 0)
    def _():
        zero_ref[...] = jnp.zeros_like(zero_ref)
        n_blocks = xs_ref.shape[0] // MOE_BLOCK
        for start_or_wait in ("start", "wait"):
            for e in range(N_EXPERTS):
                @pl.when(padded_ref[e] > 0)
                def _():
                    getattr(tail_copy(e), start_or_wait)()
            for j in range(n_blocks - N_EXPERTS, n_blocks):
                @pl.when(j * MOE_BLOCK >= pend_ref[N_EXPERTS - 1])
                def _():
                    getattr(unused_copy(j), start_or_wait)()

    def scatter(src_ref):
        def row_copy(r, k):
            slot = slot_ref[(i * tm + r) * TOP_K + k]
            return pltpu.make_async_copy(src_ref.at[r], xs_ref.at[slot], sem)

        def start(r8, carry):
            for r in range(ROWS_PER_ITER):
                for k in range(TOP_K):
                    row_copy(r8 * ROWS_PER_ITER + r, k).start(priority=k % 2)
            return carry

        def wait(r8, carry):
            for r in range(ROWS_PER_ITER):
                for k in range(TOP_K):
                    row_copy(r8 * ROWS_PER_ITER + r, k).wait()
            return carry

        lax.fori_loop(0, tm // ROWS_PER_ITER, start, 0)
        lax.fori_loop(0, tm // ROWS_PER_ITER, wait, 0)

    @pl.when(i < n_p_tiles)
    def _():
        scatter(hfp_ref)

    @pl.when(i >= n_p_tiles)
    def _():
        scatter(hfs_ref)


def _dispatch(slot_flat, pend, padded, hf_p, hf_s, n_slots, tm):
    n_p, n_s = hf_p.shape[0] // tm, hf_s.shape[0] // tm
    return pl.pallas_call(
        functools.partial(_dispatch_body, tm, n_p),
        grid_spec=pltpu.PrefetchScalarGridSpec(
            num_scalar_prefetch=3,
            grid=(n_p + n_s,),
            in_specs=[
                pl.BlockSpec((tm, D_PACK), lambda i, *_: (jnp.minimum(i, n_p - 1), 0)),
                pl.BlockSpec((tm, D_PACK), lambda i, *_: (jnp.maximum(i - n_p, 0), 0)),
            ],
            out_specs=pl.BlockSpec(memory_space=pl.ANY),
            scratch_shapes=[pltpu.VMEM((MOE_BLOCK, D_PACK), u32), pltpu.SemaphoreType.DMA],
        ),
        out_shape=jax.ShapeDtypeStruct((n_slots, D_PACK), u32),
        compiler_params=_params("arbitrary"),
        name="moe_dispatch",
    )(slot_flat, pend, padded, hf_p, hf_s)


def _experts_body(be_ref, nused_ref, xs_ref, wgu_ref, bgu_ref, wd_ref, bd_ref, perm_ref, ys_ref, wgu_bf, wd_bf):
    i = pl.program_id(0)

    @pl.when(i < nused_ref[0])
    def _():
        @pl.when((i == 0) | (be_ref[i] != be_ref[jnp.maximum(i - 1, 0)]))
        def _():
            wd_bf[...] = wd_ref[0].astype(bf16)
            half = DEINT // 2
            for j in range(2 * D_FF // DEINT):
                w = jnp.dot(wgu_ref[0, :, j * DEINT:(j + 1) * DEINT].astype(bf16), perm_ref[...],
                            preferred_element_type=f32).astype(bf16)
                wgu_bf[:, j * half:(j + 1) * half] = w[:, :half]
                wgu_bf[:, D_FF + j * half:D_FF + (j + 1) * half] = w[:, half:]

        gu = jnp.dot(_unpack_bf16_pair(xs_ref[...]), wgu_bf[...], preferred_element_type=f32) + bgu_ref[0]
        glu = jnp.minimum(gu[:, :D_FF], SWIGLU_LIMIT)
        lin = jnp.clip(gu[:, D_FF:], -SWIGLU_LIMIT, SWIGLU_LIMIT)
        act = glu * jax.nn.sigmoid(SWIGLU_ALPHA * glu) * (lin + 1.0)
        ys_ref[...] = jnp.dot(act.astype(bf16), wd_bf[...], preferred_element_type=f32) + bd_ref[0]

    @pl.when(i >= nused_ref[0])
    def _():
        ys_ref[...] = jnp.zeros_like(ys_ref)


def _experts(block_e, n_used, xs, wgu, bgu, w_down, b_down):
    n_blocks = xs.shape[0] // MOE_BLOCK
    blk = lambda i, be, nu: (jnp.maximum(jnp.minimum(i, nu[0] - 1), 0), 0)
    exp3 = lambda i, be, nu: (be[i], 0, 0)
    c = np.arange(DEINT)
    src = np.where(c < DEINT // 2, 2 * c, 2 * (c - DEINT // 2) + 1)
    perm = jnp.asarray(np.arange(DEINT)[:, None] == src[None, :], bf16)
    return pl.pallas_call(
        _experts_body,
        grid_spec=pltpu.PrefetchScalarGridSpec(
            num_scalar_prefetch=2,
            grid=(n_blocks,),
            in_specs=[
                pl.BlockSpec((MOE_BLOCK, D_PACK), blk),
                pl.BlockSpec((1, D_MODEL, 2 * D_FF), exp3),
                pl.BlockSpec((1, 1, 2 * D_FF), exp3),
                pl.BlockSpec((1, D_FF, D_MODEL), exp3),
                pl.BlockSpec((1, 1, D_MODEL), exp3),
                pl.BlockSpec((DEINT, DEINT), lambda i, be, nu: (0, 0)),
            ],
            out_specs=pl.BlockSpec((MOE_BLOCK, D_MODEL), lambda i, be, nu: (i, 0)),
            scratch_shapes=[pltpu.VMEM((D_MODEL, 2 * D_FF), bf16), pltpu.VMEM((D_FF, D_MODEL), bf16)],
        ),
        out_shape=jax.ShapeDtypeStruct((xs.shape[0], D_MODEL), f32),
        compiler_params=_params("arbitrary"),
        name="moe_experts",
    )(block_e, n_used, xs, wgu, bgu, w_down, b_down, perm)


def _combine_body(tm, slot_ref, h_ref, gate_ref, nf_ref, ys_ref, o_ref, rows_ref, sems):
    i = pl.program_id(0)
    cur = i % 2

    def row_copy(tile, buf, r, k):
        slot = slot_ref[(tile * tm + r) * TOP_K + k]
        return pltpu.make_async_copy(ys_ref.at[slot], rows_ref.at[buf, k, r], sems.at[buf])

    def for_rows(tile, buf, op):
        def body(r8, carry):
            for r in range(ROWS_PER_ITER):
                for k in range(TOP_K):
                    op(row_copy(tile, buf, r8 * ROWS_PER_ITER + r, k), k)
            return carry
        lax.fori_loop(0, tm // ROWS_PER_ITER, body, 0)

    start = lambda cp, k: cp.start(priority=k % 2)
    wait = lambda cp, k: cp.wait()

    @pl.when(i == 0)
    def _():
        for_rows(0, 0, start)

    @pl.when(i + 1 < pl.num_programs(0))
    def _():
        for_rows(i + 1, 1 - cur, start)

    for_rows(i, cur, wait)
    g = gate_ref[...]
    moe = g[:, 0:1] * rows_ref[cur, 0]
    for k in range(1, TOP_K):
        moe = moe + g[:, k:k + 1] * rows_ref[cur, k]
    h = h_ref[...] + moe
    o_ref[...] = h * lax.rsqrt(jnp.mean(h * h, axis=-1, keepdims=True) + EPS) * nf_ref[...]


def _combine(slot_flat, h, gate_t, norm_final, ys, tm):
    t = h.shape[0]
    return pl.pallas_call(
        functools.partial(_combine_body, tm),
        grid_spec=pltpu.PrefetchScalarGridSpec(
            num_scalar_prefetch=1,
            grid=(t // tm,),
            in_specs=[
                pl.BlockSpec((tm, D_MODEL), lambda i, s: (i, 0)),
                pl.BlockSpec((tm, TOP_K), lambda i, s: (i, 0)),
                pl.BlockSpec((1, D_MODEL), lambda i, s: (0, 0)),
                pl.BlockSpec(memory_space=pl.ANY),
            ],
            out_specs=pl.BlockSpec((tm, D_MODEL), lambda i, s: (i, 0)),
            scratch_shapes=[pltpu.VMEM((2, TOP_K, tm, D_MODEL), f32), pltpu.SemaphoreType.DMA((2,))],
        ),
        out_shape=jax.ShapeDtypeStruct((t, D_MODEL), f32),
        compiler_params=_params("arbitrary"),
        name="moe_combine",
    )(slot_flat, h, gate_t, norm_final, ys)


def kernel(x_prompt, x_sample, state_conv, state_ssm, norm_mix, w_in, conv_w, conv_b, dt_bias, a_log, d_skip, gnorm_w, v_norm_w, v_norm_b, w_spatial, b_spatial, w_branch_ssd, w_branch_mlp, w_out, norm_ffn, w_router, b_router, w_gu, b_gu, w_down, b_down, norm_final):
    assert w_in.shape[0] == 1, "single-layer trunk"
    batch, seq, _ = x_prompt.shape
    nseq, dec_seq, _ = x_sample.shape
    assert seq % CHUNK == 0 and CHUNK % dec_seq == 0 and nseq % (CHUNK // dec_seq) == 0 and dec_seq >= CONV_W - 1
    t_p, t_s = batch * seq, nseq * dec_seq
    tm = 256
    assert t_p % tm == 0 and t_s % tm == 0

    wi = w_in[0]
    z0, x0, d0, u0 = D_INNER, D_INNER + CONV_DIM, D_INNER + CONV_DIM + HEADS, D_INNER + CONV_DIM + HEADS + 2 * D_GMLP
    w_main = jnp.concatenate([wi[:, :z0], wi[:, d0:u0], wi[:, u0:], wi[:, z0:x0]], axis=1).astype(bf16)
    w_dt = wi[:, x0:d0].astype(bf16)
    w_dtT = w_dt.T
    nm = norm_mix[0].reshape(1, D_MODEL)
    sc = lambda L: _ssd_consts(conv_w[0], conv_b[0], dt_bias[0], a_log[0], d_skip[0], gnorm_w[0], L)
    xp2, xs2 = x_prompt.reshape(t_p, D_MODEL), x_sample.reshape(t_s, D_MODEL)

    tril = np.tril(np.ones((CHUNK, CHUNK), bool))
    ws_p = jnp.where(tril[None], w_spatial[0], 0).astype(bf16)
    per = CHUNK // dec_seq
    blockdiag = (np.arange(CHUNK)[:, None] // dec_seq) == (np.arange(CHUNK)[None, :] // dec_seq)
    ws_s = jnp.where((tril & blockdiag)[None], jnp.tile(w_spatial[0][:, :dec_seq, :dec_seq], (1, per, per)), 0).astype(bf16)
    bs_p = b_spatial[0].T
    bs_s = jnp.tile(b_spatial[0][:, :dec_seq], (1, per)).T
    mc = dict(
        vnw=v_norm_w[0].reshape(1, D_GMLP), vnb=v_norm_b[0].reshape(1, D_GMLP),
        wbs=w_branch_ssd[0].astype(bf16), wbm=w_branch_mlp[0].astype(bf16), wo=w_out[0].astype(bf16),
        nffn=norm_ffn[0].reshape(1, D_MODEL), wrT=w_router[0].T, br=b_router[0].reshape(N_EXPERTS, 1),
        tris=jnp.asarray(np.triu(np.ones((tm, tm), np.float32), 1), bf16),
    )

    z_p, uv_p, g_p, xbc_p, dt_p, dtT_p = _in_proj(xp2, nm, w_main, w_dt, w_dtT)
    yssd_p, ssm_p = _ssd_prompt(z_p, xbc_p, dt_p, dtT_p, sc(CHUNK), batch, seq)
    h_p, hf_p, te_p, gate_p, rank_p, cnt_p = _mix(uv_p, g_p, yssd_p, xp2, dict(mc, ws=ws_p, bs=bs_p), tm, False)

    z_s, uv_s, g_s, xbc_s, dt_s, dtT_s = _in_proj(xs2, nm, w_main, w_dt, w_dtT)
    pre = jnp.pad(state_conv[0], ((0, 0), (dec_seq - (CONV_W - 1), 0), (0, 0))).reshape(t_s, CONV_DIM)
    yssd_s, ssm_s = _ssd_sample(z_s, xbc_s, pre, dt_s, dtT_s, state_ssm[0].reshape(nseq, D_INNER, D_STATE),
                                sc(dec_seq), nseq, dec_seq)
    h_s, hf_s, v_s, te_s, gate_s, rank_s, cnt_s = _mix(uv_s, g_s, yssd_s, xs2, dict(mc, ws=ws_s, bs=bs_s), tm, True)

    cp, cs = cnt_p[:, 0], cnt_s[:, 0]
    padded = (cp + cs + MOE_BLOCK - 1) // MOE_BLOCK * MOE_BLOCK
    pend = jnp.cumsum(padded)
    pstart = pend - padded

    def lookup(table, te):
        eid = jnp.arange(N_EXPERTS, dtype=i32)[:, None, None]
        return jnp.sum(jnp.where(te[None] == eid, table[:, None, None], 0), axis=0)

    slot_p = (lookup(pstart, te_p) + rank_p).T.reshape(-1).astype(i32)
    slot_s = (lookup(pstart + cp, te_s) + rank_s).T.reshape(-1).astype(i32)
    slot_all = jnp.concatenate([slot_p, slot_s])
    n_blocks = (t_p + t_s) * TOP_K // MOE_BLOCK + N_EXPERTS
    n_used = (pend[-1] // MOE_BLOCK).astype(i32).reshape(1)
    first_row = jnp.arange(n_blocks, dtype=i32) * MOE_BLOCK
    block_e = jnp.minimum(jnp.sum(pend[None, :] <= first_row[:, None], axis=1), N_EXPERTS - 1).astype(i32)

    xs_sorted = _dispatch(slot_all, pend.astype(i32), padded.astype(i32), hf_p, hf_s, n_blocks * MOE_BLOCK, tm)
    bgu = jnp.concatenate([b_gu[0][..., 0::2], b_gu[0][..., 1::2]], axis=-1).reshape(N_EXPERTS, 1, 2 * D_FF)
    ys = _experts(block_e, n_used, xs_sorted, w_gu[0], bgu, w_down[0], b_down[0].reshape(N_EXPERTS, 1, D_MODEL))

    nf = norm_final.reshape(1, D_MODEL)
    y_p = _combine(slot_p, h_p, gate_p.T, nf, ys, tm)
    y_s = _combine(slot_s, h_s, gate_s.T, nf, ys, tm)

    conv_p = xbc_p.reshape(batch, seq, CONV_DIM)[:, seq - (CONV_W - 1):]
    conv_s = xbc_s.reshape(nseq, dec_seq, CONV_DIM)[:, dec_seq - (CONV_W - 1):]
    st_shape = (HEADS, HEAD_DIM, D_STATE)
    return (
        y_p.reshape(batch, seq, D_MODEL),
        y_s.reshape(nseq, dec_seq, D_MODEL),
        conv_p[None],
        ssm_p.reshape(1, batch, *st_shape),
        conv_s[None],
        ssm_s.reshape(1, nseq, *st_shape),
        v_s.reshape(1, nseq, dec_seq, D_GMLP),
    )
```

```python
import functools

import numpy as np
import jax
import jax.numpy as jnp
from jax import lax
from jax.experimental import pallas as pl
from jax.experimental.pallas import tpu as pltpu

f32, bf16, i32, u32 = jnp.float32, jnp.bfloat16, jnp.int32, jnp.uint32

D_MODEL = 1024
D_INNER = 2048
HEAD_DIM = 64
HEADS = 32
GROUPS = 4
HEADS_PER_GROUP = 8
GROUP_WIDTH = HEADS_PER_GROUP * HEAD_DIM
D_STATE = 128
CONV_W = 4
CONV_DIM = D_INNER + 2 * GROUPS * D_STATE
CHUNK = 128
SUBLANES = 8
D_GMLP = 1024
GMLP_HEADS = 8
N_EXPERTS = 32
TOP_K = 4
D_FF = 1024
SWIGLU_ALPHA = 1.702
SWIGLU_LIMIT = 7.0
EPS = 1e-5
MOE_BLOCK = 256
DEINT = 256
ROW_DMA_TILE = 256
MIX_TILE = 512
D_MAIN = 3 * D_INNER + CONV_DIM
PROJ_TN = 512
PROMPT_SEQS_PER_STEP = 2
D_PACK = D_MODEL // 2
VMEM_LIMIT = 56 * 1024 * 1024

_NT = (((1,), (1,)), ((), ()))
_HI16 = np.uint32(0xFFFF0000)


def _params(*sem):
    return pltpu.CompilerParams(dimension_semantics=sem, vmem_limit_bytes=VMEM_LIMIT)


def _split(x, parts):
    out = []
    for _ in range(parts - 1):
        p = x.astype(bf16)
        out.append(p)
        x = x - p.astype(f32)
    out.append(x.astype(bf16))
    return out


def _dot_sel_rhs(x, sel, parts=3):
    return sum(jnp.dot(p, sel, preferred_element_type=f32) for p in _split(x, parts))


def _dot_sel_lhs(sel, x, parts=3):
    return sum(jnp.dot(sel, p, preferred_element_type=f32) for p in _split(x, parts))


def _silu(x):
    return x * jax.nn.sigmoid(x)


def _pack_bf16_pair(x):
    n = x.shape[1] // 2
    lo = lax.bitcast_convert_type(x[:, :n].astype(bf16).astype(f32), u32) >> 16
    hi = lax.bitcast_convert_type(x[:, n:].astype(bf16).astype(f32), u32) & _HI16
    return lo | hi


def _unpack_bf16_pair(p):
    lo = lax.bitcast_convert_type(p << 16, f32).astype(bf16)
    hi = lax.bitcast_convert_type(p & _HI16, f32).astype(bf16)
    return jnp.concatenate([lo, hi], axis=1)


_PROJ_SEGMENTS = ((0, D_INNER), (D_INNER, D_INNER), (2 * D_INNER, D_INNER), (3 * D_INNER, CONV_DIM))


def _in_proj_body(x_ref, nw_ref, w_ref, wdt_ref, wdtT_ref, z_ref, uv_ref, g_ref, xbc_ref, dt_ref, dtT_ref):
    x = x_ref[...]
    ms = jnp.mean(x * x, axis=-1, keepdims=True)
    hn = (x * lax.rsqrt(ms + EPS) * nw_ref[...]).astype(bf16)
    dt_ref[...] = jnp.dot(hn, wdt_ref[...], preferred_element_type=f32)
    dtT_ref[...] = lax.dot_general(wdtT_ref[...], hn, _NT, preferred_element_type=f32)
    for o_ref, (col0, width) in zip((z_ref, uv_ref, g_ref, xbc_ref), _PROJ_SEGMENTS):
        for c in range(0, width, PROJ_TN):
            acc = jnp.dot(hn, w_ref[:, col0 + c:col0 + c + PROJ_TN], preferred_element_type=f32)
            o_ref[:, c:c + PROJ_TN] = acc.astype(o_ref.dtype)


def _in_proj(x2d, norm_w, w_main, w_dt, w_dtT):
    t = x2d.shape[0]
    tm = min(512, t)
    row = lambda i: (i, 0)
    fixed = lambda i: (0, 0)
    return pl.pallas_call(
        _in_proj_body,
        grid=(t // tm,),
        in_specs=[
            pl.BlockSpec((tm, D_MODEL), row),
            pl.BlockSpec((1, D_MODEL), fixed),
            pl.BlockSpec((D_MODEL, D_MAIN), fixed, pipeline_mode=pl.Buffered(1)),
            pl.BlockSpec((D_MODEL, HEADS), fixed),
            pl.BlockSpec((HEADS, D_MODEL), fixed),
        ],
        out_specs=[
            pl.BlockSpec((tm, D_INNER), row),
            pl.BlockSpec((tm, D_INNER), row),
            pl.BlockSpec((tm, D_INNER), row),
            pl.BlockSpec((tm, CONV_DIM), row),
            pl.BlockSpec((tm, HEADS), row),
            pl.BlockSpec((HEADS, tm), lambda i: (0, i)),
        ],
        out_shape=[
            jax.ShapeDtypeStruct((t, D_INNER), bf16),
            jax.ShapeDtypeStruct((t, D_INNER), bf16),
            jax.ShapeDtypeStruct((t, D_INNER), bf16),
            jax.ShapeDtypeStruct((t, CONV_DIM), f32),
            jax.ShapeDtypeStruct((t, HEADS), f32),
            jax.ShapeDtypeStruct((HEADS, t), f32),
        ],
        compiler_params=_params("arbitrary"),
        name="in_proj",
    )(x2d, norm_w, w_main, w_dt, w_dtT)


def _conv_taps(shifted, cw_ref, cb_ref):
    acc = cb_ref[...]
    for k in range(CONV_W):
        acc = acc + shifted(CONV_W - 1 - k) * cw_ref[k:k + 1, :]
    return _silu(acc)


def _conv_silu_one_seq(cur, prev8, cw_ref, cb_ref):
    row8 = lax.broadcasted_iota(i32, (SUBLANES, 1), 0)

    def shifted(j):
        if j == 0:
            return cur
        xr = pltpu.roll(cur, j, 0)
        top = jnp.where(row8 >= j, xr[:SUBLANES], pltpu.roll(prev8, j, 0))
        return jnp.concatenate([top, xr[SUBLANES:]], axis=0)

    return _conv_taps(shifted, cw_ref, cb_ref)


def _conv_silu_many_seq(cur, pre, cw_ref, cb_ref, seq_len):
    n = cur.shape[0]
    pos = lax.broadcasted_iota(i32, (n, 1), 0) % seq_len

    def shifted(j):
        if j == 0:
            return cur
        return jnp.where(pos >= j, pltpu.roll(cur, j, 0), pltpu.roll(pre, (j + n - seq_len) % n, 0))

    return _conv_taps(shifted, cw_ref, cb_ref)


def _ssd_block(xa, dt_raw, dtT_raw, c):
    tril = c["tril"][...]
    dt = jax.nn.softplus(dt_raw + c["dtb_row"][...])
    dtT = jax.nn.softplus(dtT_raw + c["dtb_col"][...])
    a = dt * c["aneg_row"][...]
    aT = dtT * c["aneg_col"][...]
    a_cum = _dot_sel_lhs(tril, a)
    a_cumT = _dot_sel_rhs(aT, c["triu"][...])
    a_tot = _dot_sel_lhs(c["same"][...], a)
    stack = jnp.concatenate([dt, jnp.exp(a_cum), jnp.exp(a_tot - a_cum)], axis=0)
    ex = _dot_sel_rhs(stack, c["expand"][...], parts=2)
    n = xa.shape[0]
    dtx, eax, dex = ex[:n], ex[n:2 * n], ex[2 * n:]

    xs = xa[:, :D_INNER]
    bm = xa[:, D_INNER:D_INNER + GROUPS * D_STATE]
    cm = xa[:, D_INNER + GROUPS * D_STATE:]
    xdt = xs * dtx
    xdt_bf = xdt.astype(bf16)
    mask = tril > 0
    lane = lax.broadcasted_iota(i32, (1, 2 * HEAD_DIM), 1)
    ys = []
    for g in range(GROUPS):
        cg = cm[:, g * D_STATE:(g + 1) * D_STATE].astype(bf16)
        bg = bm[:, g * D_STATE:(g + 1) * D_STATE].astype(bf16)
        cb = lax.dot_general(cg, bg, _NT, preferred_element_type=f32)
        for pair in range(HEADS_PER_GROUP // 2):
            halves = []
            for h in (g * HEADS_PER_GROUP + 2 * pair, g * HEADS_PER_GROUP + 2 * pair + 1):
                seg = a_cum[:, h:h + 1] - a_cumT[h:h + 1, :]
                decay = jnp.exp(jnp.where(mask, seg, -jnp.inf))
                m = (cb * decay).astype(bf16)
                col = (h // 2) * 2 * HEAD_DIM
                halves.append(jnp.dot(m, xdt_bf[:, col:col + 2 * HEAD_DIM], preferred_element_type=f32))
            ys.append(jnp.where(lane < HEAD_DIM, halves[0], halves[1]))
    y_diag = jnp.concatenate(ys, axis=1)
    return dict(xs=xs, bm=bm, cm=cm, xdt=xdt, xd=xdt * dex, eax=eax, aT=aT, y_diag=y_diag)


def _gated_norm(y, z, gw):
    g = y * _silu(z.astype(f32))
    ms = jnp.mean(g * g, axis=-1, keepdims=True)
    return (g * lax.rsqrt(ms + EPS) * gw).astype(bf16)


_SSD_CONST_NAMES = ("cw", "cb", "dtb_row", "dtb_col", "aneg_row", "aneg_col", "dskipx", "gw",
                    "tril", "triu", "same", "expand")


def _ssd_consts(conv_w, conv_b, dt_bias, a_log, d_skip, gnorm_w, seq_len):
    r = np.arange(CHUNK)
    same = (r[:, None] // seq_len) == (r[None, :] // seq_len)
    tril = same & (r[None, :] <= r[:, None])
    expand = np.repeat(np.eye(HEADS, dtype=np.float32), HEAD_DIM, axis=1)
    aneg = -jnp.exp(a_log.astype(f32))
    return dict(
        cw=conv_w, cb=conv_b.reshape(1, CONV_DIM),
        dtb_row=dt_bias.reshape(1, HEADS), dtb_col=dt_bias.reshape(HEADS, 1),
        aneg_row=aneg.reshape(1, HEADS), aneg_col=aneg.reshape(HEADS, 1),
        dskipx=jnp.repeat(d_skip.astype(f32), HEAD_DIM).reshape(1, D_INNER),
        gw=gnorm_w.reshape(1, D_INNER),
        tril=jnp.asarray(tril, bf16), triu=jnp.asarray(tril.T, bf16), same=jnp.asarray(same, bf16),
        expand=jnp.asarray(expand, bf16),
    )


def _const_specs(consts):
    zero = (lambda *_: (0, 0))
    return [pl.BlockSpec(consts[k].shape, zero) for k in _SSD_CONST_NAMES]


def _ssd_prompt_body(nb, z_ref, xbc_ref, dt_ref, dtT_ref, *rest):
    nc = len(_SSD_CONST_NAMES)
    c = dict(zip(_SSD_CONST_NAMES, rest[:nc]))
    y_ref, state_ref, prev_ref, st_ref = rest[nc:]
    ci = pl.program_id(1)

    @pl.when(ci == 0)
    def _():
        prev_ref[...] = jnp.zeros_like(prev_ref)
        st_ref[...] = jnp.zeros_like(st_ref)

    for s in range(nb):
        cur = xbc_ref[s]
        xa = _conv_silu_one_seq(cur, prev_ref[s], c["cw"], c["cb"])
        prev_ref[s] = cur[CHUNK - SUBLANES:]
        b = _ssd_block(xa, dt_ref[s], dtT_ref[s], c)

        y_off = []
        for g in range(GROUPS):
            sl = slice(g * GROUP_WIDTH, (g + 1) * GROUP_WIDTH)
            st = st_ref[s * GROUPS + g]
            cg = b["cm"][:, g * D_STATE:(g + 1) * D_STATE].astype(bf16)
            y_off.append(jnp.dot(cg, st.astype(bf16), preferred_element_type=f32))
            bgT = b["bm"][:, g * D_STATE:(g + 1) * D_STATE].T.astype(bf16)
            upd = jnp.dot(bgT, b["xd"][:, sl].astype(bf16), preferred_element_type=f32)
            st_ref[s * GROUPS + g] = st * b["eax"][CHUNK - 1:CHUNK, sl] + upd
        y = b["y_diag"] + jnp.concatenate(y_off, axis=1) * b["eax"] + c["dskipx"][...] * b["xs"]
        y_ref[s] = _gated_norm(y, z_ref[s], c["gw"][...])

    @pl.when(ci == pl.num_programs(1) - 1)
    def _():
        for s in range(nb):
            for g in range(GROUPS):
                state_ref[s, 0, g * GROUP_WIDTH:(g + 1) * GROUP_WIDTH, :] = st_ref[s * GROUPS + g].T


def _ssd_prompt(z, xbc, dt, dtT, consts, batch, seq):
    nb = PROMPT_SEQS_PER_STEP if batch % PROMPT_SEQS_PER_STEP == 0 else 1
    nchunk = seq // CHUNK
    part = batch // nb * seq
    split = lambda a: a.reshape(nb, part, a.shape[-1])
    dtT3 = dtT.reshape(HEADS, nb, part).transpose(1, 0, 2)
    row = lambda b, ci: (0, b * nchunk + ci, 0)
    y, state = pl.pallas_call(
        functools.partial(_ssd_prompt_body, nb),
        grid=(batch // nb, nchunk),
        in_specs=[
            pl.BlockSpec((nb, CHUNK, D_INNER), row),
            pl.BlockSpec((nb, CHUNK, CONV_DIM), row),
            pl.BlockSpec((nb, CHUNK, HEADS), row),
            pl.BlockSpec((nb, HEADS, CHUNK), lambda b, ci: (0, 0, b * nchunk + ci)),
        ] + _const_specs(consts),
        out_specs=[
            pl.BlockSpec((nb, CHUNK, D_INNER), row),
            pl.BlockSpec((nb, 1, D_INNER, D_STATE), lambda b, ci: (0, b, 0, 0)),
        ],
        out_shape=[
            jax.ShapeDtypeStruct((nb, part, D_INNER), bf16),
            jax.ShapeDtypeStruct((nb, batch // nb, D_INNER, D_STATE), f32),
        ],
        scratch_shapes=[pltpu.VMEM((nb, SUBLANES, CONV_DIM), f32),
                        pltpu.VMEM((nb * GROUPS, D_STATE, GROUP_WIDTH), f32)],
        compiler_params=_params("arbitrary", "arbitrary"),
        name="ssd_prompt",
    )(split(z), split(xbc), split(dt), dtT3, *[consts[k] for k in _SSD_CONST_NAMES])
    return y.reshape(batch * seq, D_INNER), state.reshape(batch, D_INNER, D_STATE)


def _ssd_sample_body(seq_len, z_ref, xbc_ref, pre_ref, dt_ref, dtT_ref, state_ref, *rest):
    nc = len(_SSD_CONST_NAMES)
    c = dict(zip(_SSD_CONST_NAMES, rest[:nc]))
    y_ref, state_out_ref, c_ref, b_ref, xdT_ref, yacc_ref, eax_ref, eatT_ref = rest[nc:]
    s = pl.program_id(1)

    @pl.when(s == 0)
    def _():
        xa = _conv_silu_many_seq(xbc_ref[...], pre_ref[...], c["cw"], c["cb"], seq_len)
        b = _ssd_block(xa, dt_ref[...], dtT_ref[...], c)
        c_ref[...] = b["cm"]
        b_ref[...] = b["bm"]
        xdT_ref[...] = b["xd"].T.astype(bf16)
        yacc_ref[...] = b["y_diag"] + c["dskipx"][...] * b["xs"]
        eax_ref[...] = b["eax"]
        eatT_ref[...] = jnp.exp(_dot_sel_rhs(b["aT"], c["same"][...]))

    r0 = pl.multiple_of(s * seq_len, seq_len)
    rows = pl.ds(r0, seq_len)
    lane = lax.broadcasted_iota(i32, (1, CHUNK), 1)
    arep = jnp.broadcast_to(jnp.sum(jnp.where(lane == r0, eatT_ref[...], 0.0), axis=1, keepdims=True), (HEADS, D_STATE))
    rmask = (lax.broadcasted_iota(i32, (CHUNK, 1), 0) // seq_len) == s
    for g in range(GROUPS):
        gs = slice(g * GROUP_WIDTH, (g + 1) * GROUP_WIDTH)
        ns = slice(g * D_STATE, (g + 1) * D_STATE)
        s0 = state_ref[0, gs, :]
        cg = c_ref[rows, ns].astype(bf16)
        yo = lax.dot_general(cg, s0.astype(bf16), _NT, preferred_element_type=f32)
        yacc_ref[rows, gs] = yacc_ref[rows, gs] + yo * eax_ref[rows, gs]
        bmask = jnp.where(rmask, b_ref[:, ns], 0.0).astype(bf16)
        upd = jnp.dot(xdT_ref[gs, :], bmask, preferred_element_type=f32)
        for r in range(HEADS_PER_GROUP):
            h = g * HEADS_PER_GROUP + r
            hs = slice(r * HEAD_DIM, (r + 1) * HEAD_DIM)
            state_out_ref[0, h * HEAD_DIM:(h + 1) * HEAD_DIM, :] = s0[hs] * arep[h:h + 1, :] + upd[hs]

    @pl.when(s == pl.num_programs(1) - 1)
    def _():
        y_ref[...] = _gated_norm(yacc_ref[...], z_ref[...], c["gw"][...])


def _ssd_sample(z, xbc, pre, dt, dtT, state, consts, nseq, seq_len):
    per = CHUNK // seq_len
    nblk = nseq // per
    t = nseq * seq_len
    blk = lambda i, s: (i, 0)
    seq3 = lambda i, s: (i * per + s, 0, 0)
    return pl.pallas_call(
        functools.partial(_ssd_sample_body, seq_len),
        grid=(nblk, per),
        in_specs=[
            pl.BlockSpec((CHUNK, D_INNER), blk),
            pl.BlockSpec((CHUNK, CONV_DIM), blk),
            pl.BlockSpec((CHUNK, CONV_DIM), blk),
            pl.BlockSpec((CHUNK, HEADS), blk),
            pl.BlockSpec((HEADS, CHUNK), lambda i, s: (0, i)),
            pl.BlockSpec((1, D_INNER, D_STATE), seq3),
        ] + _const_specs(consts),
        out_specs=[
            pl.BlockSpec((CHUNK, D_INNER), blk),
            pl.BlockSpec((1, D_INNER, D_STATE), seq3),
        ],
        out_shape=[
            jax.ShapeDtypeStruct((t, D_INNER), bf16),
            jax.ShapeDtypeStruct((nseq, D_INNER, D_STATE), f32),
        ],
        scratch_shapes=[
            pltpu.VMEM((CHUNK, GROUPS * D_STATE), f32),
            pltpu.VMEM((CHUNK, GROUPS * D_STATE), f32),
            pltpu.VMEM((D_INNER, CHUNK), bf16),
            pltpu.VMEM((CHUNK, D_INNER), f32),
            pltpu.VMEM((CHUNK, D_INNER), f32),
            pltpu.VMEM((HEADS, CHUNK), f32),
        ],
        compiler_params=_params("arbitrary", "arbitrary"),
        name="ssd_sample",
    )(z, xbc, pre, dt, dtT, state, *[consts[k] for k in _SSD_CONST_NAMES])


_MIX_CONST_NAMES = ("ws", "bs", "vnw", "vnb", "wbs", "wbm", "wo", "nffn", "wrT", "br", "tris")


def _mix_body(emit_v, uv_ref, gates_ref, yssd_ref, x_ref, *rest):
    nc = len(_MIX_CONST_NAMES)
    c = dict(zip(_MIX_CONST_NAMES, rest[:nc]))
    outs = list(rest[nc:])
    h_ref, hfp_ref = outs[:2]
    v_ref = outs[2] if emit_v else None
    te_ref, gate_ref, rank_ref, cnt_ref, cnt_acc = outs[-5:]
    tm = uv_ref.shape[0]

    @pl.when(pl.program_id(0) == 0)
    def _():
        cnt_acc[...] = jnp.zeros_like(cnt_acc)

    uv = uv_ref[...].astype(f32)
    uv = 0.5 * uv * (1.0 + lax.erf(uv * np.float32(np.sqrt(0.5))))
    u, v = uv[:, :D_GMLP], uv[:, D_GMLP:]
    mu = jnp.mean(v, axis=-1, keepdims=True)
    vc = v - mu
    vn = vc * lax.rsqrt(jnp.mean(vc * vc, axis=-1, keepdims=True) + EPS) * c["vnw"][...] + c["vnb"][...]
    if emit_v:
        v_ref[...] = vn
    vn_bf = vn.astype(bf16)
    gd = D_GMLP // GMLP_HEADS
    rows = []
    for ck in range(tm // CHUNK):
        rs = slice(ck * CHUNK, (ck + 1) * CHUNK)
        heads = []
        for g in range(GMLP_HEADS):
            mixed = jnp.dot(c["ws"][g], vn_bf[rs, g * gd:(g + 1) * gd], preferred_element_type=f32)
            heads.append(mixed + c["bs"][:, g:g + 1])
        rows.append(jnp.concatenate(heads, axis=1))
    y_mlp = u * jnp.concatenate(rows, axis=0)

    a = jnp.dot(yssd_ref[...], c["wbs"][...], preferred_element_type=f32)
    b = jnp.dot(y_mlp.astype(bf16), c["wbm"][...], preferred_element_type=f32)
    gs = jax.nn.sigmoid(gates_ref[...].astype(f32))
    merged = gs[:, :D_MODEL] * a + gs[:, D_MODEL:] * b
    h = x_ref[...] + jnp.dot(merged.astype(bf16), c["wo"][...], preferred_element_type=f32)
    h_ref[...] = h
    hf = h * lax.rsqrt(jnp.mean(h * h, axis=-1, keepdims=True) + EPS) * c["nffn"][...]
    hfp_ref[...] = _pack_bf16_pair(hf)

    lg = lax.dot_general(c["wrT"][...], hf, _NT, precision=lax.Precision.HIGHEST, preferred_element_type=f32) + c["br"][...]
    sub = lax.broadcasted_iota(i32, lg.shape, 0)
    idxs, vals = [], []
    for _ in range(TOP_K):
        m = jnp.max(lg, axis=0, keepdims=True)
        idx = jnp.min(jnp.where(lg == m, sub, N_EXPERTS), axis=0, keepdims=True)
        idxs.append(idx)
        vals.append(m)
        lg = jnp.where(sub == idx, -jnp.inf, lg)
    p = jnp.exp(jnp.concatenate(vals, axis=0) - vals[0])
    gate_ref[...] = p / jnp.sum(p, axis=0, keepdims=True)
    te_ref[...] = jnp.concatenate(idxs, axis=0)

    onehots = [sub == idx for idx in idxs]
    member = functools.reduce(jnp.logical_or, onehots).astype(f32)
    before = jnp.dot(member.astype(bf16), c["tris"][...], preferred_element_type=f32) + cnt_acc[:, 0:1]
    rank_ref[...] = jnp.concatenate(
        [jnp.sum(jnp.where(oh, before, 0.0), axis=0, keepdims=True) for oh in onehots], axis=0).astype(i32)
    cnt_acc[...] = cnt_acc[...] + jnp.sum(member, axis=1, keepdims=True)
    cnt_ref[...] = cnt_acc[...].astype(i32)


def _mix(uv, gates, yssd, x2d, mc, tm, emit_v):
    t = x2d.shape[0]
    row = lambda i: (i, 0)
    col = lambda i: (0, i)
    const = lambda a: pl.BlockSpec(a.shape, lambda i: (0,) * a.ndim, pipeline_mode=pl.Buffered(1))
    v_spec = [pl.BlockSpec((tm, D_GMLP), row)] if emit_v else []
    v_shape = [jax.ShapeDtypeStruct((t, D_GMLP), f32)] if emit_v else []
    return pl.pallas_call(
        functools.partial(_mix_body, emit_v),
        grid=(t // tm,),
        in_specs=[
            pl.BlockSpec((tm, D_INNER), row),
            pl.BlockSpec((tm, D_INNER), row),
            pl.BlockSpec((tm, D_INNER), row),
            pl.BlockSpec((tm, D_MODEL), row),
        ] + [const(mc[k]) for k in _MIX_CONST_NAMES],
        out_specs=[pl.BlockSpec((tm, D_MODEL), row), pl.BlockSpec((tm, D_PACK), row)] + v_spec + [
            pl.BlockSpec((TOP_K, tm), col),
            pl.BlockSpec((TOP_K, tm), col),
            pl.BlockSpec((TOP_K, tm), col),
            pl.BlockSpec((N_EXPERTS, 128), lambda i: (0, 0)),
        ],
        out_shape=[
            jax.ShapeDtypeStruct((t, D_MODEL), f32),
            jax.ShapeDtypeStruct((t, D_PACK), u32),
        ] + v_shape + [
            jax.ShapeDtypeStruct((TOP_K, t), i32),
            jax.ShapeDtypeStruct((TOP_K, t), f32),
            jax.ShapeDtypeStruct((TOP_K, t), i32),
            jax.ShapeDtypeStruct((N_EXPERTS, 128), i32),
        ],
        scratch_shapes=[pltpu.VMEM((N_EXPERTS, 128), f32)],
        compiler_params=_params("arbitrary"),
        name="mix_route",
    )(uv, gates, yssd, x2d, *[mc[k] for k in _MIX_CONST_NAMES])


def _dispatch_body(tm, n_p_tiles, slot_ref, pend_ref, padded_ref, hfp_ref, hfs_ref, xs_ref, zero_ref, sem):
    i = pl.program_id(0)

    def tail_copy(e):
        start = pl.multiple_of(pend_ref[e] - MOE_BLOCK, MOE_BLOCK)
        return pltpu.make_async_copy(zero_ref, xs_ref.at[pl.ds(start, MOE_BLOCK), :], sem)

    def unused_copy(j):
        return pltpu.make_async_copy(zero_ref, xs_ref.at[pl.ds(j * MOE_BLOCK, MOE_BLOCK), :], sem)

    @pl.when(i == 0)
    def _():
        zero_ref[...] = jnp.zeros_like(zero_ref)
        n_blocks = xs_ref.shape[0] // MOE_BLOCK
        for start_or_wait in ("start", "wait"):
            for e in range(N_EXPERTS):
                @pl.when(padded_ref[e] > 0)
                def _():
                    getattr(tail_copy(e), start_or_wait)()
            for j in range(n_blocks - N_EXPERTS, n_blocks):
                @pl.when(j * MOE_BLOCK >= pend_ref[N_EXPERTS - 1])
                def _():
                    getattr(unused_copy(j), start_or_wait)()

    def scatter(src_ref):
        for r in range(tm):
            for k in range(TOP_K):
                slot = slot_ref[(i * tm + r) * TOP_K + k]
                pltpu.make_async_copy(src_ref.at[r], xs_ref.at[slot], sem).start(priority=k % 2)
        for r in range(tm):
            for k in range(TOP_K):
                pltpu.make_async_copy(src_ref.at[r], xs_ref.at[0], sem).wait()

    @pl.when(i < n_p_tiles)
    def _():
        scatter(hfp_ref)

    @pl.when(i >= n_p_tiles)
    def _():
        scatter(hfs_ref)


def _dispatch(slot_flat, pend, padded, hf_p, hf_s, n_slots, tm):
    n_p, n_s = hf_p.shape[0] // tm, hf_s.shape[0] // tm
    return pl.pallas_call(
        functools.partial(_dispatch_body, tm, n_p),
        grid_spec=pltpu.PrefetchScalarGridSpec(
            num_scalar_prefetch=3,
            grid=(n_p + n_s,),
            in_specs=[
                pl.BlockSpec((tm, D_PACK), lambda i, *_: (jnp.minimum(i, n_p - 1), 0)),
                pl.BlockSpec((tm, D_PACK), lambda i, *_: (jnp.maximum(i - n_p, 0), 0)),
            ],
            out_specs=pl.BlockSpec(memory_space=pl.ANY),
            scratch_shapes=[pltpu.VMEM((MOE_BLOCK, D_PACK), u32), pltpu.SemaphoreType.DMA],
        ),
        out_shape=jax.ShapeDtypeStruct((n_slots, D_PACK), u32),
        compiler_params=_params("arbitrary"),
        name="moe_dispatch",
    )(slot_flat, pend, padded, hf_p, hf_s)


def _experts_body(be_ref, nused_ref, xs_ref, wgu_ref, bgu_ref, wd_ref, bd_ref, perm_ref, ys_ref, wgu_bf, wd_bf):
    i = pl.program_id(0)

    @pl.when(i < nused_ref[0])
    def _():
        @pl.when((i == 0) | (be_ref[i] != be_ref[jnp.maximum(i - 1, 0)]))
        def _():
            wd_bf[...] = wd_ref[0].astype(bf16)
            half = DEINT // 2
            for j in range(2 * D_FF // DEINT):
                w = jnp.dot(wgu_ref[0, :, j * DEINT:(j + 1) * DEINT].astype(bf16), perm_ref[...],
                            preferred_element_type=f32).astype(bf16)
                wgu_bf[:, j * half:(j + 1) * half] = w[:, :half]
                wgu_bf[:, D_FF + j * half:D_FF + (j + 1) * half] = w[:, half:]

        gu = jnp.dot(_unpack_bf16_pair(xs_ref[...]), wgu_bf[...], preferred_element_type=f32) + bgu_ref[0]
        glu = jnp.minimum(gu[:, :D_FF], SWIGLU_LIMIT)
        lin = jnp.clip(gu[:, D_FF:], -SWIGLU_LIMIT, SWIGLU_LIMIT)
        act = glu * jax.nn.sigmoid(SWIGLU_ALPHA * glu) * (lin + 1.0)
        ys_ref[...] = jnp.dot(act.astype(bf16), wd_bf[...], preferred_element_type=f32) + bd_ref[0]

    @pl.when(i >= nused_ref[0])
    def _():
        ys_ref[...] = jnp.zeros_like(ys_ref)


def _experts(block_e, n_used, xs, wgu, bgu, w_down, b_down):
    n_blocks = xs.shape[0] // MOE_BLOCK
    blk = lambda i, be, nu: (jnp.maximum(jnp.minimum(i, nu[0] - 1), 0), 0)
    exp3 = lambda i, be, nu: (be[i], 0, 0)
    c = np.arange(DEINT)
    src = np.where(c < DEINT // 2, 2 * c, 2 * (c - DEINT // 2) + 1)
    perm = jnp.asarray(np.arange(DEINT)[:, None] == src[None, :], bf16)
    return pl.pallas_call(
        _experts_body,
        grid_spec=pltpu.PrefetchScalarGridSpec(
            num_scalar_prefetch=2,
            grid=(n_blocks,),
            in_specs=[
                pl.BlockSpec((MOE_BLOCK, D_PACK), blk),
                pl.BlockSpec((1, D_MODEL, 2 * D_FF), exp3),
                pl.BlockSpec((1, 1, 2 * D_FF), exp3),
                pl.BlockSpec((1, D_FF, D_MODEL), exp3),
                pl.BlockSpec((1, 1, D_MODEL), exp3),
                pl.BlockSpec((DEINT, DEINT), lambda i, be, nu: (0, 0)),
            ],
            out_specs=pl.BlockSpec((MOE_BLOCK, D_MODEL), lambda i, be, nu: (i, 0)),
            scratch_shapes=[pltpu.VMEM((D_MODEL, 2 * D_FF), bf16), pltpu.VMEM((D_FF, D_MODEL), bf16)],
        ),
        out_shape=jax.ShapeDtypeStruct((xs.shape[0], D_MODEL), f32),
        compiler_params=_params("arbitrary"),
        name="moe_experts",
    )(block_e, n_used, xs, wgu, bgu, w_down, b_down, perm)


def _combine_body(tm, slot_ref, h_ref, gate_ref, nf_ref, ys_ref, o_ref, rows_ref, sems):
    s = pl.program_id(0)
    n_tiles = pl.num_programs(0) - 1

    @pl.when(s < n_tiles)
    def _():
        buf = s % 2
        for r in range(tm):
            for k in range(TOP_K):
                slot = slot_ref[(s * tm + r) * TOP_K + k]
                pltpu.make_async_copy(ys_ref.at[slot], rows_ref.at[buf, k, r], sems.at[buf]).start(priority=k % 2)

    @pl.when(s > 0)
    def _():
        buf = (s - 1) % 2
        for r in range(tm):
            for k in range(TOP_K):
                pltpu.make_async_copy(ys_ref.at[0], rows_ref.at[buf, k, r], sems.at[buf]).wait()
        g = gate_ref[...]
        moe = g[:, 0:1] * rows_ref[buf, 0]
        for k in range(1, TOP_K):
            moe = moe + g[:, k:k + 1] * rows_ref[buf, k]
        h = h_ref[...] + moe
        o_ref[...] = h * lax.rsqrt(jnp.mean(h * h, axis=-1, keepdims=True) + EPS) * nf_ref[...]


def _combine(slot_flat, h, gate_t, norm_final, ys, tm):
    t = h.shape[0]
    return pl.pallas_call(
        functools.partial(_combine_body, tm),
        grid_spec=pltpu.PrefetchScalarGridSpec(
            num_scalar_prefetch=1,
            grid=(t // tm + 1,),
            in_specs=[
                pl.BlockSpec((tm, D_MODEL), lambda i, s: (jnp.maximum(i - 1, 0), 0)),
                pl.BlockSpec((tm, TOP_K), lambda i, s: (jnp.maximum(i - 1, 0), 0)),
                pl.BlockSpec((1, D_MODEL), lambda i, s: (0, 0)),
                pl.BlockSpec(memory_space=pl.ANY),
            ],
            out_specs=pl.BlockSpec((tm, D_MODEL), lambda i, s: (jnp.maximum(i - 1, 0), 0)),
            scratch_shapes=[pltpu.VMEM((2, TOP_K, tm, D_MODEL), f32), pltpu.SemaphoreType.DMA((2,))],
        ),
        out_shape=jax.ShapeDtypeStruct((t, D_MODEL), f32),
        compiler_params=_params("arbitrary"),
        name="moe_combine",
    )(slot_flat, h, gate_t, norm_final, ys)


def kernel(x_prompt, x_sample, state_conv, state_ssm, norm_mix, w_in, conv_w, conv_b, dt_bias, a_log, d_skip, gnorm_w, v_norm_w, v_norm_b, w_spatial, b_spatial, w_branch_ssd, w_branch_mlp, w_out, norm_ffn, w_router, b_router, w_gu, b_gu, w_down, b_down, norm_final):
    assert w_in.shape[0] == 1, "single-layer trunk"
    batch, seq, _ = x_prompt.shape
    nseq, dec_seq, _ = x_sample.shape
    assert seq % CHUNK == 0 and CHUNK % dec_seq == 0 and nseq % (CHUNK // dec_seq) == 0 and dec_seq >= CONV_W - 1
    t_p, t_s = batch * seq, nseq * dec_seq
    tm = ROW_DMA_TILE
    assert t_p % tm == 0 and t_s % tm == 0
    tm_mix = MIX_TILE if t_p % MIX_TILE == 0 and t_s % MIX_TILE == 0 else tm

    wi = w_in[0]
    z0, x0, d0, u0 = D_INNER, D_INNER + CONV_DIM, D_INNER + CONV_DIM + HEADS, D_INNER + CONV_DIM + HEADS + 2 * D_GMLP
    w_main = jnp.concatenate([wi[:, :z0], wi[:, d0:u0], wi[:, u0:], wi[:, z0:x0]], axis=1).astype(bf16)
    w_dt = wi[:, x0:d0].astype(bf16)
    w_dtT = w_dt.T
    nm = norm_mix[0].reshape(1, D_MODEL)
    sc = lambda L: _ssd_consts(conv_w[0], conv_b[0], dt_bias[0], a_log[0], d_skip[0], gnorm_w[0], L)
    xp2, xs2 = x_prompt.reshape(t_p, D_MODEL), x_sample.reshape(t_s, D_MODEL)

    tril = np.tril(np.ones((CHUNK, CHUNK), bool))
    ws_p = jnp.where(tril[None], w_spatial[0], 0).astype(bf16)
    per = CHUNK // dec_seq
    blockdiag = (np.arange(CHUNK)[:, None] // dec_seq) == (np.arange(CHUNK)[None, :] // dec_seq)
    ws_s = jnp.where((tril & blockdiag)[None], jnp.tile(w_spatial[0][:, :dec_seq, :dec_seq], (1, per, per)), 0).astype(bf16)
    bs_p = b_spatial[0].T
    bs_s = jnp.tile(b_spatial[0][:, :dec_seq], (1, per)).T
    mc = dict(
        vnw=v_norm_w[0].reshape(1, D_GMLP), vnb=v_norm_b[0].reshape(1, D_GMLP),
        wbs=w_branch_ssd[0].astype(bf16), wbm=w_branch_mlp[0].astype(bf16), wo=w_out[0].astype(bf16),
        nffn=norm_ffn[0].reshape(1, D_MODEL), wrT=w_router[0].T, br=b_router[0].reshape(N_EXPERTS, 1),
        tris=jnp.asarray(np.triu(np.ones((tm_mix, tm_mix), np.float32), 1), bf16),
    )

    z_p, uv_p, g_p, xbc_p, dt_p, dtT_p = _in_proj(xp2, nm, w_main, w_dt, w_dtT)
    yssd_p, ssm_p = _ssd_prompt(z_p, xbc_p, dt_p, dtT_p, sc(CHUNK), batch, seq)
    h_p, hf_p, te_p, gate_p, rank_p, cnt_p = _mix(uv_p, g_p, yssd_p, xp2, dict(mc, ws=ws_p, bs=bs_p), tm_mix, False)

    z_s, uv_s, g_s, xbc_s, dt_s, dtT_s = _in_proj(xs2, nm, w_main, w_dt, w_dtT)
    pre = jnp.pad(state_conv[0], ((0, 0), (dec_seq - (CONV_W - 1), 0), (0, 0))).reshape(t_s, CONV_DIM)
    yssd_s, ssm_s = _ssd_sample(z_s, xbc_s, pre, dt_s, dtT_s, state_ssm[0].reshape(nseq, D_INNER, D_STATE),
                                sc(dec_seq), nseq, dec_seq)
    h_s, hf_s, v_s, te_s, gate_s, rank_s, cnt_s = _mix(uv_s, g_s, yssd_s, xs2, dict(mc, ws=ws_s, bs=bs_s), tm_mix, True)

    cp, cs = cnt_p[:, 0], cnt_s[:, 0]
    padded = (cp + cs + MOE_BLOCK - 1) // MOE_BLOCK * MOE_BLOCK
    pend = jnp.cumsum(padded)
    pstart = pend - padded

    def lookup(table, te):
        eid = jnp.arange(N_EXPERTS, dtype=i32)[:, None, None]
        return jnp.sum(jnp.where(te[None] == eid, table[:, None, None], 0), axis=0)

    slot_p = (lookup(pstart, te_p) + rank_p).T.reshape(-1).astype(i32)
    slot_s = (lookup(pstart + cp, te_s) + rank_s).T.reshape(-1).astype(i32)
    slot_all = jnp.concatenate([slot_p, slot_s])
    n_blocks = (t_p + t_s) * TOP_K // MOE_BLOCK + N_EXPERTS
    n_used = (pend[-1] // MOE_BLOCK).astype(i32).reshape(1)
    first_row = jnp.arange(n_blocks, dtype=i32) * MOE_BLOCK
    block_e = jnp.minimum(jnp.sum(pend[None, :] <= first_row[:, None], axis=1), N_EXPERTS - 1).astype(i32)

    xs_sorted = _dispatch(slot_all, pend.astype(i32), padded.astype(i32), hf_p, hf_s, n_blocks * MOE_BLOCK, tm)
    bgu = jnp.concatenate([b_gu[0][..., 0::2], b_gu[0][..., 1::2]], axis=-1).reshape(N_EXPERTS, 1, 2 * D_FF)
    ys = _experts(block_e, n_used, xs_sorted, w_gu[0], bgu, w_down[0], b_down[0].reshape(N_EXPERTS, 1, D_MODEL))

    nf = norm_final.reshape(1, D_MODEL)
    y_p = _combine(slot_p, h_p, gate_p.T, nf, ys, tm)
    y_s = _combine(slot_s, h_s, gate_s.T, nf, ys, tm)

    conv_p = xbc_p.reshape(batch, seq, CONV_DIM)[:, seq - (CONV_W - 1):]
    conv_s = xbc_s.reshape(nseq, dec_seq, CONV_DIM)[:, dec_seq - (CONV_W - 1):]
    st_shape = (HEADS, HEAD_DIM, D_STATE)
    return (
        y_p.reshape(batch, seq, D_MODEL),
        y_s.reshape(nseq, dec_seq, D_MODEL),
        conv_p[None],
        ssm_p.reshape(1, batch, *st_shape),
        conv_s[None],
        ssm_s.reshape(1, nseq, *st_shape),
        v_s.reshape(1, nseq, dec_seq, D_GMLP),
    )
```

```python
import functools

import numpy as np
import jax
import jax.numpy as jnp
from jax import lax
from jax.experimental import pallas as pl
from jax.experimental.pallas import tpu as pltpu

f32, bf16, i32, u32 = jnp.float32, jnp.bfloat16, jnp.int32, jnp.uint32

D_MODEL = 1024
D_INNER = 2048
HEAD_DIM = 64
HEADS = 32
GROUPS = 4
HEADS_PER_GROUP = 8
GROUP_WIDTH = HEADS_PER_GROUP * HEAD_DIM
D_STATE = 128
CONV_W = 4
CONV_DIM = D_INNER + 2 * GROUPS * D_STATE
CHUNK = 128
SUBLANES = 8
D_GMLP = 1024
GMLP_HEADS = 8
N_EXPERTS = 32
TOP_K = 4
D_FF = 1024
SWIGLU_ALPHA = 1.702
SWIGLU_LIMIT = 7.0
EPS = 1e-5
MOE_BLOCK = 512
DEINT = 256
ROW_DMA_TILE = 256
MIX_TILE = 512
D_MAIN = 3 * D_INNER + CONV_DIM
PROJ_TN = 512
PROMPT_SEQS_PER_STEP = 2
SAMPLE_SEQS_PER_STEP = 4
D_PACK = D_MODEL // 2
VMEM_LIMIT = 56 * 1024 * 1024

_NT = (((1,), (1,)), ((), ()))
_HI16 = np.uint32(0xFFFF0000)


def _params(*sem):
    return pltpu.CompilerParams(dimension_semantics=sem, vmem_limit_bytes=VMEM_LIMIT)


def _split(x, parts):
    out = []
    for _ in range(parts - 1):
        p = x.astype(bf16)
        out.append(p)
        x = x - p.astype(f32)
    out.append(x.astype(bf16))
    return out


def _dot_sel_rhs(x, sel, parts=3):
    return sum(jnp.dot(p, sel, preferred_element_type=f32) for p in _split(x, parts))


def _dot_sel_lhs(sel, x, parts=3):
    return sum(jnp.dot(sel, p, preferred_element_type=f32) for p in _split(x, parts))


def _silu(x):
    return x * jax.nn.sigmoid(x)


def _pack_bf16_pair(x):
    n = x.shape[1] // 2
    lo = lax.bitcast_convert_type(x[:, :n].astype(bf16).astype(f32), u32) >> 16
    hi = lax.bitcast_convert_type(x[:, n:].astype(bf16).astype(f32), u32) & _HI16
    return lo | hi


def _unpack_bf16_pair(p):
    lo = lax.bitcast_convert_type(p << 16, f32).astype(bf16)
    hi = lax.bitcast_convert_type(p & _HI16, f32).astype(bf16)
    return jnp.concatenate([lo, hi], axis=1)


_PROJ_SEGMENTS = ((0, D_INNER), (D_INNER, D_INNER), (2 * D_INNER, D_INNER), (3 * D_INNER, CONV_DIM))


def _in_proj_body(x_ref, nw_ref, w_ref, wdt_ref, wdtT_ref, z_ref, uv_ref, g_ref, xbc_ref, dt_ref, dtT_ref):
    x = x_ref[...]
    ms = jnp.mean(x * x, axis=-1, keepdims=True)
    hn = (x * lax.rsqrt(ms + EPS) * nw_ref[...]).astype(bf16)
    dt_ref[...] = jnp.dot(hn, wdt_ref[...], preferred_element_type=f32)
    dtT_ref[...] = lax.dot_general(wdtT_ref[...], hn, _NT, preferred_element_type=f32)
    for o_ref, (col0, width) in zip((z_ref, uv_ref, g_ref, xbc_ref), _PROJ_SEGMENTS):
        for c in range(0, width, PROJ_TN):
            acc = jnp.dot(hn, w_ref[:, col0 + c:col0 + c + PROJ_TN], preferred_element_type=f32)
            o_ref[:, c:c + PROJ_TN] = acc.astype(o_ref.dtype)


def _in_proj(x2d, norm_w, w_main, w_dt, w_dtT):
    t = x2d.shape[0]
    tm = min(512, t)
    row = lambda i: (i, 0)
    fixed = lambda i: (0, 0)
    return pl.pallas_call(
        _in_proj_body,
        grid=(t // tm,),
        in_specs=[
            pl.BlockSpec((tm, D_MODEL), row),
            pl.BlockSpec((1, D_MODEL), fixed),
            pl.BlockSpec((D_MODEL, D_MAIN), fixed, pipeline_mode=pl.Buffered(1)),
            pl.BlockSpec((D_MODEL, HEADS), fixed),
            pl.BlockSpec((HEADS, D_MODEL), fixed),
        ],
        out_specs=[
            pl.BlockSpec((tm, D_INNER), row),
            pl.BlockSpec((tm, D_INNER), row),
            pl.BlockSpec((tm, D_INNER), row),
            pl.BlockSpec((tm, CONV_DIM), row),
            pl.BlockSpec((tm, HEADS), row),
            pl.BlockSpec((HEADS, tm), lambda i: (0, i)),
        ],
        out_shape=[
            jax.ShapeDtypeStruct((t, D_INNER), bf16),
            jax.ShapeDtypeStruct((t, D_INNER), bf16),
            jax.ShapeDtypeStruct((t, D_INNER), bf16),
            jax.ShapeDtypeStruct((t, CONV_DIM), f32),
            jax.ShapeDtypeStruct((t, HEADS), f32),
            jax.ShapeDtypeStruct((HEADS, t), f32),
        ],
        compiler_params=_params("arbitrary"),
        name="in_proj",
    )(x2d, norm_w, w_main, w_dt, w_dtT)


def _conv_taps(shifted, cw_ref, cb_ref):
    acc = cb_ref[...]
    for k in range(CONV_W):
        acc = acc + shifted(CONV_W - 1 - k) * cw_ref[k:k + 1, :]
    return _silu(acc)


def _conv_silu_one_seq(cur, prev8, cw_ref, cb_ref):
    row8 = lax.broadcasted_iota(i32, (SUBLANES, 1), 0)

    def shifted(j):
        if j == 0:
            return cur
        xr = pltpu.roll(cur, j, 0)
        top = jnp.where(row8 >= j, xr[:SUBLANES], pltpu.roll(prev8, j, 0))
        return jnp.concatenate([top, xr[SUBLANES:]], axis=0)

    return _conv_taps(shifted, cw_ref, cb_ref)


def _conv_silu_many_seq(cur, pre, cw_ref, cb_ref, seq_len):
    n = cur.shape[0]
    pos = lax.broadcasted_iota(i32, (n, 1), 0) % seq_len

    def shifted(j):
        if j == 0:
            return cur
        return jnp.where(pos >= j, pltpu.roll(cur, j, 0), pltpu.roll(pre, (j + n - seq_len) % n, 0))

    return _conv_taps(shifted, cw_ref, cb_ref)


def _ssd_block(xa, dt_raw, dtT_raw, c):
    tril = c["tril"][...]
    dt = jax.nn.softplus(dt_raw + c["dtb_row"][...])
    dtT = jax.nn.softplus(dtT_raw + c["dtb_col"][...])
    a = dt * c["aneg_row"][...]
    aT = dtT * c["aneg_col"][...]
    a_cum = _dot_sel_lhs(tril, a)
    a_cumT = _dot_sel_rhs(aT, c["triu"][...])
    a_tot = _dot_sel_lhs(c["same"][...], a)
    stack = jnp.concatenate([dt, jnp.exp(a_cum), jnp.exp(a_tot - a_cum)], axis=0)
    ex = _dot_sel_rhs(stack, c["expand"][...], parts=2)
    n = xa.shape[0]
    dtx, eax, dex = ex[:n], ex[n:2 * n], ex[2 * n:]

    xs = xa[:, :D_INNER]
    bm = xa[:, D_INNER:D_INNER + GROUPS * D_STATE]
    cm = xa[:, D_INNER + GROUPS * D_STATE:]
    xdt = xs * dtx
    xdt_bf = xdt.astype(bf16)
    mask = tril > 0
    lane = lax.broadcasted_iota(i32, (1, 2 * HEAD_DIM), 1)
    ys = []
    for g in range(GROUPS):
        cg = cm[:, g * D_STATE:(g + 1) * D_STATE].astype(bf16)
        bg = bm[:, g * D_STATE:(g + 1) * D_STATE].astype(bf16)
        cb = lax.dot_general(cg, bg, _NT, preferred_element_type=f32)
        for pair in range(HEADS_PER_GROUP // 2):
            halves = []
            for h in (g * HEADS_PER_GROUP + 2 * pair, g * HEADS_PER_GROUP + 2 * pair + 1):
                seg = a_cum[:, h:h + 1] - a_cumT[h:h + 1, :]
                decay = jnp.exp(jnp.where(mask, seg, -jnp.inf))
                m = (cb * decay).astype(bf16)
                col = (h // 2) * 2 * HEAD_DIM
                halves.append(jnp.dot(m, xdt_bf[:, col:col + 2 * HEAD_DIM], preferred_element_type=f32))
            ys.append(jnp.where(lane < HEAD_DIM, halves[0], halves[1]))
    y_diag = jnp.concatenate(ys, axis=1)
    return dict(xs=xs, bm=bm, cm=cm, xdt=xdt, xd=xdt * dex, eax=eax, aT=aT, y_diag=y_diag)


def _gated_norm(y, z, gw):
    g = y * _silu(z.astype(f32))
    ms = jnp.mean(g * g, axis=-1, keepdims=True)
    return (g * lax.rsqrt(ms + EPS) * gw).astype(bf16)


_SSD_CONST_NAMES = ("cw", "cb", "dtb_row", "dtb_col", "aneg_row", "aneg_col", "dskipx", "gw",
                    "tril", "triu", "same", "expand")


def _ssd_consts(conv_w, conv_b, dt_bias, a_log, d_skip, gnorm_w, seq_len):
    r = np.arange(CHUNK)
    same = (r[:, None] // seq_len) == (r[None, :] // seq_len)
    tril = same & (r[None, :] <= r[:, None])
    expand = np.repeat(np.eye(HEADS, dtype=np.float32), HEAD_DIM, axis=1)
    aneg = -jnp.exp(a_log.astype(f32))
    return dict(
        cw=conv_w, cb=conv_b.reshape(1, CONV_DIM),
        dtb_row=dt_bias.reshape(1, HEADS), dtb_col=dt_bias.reshape(HEADS, 1),
        aneg_row=aneg.reshape(1, HEADS), aneg_col=aneg.reshape(HEADS, 1),
        dskipx=jnp.repeat(d_skip.astype(f32), HEAD_DIM).reshape(1, D_INNER),
        gw=gnorm_w.reshape(1, D_INNER),
        tril=jnp.asarray(tril, bf16), triu=jnp.asarray(tril.T, bf16), same=jnp.asarray(same, bf16),
        expand=jnp.asarray(expand, bf16),
    )


def _const_specs(consts):
    zero = (lambda *_: (0, 0))
    return [pl.BlockSpec(consts[k].shape, zero) for k in _SSD_CONST_NAMES]


def _ssd_prompt_body(nb, z_ref, xbc_ref, dt_ref, dtT_ref, *rest):
    nc = len(_SSD_CONST_NAMES)
    c = dict(zip(_SSD_CONST_NAMES, rest[:nc]))
    y_ref, state_ref, prev_ref, st_ref = rest[nc:]
    ci = pl.program_id(1)

    @pl.when(ci == 0)
    def _():
        prev_ref[...] = jnp.zeros_like(prev_ref)
        st_ref[...] = jnp.zeros_like(st_ref)

    for s in range(nb):
        cur = xbc_ref[s]
        xa = _conv_silu_one_seq(cur, prev_ref[s], c["cw"], c["cb"])
        prev_ref[s] = cur[CHUNK - SUBLANES:]
        b = _ssd_block(xa, dt_ref[s], dtT_ref[s], c)

        y_off = []
        for g in range(GROUPS):
            sl = slice(g * GROUP_WIDTH, (g + 1) * GROUP_WIDTH)
            st = st_ref[s * GROUPS + g]
            cg = b["cm"][:, g * D_STATE:(g + 1) * D_STATE].astype(bf16)
            y_off.append(jnp.dot(cg, st.astype(bf16), preferred_element_type=f32))
            bgT = b["bm"][:, g * D_STATE:(g + 1) * D_STATE].T.astype(bf16)
            upd = jnp.dot(bgT, b["xd"][:, sl].astype(bf16), preferred_element_type=f32)
            st_ref[s * GROUPS + g] = st * b["eax"][CHUNK - 1:CHUNK, sl] + upd
        y = b["y_diag"] + jnp.concatenate(y_off, axis=1) * b["eax"] + c["dskipx"][...] * b["xs"]
        y_ref[s] = _gated_norm(y, z_ref[s], c["gw"][...])

    @pl.when(ci == pl.num_programs(1) - 1)
    def _():
        for s in range(nb):
            for g in range(GROUPS):
                state_ref[s, 0, g * GROUP_WIDTH:(g + 1) * GROUP_WIDTH, :] = st_ref[s * GROUPS + g].T


def _ssd_prompt(z, xbc, dt, dtT, consts, batch, seq):
    nb = PROMPT_SEQS_PER_STEP if batch % PROMPT_SEQS_PER_STEP == 0 else 1
    nchunk = seq // CHUNK
    part = batch // nb * seq
    split = lambda a: a.reshape(nb, part, a.shape[-1])
    dtT3 = dtT.reshape(HEADS, nb, part).transpose(1, 0, 2)
    row = lambda b, ci: (0, b * nchunk + ci, 0)
    y, state = pl.pallas_call(
        functools.partial(_ssd_prompt_body, nb),
        grid=(batch // nb, nchunk),
        in_specs=[
            pl.BlockSpec((nb, CHUNK, D_INNER), row),
            pl.BlockSpec((nb, CHUNK, CONV_DIM), row),
            pl.BlockSpec((nb, CHUNK, HEADS), row),
            pl.BlockSpec((nb, HEADS, CHUNK), lambda b, ci: (0, 0, b * nchunk + ci)),
        ] + _const_specs(consts),
        out_specs=[
            pl.BlockSpec((nb, CHUNK, D_INNER), row),
            pl.BlockSpec((nb, 1, D_INNER, D_STATE), lambda b, ci: (0, b, 0, 0)),
        ],
        out_shape=[
            jax.ShapeDtypeStruct((nb, part, D_INNER), bf16),
            jax.ShapeDtypeStruct((nb, batch // nb, D_INNER, D_STATE), f32),
        ],
        scratch_shapes=[pltpu.VMEM((nb, SUBLANES, CONV_DIM), f32),
                        pltpu.VMEM((nb * GROUPS, D_STATE, GROUP_WIDTH), f32)],
        compiler_params=_params("arbitrary", "arbitrary"),
        name="ssd_prompt",
    )(split(z), split(xbc), split(dt), dtT3, *[consts[k] for k in _SSD_CONST_NAMES])
    return y.reshape(batch * seq, D_INNER), state.reshape(batch, D_INNER, D_STATE)


def _ssd_sample_body(seq_len, z_ref, xbc_ref, pre_ref, dt_ref, dtT_ref, state_ref, *rest):
    nc = len(_SSD_CONST_NAMES)
    c = dict(zip(_SSD_CONST_NAMES, rest[:nc]))
    y_ref, state_out_ref, c_ref, b_ref, xdT_ref, yacc_ref, eax_ref, eatT_ref = rest[nc:]
    step = pl.program_id(1)
    per_step = state_ref.shape[0]

    @pl.when(step == 0)
    def _():
        xa = _conv_silu_many_seq(xbc_ref[...], pre_ref[...], c["cw"], c["cb"], seq_len)
        b = _ssd_block(xa, dt_ref[...], dtT_ref[...], c)
        c_ref[...] = b["cm"]
        b_ref[...] = b["bm"]
        xdT_ref[...] = b["xd"].T.astype(bf16)
        yacc_ref[...] = b["y_diag"] + c["dskipx"][...] * b["xs"]
        eax_ref[...] = b["eax"]
        eatT_ref[...] = jnp.exp(_dot_sel_rhs(b["aT"], c["same"][...]))

    def one_sequence(q, carry):
        s = step * per_step + q
        r0 = pl.multiple_of(s * seq_len, seq_len)
        rows = pl.ds(r0, seq_len)
        lane = lax.broadcasted_iota(i32, (1, CHUNK), 1)
        arep = jnp.broadcast_to(jnp.sum(jnp.where(lane == r0, eatT_ref[...], 0.0), axis=1, keepdims=True), (HEADS, D_STATE))
        rmask = (lax.broadcasted_iota(i32, (CHUNK, 1), 0) // seq_len) == s
        for g in range(GROUPS):
            gs = slice(g * GROUP_WIDTH, (g + 1) * GROUP_WIDTH)
            ns = slice(g * D_STATE, (g + 1) * D_STATE)
            s0 = state_ref[q, gs, :]
            cg = c_ref[rows, ns].astype(bf16)
            yo = lax.dot_general(cg, s0.astype(bf16), _NT, preferred_element_type=f32)
            yacc_ref[rows, gs] = yacc_ref[rows, gs] + yo * eax_ref[rows, gs]
            bmask = jnp.where(rmask, b_ref[:, ns], 0.0).astype(bf16)
            upd = jnp.dot(xdT_ref[gs, :], bmask, preferred_element_type=f32)
            for r in range(HEADS_PER_GROUP):
                h = g * HEADS_PER_GROUP + r
                hs = slice(r * HEAD_DIM, (r + 1) * HEAD_DIM)
                state_out_ref[q, h * HEAD_DIM:(h + 1) * HEAD_DIM, :] = s0[hs] * arep[h:h + 1, :] + upd[hs]
        return carry

    lax.fori_loop(0, per_step, one_sequence, 0)

    @pl.when(step == pl.num_programs(1) - 1)
    def _():
        y_ref[...] = _gated_norm(yacc_ref[...], z_ref[...], c["gw"][...])


def _ssd_sample(z, xbc, pre, dt, dtT, state, consts, nseq, seq_len):
    per = CHUNK // seq_len
    nblk = nseq // per
    t = nseq * seq_len
    q = SAMPLE_SEQS_PER_STEP if per % SAMPLE_SEQS_PER_STEP == 0 else 1
    blk = lambda i, s: (i, 0)
    seq3 = lambda i, s: (i * (per // q) + s, 0, 0)
    return pl.pallas_call(
        functools.partial(_ssd_sample_body, seq_len),
        grid=(nblk, per // q),
        in_specs=[
            pl.BlockSpec((CHUNK, D_INNER), blk),
            pl.BlockSpec((CHUNK, CONV_DIM), blk),
            pl.BlockSpec((CHUNK, CONV_DIM), blk),
            pl.BlockSpec((CHUNK, HEADS), blk),
            pl.BlockSpec((HEADS, CHUNK), lambda i, s: (0, i)),
            pl.BlockSpec((q, D_INNER, D_STATE), seq3),
        ] + _const_specs(consts),
        out_specs=[
            pl.BlockSpec((CHUNK, D_INNER), blk),
            pl.BlockSpec((q, D_INNER, D_STATE), seq3),
        ],
        out_shape=[
            jax.ShapeDtypeStruct((t, D_INNER), bf16),
            jax.ShapeDtypeStruct((nseq, D_INNER, D_STATE), f32),
        ],
        scratch_shapes=[
            pltpu.VMEM((CHUNK, GROUPS * D_STATE), f32),
            pltpu.VMEM((CHUNK, GROUPS * D_STATE), f32),
            pltpu.VMEM((D_INNER, CHUNK), bf16),
            pltpu.VMEM((CHUNK, D_INNER), f32),
            pltpu.VMEM((CHUNK, D_INNER), f32),
            pltpu.VMEM((HEADS, CHUNK), f32),
        ],
        compiler_params=_params("arbitrary", "arbitrary"),
        name="ssd_sample",
    )(z, xbc, pre, dt, dtT, state, *[consts[k] for k in _SSD_CONST_NAMES])


_MIX_CONST_NAMES = ("ws", "bs", "vnw", "vnb", "wbs", "wbm", "wo", "nffn", "wrT", "br", "tris")


def _mix_body(emit_v, uv_ref, gates_ref, yssd_ref, x_ref, *rest):
    nc = len(_MIX_CONST_NAMES)
    c = dict(zip(_MIX_CONST_NAMES, rest[:nc]))
    outs = list(rest[nc:])
    h_ref, hfp_ref = outs[:2]
    v_ref = outs[2] if emit_v else None
    te_ref, gate_ref, rank_ref, cnt_ref, cnt_acc = outs[-5:]
    tm = uv_ref.shape[0]

    @pl.when(pl.program_id(0) == 0)
    def _():
        cnt_acc[...] = jnp.zeros_like(cnt_acc)

    uv = uv_ref[...].astype(f32)
    uv = 0.5 * uv * (1.0 + lax.erf(uv * np.float32(np.sqrt(0.5))))
    u, v = uv[:, :D_GMLP], uv[:, D_GMLP:]
    mu = jnp.mean(v, axis=-1, keepdims=True)
    vc = v - mu
    vn = vc * lax.rsqrt(jnp.mean(vc * vc, axis=-1, keepdims=True) + EPS) * c["vnw"][...] + c["vnb"][...]
    if emit_v:
        v_ref[...] = vn
    vn_bf = vn.astype(bf16)
    gd = D_GMLP // GMLP_HEADS
    rows = []
    for ck in range(tm // CHUNK):
        rs = slice(ck * CHUNK, (ck + 1) * CHUNK)
        heads = []
        for g in range(GMLP_HEADS):
            mixed = jnp.dot(c["ws"][g], vn_bf[rs, g * gd:(g + 1) * gd], preferred_element_type=f32)
            heads.append(mixed + c["bs"][:, g:g + 1])
        rows.append(jnp.concatenate(heads, axis=1))
    y_mlp = u * jnp.concatenate(rows, axis=0)

    a = jnp.dot(yssd_ref[...], c["wbs"][...], preferred_element_type=f32)
    b = jnp.dot(y_mlp.astype(bf16), c["wbm"][...], preferred_element_type=f32)
    gs = jax.nn.sigmoid(gates_ref[...].astype(f32))
    merged = gs[:, :D_MODEL] * a + gs[:, D_MODEL:] * b
    h = x_ref[...] + jnp.dot(merged.astype(bf16), c["wo"][...], preferred_element_type=f32)
    h_ref[...] = h
    hf = h * lax.rsqrt(jnp.mean(h * h, axis=-1, keepdims=True) + EPS) * c["nffn"][...]
    hfp_ref[...] = _pack_bf16_pair(hf)

    lg = lax.dot_general(c["wrT"][...], hf, _NT, precision=lax.Precision.HIGHEST, preferred_element_type=f32) + c["br"][...]
    sub = lax.broadcasted_iota(i32, lg.shape, 0)
    idxs, vals = [], []
    for _ in range(TOP_K):
        m = jnp.max(lg, axis=0, keepdims=True)
        idx = jnp.min(jnp.where(lg == m, sub, N_EXPERTS), axis=0, keepdims=True)
        idxs.append(idx)
        vals.append(m)
        lg = jnp.where(sub == idx, -jnp.inf, lg)
    p = jnp.exp(jnp.concatenate(vals, axis=0) - vals[0])
    gate_ref[...] = p / jnp.sum(p, axis=0, keepdims=True)
    te_ref[...] = jnp.concatenate(idxs, axis=0)

    onehots = [sub == idx for idx in idxs]
    member = functools.reduce(jnp.logical_or, onehots).astype(f32)
    before = jnp.dot(member.astype(bf16), c["tris"][...], preferred_element_type=f32) + cnt_acc[:, 0:1]
    rank_ref[...] = jnp.concatenate(
        [jnp.sum(jnp.where(oh, before, 0.0), axis=0, keepdims=True) for oh in onehots], axis=0).astype(i32)
    cnt_acc[...] = cnt_acc[...] + jnp.sum(member, axis=1, keepdims=True)
    cnt_ref[...] = cnt_acc[...].astype(i32)


def _mix(uv, gates, yssd, x2d, mc, tm, emit_v):
    t = x2d.shape[0]
    row = lambda i: (i, 0)
    col = lambda i: (0, i)
    const = lambda a: pl.BlockSpec(a.shape, lambda i: (0,) * a.ndim, pipeline_mode=pl.Buffered(1))
    v_spec = [pl.BlockSpec((tm, D_GMLP), row)] if emit_v else []
    v_shape = [jax.ShapeDtypeStruct((t, D_GMLP), f32)] if emit_v else []
    return pl.pallas_call(
        functools.partial(_mix_body, emit_v),
        grid=(t // tm,),
        in_specs=[
            pl.BlockSpec((tm, D_INNER), row),
            pl.BlockSpec((tm, D_INNER), row),
            pl.BlockSpec((tm, D_INNER), row),
            pl.BlockSpec((tm, D_MODEL), row),
        ] + [const(mc[k]) for k in _MIX_CONST_NAMES],
        out_specs=[pl.BlockSpec((tm, D_MODEL), row), pl.BlockSpec((tm, D_PACK), row)] + v_spec + [
            pl.BlockSpec((TOP_K, tm), col),
            pl.BlockSpec((TOP_K, tm), col),
            pl.BlockSpec((TOP_K, tm), col),
            pl.BlockSpec((N_EXPERTS, 128), lambda i: (0, 0)),
        ],
        out_shape=[
            jax.ShapeDtypeStruct((t, D_MODEL), f32),
            jax.ShapeDtypeStruct((t, D_PACK), u32),
        ] + v_shape + [
            jax.ShapeDtypeStruct((TOP_K, t), i32),
            jax.ShapeDtypeStruct((TOP_K, t), f32),
            jax.ShapeDtypeStruct((TOP_K, t), i32),
            jax.ShapeDtypeStruct((N_EXPERTS, 128), i32),
        ],
        scratch_shapes=[pltpu.VMEM((N_EXPERTS, 128), f32)],
        compiler_params=_params("arbitrary"),
        name="mix_route",
    )(uv, gates, yssd, x2d, *[mc[k] for k in _MIX_CONST_NAMES])


def _dispatch_body(tm, n_p_tiles, slot_ref, pend_ref, padded_ref, hfp_ref, hfs_ref, xs_ref, zero_ref, sem):
    i = pl.program_id(0)

    def tail_copy(e):
        start = pl.multiple_of(pend_ref[e] - MOE_BLOCK, MOE_BLOCK)
        return pltpu.make_async_copy(zero_ref, xs_ref.at[pl.ds(start, MOE_BLOCK), :], sem)

    def unused_copy(j):
        return pltpu.make_async_copy(zero_ref, xs_ref.at[pl.ds(j * MOE_BLOCK, MOE_BLOCK), :], sem)

    @pl.when(i == 0)
    def _():
        zero_ref[...] = jnp.zeros_like(zero_ref)
        n_blocks = xs_ref.shape[0] // MOE_BLOCK
        for start_or_wait in ("start", "wait"):
            for e in range(N_EXPERTS):
                @pl.when(padded_ref[e] > 0)
                def _():
                    getattr(tail_copy(e), start_or_wait)()
            for j in range(n_blocks - N_EXPERTS, n_blocks):
                @pl.when(j * MOE_BLOCK >= pend_ref[N_EXPERTS - 1])
                def _():
                    getattr(unused_copy(j), start_or_wait)()

    def scatter(src_ref):
        for r in range(tm):
            for k in range(TOP_K):
                slot = slot_ref[(i * tm + r) * TOP_K + k]
                pltpu.make_async_copy(src_ref.at[r], xs_ref.at[slot], sem).start(priority=k % 2)
        for r in range(tm):
            for k in range(TOP_K):
                pltpu.make_async_copy(src_ref.at[r], xs_ref.at[0], sem).wait()

    @pl.when(i < n_p_tiles)
    def _():
        scatter(hfp_ref)

    @pl.when(i >= n_p_tiles)
    def _():
        scatter(hfs_ref)


def _dispatch(slot_flat, pend, padded, hf_p, hf_s, n_slots, tm):
    n_p, n_s = hf_p.shape[0] // tm, hf_s.shape[0] // tm
    return pl.pallas_call(
        functools.partial(_dispatch_body, tm, n_p),
        grid_spec=pltpu.PrefetchScalarGridSpec(
            num_scalar_prefetch=3,
            grid=(n_p + n_s,),
            in_specs=[
                pl.BlockSpec((tm, D_PACK), lambda i, *_: (jnp.minimum(i, n_p - 1), 0)),
                pl.BlockSpec((tm, D_PACK), lambda i, *_: (jnp.maximum(i - n_p, 0), 0)),
            ],
            out_specs=pl.BlockSpec(memory_space=pl.ANY),
            scratch_shapes=[pltpu.VMEM((MOE_BLOCK, D_PACK), u32), pltpu.SemaphoreType.DMA],
        ),
        out_shape=jax.ShapeDtypeStruct((n_slots, D_PACK), u32),
        compiler_params=_params("arbitrary"),
        name="moe_dispatch",
    )(slot_flat, pend, padded, hf_p, hf_s)


def _experts_body(be_ref, nused_ref, start_ref, slot_ref, next_ref, xs_ref, bgu_ref, bd_ref, perm_ref, wgu_hbm, wd_hbm,
                  ys_ref, wgu_f32, wd_f32, wgu_bf, wd_bf, sems):
    i = pl.program_id(0)

    def weight_copies(e, slot):
        return (pltpu.make_async_copy(wgu_hbm.at[e], wgu_f32.at[slot], sems.at[0, slot]),
                pltpu.make_async_copy(wd_hbm.at[e], wd_f32.at[slot], sems.at[1, slot]))

    @pl.when(i < nused_ref[0])
    def _():
        @pl.when(start_ref[i] == 1)
        def _():
            slot = slot_ref[i]

            @pl.when(i == 0)
            def _():
                for cp in weight_copies(be_ref[0], 0):
                    cp.start()

            for cp in weight_copies(be_ref[i], slot):
                cp.wait()

            @pl.when(next_ref[i] >= 0)
            def _():
                for cp in weight_copies(next_ref[i], 1 - slot):
                    cp.start()

            wd_bf[...] = wd_f32[slot].astype(bf16)
            half = DEINT // 2
            for j in range(2 * D_FF // DEINT):
                w = jnp.dot(wgu_f32[slot, :, j * DEINT:(j + 1) * DEINT].astype(bf16), perm_ref[...],
                            preferred_element_type=f32).astype(bf16)
                wgu_bf[:, j * half:(j + 1) * half] = w[:, :half]
                wgu_bf[:, D_FF + j * half:D_FF + (j + 1) * half] = w[:, half:]

        gu = jnp.dot(_unpack_bf16_pair(xs_ref[...]), wgu_bf[...], preferred_element_type=f32) + bgu_ref[0]
        glu = jnp.minimum(gu[:, :D_FF], SWIGLU_LIMIT)
        lin = jnp.clip(gu[:, D_FF:], -SWIGLU_LIMIT, SWIGLU_LIMIT)
        act = glu * jax.nn.sigmoid(SWIGLU_ALPHA * glu) * (lin + 1.0)
        ys_ref[...] = jnp.dot(act.astype(bf16), wd_bf[...], preferred_element_type=f32) + bd_ref[0]

    @pl.when(i >= nused_ref[0])
    def _():
        ys_ref[...] = jnp.zeros_like(ys_ref)


def _expert_segments(block_e, n_used, n_blocks):
    idx = jnp.arange(n_blocks, dtype=i32)
    prev_e = jnp.concatenate([block_e[:1], block_e[:-1]])
    start = (idx < n_used) & ((idx == 0) | (block_e != prev_e))
    slot = (jnp.cumsum(start.astype(i32)) - 1) % 2
    start_idx = jnp.where(start, idx, n_blocks)
    from_here = lax.cummin(start_idx, reverse=True)
    next_start = jnp.concatenate([from_here[1:], jnp.full((1,), n_blocks, i32)])
    next_e = jnp.sum(jnp.where(next_start[:, None] == idx[None, :], block_e[None, :], 0), axis=1)
    next_e = jnp.where(next_start < n_blocks, next_e, -1)
    return start.astype(i32), slot.astype(i32), next_e.astype(i32)


def _experts(block_e, n_used, xs, wgu, bgu, w_down, b_down):
    n_blocks = xs.shape[0] // MOE_BLOCK
    start, slot, next_e = _expert_segments(block_e, n_used[0], n_blocks)
    blk = lambda i, be, nu, *_: (jnp.maximum(jnp.minimum(i, nu[0] - 1), 0), 0)
    exp3 = lambda i, be, *_: (be[i], 0, 0)
    c = np.arange(DEINT)
    src = np.where(c < DEINT // 2, 2 * c, 2 * (c - DEINT // 2) + 1)
    perm = jnp.asarray(np.arange(DEINT)[:, None] == src[None, :], bf16)
    return pl.pallas_call(
        _experts_body,
        grid_spec=pltpu.PrefetchScalarGridSpec(
            num_scalar_prefetch=5,
            grid=(n_blocks,),
            in_specs=[
                pl.BlockSpec((MOE_BLOCK, D_PACK), blk),
                pl.BlockSpec((1, 1, 2 * D_FF), exp3),
                pl.BlockSpec((1, 1, D_MODEL), exp3),
                pl.BlockSpec((DEINT, DEINT), lambda i, *_: (0, 0)),
                pl.BlockSpec(memory_space=pl.ANY),
                pl.BlockSpec(memory_space=pl.ANY),
            ],
            out_specs=pl.BlockSpec((MOE_BLOCK, D_MODEL), lambda i, *_: (i, 0)),
            scratch_shapes=[
                pltpu.VMEM((2, D_MODEL, 2 * D_FF), f32),
                pltpu.VMEM((2, D_FF, D_MODEL), f32),
                pltpu.VMEM((D_MODEL, 2 * D_FF), bf16),
                pltpu.VMEM((D_FF, D_MODEL), bf16),
                pltpu.SemaphoreType.DMA((2, 2)),
            ],
        ),
        out_shape=jax.ShapeDtypeStruct((xs.shape[0], D_MODEL), f32),
        compiler_params=_params("arbitrary"),
        name="moe_experts",
    )(block_e, n_used, start, slot, next_e, xs, bgu, b_down, perm, wgu, w_down)


def _combine_body(tm, slot_ref, h_ref, gate_ref, nf_ref, ys_ref, o_ref, rows_ref, sems):
    s = pl.program_id(0)
    n_tiles = pl.num_programs(0) - 1

    @pl.when(s < n_tiles)
    def _():
        buf = s % 2
        for r in range(tm):
            for k in range(TOP_K):
                slot = slot_ref[(s * tm + r) * TOP_K + k]
                pltpu.make_async_copy(ys_ref.at[slot], rows_ref.at[buf, k, r], sems.at[buf]).start(priority=k % 2)

    @pl.when(s > 0)
    def _():
        buf = (s - 1) % 2
        for r in range(tm):
            for k in range(TOP_K):
                pltpu.make_async_copy(ys_ref.at[0], rows_ref.at[buf, k, r], sems.at[buf]).wait()
        g = gate_ref[...]
        moe = g[:, 0:1] * rows_ref[buf, 0]
        for k in range(1, TOP_K):
            moe = moe + g[:, k:k + 1] * rows_ref[buf, k]
        h = h_ref[...] + moe
        o_ref[...] = h * lax.rsqrt(jnp.mean(h * h, axis=-1, keepdims=True) + EPS) * nf_ref[...]


def _combine(slot_flat, h, gate_t, norm_final, ys, tm):
    t = h.shape[0]
    return pl.pallas_call(
        functools.partial(_combine_body, tm),
        grid_spec=pltpu.PrefetchScalarGridSpec(
            num_scalar_prefetch=1,
            grid=(t // tm + 1,),
            in_specs=[
                pl.BlockSpec((tm, D_MODEL), lambda i, s: (jnp.maximum(i - 1, 0), 0)),
                pl.BlockSpec((tm, TOP_K), lambda i, s: (jnp.maximum(i - 1, 0), 0)),
                pl.BlockSpec((1, D_MODEL), lambda i, s: (0, 0)),
                pl.BlockSpec(memory_space=pl.ANY),
            ],
            out_specs=pl.BlockSpec((tm, D_MODEL), lambda i, s: (jnp.maximum(i - 1, 0), 0)),
            scratch_shapes=[pltpu.VMEM((2, TOP_K, tm, D_MODEL), f32), pltpu.SemaphoreType.DMA((2,))],
        ),
        out_shape=jax.ShapeDtypeStruct((t, D_MODEL), f32),
        compiler_params=_params("arbitrary"),
        name="moe_combine",
    )(slot_flat, h, gate_t, norm_final, ys)


def kernel(x_prompt, x_sample, state_conv, state_ssm, norm_mix, w_in, conv_w, conv_b, dt_bias, a_log, d_skip, gnorm_w, v_norm_w, v_norm_b, w_spatial, b_spatial, w_branch_ssd, w_branch_mlp, w_out, norm_ffn, w_router, b_router, w_gu, b_gu, w_down, b_down, norm_final):
    assert w_in.shape[0] == 1, "single-layer trunk"
    batch, seq, _ = x_prompt.shape
    nseq, dec_seq, _ = x_sample.shape
    assert seq % CHUNK == 0 and CHUNK % dec_seq == 0 and nseq % (CHUNK // dec_seq) == 0 and dec_seq >= CONV_W - 1
    t_p, t_s = batch * seq, nseq * dec_seq
    tm = ROW_DMA_TILE
    assert t_p % tm == 0 and t_s % tm == 0
    tm_mix = MIX_TILE if t_p % MIX_TILE == 0 and t_s % MIX_TILE == 0 else tm

    wi = w_in[0]
    z0, x0, d0, u0 = D_INNER, D_INNER + CONV_DIM, D_INNER + CONV_DIM + HEADS, D_INNER + CONV_DIM + HEADS + 2 * D_GMLP
    w_main = jnp.concatenate([wi[:, :z0], wi[:, d0:u0], wi[:, u0:], wi[:, z0:x0]], axis=1).astype(bf16)
    w_dt = wi[:, x0:d0].astype(bf16)
    w_dtT = w_dt.T
    nm = norm_mix[0].reshape(1, D_MODEL)
    sc = lambda L: _ssd_consts(conv_w[0], conv_b[0], dt_bias[0], a_log[0], d_skip[0], gnorm_w[0], L)
    xp2, xs2 = x_prompt.reshape(t_p, D_MODEL), x_sample.reshape(t_s, D_MODEL)

    tril = np.tril(np.ones((CHUNK, CHUNK), bool))
    ws_p = jnp.where(tril[None], w_spatial[0], 0).astype(bf16)
    per = CHUNK // dec_seq
    blockdiag = (np.arange(CHUNK)[:, None] // dec_seq) == (np.arange(CHUNK)[None, :] // dec_seq)
    ws_s = jnp.where((tril & blockdiag)[None], jnp.tile(w_spatial[0][:, :dec_seq, :dec_seq], (1, per, per)), 0).astype(bf16)
    bs_p = b_spatial[0].T
    bs_s = jnp.tile(b_spatial[0][:, :dec_seq], (1, per)).T
    mc = dict(
        vnw=v_norm_w[0].reshape(1, D_GMLP), vnb=v_norm_b[0].reshape(1, D_GMLP),
        wbs=w_branch_ssd[0].astype(bf16), wbm=w_branch_mlp[0].astype(bf16), wo=w_out[0].astype(bf16),
        nffn=norm_ffn[0].reshape(1, D_MODEL), wrT=w_router[0].T, br=b_router[0].reshape(N_EXPERTS, 1),
        tris=jnp.asarray(np.triu(np.ones((tm_mix, tm_mix), np.float32), 1), bf16),
    )

    z_p, uv_p, g_p, xbc_p, dt_p, dtT_p = _in_proj(xp2, nm, w_main, w_dt, w_dtT)
    yssd_p, ssm_p = _ssd_prompt(z_p, xbc_p, dt_p, dtT_p, sc(CHUNK), batch, seq)
    h_p, hf_p, te_p, gate_p, rank_p, cnt_p = _mix(uv_p, g_p, yssd_p, xp2, dict(mc, ws=ws_p, bs=bs_p), tm_mix, False)

    z_s, uv_s, g_s, xbc_s, dt_s, dtT_s = _in_proj(xs2, nm, w_main, w_dt, w_dtT)
    pre = jnp.pad(state_conv[0], ((0, 0), (dec_seq - (CONV_W - 1), 0), (0, 0))).reshape(t_s, CONV_DIM)
    yssd_s, ssm_s = _ssd_sample(z_s, xbc_s, pre, dt_s, dtT_s, state_ssm[0].reshape(nseq, D_INNER, D_STATE),
                                sc(dec_seq), nseq, dec_seq)
    h_s, hf_s, v_s, te_s, gate_s, rank_s, cnt_s = _mix(uv_s, g_s, yssd_s, xs2, dict(mc, ws=ws_s, bs=bs_s), tm_mix, True)

    cp, cs = cnt_p[:, 0], cnt_s[:, 0]
    padded = (cp + cs + MOE_BLOCK - 1) // MOE_BLOCK * MOE_BLOCK
    pend = jnp.cumsum(padded)
    pstart = pend - padded

    def lookup(table, te):
        eid = jnp.arange(N_EXPERTS, dtype=i32)[:, None, None]
        return jnp.sum(jnp.where(te[None] == eid, table[:, None, None], 0), axis=0)

    slot_p = (lookup(pstart, te_p) + rank_p).T.reshape(-1).astype(i32)
    slot_s = (lookup(pstart + cp, te_s) + rank_s).T.reshape(-1).astype(i32)
    slot_all = jnp.concatenate([slot_p, slot_s])
    n_blocks = (t_p + t_s) * TOP_K // MOE_BLOCK + N_EXPERTS
    n_used = (pend[-1] // MOE_BLOCK).astype(i32).reshape(1)
    first_row = jnp.arange(n_blocks, dtype=i32) * MOE_BLOCK
    block_e = jnp.minimum(jnp.sum(pend[None, :] <= first_row[:, None], axis=1), N_EXPERTS - 1).astype(i32)

    xs_sorted = _dispatch(slot_all, pend.astype(i32), padded.astype(i32), hf_p, hf_s, n_blocks * MOE_BLOCK, tm)
    bgu = jnp.concatenate([b_gu[0][..., 0::2], b_gu[0][..., 1::2]], axis=-1).reshape(N_EXPERTS, 1, 2 * D_FF)
    ys = _experts(block_e, n_used, xs_sorted, w_gu[0], bgu, w_down[0], b_down[0].reshape(N_EXPERTS, 1, D_MODEL))

    nf = norm_final.reshape(1, D_MODEL)
    y_p = _combine(slot_p, h_p, gate_p.T, nf, ys, tm)
    y_s = _combine(slot_s, h_s, gate_s.T, nf, ys, tm)

    conv_p = xbc_p.reshape(batch, seq, CONV_DIM)[:, seq - (CONV_W - 1):]
    conv_s = xbc_s.reshape(nseq, dec_seq, CONV_DIM)[:, dec_seq - (CONV_W - 1):]
    st_shape = (HEADS, HEAD_DIM, D_STATE)
    return (
        y_p.reshape(batch, seq, D_MODEL),
        y_s.reshape(nseq, dec_seq, D_MODEL),
        conv_p[None],
        ssm_p.reshape(1, batch, *st_shape),
        conv_s[None],
        ssm_s.reshape(1, nseq, *st_shape),
        v_s.reshape(1, nseq, dec_seq, D_GMLP),
    )
```

```python
import functools

import numpy as np
import jax
import jax.numpy as jnp
from jax import lax
from jax.experimental import pallas as pl
from jax.experimental.pallas import tpu as pltpu

f32, bf16, i32, u32 = jnp.float32, jnp.bfloat16, jnp.int32, jnp.uint32

D_MODEL = 1024
D_INNER = 2048
HEAD_DIM = 64
HEADS = 32
GROUPS = 4
HEADS_PER_GROUP = 8
GROUP_WIDTH = HEADS_PER_GROUP * HEAD_DIM
D_STATE = 128
CONV_W = 4
CONV_DIM = D_INNER + 2 * GROUPS * D_STATE
CHUNK = 128
SUBLANES = 8
D_GMLP = 1024
GMLP_HEADS = 8
N_EXPERTS = 32
TOP_K = 4
D_FF = 1024
SWIGLU_ALPHA = 1.702
SWIGLU_LIMIT = 7.0
EPS = 1e-5
MOE_BLOCK = 512
DEINT = 256
ROW_DMA_TILE = 256
MIX_TILE = 512
D_IN_PROJ = 3 * D_INNER + CONV_DIM + HEADS
PROJ_TN = 512
PROMPT_SEQS_PER_STEP = 2
SAMPLE_SEQS_PER_STEP = 8
D_PACK = D_MODEL // 2
VMEM_LIMIT = 56 * 1024 * 1024

_NT = (((1,), (1,)), ((), ()))
_HI16 = np.uint32(0xFFFF0000)


def _params(*sem):
    return pltpu.CompilerParams(dimension_semantics=sem, vmem_limit_bytes=VMEM_LIMIT)


def _split(x, parts):
    out = []
    for _ in range(parts - 1):
        p = x.astype(bf16)
        out.append(p)
        x = x - p.astype(f32)
    out.append(x.astype(bf16))
    return out


def _dot_sel_rhs(x, sel, parts=3):
    return sum(jnp.dot(p, sel, preferred_element_type=f32) for p in _split(x, parts))


def _dot_sel_lhs(sel, x, parts=3):
    return sum(jnp.dot(sel, p, preferred_element_type=f32) for p in _split(x, parts))


def _silu(x):
    return x * jax.nn.sigmoid(x)


def _pack_bf16_pair(x):
    n = x.shape[1] // 2
    lo = lax.bitcast_convert_type(x[:, :n].astype(bf16).astype(f32), u32) >> 16
    hi = lax.bitcast_convert_type(x[:, n:].astype(bf16).astype(f32), u32) & _HI16
    return lo | hi


def _unpack_bf16_pair(p):
    lo = lax.bitcast_convert_type(p << 16, f32).astype(bf16)
    hi = lax.bitcast_convert_type(p & _HI16, f32).astype(bf16)
    return jnp.concatenate([lo, hi], axis=1)


def _in_proj_body(x_ref, nw_ref, wz_ref, wuv_ref, wg_ref, wxbc_ref, wdt_ref, wdtT_ref,
                  z_ref, uv_ref, g_ref, xbc_ref, dt_ref, dtT_ref):
    x = x_ref[...]
    ms = jnp.mean(x * x, axis=-1, keepdims=True)
    hn = (x * lax.rsqrt(ms + EPS) * nw_ref[...]).astype(bf16)
    dt_ref[...] = jnp.dot(hn, wdt_ref[...], preferred_element_type=f32)
    dtT_ref[...] = lax.dot_general(wdtT_ref[...], hn, _NT, preferred_element_type=f32)
    for o_ref, w_ref in ((z_ref, wz_ref), (uv_ref, wuv_ref), (g_ref, wg_ref), (xbc_ref, wxbc_ref)):
        for c in range(0, w_ref.shape[1], PROJ_TN):
            acc = jnp.dot(hn, w_ref[:, c:c + PROJ_TN], preferred_element_type=f32)
            o_ref[:, c:c + PROJ_TN] = acc.astype(o_ref.dtype)


def _in_proj(x2d, norm_w, w_segments, w_dt, w_dtT):
    t = x2d.shape[0]
    tm = min(512, t)
    row = lambda i: (i, 0)
    fixed = lambda i: (0, 0)
    return pl.pallas_call(
        _in_proj_body,
        grid=(t // tm,),
        in_specs=[
            pl.BlockSpec((tm, D_MODEL), row),
            pl.BlockSpec((1, D_MODEL), fixed),
        ] + [pl.BlockSpec(w.shape, fixed, pipeline_mode=pl.Buffered(1)) for w in w_segments] + [
            pl.BlockSpec((D_MODEL, HEADS), fixed),
            pl.BlockSpec((HEADS, D_MODEL), fixed),
        ],
        out_specs=[
            pl.BlockSpec((tm, D_INNER), row),
            pl.BlockSpec((tm, D_INNER), row),
            pl.BlockSpec((tm, D_INNER), row),
            pl.BlockSpec((tm, CONV_DIM), row),
            pl.BlockSpec((tm, HEADS), row),
            pl.BlockSpec((HEADS, tm), lambda i: (0, i)),
        ],
        out_shape=[
            jax.ShapeDtypeStruct((t, D_INNER), bf16),
            jax.ShapeDtypeStruct((t, D_INNER), bf16),
            jax.ShapeDtypeStruct((t, D_INNER), bf16),
            jax.ShapeDtypeStruct((t, CONV_DIM), f32),
            jax.ShapeDtypeStruct((t, HEADS), f32),
            jax.ShapeDtypeStruct((HEADS, t), f32),
        ],
        compiler_params=_params("arbitrary"),
        name="in_proj",
    )(x2d, norm_w, *w_segments, w_dt, w_dtT)


def _conv_taps(shifted, cw_ref, cb_ref):
    acc = cb_ref[...]
    for k in range(CONV_W):
        acc = acc + shifted(CONV_W - 1 - k) * cw_ref[k:k + 1, :]
    return _silu(acc)


def _conv_silu_one_seq(cur, prev8, cw_ref, cb_ref):
    row8 = lax.broadcasted_iota(i32, (SUBLANES, 1), 0)

    def shifted(j):
        if j == 0:
            return cur
        xr = pltpu.roll(cur, j, 0)
        top = jnp.where(row8 >= j, xr[:SUBLANES], pltpu.roll(prev8, j, 0))
        return jnp.concatenate([top, xr[SUBLANES:]], axis=0)

    return _conv_taps(shifted, cw_ref, cb_ref)


def _conv_silu_many_seq(cur, pre, cw_ref, cb_ref, seq_len):
    n = cur.shape[0]
    pos = lax.broadcasted_iota(i32, (n, 1), 0) % seq_len

    def shifted(j):
        if j == 0:
            return cur
        return jnp.where(pos >= j, pltpu.roll(cur, j, 0), pltpu.roll(pre, (j + n - seq_len) % n, 0))

    return _conv_taps(shifted, cw_ref, cb_ref)


def _ssd_block(xa, dt_raw, dtT_raw, c):
    tril = c["tril"][...]
    dt = jax.nn.softplus(dt_raw + c["dtb_row"][...])
    dtT = jax.nn.softplus(dtT_raw + c["dtb_col"][...])
    a = dt * c["aneg_row"][...]
    aT = dtT * c["aneg_col"][...]
    a_cum = _dot_sel_lhs(tril, a)
    a_cumT = _dot_sel_rhs(aT, c["triu"][...])
    a_tot = _dot_sel_lhs(c["same"][...], a)
    stack = jnp.concatenate([dt, jnp.exp(a_cum), jnp.exp(a_tot - a_cum)], axis=0)
    ex = _dot_sel_rhs(stack, c["expand"][...], parts=2)
    n = xa.shape[0]
    dtx, eax, dex = ex[:n], ex[n:2 * n], ex[2 * n:]

    xs = xa[:, :D_INNER]
    bm = xa[:, D_INNER:D_INNER + GROUPS * D_STATE]
    cm = xa[:, D_INNER + GROUPS * D_STATE:]
    xdt = xs * dtx
    xdt_bf = xdt.astype(bf16)
    mask = tril > 0
    lane = lax.broadcasted_iota(i32, (1, 2 * HEAD_DIM), 1)
    ys = []
    for g in range(GROUPS):
        cg = cm[:, g * D_STATE:(g + 1) * D_STATE].astype(bf16)
        bg = bm[:, g * D_STATE:(g + 1) * D_STATE].astype(bf16)
        cb = lax.dot_general(cg, bg, _NT, preferred_element_type=f32)
        for pair in range(HEADS_PER_GROUP // 2):
            halves = []
            for h in (g * HEADS_PER_GROUP + 2 * pair, g * HEADS_PER_GROUP + 2 * pair + 1):
                seg = a_cum[:, h:h + 1] - a_cumT[h:h + 1, :]
                decay = jnp.exp(jnp.where(mask, seg, -jnp.inf))
                m = (cb * decay).astype(bf16)
                col = (h // 2) * 2 * HEAD_DIM
                halves.append(jnp.dot(m, xdt_bf[:, col:col + 2 * HEAD_DIM], preferred_element_type=f32))
            ys.append(jnp.where(lane < HEAD_DIM, halves[0], halves[1]))
    y_diag = jnp.concatenate(ys, axis=1)
    return dict(xs=xs, bm=bm, cm=cm, xdt=xdt, xd=xdt * dex, eax=eax, aT=aT, y_diag=y_diag)


def _gated_norm(y, z, gw):
    g = y * _silu(z.astype(f32))
    ms = jnp.mean(g * g, axis=-1, keepdims=True)
    return (g * lax.rsqrt(ms + EPS) * gw).astype(bf16)


_SSD_CONST_NAMES = ("cw", "cb", "dtb_row", "dtb_col", "aneg_row", "aneg_col", "dskipx", "gw",
                    "tril", "triu", "same", "expand")


def _ssd_consts(conv_w, conv_b, dt_bias, a_log, d_skip, gnorm_w, seq_len):
    r = np.arange(CHUNK)
    same = (r[:, None] // seq_len) == (r[None, :] // seq_len)
    tril = same & (r[None, :] <= r[:, None])
    expand = np.repeat(np.eye(HEADS, dtype=np.float32), HEAD_DIM, axis=1)
    aneg = -jnp.exp(a_log.astype(f32))
    return dict(
        cw=conv_w, cb=conv_b.reshape(1, CONV_DIM),
        dtb_row=dt_bias.reshape(1, HEADS), dtb_col=dt_bias.reshape(HEADS, 1),
        aneg_row=aneg.reshape(1, HEADS), aneg_col=aneg.reshape(HEADS, 1),
        dskipx=jnp.repeat(d_skip.astype(f32), HEAD_DIM).reshape(1, D_INNER),
        gw=gnorm_w.reshape(1, D_INNER),
        tril=jnp.asarray(tril, bf16), triu=jnp.asarray(tril.T, bf16), same=jnp.asarray(same, bf16),
        expand=jnp.asarray(expand, bf16),
    )


def _const_specs(consts):
    zero = (lambda *_: (0, 0))
    return [pl.BlockSpec(consts[k].shape, zero) for k in _SSD_CONST_NAMES]


def _ssd_prompt_body(nb, z_ref, xbc_ref, dt_ref, dtT_ref, *rest):
    nc = len(_SSD_CONST_NAMES)
    c = dict(zip(_SSD_CONST_NAMES, rest[:nc]))
    y_ref, state_ref, prev_ref, st_ref = rest[nc:]
    ci = pl.program_id(1)

    @pl.when(ci == 0)
    def _():
        prev_ref[...] = jnp.zeros_like(prev_ref)
        st_ref[...] = jnp.zeros_like(st_ref)

    for s in range(nb):
        cur = xbc_ref[s]
        xa = _conv_silu_one_seq(cur, prev_ref[s], c["cw"], c["cb"])
        prev_ref[s] = cur[CHUNK - SUBLANES:]
        b = _ssd_block(xa, dt_ref[s], dtT_ref[s], c)

        y_off = []
        for g in range(GROUPS):
            sl = slice(g * GROUP_WIDTH, (g + 1) * GROUP_WIDTH)
            st = st_ref[s * GROUPS + g]
            cg = b["cm"][:, g * D_STATE:(g + 1) * D_STATE].astype(bf16)
            y_off.append(jnp.dot(cg, st.astype(bf16), preferred_element_type=f32))
            bgT = b["bm"][:, g * D_STATE:(g + 1) * D_STATE].T.astype(bf16)
            upd = jnp.dot(bgT, b["xd"][:, sl].astype(bf16), preferred_element_type=f32)
            st_ref[s * GROUPS + g] = st * b["eax"][CHUNK - 1:CHUNK, sl] + upd
        y = b["y_diag"] + jnp.concatenate(y_off, axis=1) * b["eax"] + c["dskipx"][...] * b["xs"]
        y_ref[s] = _gated_norm(y, z_ref[s], c["gw"][...])

    @pl.when(ci == pl.num_programs(1) - 1)
    def _():
        for s in range(nb):
            for g in range(GROUPS):
                state_ref[s, 0, g * GROUP_WIDTH:(g + 1) * GROUP_WIDTH, :] = st_ref[s * GROUPS + g].T


def _ssd_prompt(z, xbc, dt, dtT, consts, batch, seq):
    nb = PROMPT_SEQS_PER_STEP if batch % PROMPT_SEQS_PER_STEP == 0 else 1
    nchunk = seq // CHUNK
    part = batch // nb * seq
    split = lambda a: a.reshape(nb, part, a.shape[-1])
    dtT3 = dtT.reshape(HEADS, nb, part).transpose(1, 0, 2)
    row = lambda b, ci: (0, b * nchunk + ci, 0)
    y, state = pl.pallas_call(
        functools.partial(_ssd_prompt_body, nb),
        grid=(batch // nb, nchunk),
        in_specs=[
            pl.BlockSpec((nb, CHUNK, D_INNER), row),
            pl.BlockSpec((nb, CHUNK, CONV_DIM), row),
            pl.BlockSpec((nb, CHUNK, HEADS), row),
            pl.BlockSpec((nb, HEADS, CHUNK), lambda b, ci: (0, 0, b * nchunk + ci)),
        ] + _const_specs(consts),
        out_specs=[
            pl.BlockSpec((nb, CHUNK, D_INNER), row),
            pl.BlockSpec((nb, 1, D_INNER, D_STATE), lambda b, ci: (0, b, 0, 0)),
        ],
        out_shape=[
            jax.ShapeDtypeStruct((nb, part, D_INNER), bf16),
            jax.ShapeDtypeStruct((nb, batch // nb, D_INNER, D_STATE), f32),
        ],
        scratch_shapes=[pltpu.VMEM((nb, SUBLANES, CONV_DIM), f32),
                        pltpu.VMEM((nb * GROUPS, D_STATE, GROUP_WIDTH), f32)],
        compiler_params=_params("arbitrary", "arbitrary"),
        name="ssd_prompt",
    )(split(z), split(xbc), split(dt), dtT3, *[consts[k] for k in _SSD_CONST_NAMES])
    return y.reshape(batch * seq, D_INNER), state.reshape(batch, D_INNER, D_STATE)


def _ssd_sample_body(seq_len, z_ref, xbc_ref, pre_ref, dt_ref, dtT_ref, state_ref, *rest):
    nc = len(_SSD_CONST_NAMES)
    c = dict(zip(_SSD_CONST_NAMES, rest[:nc]))
    y_ref, state_out_ref, c_ref, b_ref, xdT_ref, yacc_ref, eax_ref, eatT_ref = rest[nc:]
    step = pl.program_id(1)
    per_step = state_ref.shape[0]

    @pl.when(step == 0)
    def _():
        xa = _conv_silu_many_seq(xbc_ref[...], pre_ref[...], c["cw"], c["cb"], seq_len)
        b = _ssd_block(xa, dt_ref[...], dtT_ref[...], c)
        c_ref[...] = b["cm"]
        b_ref[...] = b["bm"]
        xdT_ref[...] = b["xd"].T.astype(bf16)
        yacc_ref[...] = b["y_diag"] + c["dskipx"][...] * b["xs"]
        eax_ref[...] = b["eax"]
        eatT_ref[...] = jnp.exp(_dot_sel_rhs(b["aT"], c["same"][...]))

    def one_sequence(q, carry):
        s = step * per_step + q
        r0 = pl.multiple_of(s * seq_len, seq_len)
        rows = pl.ds(r0, seq_len)
        lane = lax.broadcasted_iota(i32, (1, CHUNK), 1)
        arep = jnp.broadcast_to(jnp.sum(jnp.where(lane == r0, eatT_ref[...], 0.0), axis=1, keepdims=True), (HEADS, D_STATE))
        rmask = (lax.broadcasted_iota(i32, (CHUNK, 1), 0) // seq_len) == s
        for g in range(GROUPS):
            gs = slice(g * GROUP_WIDTH, (g + 1) * GROUP_WIDTH)
            ns = slice(g * D_STATE, (g + 1) * D_STATE)
            s0 = state_ref[q, gs, :]
            cg = c_ref[rows, ns].astype(bf16)
            yo = lax.dot_general(cg, s0.astype(bf16), _NT, preferred_element_type=f32)
            yacc_ref[rows, gs] = yacc_ref[rows, gs] + yo * eax_ref[rows, gs]
            bmask = jnp.where(rmask, b_ref[:, ns], 0.0).astype(bf16)
            upd = jnp.dot(xdT_ref[gs, :], bmask, preferred_element_type=f32)
            for r in range(HEADS_PER_GROUP):
                h = g * HEADS_PER_GROUP + r
                hs = slice(r * HEAD_DIM, (r + 1) * HEAD_DIM)
                state_out_ref[q, h * HEAD_DIM:(h + 1) * HEAD_DIM, :] = s0[hs] * arep[h:h + 1, :] + upd[hs]
        return carry

    lax.fori_loop(0, per_step, one_sequence, 0)

    @pl.when(step == pl.num_programs(1) - 1)
    def _():
        y_ref[...] = _gated_norm(yacc_ref[...], z_ref[...], c["gw"][...])


def _ssd_sample(z, xbc, pre, dt, dtT, state, consts, nseq, seq_len):
    per = CHUNK // seq_len
    nblk = nseq // per
    t = nseq * seq_len
    q = SAMPLE_SEQS_PER_STEP if per % SAMPLE_SEQS_PER_STEP == 0 else 1
    blk = lambda i, s: (i, 0)
    seq3 = lambda i, s: (i * (per // q) + s, 0, 0)
    return pl.pallas_call(
        functools.partial(_ssd_sample_body, seq_len),
        grid=(nblk, per // q),
        in_specs=[
            pl.BlockSpec((CHUNK, D_INNER), blk),
            pl.BlockSpec((CHUNK, CONV_DIM), blk),
            pl.BlockSpec((CHUNK, CONV_DIM), blk),
            pl.BlockSpec((CHUNK, HEADS), blk),
            pl.BlockSpec((HEADS, CHUNK), lambda i, s: (0, i)),
            pl.BlockSpec((q, D_INNER, D_STATE), seq3),
        ] + _const_specs(consts),
        out_specs=[
            pl.BlockSpec((CHUNK, D_INNER), blk),
            pl.BlockSpec((q, D_INNER, D_STATE), seq3),
        ],
        out_shape=[
            jax.ShapeDtypeStruct((t, D_INNER), bf16),
            jax.ShapeDtypeStruct((nseq, D_INNER, D_STATE), f32),
        ],
        scratch_shapes=[
            pltpu.VMEM((CHUNK, GROUPS * D_STATE), f32),
            pltpu.VMEM((CHUNK, GROUPS * D_STATE), f32),
            pltpu.VMEM((D_INNER, CHUNK), bf16),
            pltpu.VMEM((CHUNK, D_INNER), f32),
            pltpu.VMEM((CHUNK, D_INNER), f32),
            pltpu.VMEM((HEADS, CHUNK), f32),
        ],
        compiler_params=_params("arbitrary", "arbitrary"),
        name="ssd_sample",
    )(z, xbc, pre, dt, dtT, state, *[consts[k] for k in _SSD_CONST_NAMES])


_MIX_CONST_NAMES = ("ws", "bs", "vnw", "vnb", "wbs", "wbm", "wo", "nffn", "wrT", "br", "tris")


def _mix_body(emit_v, uv_ref, gates_ref, yssd_ref, x_ref, *rest):
    nc = len(_MIX_CONST_NAMES)
    c = dict(zip(_MIX_CONST_NAMES, rest[:nc]))
    outs = list(rest[nc:])
    h_ref, hfp_ref = outs[:2]
    v_ref = outs[2] if emit_v else None
    te_ref, gate_ref, rank_ref, cnt_ref, cnt_acc = outs[-5:]
    tm = uv_ref.shape[0]

    @pl.when(pl.program_id(0) == 0)
    def _():
        cnt_acc[...] = jnp.zeros_like(cnt_acc)

    uv = uv_ref[...].astype(f32)
    uv = 0.5 * uv * (1.0 + lax.erf(uv * np.float32(np.sqrt(0.5))))
    u, v = uv[:, :D_GMLP], uv[:, D_GMLP:]
    mu = jnp.mean(v, axis=-1, keepdims=True)
    vc = v - mu
    vn = vc * lax.rsqrt(jnp.mean(vc * vc, axis=-1, keepdims=True) + EPS) * c["vnw"][...] + c["vnb"][...]
    if emit_v:
        v_ref[...] = vn
    vn_bf = vn.astype(bf16)
    gd = D_GMLP // GMLP_HEADS
    rows = []
    for ck in range(tm // CHUNK):
        rs = slice(ck * CHUNK, (ck + 1) * CHUNK)
        heads = []
        for g in range(GMLP_HEADS):
            mixed = jnp.dot(c["ws"][g], vn_bf[rs, g * gd:(g + 1) * gd], preferred_element_type=f32)
            heads.append(mixed + c["bs"][:, g:g + 1])
        rows.append(jnp.concatenate(heads, axis=1))
    y_mlp = u * jnp.concatenate(rows, axis=0)

    a = jnp.dot(yssd_ref[...], c["wbs"][...], preferred_element_type=f32)
    b = jnp.dot(y_mlp.astype(bf16), c["wbm"][...], preferred_element_type=f32)
    gs = jax.nn.sigmoid(gates_ref[...].astype(f32))
    merged = gs[:, :D_MODEL] * a + gs[:, D_MODEL:] * b
    h = x_ref[...] + jnp.dot(merged.astype(bf16), c["wo"][...], preferred_element_type=f32)
    h_ref[...] = h
    hf = h * lax.rsqrt(jnp.mean(h * h, axis=-1, keepdims=True) + EPS) * c["nffn"][...]
    hfp_ref[...] = _pack_bf16_pair(hf)

    lg = lax.dot_general(c["wrT"][...], hf, _NT, precision=lax.Precision.HIGHEST, preferred_element_type=f32) + c["br"][...]
    sub = lax.broadcasted_iota(i32, lg.shape, 0)
    idxs, vals = [], []
    for _ in range(TOP_K):
        m = jnp.max(lg, axis=0, keepdims=True)
        idx = jnp.min(jnp.where(lg == m, sub, N_EXPERTS), axis=0, keepdims=True)
        idxs.append(idx)
        vals.append(m)
        lg = jnp.where(sub == idx, -jnp.inf, lg)
    p = jnp.exp(jnp.concatenate(vals, axis=0) - vals[0])
    gate_ref[...] = p / jnp.sum(p, axis=0, keepdims=True)
    te_ref[...] = jnp.concatenate(idxs, axis=0)

    onehots = [sub == idx for idx in idxs]
    member = functools.reduce(jnp.logical_or, onehots).astype(f32)
    before = jnp.dot(member.astype(bf16), c["tris"][...], preferred_element_type=f32) + cnt_acc[:, 0:1]
    rank_ref[...] = jnp.concatenate(
        [jnp.sum(jnp.where(oh, before, 0.0), axis=0, keepdims=True) for oh in onehots], axis=0).astype(i32)
    cnt_acc[...] = cnt_acc[...] + jnp.sum(member, axis=1, keepdims=True)
    cnt_ref[...] = cnt_acc[...].astype(i32)


def _mix(uv, gates, yssd, x2d, mc, tm, emit_v):
    t = x2d.shape[0]
    row = lambda i: (i, 0)
    col = lambda i: (0, i)
    const = lambda a: pl.BlockSpec(a.shape, lambda i: (0,) * a.ndim, pipeline_mode=pl.Buffered(1))
    v_spec = [pl.BlockSpec((tm, D_GMLP), row)] if emit_v else []
    v_shape = [jax.ShapeDtypeStruct((t, D_GMLP), f32)] if emit_v else []
    return pl.pallas_call(
        functools.partial(_mix_body, emit_v),
        grid=(t // tm,),
        in_specs=[
            pl.BlockSpec((tm, D_INNER), row),
            pl.BlockSpec((tm, D_INNER), row),
            pl.BlockSpec((tm, D_INNER), row),
            pl.BlockSpec((tm, D_MODEL), row),
        ] + [const(mc[k]) for k in _MIX_CONST_NAMES],
        out_specs=[pl.BlockSpec((tm, D_MODEL), row), pl.BlockSpec((tm, D_PACK), row)] + v_spec + [
            pl.BlockSpec((TOP_K, tm), col),
            pl.BlockSpec((TOP_K, tm), col),
            pl.BlockSpec((TOP_K, tm), col),
            pl.BlockSpec((N_EXPERTS, 128), lambda i: (0, 0)),
        ],
        out_shape=[
            jax.ShapeDtypeStruct((t, D_MODEL), f32),
            jax.ShapeDtypeStruct((t, D_PACK), u32),
        ] + v_shape + [
            jax.ShapeDtypeStruct((TOP_K, t), i32),
            jax.ShapeDtypeStruct((TOP_K, t), f32),
            jax.ShapeDtypeStruct((TOP_K, t), i32),
            jax.ShapeDtypeStruct((N_EXPERTS, 128), i32),
        ],
        scratch_shapes=[pltpu.VMEM((N_EXPERTS, 128), f32)],
        compiler_params=_params("arbitrary"),
        name="mix_route",
    )(uv, gates, yssd, x2d, *[mc[k] for k in _MIX_CONST_NAMES])


def _dispatch_body(tm, n_p_tiles, slot_ref, pend_ref, padded_ref, hfp_ref, hfs_ref, xs_ref, zero_ref, sem):
    i = pl.program_id(0)

    def tail_copy(e):
        start = pl.multiple_of(pend_ref[e] - MOE_BLOCK, MOE_BLOCK)
        return pltpu.make_async_copy(zero_ref, xs_ref.at[pl.ds(start, MOE_BLOCK), :], sem)

    def unused_copy(j):
        return pltpu.make_async_copy(zero_ref, xs_ref.at[pl.ds(j * MOE_BLOCK, MOE_BLOCK), :], sem)

    @pl.when(i == 0)
    def _():
        zero_ref[...] = jnp.zeros_like(zero_ref)
        n_blocks = xs_ref.shape[0] // MOE_BLOCK
        for start_or_wait in ("start", "wait"):
            for e in range(N_EXPERTS):
                @pl.when(padded_ref[e] > 0)
                def _():
                    getattr(tail_copy(e), start_or_wait)()
            for j in range(n_blocks - N_EXPERTS, n_blocks):
                @pl.when(j * MOE_BLOCK >= pend_ref[N_EXPERTS - 1])
                def _():
                    getattr(unused_copy(j), start_or_wait)()

    def scatter(src_ref):
        for r in range(tm):
            for k in range(TOP_K):
                slot = slot_ref[(i * tm + r) * TOP_K + k]
                pltpu.make_async_copy(src_ref.at[r], xs_ref.at[slot], sem).start(priority=k % 2)
        for r in range(tm):
            for k in range(TOP_K):
                pltpu.make_async_copy(src_ref.at[r], xs_ref.at[0], sem).wait()

    @pl.when(i < n_p_tiles)
    def _():
        scatter(hfp_ref)

    @pl.when(i >= n_p_tiles)
    def _():
        scatter(hfs_ref)


def _dispatch(slot_flat, pend, padded, hf_p, hf_s, n_slots, tm):
    n_p, n_s = hf_p.shape[0] // tm, hf_s.shape[0] // tm
    return pl.pallas_call(
        functools.partial(_dispatch_body, tm, n_p),
        grid_spec=pltpu.PrefetchScalarGridSpec(
            num_scalar_prefetch=3,
            grid=(n_p + n_s,),
            in_specs=[
                pl.BlockSpec((tm, D_PACK), lambda i, *_: (jnp.minimum(i, n_p - 1), 0)),
                pl.BlockSpec((tm, D_PACK), lambda i, *_: (jnp.maximum(i - n_p, 0), 0)),
            ],
            out_specs=pl.BlockSpec(memory_space=pl.ANY),
            scratch_shapes=[pltpu.VMEM((MOE_BLOCK, D_PACK), u32), pltpu.SemaphoreType.DMA],
        ),
        out_shape=jax.ShapeDtypeStruct((n_slots, D_PACK), u32),
        compiler_params=_params("arbitrary"),
        name="moe_dispatch",
    )(slot_flat, pend, padded, hf_p, hf_s)


def _experts_body(be_ref, nused_ref, start_ref, slot_ref, next_ref, xs_ref, bgu_ref, bd_ref, perm_ref, wgu_hbm, wd_hbm,
                  ys_ref, wgu_f32, wd_f32, wgu_bf, wd_bf, sems):
    i = pl.program_id(0)

    def weight_copies(e, slot):
        return (pltpu.make_async_copy(wgu_hbm.at[e], wgu_f32.at[slot], sems.at[0, slot]),
                pltpu.make_async_copy(wd_hbm.at[e], wd_f32.at[slot], sems.at[1, slot]))

    @pl.when(i < nused_ref[0])
    def _():
        @pl.when(start_ref[i] == 1)
        def _():
            slot = slot_ref[i]

            @pl.when(i == 0)
            def _():
                for cp in weight_copies(be_ref[0], 0):
                    cp.start()

            for cp in weight_copies(be_ref[i], slot):
                cp.wait()

            @pl.when(next_ref[i] >= 0)
            def _():
                for cp in weight_copies(next_ref[i], 1 - slot):
                    cp.start()

            wd_bf[...] = wd_f32[slot].astype(bf16)
            half = DEINT // 2
            for j in range(2 * D_FF // DEINT):
                w = jnp.dot(wgu_f32[slot, :, j * DEINT:(j + 1) * DEINT].astype(bf16), perm_ref[...],
                            preferred_element_type=f32).astype(bf16)
                wgu_bf[:, j * half:(j + 1) * half] = w[:, :half]
                wgu_bf[:, D_FF + j * half:D_FF + (j + 1) * half] = w[:, half:]

        gu = jnp.dot(_unpack_bf16_pair(xs_ref[...]), wgu_bf[...], preferred_element_type=f32) + bgu_ref[0]
        glu = jnp.minimum(gu[:, :D_FF], SWIGLU_LIMIT)
        lin = jnp.clip(gu[:, D_FF:], -SWIGLU_LIMIT, SWIGLU_LIMIT)
        act = glu * jax.nn.sigmoid(SWIGLU_ALPHA * glu) * (lin + 1.0)
        ys_ref[...] = jnp.dot(act.astype(bf16), wd_bf[...], preferred_element_type=f32) + bd_ref[0]

    @pl.when(i >= nused_ref[0])
    def _():
        ys_ref[...] = jnp.zeros_like(ys_ref)


def _expert_segments(block_e, n_used, n_blocks):
    idx = jnp.arange(n_blocks, dtype=i32)
    prev_e = jnp.concatenate([block_e[:1], block_e[:-1]])
    start = (idx < n_used) & ((idx == 0) | (block_e != prev_e))
    slot = (jnp.cumsum(start.astype(i32)) - 1) % 2
    start_idx = jnp.where(start, idx, n_blocks)
    from_here = lax.cummin(start_idx, reverse=True)
    next_start = jnp.concatenate([from_here[1:], jnp.full((1,), n_blocks, i32)])
    next_e = jnp.sum(jnp.where(next_start[:, None] == idx[None, :], block_e[None, :], 0), axis=1)
    next_e = jnp.where(next_start < n_blocks, next_e, -1)
    return start.astype(i32), slot.astype(i32), next_e.astype(i32)


def _experts(block_e, n_used, xs, wgu, bgu, w_down, b_down):
    n_blocks = xs.shape[0] // MOE_BLOCK
    start, slot, next_e = _expert_segments(block_e, n_used[0], n_blocks)
    blk = lambda i, be, nu, *_: (jnp.maximum(jnp.minimum(i, nu[0] - 1), 0), 0)
    exp3 = lambda i, be, *_: (be[i], 0, 0)
    c = np.arange(DEINT)
    src = np.where(c < DEINT // 2, 2 * c, 2 * (c - DEINT // 2) + 1)
    perm = jnp.asarray(np.arange(DEINT)[:, None] == src[None, :], bf16)
    return pl.pallas_call(
        _experts_body,
        grid_spec=pltpu.PrefetchScalarGridSpec(
            num_scalar_prefetch=5,
            grid=(n_blocks,),
            in_specs=[
                pl.BlockSpec((MOE_BLOCK, D_PACK), blk),
                pl.BlockSpec((1, 1, 2 * D_FF), exp3),
                pl.BlockSpec((1, 1, D_MODEL), exp3),
                pl.BlockSpec((DEINT, DEINT), lambda i, *_: (0, 0)),
                pl.BlockSpec(memory_space=pl.ANY),
                pl.BlockSpec(memory_space=pl.ANY),
            ],
            out_specs=pl.BlockSpec((MOE_BLOCK, D_MODEL), lambda i, *_: (i, 0)),
            scratch_shapes=[
                pltpu.VMEM((2, D_MODEL, 2 * D_FF), f32),
                pltpu.VMEM((2, D_FF, D_MODEL), f32),
                pltpu.VMEM((D_MODEL, 2 * D_FF), bf16),
                pltpu.VMEM((D_FF, D_MODEL), bf16),
                pltpu.SemaphoreType.DMA((2, 2)),
            ],
        ),
        out_shape=jax.ShapeDtypeStruct((xs.shape[0], D_MODEL), f32),
        compiler_params=_params("arbitrary"),
        name="moe_experts",
    )(block_e, n_used, start, slot, next_e, xs, bgu, b_down, perm, wgu, w_down)


def _combine_body(tm, slot_ref, h_ref, gate_ref, nf_ref, ys_ref, o_ref, rows_ref, sems):
    s = pl.program_id(0)
    n_tiles = pl.num_programs(0) - 1

    def request():
        buf = s % 2
        for r in range(tm):
            for k in range(TOP_K):
                slot = slot_ref[(s * tm + r) * TOP_K + k]
                pltpu.make_async_copy(ys_ref.at[slot], rows_ref.at[buf, k, r], sems.at[buf]).start(priority=k % 2)

    def finish():
        buf = (s - 1) % 2
        for r in range(tm):
            for k in range(TOP_K):
                pltpu.make_async_copy(ys_ref.at[0], rows_ref.at[buf, k, r], sems.at[buf]).wait()
        g = gate_ref[...]
        moe = g[:, 0:1] * rows_ref[buf, 0]
        for k in range(1, TOP_K):
            moe = moe + g[:, k:k + 1] * rows_ref[buf, k]
        h = h_ref[...] + moe
        o_ref[...] = h * lax.rsqrt(jnp.mean(h * h, axis=-1, keepdims=True) + EPS) * nf_ref[...]

    @pl.when(s == 0)
    def _():
        request()

    @pl.when((s > 0) & (s < n_tiles))
    def _():
        finish()
        request()

    @pl.when(s == n_tiles)
    def _():
        finish()


def _combine(slot_flat, h, gate_t, norm_final, ys, tm):
    t = h.shape[0]
    return pl.pallas_call(
        functools.partial(_combine_body, tm),
        grid_spec=pltpu.PrefetchScalarGridSpec(
            num_scalar_prefetch=1,
            grid=(t // tm + 1,),
            in_specs=[
                pl.BlockSpec((tm, D_MODEL), lambda i, s: (jnp.maximum(i - 1, 0), 0)),
                pl.BlockSpec((tm, TOP_K), lambda i, s: (jnp.maximum(i - 1, 0), 0)),
                pl.BlockSpec((1, D_MODEL), lambda i, s: (0, 0)),
                pl.BlockSpec(memory_space=pl.ANY),
            ],
            out_specs=pl.BlockSpec((tm, D_MODEL), lambda i, s: (jnp.maximum(i - 1, 0), 0)),
            scratch_shapes=[pltpu.VMEM((2, TOP_K, tm, D_MODEL), f32), pltpu.SemaphoreType.DMA((2,))],
        ),
        out_shape=jax.ShapeDtypeStruct((t, D_MODEL), f32),
        compiler_params=_params("arbitrary"),
        name="moe_combine",
    )(slot_flat, h, gate_t, norm_final, ys)


def kernel(x_prompt, x_sample, state_conv, state_ssm, norm_mix, w_in, conv_w, conv_b, dt_bias, a_log, d_skip, gnorm_w, v_norm_w, v_norm_b, w_spatial, b_spatial, w_branch_ssd, w_branch_mlp, w_out, norm_ffn, w_router, b_router, w_gu, b_gu, w_down, b_down, norm_final):
    assert w_in.shape[0] == 1, "single-layer trunk"
    batch, seq, _ = x_prompt.shape
    nseq, dec_seq, _ = x_sample.shape
    assert seq % CHUNK == 0 and CHUNK % dec_seq == 0 and nseq % (CHUNK // dec_seq) == 0 and dec_seq >= CONV_W - 1
    t_p, t_s = batch * seq, nseq * dec_seq
    tm = ROW_DMA_TILE
    assert t_p % tm == 0 and t_s % tm == 0
    tm_mix = MIX_TILE if t_p % MIX_TILE == 0 and t_s % MIX_TILE == 0 else tm

    wi = w_in[0]
    z0, x0, d0, u0 = D_INNER, D_INNER + CONV_DIM, D_INNER + CONV_DIM + HEADS, D_INNER + CONV_DIM + HEADS + 2 * D_GMLP
    w_segments = tuple(wi[:, a:b].astype(bf16) for a, b in ((0, z0), (d0, u0), (u0, D_IN_PROJ), (z0, x0)))
    w_dt = wi[:, x0:d0].astype(bf16)
    w_dtT = w_dt.T
    nm = norm_mix[0].reshape(1, D_MODEL)
    sc = lambda L: _ssd_consts(conv_w[0], conv_b[0], dt_bias[0], a_log[0], d_skip[0], gnorm_w[0], L)
    xp2, xs2 = x_prompt.reshape(t_p, D_MODEL), x_sample.reshape(t_s, D_MODEL)

    tril = np.tril(np.ones((CHUNK, CHUNK), bool))
    ws_p = jnp.where(tril[None], w_spatial[0], 0).astype(bf16)
    per = CHUNK // dec_seq
    blockdiag = (np.arange(CHUNK)[:, None] // dec_seq) == (np.arange(CHUNK)[None, :] // dec_seq)
    ws_s = jnp.where((tril & blockdiag)[None], jnp.tile(w_spatial[0][:, :dec_seq, :dec_seq], (1, per, per)), 0).astype(bf16)
    bs_p = b_spatial[0].T
    bs_s = jnp.tile(b_spatial[0][:, :dec_seq], (1, per)).T
    mc = dict(
        vnw=v_norm_w[0].reshape(1, D_GMLP), vnb=v_norm_b[0].reshape(1, D_GMLP),
        wbs=w_branch_ssd[0].astype(bf16), wbm=w_branch_mlp[0].astype(bf16), wo=w_out[0].astype(bf16),
        nffn=norm_ffn[0].reshape(1, D_MODEL), wrT=w_router[0].T, br=b_router[0].reshape(N_EXPERTS, 1),
        tris=jnp.asarray(np.triu(np.ones((tm_mix, tm_mix), np.float32), 1), bf16),
    )

    z_p, uv_p, g_p, xbc_p, dt_p, dtT_p = _in_proj(xp2, nm, w_segments, w_dt, w_dtT)
    yssd_p, ssm_p = _ssd_prompt(z_p, xbc_p, dt_p, dtT_p, sc(CHUNK), batch, seq)
    h_p, hf_p, te_p, gate_p, rank_p, cnt_p = _mix(uv_p, g_p, yssd_p, xp2, dict(mc, ws=ws_p, bs=bs_p), tm_mix, False)

    z_s, uv_s, g_s, xbc_s, dt_s, dtT_s = _in_proj(xs2, nm, w_segments, w_dt, w_dtT)
    pre = jnp.pad(state_conv[0], ((0, 0), (dec_seq - (CONV_W - 1), 0), (0, 0))).reshape(t_s, CONV_DIM)
    yssd_s, ssm_s = _ssd_sample(z_s, xbc_s, pre, dt_s, dtT_s, state_ssm[0].reshape(nseq, D_INNER, D_STATE),
                                sc(dec_seq), nseq, dec_seq)
    h_s, hf_s, v_s, te_s, gate_s, rank_s, cnt_s = _mix(uv_s, g_s, yssd_s, xs2, dict(mc, ws=ws_s, bs=bs_s), tm_mix, True)

    cp, cs = cnt_p[:, 0], cnt_s[:, 0]
    padded = (cp + cs + MOE_BLOCK - 1) // MOE_BLOCK * MOE_BLOCK
    pend = jnp.cumsum(padded)
    pstart = pend - padded

    def lookup(table, te):
        eid = jnp.arange(N_EXPERTS, dtype=i32)[:, None, None]
        return jnp.sum(jnp.where(te[None] == eid, table[:, None, None], 0), axis=0)

    slot_p = (lookup(pstart, te_p) + rank_p).T.reshape(-1).astype(i32)
    slot_s = (lookup(pstart + cp, te_s) + rank_s).T.reshape(-1).astype(i32)
    slot_all = jnp.concatenate([slot_p, slot_s])
    n_blocks = (t_p + t_s) * TOP_K // MOE_BLOCK + N_EXPERTS
    n_used = (pend[-1] // MOE_BLOCK).astype(i32).reshape(1)
    first_row = jnp.arange(n_blocks, dtype=i32) * MOE_BLOCK
    block_e = jnp.minimum(jnp.sum(pend[None, :] <= first_row[:, None], axis=1), N_EXPERTS - 1).astype(i32)

    xs_sorted = _dispatch(slot_all, pend.astype(i32), padded.astype(i32), hf_p, hf_s, n_blocks * MOE_BLOCK, tm)
    bgu = jnp.concatenate([b_gu[0][..., 0::2], b_gu[0][..., 1::2]], axis=-1).reshape(N_EXPERTS, 1, 2 * D_FF)
    ys = _experts(block_e, n_used, xs_sorted, w_gu[0], bgu, w_down[0], b_down[0].reshape(N_EXPERTS, 1, D_MODEL))

    nf = norm_final.reshape(1, D_MODEL)
    y_p = _combine(slot_p, h_p, gate_p.T, nf, ys, tm)
    y_s = _combine(slot_s, h_s, gate_s.T, nf, ys, tm)

    conv_p = xbc_p.reshape(batch, seq, CONV_DIM)[:, seq - (CONV_W - 1):]
    conv_s = xbc_s.reshape(nseq, dec_seq, CONV_DIM)[:, dec_seq - (CONV_W - 1):]
    st_shape = (HEADS, HEAD_DIM, D_STATE)
    return (
        y_p.reshape(batch, seq, D_MODEL),
        y_s.reshape(nseq, dec_seq, D_MODEL),
        conv_p[None],
        ssm_p.reshape(1, batch, *st_shape),
        conv_s[None],
        ssm_s.reshape(1, nseq, *st_shape),
        v_s.reshape(1, nseq, dec_seq, D_GMLP),
    )
```

```python
import functools

import numpy as np
import jax
import jax.numpy as jnp
from jax import lax
from jax.experimental import pallas as pl
from jax.experimental.pallas import tpu as pltpu
from jax.experimental.pallas import tpu_sc as plsc

f32, bf16, i32, u32 = jnp.float32, jnp.bfloat16, jnp.int32, jnp.uint32

D_MODEL = 1024
D_INNER = 2048
HEAD_DIM = 64
HEADS = 32
GROUPS = 4
HEADS_PER_GROUP = 8
GROUP_WIDTH = HEADS_PER_GROUP * HEAD_DIM
D_STATE = 128
CONV_W = 4
CONV_DIM = D_INNER + 2 * GROUPS * D_STATE
CHUNK = 128
SUBLANES = 8
D_GMLP = 1024
GMLP_HEADS = 8
N_EXPERTS = 32
TOP_K = 4
D_FF = 1024
SWIGLU_ALPHA = 1.702
SWIGLU_LIMIT = 7.0
EPS = 1e-5
MOE_BLOCK = 512
DEINT = 256
ROW_DMA_TILE = 256
SC_CHUNK = 32
MIX_TILE = 512
D_IN_PROJ = 3 * D_INNER + CONV_DIM + HEADS
PROJ_TN = 512
PROMPT_SEQS_PER_STEP = 2
SAMPLE_SEQS_PER_STEP = 8
D_PACK = D_MODEL // 2
VMEM_LIMIT = 56 * 1024 * 1024

_NT = (((1,), (1,)), ((), ()))
_HI16 = np.uint32(0xFFFF0000)


def _params(*sem):
    return pltpu.CompilerParams(dimension_semantics=sem, vmem_limit_bytes=VMEM_LIMIT)


def _split(x, parts):
    out = []
    for _ in range(parts - 1):
        p = x.astype(bf16)
        out.append(p)
        x = x - p.astype(f32)
    out.append(x.astype(bf16))
    return out


def _dot_sel_rhs(x, sel, parts=3):
    return sum(jnp.dot(p, sel, preferred_element_type=f32) for p in _split(x, parts))


def _dot_sel_lhs(sel, x, parts=3):
    return sum(jnp.dot(sel, p, preferred_element_type=f32) for p in _split(x, parts))


def _silu(x):
    return x * jax.nn.sigmoid(x)


def _pack_bf16_pair(x):
    n = x.shape[1] // 2
    lo = lax.bitcast_convert_type(x[:, :n].astype(bf16).astype(f32), u32) >> 16
    hi = lax.bitcast_convert_type(x[:, n:].astype(bf16).astype(f32), u32) & _HI16
    return lo | hi


def _unpack_bf16_pair(p):
    lo = lax.bitcast_convert_type(p << 16, f32).astype(bf16)
    hi = lax.bitcast_convert_type(p & _HI16, f32).astype(bf16)
    return jnp.concatenate([lo, hi], axis=1)


def _in_proj_body(x_ref, nw_ref, wz_ref, wuv_ref, wg_ref, wxbc_ref, wdt_ref, wdtT_ref,
                  z_ref, uv_ref, g_ref, xbc_ref, dt_ref, dtT_ref):
    x = x_ref[...]
    ms = jnp.mean(x * x, axis=-1, keepdims=True)
    hn = (x * lax.rsqrt(ms + EPS) * nw_ref[...]).astype(bf16)
    dt_ref[...] = jnp.dot(hn, wdt_ref[...], preferred_element_type=f32)
    dtT_ref[...] = lax.dot_general(wdtT_ref[...], hn, _NT, preferred_element_type=f32)
    for o_ref, w_ref in ((z_ref, wz_ref), (uv_ref, wuv_ref), (g_ref, wg_ref), (xbc_ref, wxbc_ref)):
        for c in range(0, w_ref.shape[1], PROJ_TN):
            acc = jnp.dot(hn, w_ref[:, c:c + PROJ_TN], preferred_element_type=f32)
            o_ref[:, c:c + PROJ_TN] = acc.astype(o_ref.dtype)


def _in_proj(x2d, norm_w, w_segments, w_dt, w_dtT):
    t = x2d.shape[0]
    tm = min(512, t)
    row = lambda i: (i, 0)
    fixed = lambda i: (0, 0)
    return pl.pallas_call(
        _in_proj_body,
        grid=(t // tm,),
        in_specs=[
            pl.BlockSpec((tm, D_MODEL), row),
            pl.BlockSpec((1, D_MODEL), fixed),
        ] + [pl.BlockSpec(w.shape, fixed, pipeline_mode=pl.Buffered(1)) for w in w_segments] + [
            pl.BlockSpec((D_MODEL, HEADS), fixed),
            pl.BlockSpec((HEADS, D_MODEL), fixed),
        ],
        out_specs=[
            pl.BlockSpec((tm, D_INNER), row),
            pl.BlockSpec((tm, D_INNER), row),
            pl.BlockSpec((tm, D_INNER), row),
            pl.BlockSpec((tm, CONV_DIM), row),
            pl.BlockSpec((tm, HEADS), row),
            pl.BlockSpec((HEADS, tm), lambda i: (0, i)),
        ],
        out_shape=[
            jax.ShapeDtypeStruct((t, D_INNER), bf16),
            jax.ShapeDtypeStruct((t, D_INNER), bf16),
            jax.ShapeDtypeStruct((t, D_INNER), bf16),
            jax.ShapeDtypeStruct((t, CONV_DIM), f32),
            jax.ShapeDtypeStruct((t, HEADS), f32),
            jax.ShapeDtypeStruct((HEADS, t), f32),
        ],
        compiler_params=_params("arbitrary"),
        name="in_proj",
    )(x2d, norm_w, *w_segments, w_dt, w_dtT)


def _conv_taps(shifted, cw_ref, cb_ref):
    acc = cb_ref[...]
    for k in range(CONV_W):
        acc = acc + shifted(CONV_W - 1 - k) * cw_ref[k:k + 1, :]
    return _silu(acc)


def _conv_silu_one_seq(cur, prev8, cw_ref, cb_ref):
    row8 = lax.broadcasted_iota(i32, (SUBLANES, 1), 0)

    def shifted(j):
        if j == 0:
            return cur
        xr = pltpu.roll(cur, j, 0)
        top = jnp.where(row8 >= j, xr[:SUBLANES], pltpu.roll(prev8, j, 0))
        return jnp.concatenate([top, xr[SUBLANES:]], axis=0)

    return _conv_taps(shifted, cw_ref, cb_ref)


def _conv_silu_many_seq(cur, pre, cw_ref, cb_ref, seq_len):
    n = cur.shape[0]
    pos = lax.broadcasted_iota(i32, (n, 1), 0) % seq_len

    def shifted(j):
        if j == 0:
            return cur
        return jnp.where(pos >= j, pltpu.roll(cur, j, 0), pltpu.roll(pre, (j + n - seq_len) % n, 0))

    return _conv_taps(shifted, cw_ref, cb_ref)


def _ssd_block(xa, dt_raw, dtT_raw, c):
    tril = c["tril"][...]
    dt = jax.nn.softplus(dt_raw + c["dtb_row"][...])
    dtT = jax.nn.softplus(dtT_raw + c["dtb_col"][...])
    a = dt * c["aneg_row"][...]
    aT = dtT * c["aneg_col"][...]
    a_cum = _dot_sel_lhs(tril, a)
    a_cumT = _dot_sel_rhs(aT, c["triu"][...])
    a_tot = _dot_sel_lhs(c["same"][...], a)
    stack = jnp.concatenate([dt, jnp.exp(a_cum), jnp.exp(a_tot - a_cum)], axis=0)
    ex = _dot_sel_rhs(stack, c["expand"][...], parts=2)
    n = xa.shape[0]
    dtx, eax, dex = ex[:n], ex[n:2 * n], ex[2 * n:]

    xs = xa[:, :D_INNER]
    bm = xa[:, D_INNER:D_INNER + GROUPS * D_STATE]
    cm = xa[:, D_INNER + GROUPS * D_STATE:]
    xdt = xs * dtx
    xdt_bf = xdt.astype(bf16)
    mask = tril > 0
    lane = lax.broadcasted_iota(i32, (1, 2 * HEAD_DIM), 1)
    ys = []
    for g in range(GROUPS):
        cg = cm[:, g * D_STATE:(g + 1) * D_STATE].astype(bf16)
        bg = bm[:, g * D_STATE:(g + 1) * D_STATE].astype(bf16)
        cb = lax.dot_general(cg, bg, _NT, preferred_element_type=f32)
        for pair in range(HEADS_PER_GROUP // 2):
            halves = []
            for h in (g * HEADS_PER_GROUP + 2 * pair, g * HEADS_PER_GROUP + 2 * pair + 1):
                seg = a_cum[:, h:h + 1] - a_cumT[h:h + 1, :]
                decay = jnp.exp(jnp.where(mask, seg, -jnp.inf))
                m = (cb * decay).astype(bf16)
                col = (h // 2) * 2 * HEAD_DIM
                halves.append(jnp.dot(m, xdt_bf[:, col:col + 2 * HEAD_DIM], preferred_element_type=f32))
            ys.append(jnp.where(lane < HEAD_DIM, halves[0], halves[1]))
    y_diag = jnp.concatenate(ys, axis=1)
    return dict(xs=xs, bm=bm, cm=cm, xdt=xdt, xd=xdt * dex, eax=eax, aT=aT, y_diag=y_diag)


def _gated_norm(y, z, gw):
    g = y * _silu(z.astype(f32))
    ms = jnp.mean(g * g, axis=-1, keepdims=True)
    return (g * lax.rsqrt(ms + EPS) * gw).astype(bf16)


_SSD_CONST_NAMES = ("cw", "cb", "dtb_row", "dtb_col", "aneg_row", "aneg_col", "dskipx", "gw",
                    "tril", "triu", "same", "expand")


def _ssd_consts(conv_w, conv_b, dt_bias, a_log, d_skip, gnorm_w, seq_len):
    r = np.arange(CHUNK)
    same = (r[:, None] // seq_len) == (r[None, :] // seq_len)
    tril = same & (r[None, :] <= r[:, None])
    expand = np.repeat(np.eye(HEADS, dtype=np.float32), HEAD_DIM, axis=1)
    aneg = -jnp.exp(a_log.astype(f32))
    return dict(
        cw=conv_w, cb=conv_b.reshape(1, CONV_DIM),
        dtb_row=dt_bias.reshape(1, HEADS), dtb_col=dt_bias.reshape(HEADS, 1),
        aneg_row=aneg.reshape(1, HEADS), aneg_col=aneg.reshape(HEADS, 1),
        dskipx=jnp.repeat(d_skip.astype(f32), HEAD_DIM).reshape(1, D_INNER),
        gw=gnorm_w.reshape(1, D_INNER),
        tril=jnp.asarray(tril, bf16), triu=jnp.asarray(tril.T, bf16), same=jnp.asarray(same, bf16),
        expand=jnp.asarray(expand, bf16),
    )


def _const_specs(consts):
    zero = (lambda *_: (0, 0))
    return [pl.BlockSpec(consts[k].shape, zero) for k in _SSD_CONST_NAMES]


def _ssd_prompt_body(nb, z_ref, xbc_ref, dt_ref, dtT_ref, *rest):
    nc = len(_SSD_CONST_NAMES)
    c = dict(zip(_SSD_CONST_NAMES, rest[:nc]))
    y_ref, state_ref, prev_ref, st_ref = rest[nc:]
    ci = pl.program_id(1)

    @pl.when(ci == 0)
    def _():
        prev_ref[...] = jnp.zeros_like(prev_ref)
        st_ref[...] = jnp.zeros_like(st_ref)

    for s in range(nb):
        cur = xbc_ref[s]
        xa = _conv_silu_one_seq(cur, prev_ref[s], c["cw"], c["cb"])
        prev_ref[s] = cur[CHUNK - SUBLANES:]
        b = _ssd_block(xa, dt_ref[s], dtT_ref[s], c)

        y_off = []
        for g in range(GROUPS):
            sl = slice(g * GROUP_WIDTH, (g + 1) * GROUP_WIDTH)
            st = st_ref[s * GROUPS + g]
            cg = b["cm"][:, g * D_STATE:(g + 1) * D_STATE].astype(bf16)
            y_off.append(jnp.dot(cg, st.astype(bf16), preferred_element_type=f32))
            bgT = b["bm"][:, g * D_STATE:(g + 1) * D_STATE].T.astype(bf16)
            upd = jnp.dot(bgT, b["xd"][:, sl].astype(bf16), preferred_element_type=f32)
            st_ref[s * GROUPS + g] = st * b["eax"][CHUNK - 1:CHUNK, sl] + upd
        y = b["y_diag"] + jnp.concatenate(y_off, axis=1) * b["eax"] + c["dskipx"][...] * b["xs"]
        y_ref[s] = _gated_norm(y, z_ref[s], c["gw"][...])

    @pl.when(ci == pl.num_programs(1) - 1)
    def _():
        for s in range(nb):
            for g in range(GROUPS):
                state_ref[s, 0, g * GROUP_WIDTH:(g + 1) * GROUP_WIDTH, :] = st_ref[s * GROUPS + g].T


def _ssd_prompt(z, xbc, dt, dtT, consts, batch, seq):
    nb = PROMPT_SEQS_PER_STEP if batch % PROMPT_SEQS_PER_STEP == 0 else 1
    nchunk = seq // CHUNK
    part = batch // nb * seq
    split = lambda a: a.reshape(nb, part, a.shape[-1])
    dtT3 = dtT.reshape(HEADS, nb, part).transpose(1, 0, 2)
    row = lambda b, ci: (0, b * nchunk + ci, 0)
    y, state = pl.pallas_call(
        functools.partial(_ssd_prompt_body, nb),
        grid=(batch // nb, nchunk),
        in_specs=[
            pl.BlockSpec((nb, CHUNK, D_INNER), row),
            pl.BlockSpec((nb, CHUNK, CONV_DIM), row),
            pl.BlockSpec((nb, CHUNK, HEADS), row),
            pl.BlockSpec((nb, HEADS, CHUNK), lambda b, ci: (0, 0, b * nchunk + ci)),
        ] + _const_specs(consts),
        out_specs=[
            pl.BlockSpec((nb, CHUNK, D_INNER), row),
            pl.BlockSpec((nb, 1, D_INNER, D_STATE), lambda b, ci: (0, b, 0, 0)),
        ],
        out_shape=[
            jax.ShapeDtypeStruct((nb, part, D_INNER), bf16),
            jax.ShapeDtypeStruct((nb, batch // nb, D_INNER, D_STATE), f32),
        ],
        scratch_shapes=[pltpu.VMEM((nb, SUBLANES, CONV_DIM), f32),
                        pltpu.VMEM((nb * GROUPS, D_STATE, GROUP_WIDTH), f32)],
        compiler_params=_params("arbitrary", "arbitrary"),
        name="ssd_prompt",
    )(split(z), split(xbc), split(dt), dtT3, *[consts[k] for k in _SSD_CONST_NAMES])
    return y.reshape(batch * seq, D_INNER), state.reshape(batch, D_INNER, D_STATE)


def _ssd_sample_body(seq_len, z_ref, xbc_ref, pre_ref, dt_ref, dtT_ref, state_ref, *rest):
    nc = len(_SSD_CONST_NAMES)
    c = dict(zip(_SSD_CONST_NAMES, rest[:nc]))
    y_ref, state_out_ref, c_ref, b_ref, xdT_ref, yacc_ref, eax_ref, eatT_ref = rest[nc:]
    step = pl.program_id(1)
    per_step = state_ref.shape[0]

    @pl.when(step == 0)
    def _():
        xa = _conv_silu_many_seq(xbc_ref[...], pre_ref[...], c["cw"], c["cb"], seq_len)
        b = _ssd_block(xa, dt_ref[...], dtT_ref[...], c)
        c_ref[...] = b["cm"]
        b_ref[...] = b["bm"]
        xdT_ref[...] = b["xd"].T.astype(bf16)
        yacc_ref[...] = b["y_diag"] + c["dskipx"][...] * b["xs"]
        eax_ref[...] = b["eax"]
        eatT_ref[...] = jnp.exp(_dot_sel_rhs(b["aT"], c["same"][...]))

    def one_sequence(q, carry):
        s = step * per_step + q
        r0 = pl.multiple_of(s * seq_len, seq_len)
        rows = pl.ds(r0, seq_len)
        lane = lax.broadcasted_iota(i32, (1, CHUNK), 1)
        arep = jnp.broadcast_to(jnp.sum(jnp.where(lane == r0, eatT_ref[...], 0.0), axis=1, keepdims=True), (HEADS, D_STATE))
        rmask = (lax.broadcasted_iota(i32, (CHUNK, 1), 0) // seq_len) == s
        for g in range(GROUPS):
            gs = slice(g * GROUP_WIDTH, (g + 1) * GROUP_WIDTH)
            ns = slice(g * D_STATE, (g + 1) * D_STATE)
            s0 = state_ref[q, gs, :]
            cg = c_ref[rows, ns].astype(bf16)
            yo = lax.dot_general(cg, s0.astype(bf16), _NT, preferred_element_type=f32)
            yacc_ref[rows, gs] = yacc_ref[rows, gs] + yo * eax_ref[rows, gs]
            bmask = jnp.where(rmask, b_ref[:, ns], 0.0).astype(bf16)
            upd = jnp.dot(xdT_ref[gs, :], bmask, preferred_element_type=f32)
            for r in range(HEADS_PER_GROUP):
                h = g * HEADS_PER_GROUP + r
                hs = slice(r * HEAD_DIM, (r + 1) * HEAD_DIM)
                state_out_ref[q, h * HEAD_DIM:(h + 1) * HEAD_DIM, :] = s0[hs] * arep[h:h + 1, :] + upd[hs]
        return carry

    lax.fori_loop(0, per_step, one_sequence, 0)

    @pl.when(step == pl.num_programs(1) - 1)
    def _():
        y_ref[...] = _gated_norm(yacc_ref[...], z_ref[...], c["gw"][...])


def _ssd_sample(z, xbc, pre, dt, dtT, state, consts, nseq, seq_len):
    per = CHUNK // seq_len
    nblk = nseq // per
    t = nseq * seq_len
    q = SAMPLE_SEQS_PER_STEP if per % SAMPLE_SEQS_PER_STEP == 0 else 1
    blk = lambda i, s: (i, 0)
    seq3 = lambda i, s: (i * (per // q) + s, 0, 0)
    return pl.pallas_call(
        functools.partial(_ssd_sample_body, seq_len),
        grid=(nblk, per // q),
        in_specs=[
            pl.BlockSpec((CHUNK, D_INNER), blk),
            pl.BlockSpec((CHUNK, CONV_DIM), blk),
            pl.BlockSpec((CHUNK, CONV_DIM), blk),
            pl.BlockSpec((CHUNK, HEADS), blk),
            pl.BlockSpec((HEADS, CHUNK), lambda i, s: (0, i)),
            pl.BlockSpec((q, D_INNER, D_STATE), seq3),
        ] + _const_specs(consts),
        out_specs=[
            pl.BlockSpec((CHUNK, D_INNER), blk),
            pl.BlockSpec((q, D_INNER, D_STATE), seq3),
        ],
        out_shape=[
            jax.ShapeDtypeStruct((t, D_INNER), bf16),
            jax.ShapeDtypeStruct((nseq, D_INNER, D_STATE), f32),
        ],
        scratch_shapes=[
            pltpu.VMEM((CHUNK, GROUPS * D_STATE), f32),
            pltpu.VMEM((CHUNK, GROUPS * D_STATE), f32),
            pltpu.VMEM((D_INNER, CHUNK), bf16),
            pltpu.VMEM((CHUNK, D_INNER), f32),
            pltpu.VMEM((CHUNK, D_INNER), f32),
            pltpu.VMEM((HEADS, CHUNK), f32),
        ],
        compiler_params=_params("arbitrary", "arbitrary"),
        name="ssd_sample",
    )(z, xbc, pre, dt, dtT, state, *[consts[k] for k in _SSD_CONST_NAMES])


_MIX_CONST_NAMES = ("ws", "bs", "vnw", "vnb", "wbs", "wbm", "wo", "nffn", "wrT", "br", "tris")


def _mix_body(emit_v, uv_ref, gates_ref, yssd_ref, x_ref, *rest):
    nc = len(_MIX_CONST_NAMES)
    c = dict(zip(_MIX_CONST_NAMES, rest[:nc]))
    outs = list(rest[nc:])
    h_ref, hfp_ref = outs[:2]
    v_ref = outs[2] if emit_v else None
    te_ref, gate_ref, rank_ref, cnt_ref, cnt_acc = outs[-5:]
    tm = uv_ref.shape[0]

    @pl.when(pl.program_id(0) == 0)
    def _():
        cnt_acc[...] = jnp.zeros_like(cnt_acc)

    uv = uv_ref[...].astype(f32)
    uv = 0.5 * uv * (1.0 + lax.erf(uv * np.float32(np.sqrt(0.5))))
    u, v = uv[:, :D_GMLP], uv[:, D_GMLP:]
    mu = jnp.mean(v, axis=-1, keepdims=True)
    vc = v - mu
    vn = vc * lax.rsqrt(jnp.mean(vc * vc, axis=-1, keepdims=True) + EPS) * c["vnw"][...] + c["vnb"][...]
    if emit_v:
        v_ref[...] = vn
    vn_bf = vn.astype(bf16)
    gd = D_GMLP // GMLP_HEADS
    rows = []
    for ck in range(tm // CHUNK):
        rs = slice(ck * CHUNK, (ck + 1) * CHUNK)
        heads = []
        for g in range(GMLP_HEADS):
            mixed = jnp.dot(c["ws"][g], vn_bf[rs, g * gd:(g + 1) * gd], preferred_element_type=f32)
            heads.append(mixed + c["bs"][:, g:g + 1])
        rows.append(jnp.concatenate(heads, axis=1))
    y_mlp = u * jnp.concatenate(rows, axis=0)

    a = jnp.dot(yssd_ref[...], c["wbs"][...], preferred_element_type=f32)
    b = jnp.dot(y_mlp.astype(bf16), c["wbm"][...], preferred_element_type=f32)
    gs = jax.nn.sigmoid(gates_ref[...].astype(f32))
    merged = gs[:, :D_MODEL] * a + gs[:, D_MODEL:] * b
    h = x_ref[...] + jnp.dot(merged.astype(bf16), c["wo"][...], preferred_element_type=f32)
    h_ref[...] = h
    hf = h * lax.rsqrt(jnp.mean(h * h, axis=-1, keepdims=True) + EPS) * c["nffn"][...]
    hfp_ref[...] = _pack_bf16_pair(hf)

    lg = lax.dot_general(c["wrT"][...], hf, _NT, precision=lax.Precision.HIGHEST, preferred_element_type=f32) + c["br"][...]
    sub = lax.broadcasted_iota(i32, lg.shape, 0)
    idxs, vals = [], []
    for _ in range(TOP_K):
        m = jnp.max(lg, axis=0, keepdims=True)
        idx = jnp.min(jnp.where(lg == m, sub, N_EXPERTS), axis=0, keepdims=True)
        idxs.append(idx)
        vals.append(m)
        lg = jnp.where(sub == idx, -jnp.inf, lg)
    p = jnp.exp(jnp.concatenate(vals, axis=0) - vals[0])
    gate_ref[...] = p / jnp.sum(p, axis=0, keepdims=True)
    te_ref[...] = jnp.concatenate(idxs, axis=0)

    onehots = [sub == idx for idx in idxs]
    member = functools.reduce(jnp.logical_or, onehots).astype(f32)
    before = jnp.dot(member.astype(bf16), c["tris"][...], preferred_element_type=f32) + cnt_acc[:, 0:1]
    rank_ref[...] = jnp.concatenate(
        [jnp.sum(jnp.where(oh, before, 0.0), axis=0, keepdims=True) for oh in onehots], axis=0).astype(i32)
    cnt_acc[...] = cnt_acc[...] + jnp.sum(member, axis=1, keepdims=True)
    cnt_ref[...] = cnt_acc[...].astype(i32)


def _mix(uv, gates, yssd, x2d, mc, tm, emit_v):
    t = x2d.shape[0]
    row = lambda i: (i, 0)
    col = lambda i: (0, i)
    const = lambda a: pl.BlockSpec(a.shape, lambda i: (0,) * a.ndim, pipeline_mode=pl.Buffered(1))
    v_spec = [pl.BlockSpec((tm, D_GMLP), row)] if emit_v else []
    v_shape = [jax.ShapeDtypeStruct((t, D_GMLP), f32)] if emit_v else []
    return pl.pallas_call(
        functools.partial(_mix_body, emit_v),
        grid=(t // tm,),
        in_specs=[
            pl.BlockSpec((tm, D_INNER), row),
            pl.BlockSpec((tm, D_INNER), row),
            pl.BlockSpec((tm, D_INNER), row),
            pl.BlockSpec((tm, D_MODEL), row),
        ] + [const(mc[k]) for k in _MIX_CONST_NAMES],
        out_specs=[pl.BlockSpec((tm, D_MODEL), row), pl.BlockSpec((tm, D_PACK), row)] + v_spec + [
            pl.BlockSpec((TOP_K, tm), col),
            pl.BlockSpec((TOP_K, tm), col),
            pl.BlockSpec((TOP_K, tm), col),
            pl.BlockSpec((N_EXPERTS, 128), lambda i: (0, 0)),
        ],
        out_shape=[
            jax.ShapeDtypeStruct((t, D_MODEL), f32),
            jax.ShapeDtypeStruct((t, D_PACK), u32),
        ] + v_shape + [
            jax.ShapeDtypeStruct((TOP_K, t), i32),
            jax.ShapeDtypeStruct((TOP_K, t), f32),
            jax.ShapeDtypeStruct((TOP_K, t), i32),
            jax.ShapeDtypeStruct((N_EXPERTS, 128), i32),
        ],
        scratch_shapes=[pltpu.VMEM((N_EXPERTS, 128), f32)],
        compiler_params=_params("arbitrary"),
        name="mix_route",
    )(uv, gates, yssd, x2d, *[mc[k] for k in _MIX_CONST_NAMES])


def _dispatch(slot_kt, hf_all, n_slots):
    info = plsc.get_sparse_core_info()
    workers = info.num_cores * info.num_subcores
    t = hf_all.shape[0]
    per_w = t // workers
    nch = per_w // SC_CHUNK
    assert per_w * workers == t and nch * SC_CHUNK == per_w
    slots = slot_kt.reshape(TOP_K, workers, nch, SC_CHUNK).transpose(1, 0, 2, 3)
    mesh = plsc.VectorSubcoreMesh(core_axis_name="c", subcore_axis_name="s")

    @functools.partial(
        pl.kernel, mesh=mesh,
        out_type=jax.ShapeDtypeStruct((n_slots, D_PACK), u32),
        scratch_types=[pltpu.VMEM((TOP_K, nch, SC_CHUNK), i32), pltpu.VMEM((2, SC_CHUNK, D_PACK), u32)]
        + [pltpu.SemaphoreType.DMA] * 4,
    )
    def scatter_rows(hf_hbm, slot_hbm, out_hbm, idx_v, rows_v, sin0, sin1, sout0, sout1):
        sin, sout = (sin0, sin1), (sout0, sout1)
        wid = lax.axis_index("s") * info.num_cores + lax.axis_index("c")
        base = wid * per_w
        pltpu.sync_copy(slot_hbm.at[wid], idx_v)

        def load(j):
            rows = pl.ds(pl.multiple_of(base + j * SC_CHUNK, SUBLANES), SC_CHUNK)
            return pltpu.make_async_copy(hf_hbm.at[rows], rows_v.at[j % 2], sin[j % 2])

        def scatters(j):
            return [pltpu.make_async_copy(rows_v.at[j % 2], out_hbm.at[idx_v.at[k, j]], sout[j % 2]) for k in range(TOP_K)]

        load(0).start()
        for j in range(nch):
            load(j).wait()
            if j >= 1:
                for cp in scatters(j - 1):
                    cp.wait()
            if j + 1 < nch:
                load(j + 1).start()
            for cp in scatters(j):
                cp.start()
        for cp in scatters(nch - 1):
            cp.wait()

    return scatter_rows(hf_all, slots)


def _experts_body(be_ref, nused_ref, start_ref, slot_ref, next_ref, nvalid_ref, xs_ref, bgu_ref, bd_ref, perm_ref, wgu_hbm, wd_hbm,
                  ys_ref, wgu_f32, wd_f32, wgu_bf, wd_bf, sems):
    i = pl.program_id(0)

    def weight_copies(e, slot):
        return (pltpu.make_async_copy(wgu_hbm.at[e], wgu_f32.at[slot], sems.at[0, slot]),
                pltpu.make_async_copy(wd_hbm.at[e], wd_f32.at[slot], sems.at[1, slot]))

    @pl.when(i < nused_ref[0])
    def _():
        @pl.when(start_ref[i] == 1)
        def _():
            slot = slot_ref[i]

            @pl.when(i == 0)
            def _():
                for cp in weight_copies(be_ref[0], 0):
                    cp.start()

            for cp in weight_copies(be_ref[i], slot):
                cp.wait()

            @pl.when(next_ref[i] >= 0)
            def _():
                for cp in weight_copies(next_ref[i], 1 - slot):
                    cp.start()

            wd_bf[...] = wd_f32[slot].astype(bf16)
            half = DEINT // 2
            for j in range(2 * D_FF // DEINT):
                w = jnp.dot(wgu_f32[slot, :, j * DEINT:(j + 1) * DEINT].astype(bf16), perm_ref[...],
                            preferred_element_type=f32).astype(bf16)
                wgu_bf[:, j * half:(j + 1) * half] = w[:, :half]
                wgu_bf[:, D_FF + j * half:D_FF + (j + 1) * half] = w[:, half:]

        row = lax.broadcasted_iota(i32, (MOE_BLOCK, 1), 0)
        xs = jnp.where(row < nvalid_ref[i], xs_ref[...], jnp.zeros((), u32))
        gu = jnp.dot(_unpack_bf16_pair(xs), wgu_bf[...], preferred_element_type=f32) + bgu_ref[0]
        glu = jnp.minimum(gu[:, :D_FF], SWIGLU_LIMIT)
        lin = jnp.clip(gu[:, D_FF:], -SWIGLU_LIMIT, SWIGLU_LIMIT)
        act = glu * jax.nn.sigmoid(SWIGLU_ALPHA * glu) * (lin + 1.0)
        ys_ref[...] = jnp.dot(act.astype(bf16), wd_bf[...], preferred_element_type=f32) + bd_ref[0]

    @pl.when(i >= nused_ref[0])
    def _():
        ys_ref[...] = jnp.zeros_like(ys_ref)


def _expert_segments(block_e, n_used, n_blocks):
    idx = jnp.arange(n_blocks, dtype=i32)
    prev_e = jnp.concatenate([block_e[:1], block_e[:-1]])
    start = (idx < n_used) & ((idx == 0) | (block_e != prev_e))
    slot = (jnp.cumsum(start.astype(i32)) - 1) % 2
    start_idx = jnp.where(start, idx, n_blocks)
    from_here = lax.cummin(start_idx, reverse=True)
    next_start = jnp.concatenate([from_here[1:], jnp.full((1,), n_blocks, i32)])
    next_e = jnp.sum(jnp.where(next_start[:, None] == idx[None, :], block_e[None, :], 0), axis=1)
    next_e = jnp.where(next_start < n_blocks, next_e, -1)
    return start.astype(i32), slot.astype(i32), next_e.astype(i32)


def _experts(block_e, n_used, n_valid, xs, wgu, bgu, w_down, b_down):
    n_blocks = xs.shape[0] // MOE_BLOCK
    start, slot, next_e = _expert_segments(block_e, n_used[0], n_blocks)
    blk = lambda i, be, nu, *_: (jnp.maximum(jnp.minimum(i, nu[0] - 1), 0), 0)
    exp3 = lambda i, be, *_: (be[i], 0, 0)
    c = np.arange(DEINT)
    src = np.where(c < DEINT // 2, 2 * c, 2 * (c - DEINT // 2) + 1)
    perm = jnp.asarray(np.arange(DEINT)[:, None] == src[None, :], bf16)
    return pl.pallas_call(
        _experts_body,
        grid_spec=pltpu.PrefetchScalarGridSpec(
            num_scalar_prefetch=6,
            grid=(n_blocks,),
            in_specs=[
                pl.BlockSpec((MOE_BLOCK, D_PACK), blk),
                pl.BlockSpec((1, 1, 2 * D_FF), exp3),
                pl.BlockSpec((1, 1, D_MODEL), exp3),
                pl.BlockSpec((DEINT, DEINT), lambda i, *_: (0, 0)),
                pl.BlockSpec(memory_space=pl.ANY),
                pl.BlockSpec(memory_space=pl.ANY),
            ],
            out_specs=pl.BlockSpec((MOE_BLOCK, D_MODEL), lambda i, *_: (i, 0)),
            scratch_shapes=[
                pltpu.VMEM((2, D_MODEL, 2 * D_FF), f32),
                pltpu.VMEM((2, D_FF, D_MODEL), f32),
                pltpu.VMEM((D_MODEL, 2 * D_FF), bf16),
                pltpu.VMEM((D_FF, D_MODEL), bf16),
                pltpu.SemaphoreType.DMA((2, 2)),
            ],
        ),
        out_shape=jax.ShapeDtypeStruct((xs.shape[0], D_MODEL), f32),
        compiler_params=_params("arbitrary"),
        name="moe_experts",
    )(block_e, n_used, start, slot, next_e, n_valid, xs, bgu, b_down, perm, wgu, w_down)


def _combine_body(tm, slot_ref, h_ref, gate_ref, nf_ref, ys_ref, o_ref, rows_ref, sems):
    s = pl.program_id(0)
    n_tiles = pl.num_programs(0) - 1

    def request():
        buf = s % 2
        for r in range(tm):
            for k in range(TOP_K):
                slot = slot_ref[(s * tm + r) * TOP_K + k]
                pltpu.make_async_copy(ys_ref.at[slot], rows_ref.at[buf, k, r], sems.at[buf]).start(priority=k % 2)

    def finish():
        buf = (s - 1) % 2
        for r in range(tm):
            for k in range(TOP_K):
                pltpu.make_async_copy(ys_ref.at[0], rows_ref.at[buf, k, r], sems.at[buf]).wait()
        g = gate_ref[...]
        moe = g[:, 0:1] * rows_ref[buf, 0]
        for k in range(1, TOP_K):
            moe = moe + g[:, k:k + 1] * rows_ref[buf, k]
        h = h_ref[...] + moe
        o_ref[...] = h * lax.rsqrt(jnp.mean(h * h, axis=-1, keepdims=True) + EPS) * nf_ref[...]

    @pl.when(s < n_tiles)
    def _():
        request()

    @pl.when(s > 0)
    def _():
        finish()


def _combine(slot_flat, h, gate_t, norm_final, ys, tm):
    t = h.shape[0]
    return pl.pallas_call(
        functools.partial(_combine_body, tm),
        grid_spec=pltpu.PrefetchScalarGridSpec(
            num_scalar_prefetch=1,
            grid=(t // tm + 1,),
            in_specs=[
                pl.BlockSpec((tm, D_MODEL), lambda i, s: (jnp.maximum(i - 1, 0), 0)),
                pl.BlockSpec((tm, TOP_K), lambda i, s: (jnp.maximum(i - 1, 0), 0)),
                pl.BlockSpec((1, D_MODEL), lambda i, s: (0, 0)),
                pl.BlockSpec(memory_space=pl.ANY),
            ],
            out_specs=pl.BlockSpec((tm, D_MODEL), lambda i, s: (jnp.maximum(i - 1, 0), 0)),
            scratch_shapes=[pltpu.VMEM((2, TOP_K, tm, D_MODEL), f32), pltpu.SemaphoreType.DMA((2,))],
        ),
        out_shape=jax.ShapeDtypeStruct((t, D_MODEL), f32),
        compiler_params=_params("arbitrary"),
        name="moe_combine",
    )(slot_flat, h, gate_t, norm_final, ys)


def kernel(x_prompt, x_sample, state_conv, state_ssm, norm_mix, w_in, conv_w, conv_b, dt_bias, a_log, d_skip, gnorm_w, v_norm_w, v_norm_b, w_spatial, b_spatial, w_branch_ssd, w_branch_mlp, w_out, norm_ffn, w_router, b_router, w_gu, b_gu, w_down, b_down, norm_final):
    assert w_in.shape[0] == 1, "single-layer trunk"
    batch, seq, _ = x_prompt.shape
    nseq, dec_seq, _ = x_sample.shape
    assert seq % CHUNK == 0 and CHUNK % dec_seq == 0 and nseq % (CHUNK // dec_seq) == 0 and dec_seq >= CONV_W - 1
    t_p, t_s = batch * seq, nseq * dec_seq
    tm = ROW_DMA_TILE
    assert t_p % tm == 0 and t_s % tm == 0
    tm_mix = MIX_TILE if t_p % MIX_TILE == 0 and t_s % MIX_TILE == 0 else tm

    wi = w_in[0]
    z0, x0, d0, u0 = D_INNER, D_INNER + CONV_DIM, D_INNER + CONV_DIM + HEADS, D_INNER + CONV_DIM + HEADS + 2 * D_GMLP
    w_segments = tuple(wi[:, a:b].astype(bf16) for a, b in ((0, z0), (d0, u0), (u0, D_IN_PROJ), (z0, x0)))
    w_dt = wi[:, x0:d0].astype(bf16)
    w_dtT = w_dt.T
    nm = norm_mix[0].reshape(1, D_MODEL)
    sc = lambda L: _ssd_consts(conv_w[0], conv_b[0], dt_bias[0], a_log[0], d_skip[0], gnorm_w[0], L)
    xp2, xs2 = x_prompt.reshape(t_p, D_MODEL), x_sample.reshape(t_s, D_MODEL)

    tril = np.tril(np.ones((CHUNK, CHUNK), bool))
    ws_p = jnp.where(tril[None], w_spatial[0], 0).astype(bf16)
    per = CHUNK // dec_seq
    blockdiag = (np.arange(CHUNK)[:, None] // dec_seq) == (np.arange(CHUNK)[None, :] // dec_seq)
    ws_s = jnp.where((tril & blockdiag)[None], jnp.tile(w_spatial[0][:, :dec_seq, :dec_seq], (1, per, per)), 0).astype(bf16)
    bs_p = b_spatial[0].T
    bs_s = jnp.tile(b_spatial[0][:, :dec_seq], (1, per)).T
    mc = dict(
        vnw=v_norm_w[0].reshape(1, D_GMLP), vnb=v_norm_b[0].reshape(1, D_GMLP),
        wbs=w_branch_ssd[0].astype(bf16), wbm=w_branch_mlp[0].astype(bf16), wo=w_out[0].astype(bf16),
        nffn=norm_ffn[0].reshape(1, D_MODEL), wrT=w_router[0].T, br=b_router[0].reshape(N_EXPERTS, 1),
        tris=jnp.asarray(np.triu(np.ones((tm_mix, tm_mix), np.float32), 1), bf16),
    )

    z_p, uv_p, g_p, xbc_p, dt_p, dtT_p = _in_proj(xp2, nm, w_segments, w_dt, w_dtT)
    yssd_p, ssm_p = _ssd_prompt(z_p, xbc_p, dt_p, dtT_p, sc(CHUNK), batch, seq)
    h_p, hf_p, te_p, gate_p, rank_p, cnt_p = _mix(uv_p, g_p, yssd_p, xp2, dict(mc, ws=ws_p, bs=bs_p), tm_mix, False)

    z_s, uv_s, g_s, xbc_s, dt_s, dtT_s = _in_proj(xs2, nm, w_segments, w_dt, w_dtT)
    pre = jnp.pad(state_conv[0], ((0, 0), (dec_seq - (CONV_W - 1), 0), (0, 0))).reshape(t_s, CONV_DIM)
    yssd_s, ssm_s = _ssd_sample(z_s, xbc_s, pre, dt_s, dtT_s, state_ssm[0].reshape(nseq, D_INNER, D_STATE),
                                sc(dec_seq), nseq, dec_seq)
    h_s, hf_s, v_s, te_s, gate_s, rank_s, cnt_s = _mix(uv_s, g_s, yssd_s, xs2, dict(mc, ws=ws_s, bs=bs_s), tm_mix, True)

    cp, cs = cnt_p[:, 0], cnt_s[:, 0]
    padded = (cp + cs + MOE_BLOCK - 1) // MOE_BLOCK * MOE_BLOCK
    pend = jnp.cumsum(padded)
    pstart = pend - padded

    def lookup(table, te):
        eid = jnp.arange(N_EXPERTS, dtype=i32)[:, None, None]
        return jnp.sum(jnp.where(te[None] == eid, table[:, None, None], 0), axis=0)

    slot_kt_p = (lookup(pstart, te_p) + rank_p).astype(i32)
    slot_kt_s = (lookup(pstart + cp, te_s) + rank_s).astype(i32)
    slot_p, slot_s = slot_kt_p.T.reshape(-1), slot_kt_s.T.reshape(-1)
    n_blocks = (t_p + t_s) * TOP_K // MOE_BLOCK + N_EXPERTS
    n_used = (pend[-1] // MOE_BLOCK).astype(i32).reshape(1)
    first_row = jnp.arange(n_blocks, dtype=i32) * MOE_BLOCK
    block_e = jnp.minimum(jnp.sum(pend[None, :] <= first_row[:, None], axis=1), N_EXPERTS - 1).astype(i32)
    last_row = jnp.sum(jnp.where(block_e[:, None] == jnp.arange(N_EXPERTS)[None, :], (pstart + cp + cs)[None, :], 0), axis=1)
    n_valid = jnp.clip(last_row - first_row, 0, MOE_BLOCK).astype(i32)

    xs_sorted = _dispatch(jnp.concatenate([slot_kt_p, slot_kt_s], axis=1), jnp.concatenate([hf_p, hf_s]),
                          n_blocks * MOE_BLOCK)
    bgu = jnp.concatenate([b_gu[0][..., 0::2], b_gu[0][..., 1::2]], axis=-1).reshape(N_EXPERTS, 1, 2 * D_FF)
    ys = _experts(block_e, n_used, n_valid, xs_sorted, w_gu[0], bgu, w_down[0], b_down[0].reshape(N_EXPERTS, 1, D_MODEL))

    nf = norm_final.reshape(1, D_MODEL)
    y_p = _combine(slot_p, h_p, gate_p.T, nf, ys, tm)
    y_s = _combine(slot_s, h_s, gate_s.T, nf, ys, tm)

    conv_p = xbc_p.reshape(batch, seq, CONV_DIM)[:, seq - (CONV_W - 1):]
    conv_s = xbc_s.reshape(nseq, dec_seq, CONV_DIM)[:, dec_seq - (CONV_W - 1):]
    st_shape = (HEADS, HEAD_DIM, D_STATE)
    return (
        y_p.reshape(batch, seq, D_MODEL),
        y_s.reshape(nseq, dec_seq, D_MODEL),
        conv_p[None],
        ssm_p.reshape(1, batch, *st_shape),
        conv_s[None],
        ssm_s.reshape(1, nseq, *st_shape),
        v_s.reshape(1, nseq, dec_seq, D_GMLP),
    )
```

```python
import functools

import numpy as np
import jax
import jax.numpy as jnp
from jax import lax
from jax.experimental import pallas as pl
from jax.experimental.pallas import tpu as pltpu
from jax.experimental.pallas import tpu_sc as plsc

f32, bf16, i32, u32 = jnp.float32, jnp.bfloat16, jnp.int32, jnp.uint32

D_MODEL = 1024
D_INNER = 2048
HEAD_DIM = 64
HEADS = 32
GROUPS = 4
HEADS_PER_GROUP = 8
GROUP_WIDTH = HEADS_PER_GROUP * HEAD_DIM
D_STATE = 128
CONV_W = 4
CONV_DIM = D_INNER + 2 * GROUPS * D_STATE
CHUNK = 128
SUBLANES = 8
D_GMLP = 1024
GMLP_HEADS = 8
N_EXPERTS = 32
TOP_K = 4
D_FF = 1024
SWIGLU_ALPHA = 1.702
SWIGLU_LIMIT = 7.0
EPS = 1e-5
MOE_BLOCK = 512
DEINT = 256
ROW_DMA_TILE = 256
SC_CHUNK = 32
MIX_TILE = 512
D_IN_PROJ = 3 * D_INNER + CONV_DIM + HEADS
PROJ_TN = 512
PROMPT_SEQS_PER_STEP = 2
SAMPLE_SEQS_PER_STEP = 8
D_PACK = D_MODEL // 2
VMEM_LIMIT = 56 * 1024 * 1024

_NT = (((1,), (1,)), ((), ()))
_HI16 = np.uint32(0xFFFF0000)


def _params(*sem):
    return pltpu.CompilerParams(dimension_semantics=sem, vmem_limit_bytes=VMEM_LIMIT)


def _split(x, parts):
    out = []
    for _ in range(parts - 1):
        p = x.astype(bf16)
        out.append(p)
        x = x - p.astype(f32)
    out.append(x.astype(bf16))
    return out


def _dot_sel_rhs(x, sel, parts=3):
    return sum(jnp.dot(p, sel, preferred_element_type=f32) for p in _split(x, parts))


def _dot_sel_lhs(sel, x, parts=3):
    return sum(jnp.dot(sel, p, preferred_element_type=f32) for p in _split(x, parts))


def _silu(x):
    return x * jax.nn.sigmoid(x)


def _pack_bf16_pair(x):
    n = x.shape[1] // 2
    lo = lax.bitcast_convert_type(x[:, :n].astype(bf16).astype(f32), u32) >> 16
    hi = lax.bitcast_convert_type(x[:, n:].astype(bf16).astype(f32), u32) & _HI16
    return lo | hi


def _unpack_bf16_pair(p):
    lo = lax.bitcast_convert_type(p << 16, f32).astype(bf16)
    hi = lax.bitcast_convert_type(p & _HI16, f32).astype(bf16)
    return jnp.concatenate([lo, hi], axis=1)


def _in_proj_body(x_ref, nw_ref, wz_ref, wuv_ref, wg_ref, wxbc_ref, wdt_ref, wdtT_ref,
                  z_ref, uv_ref, g_ref, xbc_ref, dt_ref, dtT_ref):
    x = x_ref[...]
    ms = jnp.mean(x * x, axis=-1, keepdims=True)
    hn = (x * lax.rsqrt(ms + EPS) * nw_ref[...]).astype(bf16)
    dt_ref[...] = jnp.dot(hn, wdt_ref[...], preferred_element_type=f32)
    dtT_ref[...] = lax.dot_general(wdtT_ref[...], hn, _NT, preferred_element_type=f32)
    for o_ref, w_ref in ((z_ref, wz_ref), (uv_ref, wuv_ref), (g_ref, wg_ref), (xbc_ref, wxbc_ref)):
        for c in range(0, w_ref.shape[1], PROJ_TN):
            acc = jnp.dot(hn, w_ref[:, c:c + PROJ_TN], preferred_element_type=f32)
            o_ref[:, c:c + PROJ_TN] = acc.astype(o_ref.dtype)


def _in_proj(x2d, norm_w, w_segments, w_dt, w_dtT):
    t = x2d.shape[0]
    tm = min(512, t)
    row = lambda i: (i, 0)
    fixed = lambda i: (0, 0)
    return pl.pallas_call(
        _in_proj_body,
        grid=(t // tm,),
        in_specs=[
            pl.BlockSpec((tm, D_MODEL), row),
            pl.BlockSpec((1, D_MODEL), fixed),
        ] + [pl.BlockSpec(w.shape, fixed, pipeline_mode=pl.Buffered(1)) for w in w_segments] + [
            pl.BlockSpec((D_MODEL, HEADS), fixed),
            pl.BlockSpec((HEADS, D_MODEL), fixed),
        ],
        out_specs=[
            pl.BlockSpec((tm, D_INNER), row),
            pl.BlockSpec((tm, D_INNER), row),
            pl.BlockSpec((tm, D_INNER), row),
            pl.BlockSpec((tm, CONV_DIM), row),
            pl.BlockSpec((tm, HEADS), row),
            pl.BlockSpec((HEADS, tm), lambda i: (0, i)),
        ],
        out_shape=[
            jax.ShapeDtypeStruct((t, D_INNER), bf16),
            jax.ShapeDtypeStruct((t, D_INNER), bf16),
            jax.ShapeDtypeStruct((t, D_INNER), bf16),
            jax.ShapeDtypeStruct((t, CONV_DIM), f32),
            jax.ShapeDtypeStruct((t, HEADS), f32),
            jax.ShapeDtypeStruct((HEADS, t), f32),
        ],
        compiler_params=_params("arbitrary"),
        name="in_proj",
    )(x2d, norm_w, *w_segments, w_dt, w_dtT)


def _conv_taps(shifted, cw_ref, cb_ref):
    acc = cb_ref[...]
    for k in range(CONV_W):
        acc = acc + shifted(CONV_W - 1 - k) * cw_ref[k:k + 1, :]
    return _silu(acc)


def _conv_silu_one_seq(cur, prev8, cw_ref, cb_ref):
    assert CONV_W == 4
    row8 = lax.broadcasted_iota(i32, (SUBLANES, 1), 0)

    def shift_rows(x, x_prev8, j):
        xr = pltpu.roll(x, j, 0)
        top = jnp.where(row8 >= j, xr[:SUBLANES], pltpu.roll(x_prev8, j, 0))
        return jnp.concatenate([top, xr[SUBLANES:]], axis=0)

    w = [cw_ref[k:k + 1, :] for k in range(CONV_W)]
    x1 = shift_rows(cur, prev8, 1)
    near = x1 * w[2] + cur * w[3]
    far = x1 * w[0] + cur * w[1]
    far_prev8 = pltpu.roll(prev8, 1, 0) * w[0] + prev8 * w[1]
    return _silu(cb_ref[...] + shift_rows(far, far_prev8, 2) + near)


def _conv_silu_many_seq(cur, pre, cw_ref, cb_ref, seq_len):
    n = cur.shape[0]
    pos = lax.broadcasted_iota(i32, (n, 1), 0) % seq_len

    def shifted(j):
        if j == 0:
            return cur
        return jnp.where(pos >= j, pltpu.roll(cur, j, 0), pltpu.roll(pre, (j + n - seq_len) % n, 0))

    return _conv_taps(shifted, cw_ref, cb_ref)


def _ssd_block(xa, dt_raw, dtT_raw, c):
    tril = c["tril"][...]
    dt = jax.nn.softplus(dt_raw + c["dtb_row"][...])
    dtT = jax.nn.softplus(dtT_raw + c["dtb_col"][...])
    a = dt * c["aneg_row"][...]
    aT = dtT * c["aneg_col"][...]
    a_cum = _dot_sel_lhs(tril, a)
    a_cumT = _dot_sel_rhs(aT, c["triu"][...])
    a_tot = _dot_sel_lhs(c["same"][...], a)
    stack = jnp.concatenate([dt, jnp.exp(a_cum), jnp.exp(a_tot - a_cum)], axis=0)
    ex = _dot_sel_rhs(stack, c["expand"][...], parts=2)
    n = xa.shape[0]
    dtx, eax, dex = ex[:n], ex[n:2 * n], ex[2 * n:]

    xs = xa[:, :D_INNER]
    bm = xa[:, D_INNER:D_INNER + GROUPS * D_STATE]
    cm = xa[:, D_INNER + GROUPS * D_STATE:]
    xdt = xs * dtx
    xdt_bf = xdt.astype(bf16)
    mask = tril > 0
    lane = lax.broadcasted_iota(i32, (1, 2 * HEAD_DIM), 1)
    ys = []
    for g in range(GROUPS):
        cg = cm[:, g * D_STATE:(g + 1) * D_STATE].astype(bf16)
        bg = bm[:, g * D_STATE:(g + 1) * D_STATE].astype(bf16)
        cb = lax.dot_general(cg, bg, _NT, preferred_element_type=f32)
        for pair in range(HEADS_PER_GROUP // 2):
            halves = []
            for h in (g * HEADS_PER_GROUP + 2 * pair, g * HEADS_PER_GROUP + 2 * pair + 1):
                seg = a_cum[:, h:h + 1] - a_cumT[h:h + 1, :]
                decay = jnp.exp(jnp.where(mask, seg, -jnp.inf))
                m = (cb * decay).astype(bf16)
                col = (h // 2) * 2 * HEAD_DIM
                halves.append(jnp.dot(m, xdt_bf[:, col:col + 2 * HEAD_DIM], preferred_element_type=f32))
            ys.append(jnp.where(lane < HEAD_DIM, halves[0], halves[1]))
    y_diag = jnp.concatenate(ys, axis=1)
    return dict(xs=xs, bm=bm, cm=cm, xdt=xdt, xd=xdt * dex, eax=eax, aT=aT, y_diag=y_diag)


def _gated_norm(y, z, gw):
    g = y * _silu(z.astype(f32))
    ms = jnp.mean(g * g, axis=-1, keepdims=True)
    return (g * lax.rsqrt(ms + EPS) * gw).astype(bf16)


_SSD_CONST_NAMES = ("cw", "cb", "dtb_row", "dtb_col", "aneg_row", "aneg_col", "dskipx", "gw",
                    "tril", "triu", "same", "expand")


def _ssd_consts(conv_w, conv_b, dt_bias, a_log, d_skip, gnorm_w, seq_len):
    r = np.arange(CHUNK)
    same = (r[:, None] // seq_len) == (r[None, :] // seq_len)
    tril = same & (r[None, :] <= r[:, None])
    expand = np.repeat(np.eye(HEADS, dtype=np.float32), HEAD_DIM, axis=1)
    aneg = -jnp.exp(a_log.astype(f32))
    return dict(
        cw=conv_w, cb=conv_b.reshape(1, CONV_DIM),
        dtb_row=dt_bias.reshape(1, HEADS), dtb_col=dt_bias.reshape(HEADS, 1),
        aneg_row=aneg.reshape(1, HEADS), aneg_col=aneg.reshape(HEADS, 1),
        dskipx=jnp.repeat(d_skip.astype(f32), HEAD_DIM).reshape(1, D_INNER),
        gw=gnorm_w.reshape(1, D_INNER),
        tril=jnp.asarray(tril, bf16), triu=jnp.asarray(tril.T, bf16), same=jnp.asarray(same, bf16),
        expand=jnp.asarray(expand, bf16),
    )


def _const_specs(consts):
    zero = (lambda *_: (0, 0))
    return [pl.BlockSpec(consts[k].shape, zero) for k in _SSD_CONST_NAMES]


def _ssd_prompt_body(nb, z_ref, xbc_ref, dt_ref, dtT_ref, *rest):
    nc = len(_SSD_CONST_NAMES)
    c = dict(zip(_SSD_CONST_NAMES, rest[:nc]))
    y_ref, state_ref, prev_ref, st_ref = rest[nc:]
    ci = pl.program_id(1)

    @pl.when(ci == 0)
    def _():
        prev_ref[...] = jnp.zeros_like(prev_ref)
        st_ref[...] = jnp.zeros_like(st_ref)

    for s in range(nb):
        cur = xbc_ref[s]
        xa = _conv_silu_one_seq(cur, prev_ref[s], c["cw"], c["cb"])
        prev_ref[s] = cur[CHUNK - SUBLANES:]
        b = _ssd_block(xa, dt_ref[s], dtT_ref[s], c)

        y_off = []
        for g in range(GROUPS):
            sl = slice(g * GROUP_WIDTH, (g + 1) * GROUP_WIDTH)
            st = st_ref[s * GROUPS + g]
            cg = b["cm"][:, g * D_STATE:(g + 1) * D_STATE].astype(bf16)
            y_off.append(jnp.dot(cg, st.astype(bf16), preferred_element_type=f32))
            bgT = b["bm"][:, g * D_STATE:(g + 1) * D_STATE].T.astype(bf16)
            upd = jnp.dot(bgT, b["xd"][:, sl].astype(bf16), preferred_element_type=f32)
            st_ref[s * GROUPS + g] = st * b["eax"][CHUNK - 1:CHUNK, sl] + upd
        y = b["y_diag"] + jnp.concatenate(y_off, axis=1) * b["eax"] + c["dskipx"][...] * b["xs"]
        y_ref[s] = _gated_norm(y, z_ref[s], c["gw"][...])

    @pl.when(ci == pl.num_programs(1) - 1)
    def _():
        for s in range(nb):
            for g in range(GROUPS):
                state_ref[s, 0, g * GROUP_WIDTH:(g + 1) * GROUP_WIDTH, :] = st_ref[s * GROUPS + g].T


def _ssd_prompt(z, xbc, dt, dtT, consts, batch, seq):
    nb = PROMPT_SEQS_PER_STEP if batch % PROMPT_SEQS_PER_STEP == 0 else 1
    nchunk = seq // CHUNK
    part = batch // nb * seq
    split = lambda a: a.reshape(nb, part, a.shape[-1])
    dtT3 = dtT.reshape(HEADS, nb, part).transpose(1, 0, 2)
    row = lambda b, ci: (0, b * nchunk + ci, 0)
    y, state = pl.pallas_call(
        functools.partial(_ssd_prompt_body, nb),
        grid=(batch // nb, nchunk),
        in_specs=[
            pl.BlockSpec((nb, CHUNK, D_INNER), row),
            pl.BlockSpec((nb, CHUNK, CONV_DIM), row),
            pl.BlockSpec((nb, CHUNK, HEADS), row),
            pl.BlockSpec((nb, HEADS, CHUNK), lambda b, ci: (0, 0, b * nchunk + ci)),
        ] + _const_specs(consts),
        out_specs=[
            pl.BlockSpec((nb, CHUNK, D_INNER), row),
            pl.BlockSpec((nb, 1, D_INNER, D_STATE), lambda b, ci: (0, b, 0, 0)),
        ],
        out_shape=[
            jax.ShapeDtypeStruct((nb, part, D_INNER), bf16),
            jax.ShapeDtypeStruct((nb, batch // nb, D_INNER, D_STATE), f32),
        ],
        scratch_shapes=[pltpu.VMEM((nb, SUBLANES, CONV_DIM), f32),
                        pltpu.VMEM((nb * GROUPS, D_STATE, GROUP_WIDTH), f32)],
        compiler_params=_params("arbitrary", "arbitrary"),
        name="ssd_prompt",
    )(split(z), split(xbc), split(dt), dtT3, *[consts[k] for k in _SSD_CONST_NAMES])
    return y.reshape(batch * seq, D_INNER), state.reshape(batch, D_INNER, D_STATE)


def _ssd_sample_body(seq_len, z_ref, xbc_ref, pre_ref, dt_ref, dtT_ref, state_ref, *rest):
    nc = len(_SSD_CONST_NAMES)
    c = dict(zip(_SSD_CONST_NAMES, rest[:nc]))
    y_ref, state_out_ref, c_ref, b_ref, xdT_ref, yacc_ref, eax_ref, eatT_ref = rest[nc:]
    step = pl.program_id(1)
    per_step = state_ref.shape[0]

    @pl.when(step == 0)
    def _():
        xa = _conv_silu_many_seq(xbc_ref[...], pre_ref[...], c["cw"], c["cb"], seq_len)
        b = _ssd_block(xa, dt_ref[...], dtT_ref[...], c)
        c_ref[...] = b["cm"]
        b_ref[...] = b["bm"]
        xdT_ref[...] = b["xd"].T.astype(bf16)
        yacc_ref[...] = b["y_diag"] + c["dskipx"][...] * b["xs"]
        eax_ref[...] = b["eax"]
        eatT_ref[...] = jnp.exp(_dot_sel_rhs(b["aT"], c["same"][...]))

    def one_sequence(q, carry):
        s = step * per_step + q
        r0 = pl.multiple_of(s * seq_len, seq_len)
        rows = pl.ds(r0, seq_len)
        lane = lax.broadcasted_iota(i32, (1, CHUNK), 1)
        arep = jnp.broadcast_to(jnp.sum(jnp.where(lane == r0, eatT_ref[...], 0.0), axis=1, keepdims=True), (HEADS, D_STATE))
        rmask = (lax.broadcasted_iota(i32, (CHUNK, 1), 0) // seq_len) == s
        for g in range(GROUPS):
            gs = slice(g * GROUP_WIDTH, (g + 1) * GROUP_WIDTH)
            ns = slice(g * D_STATE, (g + 1) * D_STATE)
            s0 = state_ref[q, gs, :]
            cg = c_ref[rows, ns].astype(bf16)
            yo = lax.dot_general(cg, s0.astype(bf16), _NT, preferred_element_type=f32)
            yacc_ref[rows, gs] = yacc_ref[rows, gs] + yo * eax_ref[rows, gs]
            bmask = jnp.where(rmask, b_ref[:, ns], 0.0).astype(bf16)
            upd = jnp.dot(xdT_ref[gs, :], bmask, preferred_element_type=f32)
            for r in range(HEADS_PER_GROUP):
                h = g * HEADS_PER_GROUP + r
                hs = slice(r * HEAD_DIM, (r + 1) * HEAD_DIM)
                state_out_ref[q, h * HEAD_DIM:(h + 1) * HEAD_DIM, :] = s0[hs] * arep[h:h + 1, :] + upd[hs]
        return carry

    lax.fori_loop(0, per_step, one_sequence, 0)

    @pl.when(step == pl.num_programs(1) - 1)
    def _():
        y_ref[...] = _gated_norm(yacc_ref[...], z_ref[...], c["gw"][...])


def _ssd_sample(z, xbc, pre, dt, dtT, state, consts, nseq, seq_len):
    per = CHUNK // seq_len
    nblk = nseq // per
    t = nseq * seq_len
    q = SAMPLE_SEQS_PER_STEP if per % SAMPLE_SEQS_PER_STEP == 0 else 1
    blk = lambda i, s: (i, 0)
    seq3 = lambda i, s: (i * (per // q) + s, 0, 0)
    return pl.pallas_call(
        functools.partial(_ssd_sample_body, seq_len),
        grid=(nblk, per // q),
        in_specs=[
            pl.BlockSpec((CHUNK, D_INNER), blk),
            pl.BlockSpec((CHUNK, CONV_DIM), blk),
            pl.BlockSpec((CHUNK, CONV_DIM), blk),
            pl.BlockSpec((CHUNK, HEADS), blk),
            pl.BlockSpec((HEADS, CHUNK), lambda i, s: (0, i)),
            pl.BlockSpec((q, D_INNER, D_STATE), seq3),
        ] + _const_specs(consts),
        out_specs=[
            pl.BlockSpec((CHUNK, D_INNER), blk),
            pl.BlockSpec((q, D_INNER, D_STATE), seq3),
        ],
        out_shape=[
            jax.ShapeDtypeStruct((t, D_INNER), bf16),
            jax.ShapeDtypeStruct((nseq, D_INNER, D_STATE), f32),
        ],
        scratch_shapes=[
            pltpu.VMEM((CHUNK, GROUPS * D_STATE), f32),
            pltpu.VMEM((CHUNK, GROUPS * D_STATE), f32),
            pltpu.VMEM((D_INNER, CHUNK), bf16),
            pltpu.VMEM((CHUNK, D_INNER), f32),
            pltpu.VMEM((CHUNK, D_INNER), f32),
            pltpu.VMEM((HEADS, CHUNK), f32),
        ],
        compiler_params=_params("arbitrary", "arbitrary"),
        name="ssd_sample",
    )(z, xbc, pre, dt, dtT, state, *[consts[k] for k in _SSD_CONST_NAMES])


_MIX_CONST_NAMES = ("ws", "bs", "vnw", "vnb", "wbs", "wbm", "wo", "nffn", "wrT", "br", "tris")


def _mix_body(emit_v, uv_ref, gates_ref, yssd_ref, x_ref, *rest):
    nc = len(_MIX_CONST_NAMES)
    c = dict(zip(_MIX_CONST_NAMES, rest[:nc]))
    outs = list(rest[nc:])
    h_ref, hfp_ref = outs[:2]
    v_ref = outs[2] if emit_v else None
    te_ref, gate_ref, rank_ref, cnt_ref, cnt_acc = outs[-5:]
    tm = uv_ref.shape[0]

    @pl.when(pl.program_id(0) == 0)
    def _():
        cnt_acc[...] = jnp.zeros_like(cnt_acc)

    uv = uv_ref[...].astype(f32)
    uv = 0.5 * uv * (1.0 + lax.erf(uv * np.float32(np.sqrt(0.5))))
    u, v = uv[:, :D_GMLP], uv[:, D_GMLP:]
    mu = jnp.mean(v, axis=-1, keepdims=True)
    vc = v - mu
    vn = vc * lax.rsqrt(jnp.mean(vc * vc, axis=-1, keepdims=True) + EPS) * c["vnw"][...] + c["vnb"][...]
    if emit_v:
        v_ref[...] = vn
    vn_bf = vn.astype(bf16)
    gd = D_GMLP // GMLP_HEADS
    rows = []
    for ck in range(tm // CHUNK):
        rs = slice(ck * CHUNK, (ck + 1) * CHUNK)
        heads = []
        for g in range(GMLP_HEADS):
            mixed = jnp.dot(c["ws"][g], vn_bf[rs, g * gd:(g + 1) * gd], preferred_element_type=f32)
            heads.append(mixed + c["bs"][:, g:g + 1])
        rows.append(jnp.concatenate(heads, axis=1))
    y_mlp = u * jnp.concatenate(rows, axis=0)

    a = jnp.dot(yssd_ref[...], c["wbs"][...], preferred_element_type=f32)
    b = jnp.dot(y_mlp.astype(bf16), c["wbm"][...], preferred_element_type=f32)
    gs = jax.nn.sigmoid(gates_ref[...].astype(f32))
    merged = gs[:, :D_MODEL] * a + gs[:, D_MODEL:] * b
    h = x_ref[...] + jnp.dot(merged.astype(bf16), c["wo"][...], preferred_element_type=f32)
    h_ref[...] = h
    hf = h * lax.rsqrt(jnp.mean(h * h, axis=-1, keepdims=True) + EPS) * c["nffn"][...]
    hfp_ref[...] = _pack_bf16_pair(hf)

    lg = lax.dot_general(c["wrT"][...], hf, _NT, precision=lax.Precision.HIGHEST, preferred_element_type=f32) + c["br"][...]
    sub = lax.broadcasted_iota(i32, lg.shape, 0)
    idxs, vals = [], []
    for _ in range(TOP_K):
        m = jnp.max(lg, axis=0, keepdims=True)
        idx = jnp.min(jnp.where(lg == m, sub, N_EXPERTS), axis=0, keepdims=True)
        idxs.append(idx)
        vals.append(m)
        lg = jnp.where(sub == idx, -jnp.inf, lg)
    p = jnp.exp(jnp.concatenate(vals, axis=0) - vals[0])
    gate_ref[...] = p / jnp.sum(p, axis=0, keepdims=True)
    te_ref[...] = jnp.concatenate(idxs, axis=0)

    onehots = [sub == idx for idx in idxs]
    member = functools.reduce(jnp.logical_or, onehots).astype(f32)
    before = jnp.dot(member.astype(bf16), c["tris"][...], preferred_element_type=f32) + cnt_acc[:, 0:1]
    rank_ref[...] = jnp.concatenate(
        [jnp.sum(jnp.where(oh, before, 0.0), axis=0, keepdims=True) for oh in onehots], axis=0).astype(i32)
    cnt_acc[...] = cnt_acc[...] + jnp.sum(member, axis=1, keepdims=True)
    cnt_ref[...] = cnt_acc[...].astype(i32)


def _mix(uv, gates, yssd, x2d, mc, tm, emit_v):
    t = x2d.shape[0]
    row = lambda i: (i, 0)
    col = lambda i: (0, i)
    const = lambda a: pl.BlockSpec(a.shape, lambda i: (0,) * a.ndim, pipeline_mode=pl.Buffered(1))
    v_spec = [pl.BlockSpec((tm, D_GMLP), row)] if emit_v else []
    v_shape = [jax.ShapeDtypeStruct((t, D_GMLP), f32)] if emit_v else []
    return pl.pallas_call(
        functools.partial(_mix_body, emit_v),
        grid=(t // tm,),
        in_specs=[
            pl.BlockSpec((tm, D_INNER), row),
            pl.BlockSpec((tm, D_INNER), row),
            pl.BlockSpec((tm, D_INNER), row),
            pl.BlockSpec((tm, D_MODEL), row),
        ] + [const(mc[k]) for k in _MIX_CONST_NAMES],
        out_specs=[pl.BlockSpec((tm, D_MODEL), row), pl.BlockSpec((tm, D_PACK), row)] + v_spec + [
            pl.BlockSpec((TOP_K, tm), col),
            pl.BlockSpec((TOP_K, tm), col),
            pl.BlockSpec((TOP_K, tm), col),
            pl.BlockSpec((N_EXPERTS, 128), lambda i: (0, 0)),
        ],
        out_shape=[
            jax.ShapeDtypeStruct((t, D_MODEL), f32),
            jax.ShapeDtypeStruct((t, D_PACK), u32),
        ] + v_shape + [
            jax.ShapeDtypeStruct((TOP_K, t), i32),
            jax.ShapeDtypeStruct((TOP_K, t), f32),
            jax.ShapeDtypeStruct((TOP_K, t), i32),
            jax.ShapeDtypeStruct((N_EXPERTS, 128), i32),
        ],
        scratch_shapes=[pltpu.VMEM((N_EXPERTS, 128), f32)],
        compiler_params=_params("arbitrary"),
        name="mix_route",
    )(uv, gates, yssd, x2d, *[mc[k] for k in _MIX_CONST_NAMES])


def _dispatch(slot_kt, hf_all, n_slots):
    info = plsc.get_sparse_core_info()
    workers = info.num_cores * info.num_subcores
    t = hf_all.shape[0]
    per_w = t // workers
    nch = per_w // SC_CHUNK
    assert per_w * workers == t and nch * SC_CHUNK == per_w
    slots = slot_kt.reshape(TOP_K, workers, nch, SC_CHUNK).transpose(1, 0, 2, 3)
    mesh = plsc.VectorSubcoreMesh(core_axis_name="c", subcore_axis_name="s")

    @functools.partial(
        pl.kernel, mesh=mesh,
        out_type=jax.ShapeDtypeStruct((n_slots, D_PACK), u32),
        scratch_types=[pltpu.VMEM((TOP_K, nch, SC_CHUNK), i32), pltpu.VMEM((2, SC_CHUNK, D_PACK), u32)]
        + [pltpu.SemaphoreType.DMA] * 4,
    )
    def scatter_rows(hf_hbm, slot_hbm, out_hbm, idx_v, rows_v, sin0, sin1, sout0, sout1):
        sin, sout = (sin0, sin1), (sout0, sout1)
        wid = lax.axis_index("s") * info.num_cores + lax.axis_index("c")
        base = wid * per_w
        pltpu.sync_copy(slot_hbm.at[wid], idx_v)

        def load(j):
            rows = pl.ds(pl.multiple_of(base + j * SC_CHUNK, SUBLANES), SC_CHUNK)
            return pltpu.make_async_copy(hf_hbm.at[rows], rows_v.at[j % 2], sin[j % 2])

        def scatters(j):
            return [pltpu.make_async_copy(rows_v.at[j % 2], out_hbm.at[idx_v.at[k, j]], sout[j % 2]) for k in range(TOP_K)]

        load(0).start()
        for j in range(nch):
            load(j).wait()
            if j >= 1:
                for cp in scatters(j - 1):
                    cp.wait()
            if j + 1 < nch:
                load(j + 1).start()
            for cp in scatters(j):
                cp.start()
        for cp in scatters(nch - 1):
            cp.wait()

    return scatter_rows(hf_all, slots)


def _experts_body(be_ref, nused_ref, start_ref, slot_ref, next_ref, nvalid_ref, xs_ref, bgu_ref, bd_ref, perm_ref, wgu_hbm, wd_hbm,
                  ys_ref, wgu_f32, wd_f32, wgu_bf, wd_bf, sems):
    i = pl.program_id(0)

    def weight_copies(e, slot):
        return (pltpu.make_async_copy(wgu_hbm.at[e], wgu_f32.at[slot], sems.at[0, slot]),
                pltpu.make_async_copy(wd_hbm.at[e], wd_f32.at[slot], sems.at[1, slot]))

    @pl.when(i < nused_ref[0])
    def _():
        @pl.when(start_ref[i] == 1)
        def _():
            slot = slot_ref[i]

            @pl.when(i == 0)
            def _():
                for cp in weight_copies(be_ref[0], 0):
                    cp.start()

            for cp in weight_copies(be_ref[i], slot):
                cp.wait()

            @pl.when(next_ref[i] >= 0)
            def _():
                for cp in weight_copies(next_ref[i], 1 - slot):
                    cp.start()

            wd_bf[...] = wd_f32[slot].astype(bf16)
            half = DEINT // 2
            for j in range(2 * D_FF // DEINT):
                w = jnp.dot(wgu_f32[slot, :, j * DEINT:(j + 1) * DEINT].astype(bf16), perm_ref[...],
                            preferred_element_type=f32).astype(bf16)
                wgu_bf[:, j * half:(j + 1) * half] = w[:, :half]
                wgu_bf[:, D_FF + j * half:D_FF + (j + 1) * half] = w[:, half:]

        row = lax.broadcasted_iota(i32, (MOE_BLOCK, 1), 0)
        xs = jnp.where(row < nvalid_ref[i], xs_ref[...], jnp.zeros((), u32))
        gu = jnp.dot(_unpack_bf16_pair(xs), wgu_bf[...], preferred_element_type=f32) + bgu_ref[0]
        glu = jnp.minimum(gu[:, :D_FF], SWIGLU_LIMIT)
        lin = jnp.clip(gu[:, D_FF:], -SWIGLU_LIMIT, SWIGLU_LIMIT)
        act = glu * jax.nn.sigmoid(SWIGLU_ALPHA * glu) * (lin + 1.0)
        ys_ref[...] = jnp.dot(act.astype(bf16), wd_bf[...], preferred_element_type=f32) + bd_ref[0]

    @pl.when(i >= nused_ref[0])
    def _():
        ys_ref[...] = jnp.zeros_like(ys_ref)


def _expert_segments(block_e, n_used, n_blocks):
    idx = jnp.arange(n_blocks, dtype=i32)
    prev_e = jnp.concatenate([block_e[:1], block_e[:-1]])
    start = (idx < n_used) & ((idx == 0) | (block_e != prev_e))
    slot = (jnp.cumsum(start.astype(i32)) - 1) % 2
    start_idx = jnp.where(start, idx, n_blocks)
    from_here = lax.cummin(start_idx, reverse=True)
    next_start = jnp.concatenate([from_here[1:], jnp.full((1,), n_blocks, i32)])
    next_e = jnp.sum(jnp.where(next_start[:, None] == idx[None, :], block_e[None, :], 0), axis=1)
    next_e = jnp.where(next_start < n_blocks, next_e, -1)
    return start.astype(i32), slot.astype(i32), next_e.astype(i32)


def _experts(block_e, n_used, n_valid, xs, wgu, bgu, w_down, b_down):
    n_blocks = xs.shape[0] // MOE_BLOCK
    start, slot, next_e = _expert_segments(block_e, n_used[0], n_blocks)
    blk = lambda i, be, nu, *_: (jnp.maximum(jnp.minimum(i, nu[0] - 1), 0), 0)
    exp3 = lambda i, be, *_: (be[i], 0, 0)
    c = np.arange(DEINT)
    src = np.where(c < DEINT // 2, 2 * c, 2 * (c - DEINT // 2) + 1)
    perm = jnp.asarray(np.arange(DEINT)[:, None] == src[None, :], bf16)
    return pl.pallas_call(
        _experts_body,
        grid_spec=pltpu.PrefetchScalarGridSpec(
            num_scalar_prefetch=6,
            grid=(n_blocks,),
            in_specs=[
                pl.BlockSpec((MOE_BLOCK, D_PACK), blk),
                pl.BlockSpec((1, 1, 2 * D_FF), exp3),
                pl.BlockSpec((1, 1, D_MODEL), exp3),
                pl.BlockSpec((DEINT, DEINT), lambda i, *_: (0, 0)),
                pl.BlockSpec(memory_space=pl.ANY),
                pl.BlockSpec(memory_space=pl.ANY),
            ],
            out_specs=pl.BlockSpec((MOE_BLOCK, D_MODEL), lambda i, *_: (i, 0)),
            scratch_shapes=[
                pltpu.VMEM((2, D_MODEL, 2 * D_FF), f32),
                pltpu.VMEM((2, D_FF, D_MODEL), f32),
                pltpu.VMEM((D_MODEL, 2 * D_FF), bf16),
                pltpu.VMEM((D_FF, D_MODEL), bf16),
                pltpu.SemaphoreType.DMA((2, 2)),
            ],
        ),
        out_shape=jax.ShapeDtypeStruct((xs.shape[0], D_MODEL), f32),
        compiler_params=_params("arbitrary"),
        name="moe_experts",
    )(block_e, n_used, start, slot, next_e, n_valid, xs, bgu, b_down, perm, wgu, w_down)


def _combine_body(tm, t, slot_ref, h_ref, gate_ref, nf_ref, ys_ref, o_ref, rows_ref, sems):
    s = pl.program_id(0)
    n_tiles = pl.num_programs(0) - 1

    def request():
        buf = s % 2
        for r in range(tm):
            for k in range(TOP_K):
                slot = slot_ref[k * t + s * tm + r]
                pltpu.make_async_copy(ys_ref.at[slot], rows_ref.at[buf, k, r], sems.at[buf]).start(priority=k % 2)

    def finish():
        buf = (s - 1) % 2
        for r in range(tm):
            for k in range(TOP_K):
                pltpu.make_async_copy(ys_ref.at[0], rows_ref.at[buf, k, r], sems.at[buf]).wait()
        g = gate_ref[...]
        moe = g[:, 0:1] * rows_ref[buf, 0]
        for k in range(1, TOP_K):
            moe = moe + g[:, k:k + 1] * rows_ref[buf, k]
        h = h_ref[...] + moe
        o_ref[...] = h * lax.rsqrt(jnp.mean(h * h, axis=-1, keepdims=True) + EPS) * nf_ref[...]

    @pl.when(s < n_tiles)
    def _():
        request()

    @pl.when(s > 0)
    def _():
        finish()


def _combine(slot_flat, h, gate_t, norm_final, ys, tm):
    t = h.shape[0]
    return pl.pallas_call(
        functools.partial(_combine_body, tm, t),
        grid_spec=pltpu.PrefetchScalarGridSpec(
            num_scalar_prefetch=1,
            grid=(t // tm + 1,),
            in_specs=[
                pl.BlockSpec((tm, D_MODEL), lambda i, s: (jnp.maximum(i - 1, 0), 0)),
                pl.BlockSpec((tm, TOP_K), lambda i, s: (jnp.maximum(i - 1, 0), 0)),
                pl.BlockSpec((1, D_MODEL), lambda i, s: (0, 0)),
                pl.BlockSpec(memory_space=pl.ANY),
            ],
            out_specs=pl.BlockSpec((tm, D_MODEL), lambda i, s: (jnp.maximum(i - 1, 0), 0)),
            scratch_shapes=[pltpu.VMEM((2, TOP_K, tm, D_MODEL), f32), pltpu.SemaphoreType.DMA((2,))],
        ),
        out_shape=jax.ShapeDtypeStruct((t, D_MODEL), f32),
        compiler_params=_params("arbitrary"),
        name="moe_combine",
    )(slot_flat, h, gate_t, norm_final, ys)


def kernel(x_prompt, x_sample, state_conv, state_ssm, norm_mix, w_in, conv_w, conv_b, dt_bias, a_log, d_skip, gnorm_w, v_norm_w, v_norm_b, w_spatial, b_spatial, w_branch_ssd, w_branch_mlp, w_out, norm_ffn, w_router, b_router, w_gu, b_gu, w_down, b_down, norm_final):
    assert w_in.shape[0] == 1, "single-layer trunk"
    batch, seq, _ = x_prompt.shape
    nseq, dec_seq, _ = x_sample.shape
    assert seq % CHUNK == 0 and CHUNK % dec_seq == 0 and nseq % (CHUNK // dec_seq) == 0 and dec_seq >= CONV_W - 1
    t_p, t_s = batch * seq, nseq * dec_seq
    tm = ROW_DMA_TILE
    assert t_p % tm == 0 and t_s % tm == 0
    tm_mix = MIX_TILE if t_p % MIX_TILE == 0 and t_s % MIX_TILE == 0 else tm

    wi = w_in[0]
    z0, x0, d0, u0 = D_INNER, D_INNER + CONV_DIM, D_INNER + CONV_DIM + HEADS, D_INNER + CONV_DIM + HEADS + 2 * D_GMLP
    w_segments = tuple(wi[:, a:b].astype(bf16) for a, b in ((0, z0), (d0, u0), (u0, D_IN_PROJ), (z0, x0)))
    w_dt = wi[:, x0:d0].astype(bf16)
    w_dtT = w_dt.T
    nm = norm_mix[0].reshape(1, D_MODEL)
    sc = lambda L: _ssd_consts(conv_w[0], conv_b[0], dt_bias[0], a_log[0], d_skip[0], gnorm_w[0], L)
    xp2, xs2 = x_prompt.reshape(t_p, D_MODEL), x_sample.reshape(t_s, D_MODEL)

    tril = np.tril(np.ones((CHUNK, CHUNK), bool))
    ws_p = jnp.where(tril[None], w_spatial[0], 0).astype(bf16)
    per = CHUNK // dec_seq
    blockdiag = (np.arange(CHUNK)[:, None] // dec_seq) == (np.arange(CHUNK)[None, :] // dec_seq)
    ws_s = jnp.where((tril & blockdiag)[None], jnp.tile(w_spatial[0][:, :dec_seq, :dec_seq], (1, per, per)), 0).astype(bf16)
    bs_p = b_spatial[0].T
    bs_s = jnp.tile(b_spatial[0][:, :dec_seq], (1, per)).T
    mc = dict(
        vnw=v_norm_w[0].reshape(1, D_GMLP), vnb=v_norm_b[0].reshape(1, D_GMLP),
        wbs=w_branch_ssd[0].astype(bf16), wbm=w_branch_mlp[0].astype(bf16), wo=w_out[0].astype(bf16),
        nffn=norm_ffn[0].reshape(1, D_MODEL), wrT=w_router[0].T, br=b_router[0].reshape(N_EXPERTS, 1),
        tris=jnp.asarray(np.triu(np.ones((tm_mix, tm_mix), np.float32), 1), bf16),
    )

    z_p, uv_p, g_p, xbc_p, dt_p, dtT_p = _in_proj(xp2, nm, w_segments, w_dt, w_dtT)
    yssd_p, ssm_p = _ssd_prompt(z_p, xbc_p, dt_p, dtT_p, sc(CHUNK), batch, seq)
    h_p, hf_p, te_p, gate_p, rank_p, cnt_p = _mix(uv_p, g_p, yssd_p, xp2, dict(mc, ws=ws_p, bs=bs_p), tm_mix, False)

    z_s, uv_s, g_s, xbc_s, dt_s, dtT_s = _in_proj(xs2, nm, w_segments, w_dt, w_dtT)
    pre = jnp.pad(state_conv[0], ((0, 0), (dec_seq - (CONV_W - 1), 0), (0, 0))).reshape(t_s, CONV_DIM)
    yssd_s, ssm_s = _ssd_sample(z_s, xbc_s, pre, dt_s, dtT_s, state_ssm[0].reshape(nseq, D_INNER, D_STATE),
                                sc(dec_seq), nseq, dec_seq)
    h_s, hf_s, v_s, te_s, gate_s, rank_s, cnt_s = _mix(uv_s, g_s, yssd_s, xs2, dict(mc, ws=ws_s, bs=bs_s), tm_mix, True)

    cp, cs = cnt_p[:, 0], cnt_s[:, 0]
    padded = (cp + cs + MOE_BLOCK - 1) // MOE_BLOCK * MOE_BLOCK
    pend = jnp.cumsum(padded)
    pstart = pend - padded

    def lookup(table, te):
        eid = jnp.arange(N_EXPERTS, dtype=i32)[:, None, None]
        return jnp.sum(jnp.where(te[None] == eid, table[:, None, None], 0), axis=0)

    slot_kt_p = (lookup(pstart, te_p) + rank_p).astype(i32)
    slot_kt_s = (lookup(pstart + cp, te_s) + rank_s).astype(i32)
    slot_p, slot_s = slot_kt_p.reshape(-1), slot_kt_s.reshape(-1)
    n_blocks = (t_p + t_s) * TOP_K // MOE_BLOCK + N_EXPERTS
    n_used = (pend[-1] // MOE_BLOCK).astype(i32).reshape(1)
    first_row = jnp.arange(n_blocks, dtype=i32) * MOE_BLOCK
    block_e = jnp.minimum(jnp.sum(pend[None, :] <= first_row[:, None], axis=1), N_EXPERTS - 1).astype(i32)
    last_row = jnp.sum(jnp.where(block_e[:, None] == jnp.arange(N_EXPERTS)[None, :], (pstart + cp + cs)[None, :], 0), axis=1)
    n_valid = jnp.clip(last_row - first_row, 0, MOE_BLOCK).astype(i32)

    xs_sorted = _dispatch(jnp.concatenate([slot_kt_p, slot_kt_s], axis=1), jnp.concatenate([hf_p, hf_s]),
                          n_blocks * MOE_BLOCK)
    bgu = jnp.concatenate([b_gu[0][..., 0::2], b_gu[0][..., 1::2]], axis=-1).reshape(N_EXPERTS, 1, 2 * D_FF)
    ys = _experts(block_e, n_used, n_valid, xs_sorted, w_gu[0], bgu, w_down[0], b_down[0].reshape(N_EXPERTS, 1, D_MODEL))

    nf = norm_final.reshape(1, D_MODEL)
    y_p = _combine(slot_p, h_p, gate_p.T, nf, ys, tm)
    y_s = _combine(slot_s, h_s, gate_s.T, nf, ys, tm)

    conv_p = xbc_p.reshape(batch, seq, CONV_DIM)[:, seq - (CONV_W - 1):]
    conv_s = xbc_s.reshape(nseq, dec_seq, CONV_DIM)[:, dec_seq - (CONV_W - 1):]
    st_shape = (HEADS, HEAD_DIM, D_STATE)
    return (
        y_p.reshape(batch, seq, D_MODEL),
        y_s.reshape(nseq, dec_seq, D_MODEL),
        conv_p[None],
        ssm_p.reshape(1, batch, *st_shape),
        conv_s[None],
        ssm_s.reshape(1, nseq, *st_shape),
        v_s.reshape(1, nseq, dec_seq, D_GMLP),
    )
```

```python
import functools

import numpy as np
import jax
import jax.numpy as jnp
from jax import lax
from jax.experimental import pallas as pl
from jax.experimental.pallas import tpu as pltpu
from jax.experimental.pallas import tpu_sc as plsc

f32, bf16, i32, u32 = jnp.float32, jnp.bfloat16, jnp.int32, jnp.uint32

D_MODEL = 1024
D_INNER = 2048
HEAD_DIM = 64
HEADS = 32
GROUPS = 4
HEADS_PER_GROUP = 8
GROUP_WIDTH = HEADS_PER_GROUP * HEAD_DIM
D_STATE = 128
CONV_W = 4
CONV_DIM = D_INNER + 2 * GROUPS * D_STATE
CHUNK = 128
SUBLANES = 8
D_GMLP = 1024
GMLP_HEADS = 8
N_EXPERTS = 32
TOP_K = 4
D_FF = 1024
SWIGLU_ALPHA = 1.702
SWIGLU_LIMIT = 7.0
EPS = 1e-5
MOE_BLOCK = 512
DEINT = 256
ROW_DMA_TILE = 256
SC_CHUNK = 32
MIX_TILE = 512
D_IN_PROJ = 3 * D_INNER + CONV_DIM + HEADS
PROJ_TN = 512
PROMPT_SEQS_PER_STEP = 2
SAMPLE_SEQS_PER_STEP = 8
D_PACK = D_MODEL // 2
VMEM_LIMIT = 56 * 1024 * 1024

_NT = (((1,), (1,)), ((), ()))
_HI16 = np.uint32(0xFFFF0000)


def _params(*sem):
    return pltpu.CompilerParams(dimension_semantics=sem, vmem_limit_bytes=VMEM_LIMIT)


def _split(x, parts):
    out = []
    for _ in range(parts - 1):
        p = x.astype(bf16)
        out.append(p)
        x = x - p.astype(f32)
    out.append(x.astype(bf16))
    return out


def _dot_sel_rhs(x, sel, parts=3):
    return sum(jnp.dot(p, sel, preferred_element_type=f32) for p in _split(x, parts))


def _dot_sel_lhs(sel, x, parts=3):
    return sum(jnp.dot(sel, p, preferred_element_type=f32) for p in _split(x, parts))


def _silu(x):
    return x * jax.nn.sigmoid(x)


def _pack_bf16_pair(x):
    n = x.shape[1] // 2
    lo = lax.bitcast_convert_type(x[:, :n].astype(bf16).astype(f32), u32) >> 16
    hi = lax.bitcast_convert_type(x[:, n:].astype(bf16).astype(f32), u32) & _HI16
    return lo | hi


def _unpack_bf16_pair(p):
    lo = lax.bitcast_convert_type(p << 16, f32).astype(bf16)
    hi = lax.bitcast_convert_type(p & _HI16, f32).astype(bf16)
    return jnp.concatenate([lo, hi], axis=1)


def _in_proj_body(x_ref, nw_ref, wz_ref, wuv_ref, wg_ref, wxbc_ref, wdt_ref, wdtT_ref,
                  z_ref, uv_ref, g_ref, xbc_ref, dt_ref, dtT_ref):
    x = x_ref[...]
    ms = jnp.mean(x * x, axis=-1, keepdims=True)
    hn = (x * lax.rsqrt(ms + EPS) * nw_ref[...]).astype(bf16)
    dt_ref[...] = jnp.dot(hn, wdt_ref[...], preferred_element_type=f32)
    dtT_ref[...] = lax.dot_general(wdtT_ref[...], hn, _NT, preferred_element_type=f32)
    for o_ref, w_ref in ((z_ref, wz_ref), (uv_ref, wuv_ref), (g_ref, wg_ref), (xbc_ref, wxbc_ref)):
        for c in range(0, w_ref.shape[1], PROJ_TN):
            acc = jnp.dot(hn, w_ref[:, c:c + PROJ_TN], preferred_element_type=f32)
            o_ref[:, c:c + PROJ_TN] = acc.astype(o_ref.dtype)


def _in_proj(x2d, norm_w, w_segments, w_dt, w_dtT):
    t = x2d.shape[0]
    tm = min(512, t)
    row = lambda i: (i, 0)
    fixed = lambda i: (0, 0)
    return pl.pallas_call(
        _in_proj_body,
        grid=(t // tm,),
        in_specs=[
            pl.BlockSpec((tm, D_MODEL), row),
            pl.BlockSpec((1, D_MODEL), fixed),
        ] + [pl.BlockSpec(w.shape, fixed, pipeline_mode=pl.Buffered(1)) for w in w_segments] + [
            pl.BlockSpec((D_MODEL, HEADS), fixed),
            pl.BlockSpec((HEADS, D_MODEL), fixed),
        ],
        out_specs=[
            pl.BlockSpec((tm, D_INNER), row),
            pl.BlockSpec((tm, D_INNER), row),
            pl.BlockSpec((tm, D_INNER), row),
            pl.BlockSpec((tm, CONV_DIM), row),
            pl.BlockSpec((tm, HEADS), row),
            pl.BlockSpec((HEADS, tm), lambda i: (0, i)),
        ],
        out_shape=[
            jax.ShapeDtypeStruct((t, D_INNER), bf16),
            jax.ShapeDtypeStruct((t, D_INNER), bf16),
            jax.ShapeDtypeStruct((t, D_INNER), bf16),
            jax.ShapeDtypeStruct((t, CONV_DIM), f32),
            jax.ShapeDtypeStruct((t, HEADS), f32),
            jax.ShapeDtypeStruct((HEADS, t), f32),
        ],
        compiler_params=_params("arbitrary"),
        name="in_proj",
    )(x2d, norm_w, *w_segments, w_dt, w_dtT)


def _conv_taps(shifted, cw_ref, cb_ref):
    acc = cb_ref[...]
    for k in range(CONV_W):
        acc = acc + shifted(CONV_W - 1 - k) * cw_ref[k:k + 1, :]
    return _silu(acc)


def _conv_silu_one_seq(cur, prev8, cw_ref, cb_ref):
    assert CONV_W == 4
    row8 = lax.broadcasted_iota(i32, (SUBLANES, 1), 0)

    def shift_rows(x, x_prev8, j):
        xr = pltpu.roll(x, j, 0)
        top = jnp.where(row8 >= j, xr[:SUBLANES], pltpu.roll(x_prev8, j, 0))
        return jnp.concatenate([top, xr[SUBLANES:]], axis=0)

    w = [cw_ref[k:k + 1, :] for k in range(CONV_W)]
    x1 = shift_rows(cur, prev8, 1)
    near = x1 * w[2] + cur * w[3]
    far = x1 * w[0] + cur * w[1]
    far_prev8 = pltpu.roll(prev8, 1, 0) * w[0] + prev8 * w[1]
    return _silu(cb_ref[...] + shift_rows(far, far_prev8, 2) + near)


def _conv_silu_many_seq(cur, pre, cw_ref, cb_ref, seq_len):
    n = cur.shape[0]
    pos = lax.broadcasted_iota(i32, (n, 1), 0) % seq_len

    def shifted(j):
        if j == 0:
            return cur
        return jnp.where(pos >= j, pltpu.roll(cur, j, 0), pltpu.roll(pre, (j + n - seq_len) % n, 0))

    return _conv_taps(shifted, cw_ref, cb_ref)


def _ssd_block(xa, dt_raw, dtT_raw, c):
    tril = c["tril"][...]
    dt = jax.nn.softplus(dt_raw + c["dtb_row"][...])
    dtT = jax.nn.softplus(dtT_raw + c["dtb_col"][...])
    a = dt * c["aneg_row"][...]
    aT = dtT * c["aneg_col"][...]
    a_cum = _dot_sel_lhs(tril, a)
    a_cumT = _dot_sel_rhs(aT, c["triu"][...])
    a_tot = _dot_sel_lhs(c["same"][...], a)
    stack = jnp.concatenate([dt, jnp.exp(a_cum), jnp.exp(a_tot - a_cum)], axis=0)
    ex = jnp.dot(jnp.concatenate(_split(stack, 2), axis=1), c["expand"][...], preferred_element_type=f32)
    n = xa.shape[0]
    dtx, eax, dex = ex[:n], ex[n:2 * n], ex[2 * n:]

    xs = xa[:, :D_INNER]
    bm = xa[:, D_INNER:D_INNER + GROUPS * D_STATE]
    cm = xa[:, D_INNER + GROUPS * D_STATE:]
    xdt = xs * dtx
    xdt_bf = xdt.astype(bf16)
    mask = tril > 0
    lane = lax.broadcasted_iota(i32, (1, 2 * HEAD_DIM), 1)
    ys = []
    for g in range(GROUPS):
        cg = cm[:, g * D_STATE:(g + 1) * D_STATE].astype(bf16)
        bg = bm[:, g * D_STATE:(g + 1) * D_STATE].astype(bf16)
        cb = lax.dot_general(cg, bg, _NT, preferred_element_type=f32)
        for pair in range(HEADS_PER_GROUP // 2):
            halves = []
            for h in (g * HEADS_PER_GROUP + 2 * pair, g * HEADS_PER_GROUP + 2 * pair + 1):
                seg = a_cum[:, h:h + 1] - a_cumT[h:h + 1, :]
                decay = jnp.exp(jnp.where(mask, seg, -jnp.inf))
                m = (cb * decay).astype(bf16)
                col = (h // 2) * 2 * HEAD_DIM
                halves.append(jnp.dot(m, xdt_bf[:, col:col + 2 * HEAD_DIM], preferred_element_type=f32))
            ys.append(jnp.where(lane < HEAD_DIM, halves[0], halves[1]))
    y_diag = jnp.concatenate(ys, axis=1)
    return dict(xs=xs, bm=bm, cm=cm, xdt=xdt, xd=xdt * dex, eax=eax, aT=aT, y_diag=y_diag)


def _gated_norm(y, z, gw):
    g = y * _silu(z.astype(f32))
    ms = jnp.mean(g * g, axis=-1, keepdims=True)
    return (g * lax.rsqrt(ms + EPS) * gw).astype(bf16)


_SSD_CONST_NAMES = ("cw", "cb", "dtb_row", "dtb_col", "aneg_row", "aneg_col", "dskipx", "gw",
                    "tril", "triu", "same", "expand")


def _ssd_consts(conv_w, conv_b, dt_bias, a_log, d_skip, gnorm_w, seq_len):
    r = np.arange(CHUNK)
    same = (r[:, None] // seq_len) == (r[None, :] // seq_len)
    tril = same & (r[None, :] <= r[:, None])
    expand = np.tile(np.repeat(np.eye(HEADS, dtype=np.float32), HEAD_DIM, axis=1), (2, 1))
    aneg = -jnp.exp(a_log.astype(f32))
    return dict(
        cw=conv_w, cb=conv_b.reshape(1, CONV_DIM),
        dtb_row=dt_bias.reshape(1, HEADS), dtb_col=dt_bias.reshape(HEADS, 1),
        aneg_row=aneg.reshape(1, HEADS), aneg_col=aneg.reshape(HEADS, 1),
        dskipx=jnp.repeat(d_skip.astype(f32), HEAD_DIM).reshape(1, D_INNER),
        gw=gnorm_w.reshape(1, D_INNER),
        tril=jnp.asarray(tril, bf16), triu=jnp.asarray(tril.T, bf16), same=jnp.asarray(same, bf16),
        expand=jnp.asarray(expand, bf16),
    )


def _const_specs(consts):
    zero = (lambda *_: (0, 0))
    return [pl.BlockSpec(consts[k].shape, zero) for k in _SSD_CONST_NAMES]


def _ssd_prompt_body(nb, z_ref, xbc_ref, dt_ref, dtT_ref, *rest):
    nc = len(_SSD_CONST_NAMES)
    c = dict(zip(_SSD_CONST_NAMES, rest[:nc]))
    y_ref, state_ref, prev_ref, st_ref = rest[nc:]
    ci = pl.program_id(1)

    @pl.when(ci == 0)
    def _():
        prev_ref[...] = jnp.zeros_like(prev_ref)
        st_ref[...] = jnp.zeros_like(st_ref)

    for s in range(nb):
        cur = xbc_ref[s]
        xa = _conv_silu_one_seq(cur, prev_ref[s], c["cw"], c["cb"])
        prev_ref[s] = cur[CHUNK - SUBLANES:]
        b = _ssd_block(xa, dt_ref[s], dtT_ref[s], c)

        y_off = []
        for g in range(GROUPS):
            sl = slice(g * GROUP_WIDTH, (g + 1) * GROUP_WIDTH)
            st = st_ref[s * GROUPS + g]
            cg = b["cm"][:, g * D_STATE:(g + 1) * D_STATE].astype(bf16)
            y_off.append(jnp.dot(cg, st.astype(bf16), preferred_element_type=f32))
            bgT = b["bm"][:, g * D_STATE:(g + 1) * D_STATE].T.astype(bf16)
            upd = jnp.dot(bgT, b["xd"][:, sl].astype(bf16), preferred_element_type=f32)
            st_ref[s * GROUPS + g] = st * b["eax"][CHUNK - 1:CHUNK, sl] + upd
        y = b["y_diag"] + jnp.concatenate(y_off, axis=1) * b["eax"] + c["dskipx"][...] * b["xs"]
        y_ref[s] = _gated_norm(y, z_ref[s], c["gw"][...])

    @pl.when(ci == pl.num_programs(1) - 1)
    def _():
        for s in range(nb):
            for g in range(GROUPS):
                state_ref[s, 0, g * GROUP_WIDTH:(g + 1) * GROUP_WIDTH, :] = st_ref[s * GROUPS + g].T


def _ssd_prompt(z, xbc, dt, dtT, consts, batch, seq):
    nb = PROMPT_SEQS_PER_STEP if batch % PROMPT_SEQS_PER_STEP == 0 else 1
    nchunk = seq // CHUNK
    part = batch // nb * seq
    split = lambda a: a.reshape(nb, part, a.shape[-1])
    dtT3 = dtT.reshape(HEADS, nb, part).transpose(1, 0, 2)
    row = lambda b, ci: (0, b * nchunk + ci, 0)
    y, state = pl.pallas_call(
        functools.partial(_ssd_prompt_body, nb),
        grid=(batch // nb, nchunk),
        in_specs=[
            pl.BlockSpec((nb, CHUNK, D_INNER), row),
            pl.BlockSpec((nb, CHUNK, CONV_DIM), row),
            pl.BlockSpec((nb, CHUNK, HEADS), row),
            pl.BlockSpec((nb, HEADS, CHUNK), lambda b, ci: (0, 0, b * nchunk + ci)),
        ] + _const_specs(consts),
        out_specs=[
            pl.BlockSpec((nb, CHUNK, D_INNER), row),
            pl.BlockSpec((nb, 1, D_INNER, D_STATE), lambda b, ci: (0, b, 0, 0)),
        ],
        out_shape=[
            jax.ShapeDtypeStruct((nb, part, D_INNER), bf16),
            jax.ShapeDtypeStruct((nb, batch // nb, D_INNER, D_STATE), f32),
        ],
        scratch_shapes=[pltpu.VMEM((nb, SUBLANES, CONV_DIM), f32),
                        pltpu.VMEM((nb * GROUPS, D_STATE, GROUP_WIDTH), f32)],
        compiler_params=_params("arbitrary", "arbitrary"),
        name="ssd_prompt",
    )(split(z), split(xbc), split(dt), dtT3, *[consts[k] for k in _SSD_CONST_NAMES])
    return y.reshape(batch * seq, D_INNER), state.reshape(batch, D_INNER, D_STATE)


def _ssd_sample_body(seq_len, z_ref, xbc_ref, pre_ref, dt_ref, dtT_ref, state_ref, *rest):
    nc = len(_SSD_CONST_NAMES)
    c = dict(zip(_SSD_CONST_NAMES, rest[:nc]))
    y_ref, state_out_ref, c_ref, b_ref, xdT_ref, yacc_ref, eax_ref, eatT_ref = rest[nc:]
    step = pl.program_id(1)
    per_step = state_ref.shape[0]

    @pl.when(step == 0)
    def _():
        xa = _conv_silu_many_seq(xbc_ref[...], pre_ref[...], c["cw"], c["cb"], seq_len)
        b = _ssd_block(xa, dt_ref[...], dtT_ref[...], c)
        c_ref[...] = b["cm"]
        b_ref[...] = b["bm"]
        xdT_ref[...] = b["xd"].T.astype(bf16)
        yacc_ref[...] = b["y_diag"] + c["dskipx"][...] * b["xs"]
        eax_ref[...] = b["eax"]
        eatT_ref[...] = jnp.exp(_dot_sel_rhs(b["aT"], c["same"][...]))

    def one_sequence(q, carry):
        s = step * per_step + q
        r0 = pl.multiple_of(s * seq_len, seq_len)
        rows = pl.ds(r0, seq_len)
        lane = lax.broadcasted_iota(i32, (1, CHUNK), 1)
        arep = jnp.broadcast_to(jnp.sum(jnp.where(lane == r0, eatT_ref[...], 0.0), axis=1, keepdims=True), (HEADS, D_STATE))
        rmask = (lax.broadcasted_iota(i32, (CHUNK, 1), 0) // seq_len) == s
        for g in range(GROUPS):
            gs = slice(g * GROUP_WIDTH, (g + 1) * GROUP_WIDTH)
            ns = slice(g * D_STATE, (g + 1) * D_STATE)
            s0 = state_ref[q, gs, :]
            cg = c_ref[rows, ns].astype(bf16)
            yo = lax.dot_general(cg, s0.astype(bf16), _NT, preferred_element_type=f32)
            yacc_ref[rows, gs] = yacc_ref[rows, gs] + yo * eax_ref[rows, gs]
            bmask = jnp.where(rmask, b_ref[:, ns], 0.0).astype(bf16)
            upd = jnp.dot(xdT_ref[gs, :], bmask, preferred_element_type=f32)
            for r in range(HEADS_PER_GROUP):
                h = g * HEADS_PER_GROUP + r
                hs = slice(r * HEAD_DIM, (r + 1) * HEAD_DIM)
                state_out_ref[q, h * HEAD_DIM:(h + 1) * HEAD_DIM, :] = s0[hs] * arep[h:h + 1, :] + upd[hs]
        return carry

    lax.fori_loop(0, per_step, one_sequence, 0)

    @pl.when(step == pl.num_programs(1) - 1)
    def _():
        y_ref[...] = _gated_norm(yacc_ref[...], z_ref[...], c["gw"][...])


def _ssd_sample(z, xbc, pre, dt, dtT, state, consts, nseq, seq_len):
    per = CHUNK // seq_len
    nblk = nseq // per
    t = nseq * seq_len
    q = SAMPLE_SEQS_PER_STEP if per % SAMPLE_SEQS_PER_STEP == 0 else 1
    blk = lambda i, s: (i, 0)
    seq3 = lambda i, s: (i * (per // q) + s, 0, 0)
    return pl.pallas_call(
        functools.partial(_ssd_sample_body, seq_len),
        grid=(nblk, per // q),
        in_specs=[
            pl.BlockSpec((CHUNK, D_INNER), blk),
            pl.BlockSpec((CHUNK, CONV_DIM), blk),
            pl.BlockSpec((CHUNK, CONV_DIM), blk),
            pl.BlockSpec((CHUNK, HEADS), blk),
            pl.BlockSpec((HEADS, CHUNK), lambda i, s: (0, i)),
            pl.BlockSpec((q, D_INNER, D_STATE), seq3),
        ] + _const_specs(consts),
        out_specs=[
            pl.BlockSpec((CHUNK, D_INNER), blk),
            pl.BlockSpec((q, D_INNER, D_STATE), seq3),
        ],
        out_shape=[
            jax.ShapeDtypeStruct((t, D_INNER), bf16),
            jax.ShapeDtypeStruct((nseq, D_INNER, D_STATE), f32),
        ],
        scratch_shapes=[
            pltpu.VMEM((CHUNK, GROUPS * D_STATE), f32),
            pltpu.VMEM((CHUNK, GROUPS * D_STATE), f32),
            pltpu.VMEM((D_INNER, CHUNK), bf16),
            pltpu.VMEM((CHUNK, D_INNER), f32),
            pltpu.VMEM((CHUNK, D_INNER), f32),
            pltpu.VMEM((HEADS, CHUNK), f32),
        ],
        compiler_params=_params("arbitrary", "arbitrary"),
        name="ssd_sample",
    )(z, xbc, pre, dt, dtT, state, *[consts[k] for k in _SSD_CONST_NAMES])


_MIX_CONST_NAMES = ("ws", "bs", "vnw", "vnb", "wbs", "wbm", "wo", "nffn", "wrT", "br", "tris")


def _mix_body(emit_v, uv_ref, gates_ref, yssd_ref, x_ref, *rest):
    nc = len(_MIX_CONST_NAMES)
    c = dict(zip(_MIX_CONST_NAMES, rest[:nc]))
    outs = list(rest[nc:])
    h_ref, hfp_ref = outs[:2]
    v_ref = outs[2] if emit_v else None
    te_ref, gate_ref, rank_ref, cnt_ref, cnt_acc = outs[-5:]
    tm = uv_ref.shape[0]

    @pl.when(pl.program_id(0) == 0)
    def _():
        cnt_acc[...] = jnp.zeros_like(cnt_acc)

    uv = uv_ref[...].astype(f32)
    uv = 0.5 * uv * (1.0 + lax.erf(uv * np.float32(np.sqrt(0.5))))
    u, v = uv[:, :D_GMLP], uv[:, D_GMLP:]
    mu = jnp.mean(v, axis=-1, keepdims=True)
    vc = v - mu
    vn = vc * lax.rsqrt(jnp.mean(vc * vc, axis=-1, keepdims=True) + EPS) * c["vnw"][...] + c["vnb"][...]
    if emit_v:
        v_ref[...] = vn
    vn_bf = vn.astype(bf16)
    gd = D_GMLP // GMLP_HEADS
    rows = []
    for ck in range(tm // CHUNK):
        rs = slice(ck * CHUNK, (ck + 1) * CHUNK)
        heads = []
        for g in range(GMLP_HEADS):
            mixed = jnp.dot(c["ws"][g], vn_bf[rs, g * gd:(g + 1) * gd], preferred_element_type=f32)
            heads.append(mixed + c["bs"][:, g:g + 1])
        rows.append(jnp.concatenate(heads, axis=1))
    y_mlp = u * jnp.concatenate(rows, axis=0)

    a = jnp.dot(yssd_ref[...], c["wbs"][...], preferred_element_type=f32)
    b = jnp.dot(y_mlp.astype(bf16), c["wbm"][...], preferred_element_type=f32)
    gs = jax.nn.sigmoid(gates_ref[...].astype(f32))
    merged = gs[:, :D_MODEL] * a + gs[:, D_MODEL:] * b
    h = x_ref[...] + jnp.dot(merged.astype(bf16), c["wo"][...], preferred_element_type=f32)
    h_ref[...] = h
    hf = h * lax.rsqrt(jnp.mean(h * h, axis=-1, keepdims=True) + EPS) * c["nffn"][...]
    hfp_ref[...] = _pack_bf16_pair(hf)

    lg = lax.dot_general(c["wrT"][...], hf, _NT, precision=lax.Precision.HIGHEST, preferred_element_type=f32) + c["br"][...]
    sub = lax.broadcasted_iota(i32, lg.shape, 0)
    idxs, vals = [], []
    for _ in range(TOP_K):
        m = jnp.max(lg, axis=0, keepdims=True)
        idx = jnp.min(jnp.where(lg == m, sub, N_EXPERTS), axis=0, keepdims=True)
        idxs.append(idx)
        vals.append(m)
        lg = jnp.where(sub == idx, -jnp.inf, lg)
    p = jnp.exp(jnp.concatenate(vals, axis=0) - vals[0])
    gate_ref[...] = p / jnp.sum(p, axis=0, keepdims=True)
    te_ref[...] = jnp.concatenate(idxs, axis=0)

    onehots = [sub == idx for idx in idxs]
    member = functools.reduce(jnp.logical_or, onehots).astype(f32)
    before = jnp.dot(member.astype(bf16), c["tris"][...], preferred_element_type=f32) + cnt_acc[:, 0:1]
    rank_ref[...] = jnp.concatenate(
        [jnp.sum(jnp.where(oh, before, 0.0), axis=0, keepdims=True) for oh in onehots], axis=0).astype(i32)
    cnt_acc[...] = cnt_acc[...] + jnp.sum(member, axis=1, keepdims=True)
    cnt_ref[...] = cnt_acc[...].astype(i32)


def _mix(uv, gates, yssd, x2d, mc, tm, emit_v):
    t = x2d.shape[0]
    row = lambda i: (i, 0)
    col = lambda i: (0, i)
    const = lambda a: pl.BlockSpec(a.shape, lambda i: (0,) * a.ndim, pipeline_mode=pl.Buffered(1))
    v_spec = [pl.BlockSpec((tm, D_GMLP), row)] if emit_v else []
    v_shape = [jax.ShapeDtypeStruct((t, D_GMLP), f32)] if emit_v else []
    return pl.pallas_call(
        functools.partial(_mix_body, emit_v),
        grid=(t // tm,),
        in_specs=[
            pl.BlockSpec((tm, D_INNER), row),
            pl.BlockSpec((tm, D_INNER), row),
            pl.BlockSpec((tm, D_INNER), row),
            pl.BlockSpec((tm, D_MODEL), row),
        ] + [const(mc[k]) for k in _MIX_CONST_NAMES],
        out_specs=[pl.BlockSpec((tm, D_MODEL), row), pl.BlockSpec((tm, D_PACK), row)] + v_spec + [
            pl.BlockSpec((TOP_K, tm), col),
            pl.BlockSpec((TOP_K, tm), col),
            pl.BlockSpec((TOP_K, tm), col),
            pl.BlockSpec((N_EXPERTS, 128), lambda i: (0, 0)),
        ],
        out_shape=[
            jax.ShapeDtypeStruct((t, D_MODEL), f32),
            jax.ShapeDtypeStruct((t, D_PACK), u32),
        ] + v_shape + [
            jax.ShapeDtypeStruct((TOP_K, t), i32),
            jax.ShapeDtypeStruct((TOP_K, t), f32),
            jax.ShapeDtypeStruct((TOP_K, t), i32),
            jax.ShapeDtypeStruct((N_EXPERTS, 128), i32),
        ],
        scratch_shapes=[pltpu.VMEM((N_EXPERTS, 128), f32)],
        compiler_params=_params("arbitrary"),
        name="mix_route",
    )(uv, gates, yssd, x2d, *[mc[k] for k in _MIX_CONST_NAMES])


def _dispatch(slot_kt, hf_all, n_slots):
    info = plsc.get_sparse_core_info()
    workers = info.num_cores * info.num_subcores
    t = hf_all.shape[0]
    per_w = t // workers
    nch = per_w // SC_CHUNK
    assert per_w * workers == t and nch * SC_CHUNK == per_w
    slots = slot_kt.reshape(TOP_K, workers, nch, SC_CHUNK).transpose(1, 0, 2, 3)
    mesh = plsc.VectorSubcoreMesh(core_axis_name="c", subcore_axis_name="s")

    @functools.partial(
        pl.kernel, mesh=mesh,
        out_type=jax.ShapeDtypeStruct((n_slots, D_PACK), u32),
        scratch_types=[pltpu.VMEM((TOP_K, nch, SC_CHUNK), i32), pltpu.VMEM((2, SC_CHUNK, D_PACK), u32)]
        + [pltpu.SemaphoreType.DMA] * 4,
    )
    def scatter_rows(hf_hbm, slot_hbm, out_hbm, idx_v, rows_v, sin0, sin1, sout0, sout1):
        sin, sout = (sin0, sin1), (sout0, sout1)
        wid = lax.axis_index("s") * info.num_cores + lax.axis_index("c")
        base = wid * per_w
        pltpu.sync_copy(slot_hbm.at[wid], idx_v)

        def load(j):
            rows = pl.ds(pl.multiple_of(base + j * SC_CHUNK, SUBLANES), SC_CHUNK)
            return pltpu.make_async_copy(hf_hbm.at[rows], rows_v.at[j % 2], sin[j % 2])

        def scatters(j):
            return [pltpu.make_async_copy(rows_v.at[j % 2], out_hbm.at[idx_v.at[k, j]], sout[j % 2]) for k in range(TOP_K)]

        load(0).start()
        for j in range(nch):
            load(j).wait()
            if j >= 1:
                for cp in scatters(j - 1):
                    cp.wait()
            if j + 1 < nch:
                load(j + 1).start()
            for cp in scatters(j):
                cp.start()
        for cp in scatters(nch - 1):
            cp.wait()

    return scatter_rows(hf_all, slots)


def _experts_body(be_ref, nused_ref, start_ref, slot_ref, next_ref, nvalid_ref, xs_ref, bgu_ref, bd_ref, perm_ref, wgu_hbm, wd_hbm,
                  ys_ref, wgu_f32, wd_f32, wgu_bf, wd_bf, sems):
    i = pl.program_id(0)

    def weight_copies(e, slot):
        return (pltpu.make_async_copy(wgu_hbm.at[e], wgu_f32.at[slot], sems.at[0, slot]),
                pltpu.make_async_copy(wd_hbm.at[e], wd_f32.at[slot], sems.at[1, slot]))

    @pl.when(i < nused_ref[0])
    def _():
        @pl.when(start_ref[i] == 1)
        def _():
            slot = slot_ref[i]

            @pl.when(i == 0)
            def _():
                for cp in weight_copies(be_ref[0], 0):
                    cp.start()

            for cp in weight_copies(be_ref[i], slot):
                cp.wait()

            @pl.when(next_ref[i] >= 0)
            def _():
                for cp in weight_copies(next_ref[i], 1 - slot):
                    cp.start()

            wd_bf[...] = wd_f32[slot].astype(bf16)
            half = DEINT // 2
            for j in range(2 * D_FF // DEINT):
                w = jnp.dot(wgu_f32[slot, :, j * DEINT:(j + 1) * DEINT].astype(bf16), perm_ref[...],
                            preferred_element_type=f32).astype(bf16)
                wgu_bf[:, j * half:(j + 1) * half] = w[:, :half]
                wgu_bf[:, D_FF + j * half:D_FF + (j + 1) * half] = w[:, half:]

        row = lax.broadcasted_iota(i32, (MOE_BLOCK, 1), 0)
        xs = jnp.where(row < nvalid_ref[i], xs_ref[...], jnp.zeros((), u32))
        gu = jnp.dot(_unpack_bf16_pair(xs), wgu_bf[...], preferred_element_type=f32) + bgu_ref[0]
        glu = jnp.minimum(gu[:, :D_FF], SWIGLU_LIMIT)
        lin = jnp.clip(gu[:, D_FF:], -SWIGLU_LIMIT, SWIGLU_LIMIT)
        act = glu * jax.nn.sigmoid(SWIGLU_ALPHA * glu) * (lin + 1.0)
        ys_ref[...] = jnp.dot(act.astype(bf16), wd_bf[...], preferred_element_type=f32) + bd_ref[0]

    @pl.when(i >= nused_ref[0])
    def _():
        ys_ref[...] = jnp.zeros_like(ys_ref)


def _expert_segments(block_e, n_used, n_blocks):
    idx = jnp.arange(n_blocks, dtype=i32)
    prev_e = jnp.concatenate([block_e[:1], block_e[:-1]])
    start = (idx < n_used) & ((idx == 0) | (block_e != prev_e))
    slot = (jnp.cumsum(start.astype(i32)) - 1) % 2
    start_idx = jnp.where(start, idx, n_blocks)
    from_here = lax.cummin(start_idx, reverse=True)
    next_start = jnp.concatenate([from_here[1:], jnp.full((1,), n_blocks, i32)])
    next_e = jnp.sum(jnp.where(next_start[:, None] == idx[None, :], block_e[None, :], 0), axis=1)
    next_e = jnp.where(next_start < n_blocks, next_e, -1)
    return start.astype(i32), slot.astype(i32), next_e.astype(i32)


def _experts(block_e, n_used, n_valid, xs, wgu, bgu, w_down, b_down):
    n_blocks = xs.shape[0] // MOE_BLOCK
    start, slot, next_e = _expert_segments(block_e, n_used[0], n_blocks)
    blk = lambda i, be, nu, *_: (jnp.maximum(jnp.minimum(i, nu[0] - 1), 0), 0)
    exp3 = lambda i, be, *_: (be[i], 0, 0)
    c = np.arange(DEINT)
    src = np.where(c < DEINT // 2, 2 * c, 2 * (c - DEINT // 2) + 1)
    perm = jnp.asarray(np.arange(DEINT)[:, None] == src[None, :], bf16)
    return pl.pallas_call(
        _experts_body,
        grid_spec=pltpu.PrefetchScalarGridSpec(
            num_scalar_prefetch=6,
            grid=(n_blocks,),
            in_specs=[
                pl.BlockSpec((MOE_BLOCK, D_PACK), blk),
                pl.BlockSpec((1, 1, 2 * D_FF), exp3),
                pl.BlockSpec((1, 1, D_MODEL), exp3),
                pl.BlockSpec((DEINT, DEINT), lambda i, *_: (0, 0)),
                pl.BlockSpec(memory_space=pl.ANY),
                pl.BlockSpec(memory_space=pl.ANY),
            ],
            out_specs=pl.BlockSpec((MOE_BLOCK, D_MODEL), lambda i, *_: (i, 0)),
            scratch_shapes=[
                pltpu.VMEM((2, D_MODEL, 2 * D_FF), f32),
                pltpu.VMEM((2, D_FF, D_MODEL), f32),
                pltpu.VMEM((D_MODEL, 2 * D_FF), bf16),
                pltpu.VMEM((D_FF, D_MODEL), bf16),
                pltpu.SemaphoreType.DMA((2, 2)),
            ],
        ),
        out_shape=jax.ShapeDtypeStruct((xs.shape[0], D_MODEL), f32),
        compiler_params=_params("arbitrary"),
        name="moe_experts",
    )(block_e, n_used, start, slot, next_e, n_valid, xs, bgu, b_down, perm, wgu, w_down)


def _combine_body(tm, t, slot_ref, h_ref, gate_ref, nf_ref, ys_ref, o_ref, rows_ref, sems):
    s = pl.program_id(0)
    n_tiles = pl.num_programs(0) - 1

    def request():
        buf = s % 2
        for r in range(tm):
            for k in range(TOP_K):
                slot = slot_ref[k * t + s * tm + r]
                pltpu.make_async_copy(ys_ref.at[slot], rows_ref.at[buf, k, r], sems.at[buf]).start(priority=k % 2)

    def finish():
        buf = (s - 1) % 2
        for r in range(tm):
            for k in range(TOP_K):
                pltpu.make_async_copy(ys_ref.at[0], rows_ref.at[buf, k, r], sems.at[buf]).wait()
        g = gate_ref[...]
        moe = g[:, 0:1] * rows_ref[buf, 0]
        for k in range(1, TOP_K):
            moe = moe + g[:, k:k + 1] * rows_ref[buf, k]
        h = h_ref[...] + moe
        o_ref[...] = h * lax.rsqrt(jnp.mean(h * h, axis=-1, keepdims=True) + EPS) * nf_ref[...]

    @pl.when(s < n_tiles)
    def _():
        request()

    @pl.when(s > 0)
    def _():
        finish()


def _combine(slot_flat, h, gate_t, norm_final, ys, tm):
    t = h.shape[0]
    return pl.pallas_call(
        functools.partial(_combine_body, tm, t),
        grid_spec=pltpu.PrefetchScalarGridSpec(
            num_scalar_prefetch=1,
            grid=(t // tm + 1,),
            in_specs=[
                pl.BlockSpec((tm, D_MODEL), lambda i, s: (jnp.maximum(i - 1, 0), 0)),
                pl.BlockSpec((tm, TOP_K), lambda i, s: (jnp.maximum(i - 1, 0), 0)),
                pl.BlockSpec((1, D_MODEL), lambda i, s: (0, 0)),
                pl.BlockSpec(memory_space=pl.ANY),
            ],
            out_specs=pl.BlockSpec((tm, D_MODEL), lambda i, s: (jnp.maximum(i - 1, 0), 0)),
            scratch_shapes=[pltpu.VMEM((2, TOP_K, tm, D_MODEL), f32), pltpu.SemaphoreType.DMA((2,))],
        ),
        out_shape=jax.ShapeDtypeStruct((t, D_MODEL), f32),
        compiler_params=_params("arbitrary"),
        name="moe_combine",
    )(slot_flat, h, gate_t, norm_final, ys)


def kernel(x_prompt, x_sample, state_conv, state_ssm, norm_mix, w_in, conv_w, conv_b, dt_bias, a_log, d_skip, gnorm_w, v_norm_w, v_norm_b, w_spatial, b_spatial, w_branch_ssd, w_branch_mlp, w_out, norm_ffn, w_router, b_router, w_gu, b_gu, w_down, b_down, norm_final):
    assert w_in.shape[0] == 1, "single-layer trunk"
    batch, seq, _ = x_prompt.shape
    nseq, dec_seq, _ = x_sample.shape
    assert seq % CHUNK == 0 and CHUNK % dec_seq == 0 and nseq % (CHUNK // dec_seq) == 0 and dec_seq >= CONV_W - 1
    t_p, t_s = batch * seq, nseq * dec_seq
    tm = ROW_DMA_TILE
    assert t_p % tm == 0 and t_s % tm == 0
    tm_mix = MIX_TILE if t_p % MIX_TILE == 0 and t_s % MIX_TILE == 0 else tm

    wi = w_in[0]
    z0, x0, d0, u0 = D_INNER, D_INNER + CONV_DIM, D_INNER + CONV_DIM + HEADS, D_INNER + CONV_DIM + HEADS + 2 * D_GMLP
    w_segments = tuple(wi[:, a:b].astype(bf16) for a, b in ((0, z0), (d0, u0), (u0, D_IN_PROJ), (z0, x0)))
    w_dt = wi[:, x0:d0].astype(bf16)
    w_dtT = w_dt.T
    nm = norm_mix[0].reshape(1, D_MODEL)
    sc = lambda L: _ssd_consts(conv_w[0], conv_b[0], dt_bias[0], a_log[0], d_skip[0], gnorm_w[0], L)
    xp2, xs2 = x_prompt.reshape(t_p, D_MODEL), x_sample.reshape(t_s, D_MODEL)

    tril = np.tril(np.ones((CHUNK, CHUNK), bool))
    ws_p = jnp.where(tril[None], w_spatial[0], 0).astype(bf16)
    per = CHUNK // dec_seq
    blockdiag = (np.arange(CHUNK)[:, None] // dec_seq) == (np.arange(CHUNK)[None, :] // dec_seq)
    ws_s = jnp.where((tril & blockdiag)[None], jnp.tile(w_spatial[0][:, :dec_seq, :dec_seq], (1, per, per)), 0).astype(bf16)
    bs_p = b_spatial[0].T
    bs_s = jnp.tile(b_spatial[0][:, :dec_seq], (1, per)).T
    mc = dict(
        vnw=v_norm_w[0].reshape(1, D_GMLP), vnb=v_norm_b[0].reshape(1, D_GMLP),
        wbs=w_branch_ssd[0].astype(bf16), wbm=w_branch_mlp[0].astype(bf16), wo=w_out[0].astype(bf16),
        nffn=norm_ffn[0].reshape(1, D_MODEL), wrT=w_router[0].T, br=b_router[0].reshape(N_EXPERTS, 1),
        tris=jnp.asarray(np.triu(np.ones((tm_mix, tm_mix), np.float32), 1), bf16),
    )

    z_p, uv_p, g_p, xbc_p, dt_p, dtT_p = _in_proj(xp2, nm, w_segments, w_dt, w_dtT)
    yssd_p, ssm_p = _ssd_prompt(z_p, xbc_p, dt_p, dtT_p, sc(CHUNK), batch, seq)
    h_p, hf_p, te_p, gate_p, rank_p, cnt_p = _mix(uv_p, g_p, yssd_p, xp2, dict(mc, ws=ws_p, bs=bs_p), tm_mix, False)

    z_s, uv_s, g_s, xbc_s, dt_s, dtT_s = _in_proj(xs2, nm, w_segments, w_dt, w_dtT)
    pre = jnp.pad(state_conv[0], ((0, 0), (dec_seq - (CONV_W - 1), 0), (0, 0))).reshape(t_s, CONV_DIM)
    yssd_s, ssm_s = _ssd_sample(z_s, xbc_s, pre, dt_s, dtT_s, state_ssm[0].reshape(nseq, D_INNER, D_STATE),
                                sc(dec_seq), nseq, dec_seq)
    h_s, hf_s, v_s, te_s, gate_s, rank_s, cnt_s = _mix(uv_s, g_s, yssd_s, xs2, dict(mc, ws=ws_s, bs=bs_s), tm_mix, True)

    cp, cs = cnt_p[:, 0], cnt_s[:, 0]
    padded = (cp + cs + MOE_BLOCK - 1) // MOE_BLOCK * MOE_BLOCK
    pend = jnp.cumsum(padded)
    pstart = pend - padded

    def lookup(table, te):
        eid = jnp.arange(N_EXPERTS, dtype=i32)[:, None, None]
        return jnp.sum(jnp.where(te[None] == eid, table[:, None, None], 0), axis=0)

    slot_kt_p = (lookup(pstart, te_p) + rank_p).astype(i32)
    slot_kt_s = (lookup(pstart + cp, te_s) + rank_s).astype(i32)
    slot_p, slot_s = slot_kt_p.reshape(-1), slot_kt_s.reshape(-1)
    n_blocks = (t_p + t_s) * TOP_K // MOE_BLOCK + N_EXPERTS
    n_used = (pend[-1] // MOE_BLOCK).astype(i32).reshape(1)
    first_row = jnp.arange(n_blocks, dtype=i32) * MOE_BLOCK
    block_e = jnp.minimum(jnp.sum(pend[None, :] <= first_row[:, None], axis=1), N_EXPERTS - 1).astype(i32)
    last_row = jnp.sum(jnp.where(block_e[:, None] == jnp.arange(N_EXPERTS)[None, :], (pstart + cp + cs)[None, :], 0), axis=1)
    n_valid = jnp.clip(last_row - first_row, 0, MOE_BLOCK).astype(i32)

    xs_sorted = _dispatch(jnp.concatenate([slot_kt_p, slot_kt_s], axis=1), jnp.concatenate([hf_p, hf_s]),
                          n_blocks * MOE_BLOCK)
    bgu = jnp.concatenate([b_gu[0][..., 0::2], b_gu[0][..., 1::2]], axis=-1).reshape(N_EXPERTS, 1, 2 * D_FF)
    ys = _experts(block_e, n_used, n_valid, xs_sorted, w_gu[0], bgu, w_down[0], b_down[0].reshape(N_EXPERTS, 1, D_MODEL))

    nf = norm_final.reshape(1, D_MODEL)
    y_p = _combine(slot_p, h_p, gate_p.T, nf, ys, tm)
    y_s = _combine(slot_s, h_s, gate_s.T, nf, ys, tm)

    conv_p = xbc_p.reshape(batch, seq, CONV_DIM)[:, seq - (CONV_W - 1):]
    conv_s = xbc_s.reshape(nseq, dec_seq, CONV_DIM)[:, dec_seq - (CONV_W - 1):]
    st_shape = (HEADS, HEAD_DIM, D_STATE)
    return (
        y_p.reshape(batch, seq, D_MODEL),
        y_s.reshape(nseq, dec_seq, D_MODEL),
        conv_p[None],
        ssm_p.reshape(1, batch, *st_shape),
        conv_s[None],
        ssm_s.reshape(1, nseq, *st_shape),
        v_s.reshape(1, nseq, dec_seq, D_GMLP),
    )
```

```python
import functools

import numpy as np
import jax
import jax.numpy as jnp
from jax import lax
from jax.experimental import pallas as pl
from jax.experimental.pallas import tpu as pltpu
from jax.experimental.pallas import tpu_sc as plsc

f32, bf16, i32, u32 = jnp.float32, jnp.bfloat16, jnp.int32, jnp.uint32

D_MODEL = 1024
D_INNER = 2048
HEAD_DIM = 64
HEADS = 32
GROUPS = 4
HEADS_PER_GROUP = 8
GROUP_WIDTH = HEADS_PER_GROUP * HEAD_DIM
D_STATE = 128
CONV_W = 4
CONV_DIM = D_INNER + 2 * GROUPS * D_STATE
CHUNK = 128
SUBLANES = 8
D_GMLP = 1024
GMLP_HEADS = 8
N_EXPERTS = 32
TOP_K = 4
D_FF = 1024
SWIGLU_ALPHA = 1.702
SWIGLU_LIMIT = 7.0
EPS = 1e-5
LOG2_E = float(np.log2(np.e))
MOE_BLOCK = 512
DEINT = 256
ROW_DMA_TILE = 256
SC_CHUNK = 32
MIX_TILE = 512
D_IN_PROJ = 3 * D_INNER + CONV_DIM + HEADS
PROJ_TN = 512
PROMPT_SEQS_PER_STEP = 2
SAMPLE_SEQS_PER_STEP = 8
D_PACK = D_MODEL // 2
VMEM_LIMIT = 56 * 1024 * 1024

_NT = (((1,), (1,)), ((), ()))
_HI16 = np.uint32(0xFFFF0000)


def _params(*sem):
    return pltpu.CompilerParams(dimension_semantics=sem, vmem_limit_bytes=VMEM_LIMIT)


def _split(x, parts):
    out = []
    for _ in range(parts - 1):
        p = x.astype(bf16)
        out.append(p)
        x = x - p.astype(f32)
    out.append(x.astype(bf16))
    return out


def _dot_sel_rhs(x, sel, parts=3):
    return sum(jnp.dot(p, sel, preferred_element_type=f32) for p in _split(x, parts))


def _dot_sel_lhs(sel, x, parts=3):
    return sum(jnp.dot(sel, p, preferred_element_type=f32) for p in _split(x, parts))


def _silu(x):
    return x * jax.nn.sigmoid(x)


def _pack_bf16_pair(x):
    n = x.shape[1] // 2
    lo = lax.bitcast_convert_type(x[:, :n].astype(bf16).astype(f32), u32) >> 16
    hi = lax.bitcast_convert_type(x[:, n:].astype(bf16).astype(f32), u32) & _HI16
    return lo | hi


def _unpack_bf16_pair(p):
    lo = lax.bitcast_convert_type(p << 16, f32).astype(bf16)
    hi = lax.bitcast_convert_type(p & _HI16, f32).astype(bf16)
    return jnp.concatenate([lo, hi], axis=1)


def _in_proj_body(x_ref, nw_ref, wz_ref, wuv_ref, wg_ref, wxbc_ref, wdt_ref, wdtT_ref,
                  z_ref, uv_ref, g_ref, xbc_ref, dt_ref, dtT_ref):
    x = x_ref[...]
    ms = jnp.mean(x * x, axis=-1, keepdims=True)
    hn = (x * lax.rsqrt(ms + EPS) * nw_ref[...]).astype(bf16)
    dt_ref[...] = jnp.dot(hn, wdt_ref[...], preferred_element_type=f32)
    dtT_ref[...] = lax.dot_general(wdtT_ref[...], hn, _NT, preferred_element_type=f32)
    for o_ref, w_ref in ((z_ref, wz_ref), (uv_ref, wuv_ref), (g_ref, wg_ref), (xbc_ref, wxbc_ref)):
        for c in range(0, w_ref.shape[1], PROJ_TN):
            acc = jnp.dot(hn, w_ref[:, c:c + PROJ_TN], preferred_element_type=f32)
            o_ref[:, c:c + PROJ_TN] = acc.astype(o_ref.dtype)


def _in_proj(x2d, norm_w, w_segments, w_dt, w_dtT):
    t = x2d.shape[0]
    tm = min(512, t)
    row = lambda i: (i, 0)
    fixed = lambda i: (0, 0)
    return pl.pallas_call(
        _in_proj_body,
        grid=(t // tm,),
        in_specs=[
            pl.BlockSpec((tm, D_MODEL), row),
            pl.BlockSpec((1, D_MODEL), fixed),
        ] + [pl.BlockSpec(w.shape, fixed, pipeline_mode=pl.Buffered(1)) for w in w_segments] + [
            pl.BlockSpec((D_MODEL, HEADS), fixed),
            pl.BlockSpec((HEADS, D_MODEL), fixed),
        ],
        out_specs=[
            pl.BlockSpec((tm, D_INNER), row),
            pl.BlockSpec((tm, D_INNER), row),
            pl.BlockSpec((tm, D_INNER), row),
            pl.BlockSpec((tm, CONV_DIM), row),
            pl.BlockSpec((tm, HEADS), row),
            pl.BlockSpec((HEADS, tm), lambda i: (0, i)),
        ],
        out_shape=[
            jax.ShapeDtypeStruct((t, D_INNER), bf16),
            jax.ShapeDtypeStruct((t, D_INNER), bf16),
            jax.ShapeDtypeStruct((t, D_INNER), bf16),
            jax.ShapeDtypeStruct((t, CONV_DIM), f32),
            jax.ShapeDtypeStruct((t, HEADS), f32),
            jax.ShapeDtypeStruct((HEADS, t), f32),
        ],
        compiler_params=_params("arbitrary"),
        name="in_proj",
    )(x2d, norm_w, *w_segments, w_dt, w_dtT)


def _conv_taps(shifted, cw_ref, cb_ref):
    acc = cb_ref[...]
    for k in range(CONV_W):
        acc = acc + shifted(CONV_W - 1 - k) * cw_ref[k:k + 1, :]
    return _silu(acc)


def _conv_silu_one_seq(cur, prev8, cw_ref, cb_ref):
    assert CONV_W == 4
    row8 = lax.broadcasted_iota(i32, (SUBLANES, 1), 0)

    def shift_rows(x, x_prev8, j):
        xr = pltpu.roll(x, j, 0)
        top = jnp.where(row8 >= j, xr[:SUBLANES], pltpu.roll(x_prev8, j, 0))
        return jnp.concatenate([top, xr[SUBLANES:]], axis=0)

    w = [cw_ref[k:k + 1, :] for k in range(CONV_W)]
    x1 = shift_rows(cur, prev8, 1)
    near = x1 * w[2] + cur * w[3]
    far = x1 * w[0] + cur * w[1]
    far_prev8 = pltpu.roll(prev8, 1, 0) * w[0] + prev8 * w[1]
    return _silu(cb_ref[...] + shift_rows(far, far_prev8, 2) + near)


def _conv_silu_many_seq(cur, pre, cw_ref, cb_ref, seq_len):
    n = cur.shape[0]
    pos = lax.broadcasted_iota(i32, (n, 1), 0) % seq_len

    def shifted(j):
        if j == 0:
            return cur
        return jnp.where(pos >= j, pltpu.roll(cur, j, 0), pltpu.roll(pre, (j + n - seq_len) % n, 0))

    return _conv_taps(shifted, cw_ref, cb_ref)


def _ssd_block(xa, dt_raw, dtT_raw, c):
    tril = c["tril"][...]
    dt = jax.nn.softplus(dt_raw + c["dtb_row"][...])
    dtT = jax.nn.softplus(dtT_raw + c["dtb_col"][...])
    a = dt * c["aneg_row"][...]
    aT = dtT * c["aneg_col"][...]
    a_cum = _dot_sel_lhs(tril, a)
    a_cumT = _dot_sel_rhs(aT, c["triu"][...])
    a_tot = _dot_sel_lhs(c["same"][...], a)
    stack = jnp.concatenate([dt, jnp.exp(a_cum), jnp.exp(a_tot - a_cum)], axis=0)
    ex = jnp.dot(jnp.concatenate(_split(stack, 2), axis=1), c["expand"][...], preferred_element_type=f32)
    n = xa.shape[0]
    dtx, eax, dex = ex[:n], ex[n:2 * n], ex[2 * n:]

    xs = xa[:, :D_INNER]
    bm = xa[:, D_INNER:D_INNER + GROUPS * D_STATE]
    cm = xa[:, D_INNER + GROUPS * D_STATE:]
    xdt = xs * dtx
    xdt_bf = xdt.astype(bf16)
    mask = tril > 0
    a_cum2, a_cumT2 = a_cum * LOG2_E, a_cumT * LOG2_E
    lane = lax.broadcasted_iota(i32, (1, 2 * HEAD_DIM), 1)
    ys = []
    for g in range(GROUPS):
        cg = cm[:, g * D_STATE:(g + 1) * D_STATE].astype(bf16)
        bg = bm[:, g * D_STATE:(g + 1) * D_STATE].astype(bf16)
        cb = lax.dot_general(cg, bg, _NT, preferred_element_type=f32)
        for pair in range(HEADS_PER_GROUP // 2):
            halves = []
            for h in (g * HEADS_PER_GROUP + 2 * pair, g * HEADS_PER_GROUP + 2 * pair + 1):
                seg = a_cum2[:, h:h + 1] - a_cumT2[h:h + 1, :]
                decay = jnp.exp2(jnp.where(mask, seg, -jnp.inf))
                m = (cb * decay).astype(bf16)
                col = (h // 2) * 2 * HEAD_DIM
                halves.append(jnp.dot(m, xdt_bf[:, col:col + 2 * HEAD_DIM], preferred_element_type=f32))
            ys.append(jnp.where(lane < HEAD_DIM, halves[0], halves[1]))
    y_diag = jnp.concatenate(ys, axis=1)
    return dict(xs=xs, bm=bm, cm=cm, xdt=xdt, xd=xdt * dex, eax=eax, aT=aT, y_diag=y_diag)


def _gated_norm(y, z, gw):
    g = y * _silu(z.astype(f32))
    ms = jnp.mean(g * g, axis=-1, keepdims=True)
    return (g * lax.rsqrt(ms + EPS) * gw).astype(bf16)


_SSD_CONST_NAMES = ("cw", "cb", "dtb_row", "dtb_col", "aneg_row", "aneg_col", "dskipx", "gw",
                    "tril", "triu", "same", "expand")


def _ssd_consts(conv_w, conv_b, dt_bias, a_log, d_skip, gnorm_w, seq_len):
    r = np.arange(CHUNK)
    same = (r[:, None] // seq_len) == (r[None, :] // seq_len)
    tril = same & (r[None, :] <= r[:, None])
    expand = np.tile(np.repeat(np.eye(HEADS, dtype=np.float32), HEAD_DIM, axis=1), (2, 1))
    aneg = -jnp.exp(a_log.astype(f32))
    return dict(
        cw=conv_w, cb=conv_b.reshape(1, CONV_DIM),
        dtb_row=dt_bias.reshape(1, HEADS), dtb_col=dt_bias.reshape(HEADS, 1),
        aneg_row=aneg.reshape(1, HEADS), aneg_col=aneg.reshape(HEADS, 1),
        dskipx=jnp.repeat(d_skip.astype(f32), HEAD_DIM).reshape(1, D_INNER),
        gw=gnorm_w.reshape(1, D_INNER),
        tril=jnp.asarray(tril, bf16), triu=jnp.asarray(tril.T, bf16), same=jnp.asarray(same, bf16),
        expand=jnp.asarray(expand, bf16),
    )


def _const_specs(consts):
    zero = (lambda *_: (0, 0))
    return [pl.BlockSpec(consts[k].shape, zero) for k in _SSD_CONST_NAMES]


def _ssd_prompt_body(nb, z_ref, xbc_ref, dt_ref, dtT_ref, *rest):
    nc = len(_SSD_CONST_NAMES)
    c = dict(zip(_SSD_CONST_NAMES, rest[:nc]))
    y_ref, state_ref, prev_ref, st_ref = rest[nc:]
    ci = pl.program_id(1)

    @pl.when(ci == 0)
    def _():
        prev_ref[...] = jnp.zeros_like(prev_ref)
        st_ref[...] = jnp.zeros_like(st_ref)

    for s in range(nb):
        cur = xbc_ref[s]
        xa = _conv_silu_one_seq(cur, prev_ref[s], c["cw"], c["cb"])
        prev_ref[s] = cur[CHUNK - SUBLANES:]
        b = _ssd_block(xa, dt_ref[s], dtT_ref[s], c)

        y_off = []
        for g in range(GROUPS):
            sl = slice(g * GROUP_WIDTH, (g + 1) * GROUP_WIDTH)
            st = st_ref[s * GROUPS + g]
            cg = b["cm"][:, g * D_STATE:(g + 1) * D_STATE].astype(bf16)
            y_off.append(jnp.dot(cg, st.astype(bf16), preferred_element_type=f32))
            bgT = b["bm"][:, g * D_STATE:(g + 1) * D_STATE].T.astype(bf16)
            upd = jnp.dot(bgT, b["xd"][:, sl].astype(bf16), preferred_element_type=f32)
            st_ref[s * GROUPS + g] = st * b["eax"][CHUNK - 1:CHUNK, sl] + upd
        y = b["y_diag"] + jnp.concatenate(y_off, axis=1) * b["eax"] + c["dskipx"][...] * b["xs"]
        y_ref[s] = _gated_norm(y, z_ref[s], c["gw"][...])

    @pl.when(ci == pl.num_programs(1) - 1)
    def _():
        for s in range(nb):
            for g in range(GROUPS):
                state_ref[s, 0, g * GROUP_WIDTH:(g + 1) * GROUP_WIDTH, :] = st_ref[s * GROUPS + g].T


def _ssd_prompt(z, xbc, dt, dtT, consts, batch, seq):
    nb = PROMPT_SEQS_PER_STEP if batch % PROMPT_SEQS_PER_STEP == 0 else 1
    nchunk = seq // CHUNK
    part = batch // nb * seq
    split = lambda a: a.reshape(nb, part, a.shape[-1])
    dtT3 = dtT.reshape(HEADS, nb, part).transpose(1, 0, 2)
    row = lambda b, ci: (0, b * nchunk + ci, 0)
    y, state = pl.pallas_call(
        functools.partial(_ssd_prompt_body, nb),
        grid=(batch // nb, nchunk),
        in_specs=[
            pl.BlockSpec((nb, CHUNK, D_INNER), row),
            pl.BlockSpec((nb, CHUNK, CONV_DIM), row),
            pl.BlockSpec((nb, CHUNK, HEADS), row),
            pl.BlockSpec((nb, HEADS, CHUNK), lambda b, ci: (0, 0, b * nchunk + ci)),
        ] + _const_specs(consts),
        out_specs=[
            pl.BlockSpec((nb, CHUNK, D_INNER), row),
            pl.BlockSpec((nb, 1, D_INNER, D_STATE), lambda b, ci: (0, b, 0, 0)),
        ],
        out_shape=[
            jax.ShapeDtypeStruct((nb, part, D_INNER), bf16),
            jax.ShapeDtypeStruct((nb, batch // nb, D_INNER, D_STATE), f32),
        ],
        scratch_shapes=[pltpu.VMEM((nb, SUBLANES, CONV_DIM), f32),
                        pltpu.VMEM((nb * GROUPS, D_STATE, GROUP_WIDTH), f32)],
        compiler_params=_params("arbitrary", "arbitrary"),
        name="ssd_prompt",
    )(split(z), split(xbc), split(dt), dtT3, *[consts[k] for k in _SSD_CONST_NAMES])
    return y.reshape(batch * seq, D_INNER), state.reshape(batch, D_INNER, D_STATE)


def _ssd_sample_body(seq_len, z_ref, xbc_ref, pre_ref, dt_ref, dtT_ref, state_ref, *rest):
    nc = len(_SSD_CONST_NAMES)
    c = dict(zip(_SSD_CONST_NAMES, rest[:nc]))
    y_ref, state_out_ref, c_ref, b_ref, xdT_ref, yacc_ref, eax_ref, eatT_ref = rest[nc:]
    step = pl.program_id(1)
    per_step = state_ref.shape[0]

    @pl.when(step == 0)
    def _():
        xa = _conv_silu_many_seq(xbc_ref[...], pre_ref[...], c["cw"], c["cb"], seq_len)
        b = _ssd_block(xa, dt_ref[...], dtT_ref[...], c)
        c_ref[...] = b["cm"]
        b_ref[...] = b["bm"]
        xdT_ref[...] = b["xd"].T.astype(bf16)
        yacc_ref[...] = b["y_diag"] + c["dskipx"][...] * b["xs"]
        eax_ref[...] = b["eax"]
        eatT_ref[...] = jnp.exp(_dot_sel_rhs(b["aT"], c["same"][...]))

    def one_sequence(q, carry):
        s = step * per_step + q
        r0 = pl.multiple_of(s * seq_len, seq_len)
        rows = pl.ds(r0, seq_len)
        lane = lax.broadcasted_iota(i32, (1, CHUNK), 1)
        arep = jnp.broadcast_to(jnp.sum(jnp.where(lane == r0, eatT_ref[...], 0.0), axis=1, keepdims=True), (HEADS, D_STATE))
        rmask = (lax.broadcasted_iota(i32, (CHUNK, 1), 0) // seq_len) == s
        for g in range(GROUPS):
            gs = slice(g * GROUP_WIDTH, (g + 1) * GROUP_WIDTH)
            ns = slice(g * D_STATE, (g + 1) * D_STATE)
            s0 = state_ref[q, gs, :]
            cg = c_ref[rows, ns].astype(bf16)
            yo = lax.dot_general(cg, s0.astype(bf16), _NT, preferred_element_type=f32)
            yacc_ref[rows, gs] = yacc_ref[rows, gs] + yo * eax_ref[rows, gs]
            bmask = jnp.where(rmask, b_ref[:, ns], 0.0).astype(bf16)
            upd = jnp.dot(xdT_ref[gs, :], bmask, preferred_element_type=f32)
            for r in range(HEADS_PER_GROUP):
                h = g * HEADS_PER_GROUP + r
                hs = slice(r * HEAD_DIM, (r + 1) * HEAD_DIM)
                state_out_ref[q, h * HEAD_DIM:(h + 1) * HEAD_DIM, :] = s0[hs] * arep[h:h + 1, :] + upd[hs]
        return carry

    lax.fori_loop(0, per_step, one_sequence, 0)

    @pl.when(step == pl.num_programs(1) - 1)
    def _():
        y_ref[...] = _gated_norm(yacc_ref[...], z_ref[...], c["gw"][...])


def _ssd_sample(z, xbc, pre, dt, dtT, state, consts, nseq, seq_len):
    per = CHUNK // seq_len
    nblk = nseq // per
    t = nseq * seq_len
    q = SAMPLE_SEQS_PER_STEP if per % SAMPLE_SEQS_PER_STEP == 0 else 1
    blk = lambda i, s: (i, 0)
    seq3 = lambda i, s: (i * (per // q) + s, 0, 0)
    return pl.pallas_call(
        functools.partial(_ssd_sample_body, seq_len),
        grid=(nblk, per // q),
        in_specs=[
            pl.BlockSpec((CHUNK, D_INNER), blk),
            pl.BlockSpec((CHUNK, CONV_DIM), blk),
            pl.BlockSpec((CHUNK, CONV_DIM), blk),
            pl.BlockSpec((CHUNK, HEADS), blk),
            pl.BlockSpec((HEADS, CHUNK), lambda i, s: (0, i)),
            pl.BlockSpec((q, D_INNER, D_STATE), seq3),
        ] + _const_specs(consts),
        out_specs=[
            pl.BlockSpec((CHUNK, D_INNER), blk),
            pl.BlockSpec((q, D_INNER, D_STATE), seq3),
        ],
        out_shape=[
            jax.ShapeDtypeStruct((t, D_INNER), bf16),
            jax.ShapeDtypeStruct((nseq, D_INNER, D_STATE), f32),
        ],
        scratch_shapes=[
            pltpu.VMEM((CHUNK, GROUPS * D_STATE), f32),
            pltpu.VMEM((CHUNK, GROUPS * D_STATE), f32),
            pltpu.VMEM((D_INNER, CHUNK), bf16),
            pltpu.VMEM((CHUNK, D_INNER), f32),
            pltpu.VMEM((CHUNK, D_INNER), f32),
            pltpu.VMEM((HEADS, CHUNK), f32),
        ],
        compiler_params=_params("arbitrary", "arbitrary"),
        name="ssd_sample",
    )(z, xbc, pre, dt, dtT, state, *[consts[k] for k in _SSD_CONST_NAMES])


_MIX_CONST_NAMES = ("ws", "bs", "vnw", "vnb", "wbs", "wbm", "wo", "nffn", "wrT", "br", "tris")


def _mix_body(emit_v, uv_ref, gates_ref, yssd_ref, x_ref, *rest):
    nc = len(_MIX_CONST_NAMES)
    c = dict(zip(_MIX_CONST_NAMES, rest[:nc]))
    outs = list(rest[nc:])
    h_ref, hfp_ref = outs[:2]
    v_ref = outs[2] if emit_v else None
    te_ref, gate_ref, rank_ref, cnt_ref, cnt_acc = outs[-5:]
    tm = uv_ref.shape[0]

    @pl.when(pl.program_id(0) == 0)
    def _():
        cnt_acc[...] = jnp.zeros_like(cnt_acc)

    uv = uv_ref[...].astype(f32)
    uv = 0.5 * uv * (1.0 + lax.erf(uv * np.float32(np.sqrt(0.5))))
    u, v = uv[:, :D_GMLP], uv[:, D_GMLP:]
    mu = jnp.mean(v, axis=-1, keepdims=True)
    vc = v - mu
    vn = vc * lax.rsqrt(jnp.mean(vc * vc, axis=-1, keepdims=True) + EPS) * c["vnw"][...] + c["vnb"][...]
    if emit_v:
        v_ref[...] = vn
    vn_bf = vn.astype(bf16)
    gd = D_GMLP // GMLP_HEADS
    rows = []
    for ck in range(tm // CHUNK):
        rs = slice(ck * CHUNK, (ck + 1) * CHUNK)
        heads = []
        for g in range(GMLP_HEADS):
            mixed = jnp.dot(c["ws"][g], vn_bf[rs, g * gd:(g + 1) * gd], preferred_element_type=f32)
            heads.append(mixed + c["bs"][:, g:g + 1])
        rows.append(jnp.concatenate(heads, axis=1))
    y_mlp = u * jnp.concatenate(rows, axis=0)

    a = jnp.dot(yssd_ref[...], c["wbs"][...], preferred_element_type=f32)
    b = jnp.dot(y_mlp.astype(bf16), c["wbm"][...], preferred_element_type=f32)
    gs = jax.nn.sigmoid(gates_ref[...].astype(f32))
    merged = gs[:, :D_MODEL] * a + gs[:, D_MODEL:] * b
    h = x_ref[...] + jnp.dot(merged.astype(bf16), c["wo"][...], preferred_element_type=f32)
    h_ref[...] = h
    hf = h * lax.rsqrt(jnp.mean(h * h, axis=-1, keepdims=True) + EPS) * c["nffn"][...]
    hfp_ref[...] = _pack_bf16_pair(hf)

    lg = lax.dot_general(c["wrT"][...], hf, _NT, precision=lax.Precision.HIGHEST, preferred_element_type=f32) + c["br"][...]
    sub = lax.broadcasted_iota(i32, lg.shape, 0)
    idxs, vals = [], []
    for _ in range(TOP_K):
        m = jnp.max(lg, axis=0, keepdims=True)
        idx = jnp.min(jnp.where(lg == m, sub, N_EXPERTS), axis=0, keepdims=True)
        idxs.append(idx)
        vals.append(m)
        lg = jnp.where(sub == idx, -jnp.inf, lg)
    p = jnp.exp(jnp.concatenate(vals, axis=0) - vals[0])
    gate_ref[...] = p / jnp.sum(p, axis=0, keepdims=True)
    te_ref[...] = jnp.concatenate(idxs, axis=0)

    onehots = [sub == idx for idx in idxs]
    member = functools.reduce(jnp.logical_or, onehots).astype(f32)
    before = jnp.dot(member.astype(bf16), c["tris"][...], preferred_element_type=f32) + cnt_acc[:, 0:1]
    rank_ref[...] = jnp.concatenate(
        [jnp.sum(jnp.where(oh, before, 0.0), axis=0, keepdims=True) for oh in onehots], axis=0).astype(i32)
    cnt_acc[...] = cnt_acc[...] + jnp.sum(member, axis=1, keepdims=True)
    cnt_ref[...] = cnt_acc[...].astype(i32)


def _mix(uv, gates, yssd, x2d, mc, tm, emit_v):
    t = x2d.shape[0]
    row = lambda i: (i, 0)
    col = lambda i: (0, i)
    const = lambda a: pl.BlockSpec(a.shape, lambda i: (0,) * a.ndim, pipeline_mode=pl.Buffered(1))
    v_spec = [pl.BlockSpec((tm, D_GMLP), row)] if emit_v else []
    v_shape = [jax.ShapeDtypeStruct((t, D_GMLP), f32)] if emit_v else []
    return pl.pallas_call(
        functools.partial(_mix_body, emit_v),
        grid=(t // tm,),
        in_specs=[
            pl.BlockSpec((tm, D_INNER), row),
            pl.BlockSpec((tm, D_INNER), row),
            pl.BlockSpec((tm, D_INNER), row),
            pl.BlockSpec((tm, D_MODEL), row),
        ] + [const(mc[k]) for k in _MIX_CONST_NAMES],
        out_specs=[pl.BlockSpec((tm, D_MODEL), row), pl.BlockSpec((tm, D_PACK), row)] + v_spec + [
            pl.BlockSpec((TOP_K, tm), col),
            pl.BlockSpec((TOP_K, tm), col),
            pl.BlockSpec((TOP_K, tm), col),
            pl.BlockSpec((N_EXPERTS, 128), lambda i: (0, 0)),
        ],
        out_shape=[
            jax.ShapeDtypeStruct((t, D_MODEL), f32),
            jax.ShapeDtypeStruct((t, D_PACK), u32),
        ] + v_shape + [
            jax.ShapeDtypeStruct((TOP_K, t), i32),
            jax.ShapeDtypeStruct((TOP_K, t), f32),
            jax.ShapeDtypeStruct((TOP_K, t), i32),
            jax.ShapeDtypeStruct((N_EXPERTS, 128), i32),
        ],
        scratch_shapes=[pltpu.VMEM((N_EXPERTS, 128), f32)],
        compiler_params=_params("arbitrary"),
        name="mix_route",
    )(uv, gates, yssd, x2d, *[mc[k] for k in _MIX_CONST_NAMES])


def _dispatch(slot_kt, hf_all, n_slots):
    info = plsc.get_sparse_core_info()
    workers = info.num_cores * info.num_subcores
    t = hf_all.shape[0]
    per_w = t // workers
    nch = per_w // SC_CHUNK
    assert per_w * workers == t and nch * SC_CHUNK == per_w
    slots = slot_kt.reshape(TOP_K, workers, nch, SC_CHUNK).transpose(1, 0, 2, 3)
    mesh = plsc.VectorSubcoreMesh(core_axis_name="c", subcore_axis_name="s")

    @functools.partial(
        pl.kernel, mesh=mesh,
        out_type=jax.ShapeDtypeStruct((n_slots, D_PACK), u32),
        scratch_types=[pltpu.VMEM((TOP_K, nch, SC_CHUNK), i32), pltpu.VMEM((2, SC_CHUNK, D_PACK), u32)]
        + [pltpu.SemaphoreType.DMA] * 4,
    )
    def scatter_rows(hf_hbm, slot_hbm, out_hbm, idx_v, rows_v, sin0, sin1, sout0, sout1):
        sin, sout = (sin0, sin1), (sout0, sout1)
        wid = lax.axis_index("s") * info.num_cores + lax.axis_index("c")
        base = wid * per_w
        pltpu.sync_copy(slot_hbm.at[wid], idx_v)

        def load(j):
            rows = pl.ds(pl.multiple_of(base + j * SC_CHUNK, SUBLANES), SC_CHUNK)
            return pltpu.make_async_copy(hf_hbm.at[rows], rows_v.at[j % 2], sin[j % 2])

        def scatters(j):
            return [pltpu.make_async_copy(rows_v.at[j % 2], out_hbm.at[idx_v.at[k, j]], sout[j % 2]) for k in range(TOP_K)]

        load(0).start()
        for j in range(nch):
            load(j).wait()
            if j >= 1:
                for cp in scatters(j - 1):
                    cp.wait()
            if j + 1 < nch:
                load(j + 1).start()
            for cp in scatters(j):
                cp.start()
        for cp in scatters(nch - 1):
            cp.wait()

    return scatter_rows(hf_all, slots)


def _experts_body(be_ref, nused_ref, start_ref, slot_ref, next_ref, nvalid_ref, xs_ref, bgu_ref, bd_ref, perm_ref, wgu_hbm, wd_hbm,
                  ys_ref, wgu_f32, wd_f32, wgu_bf, wd_bf, sems):
    i = pl.program_id(0)

    def weight_copies(e, slot):
        return (pltpu.make_async_copy(wgu_hbm.at[e], wgu_f32.at[slot], sems.at[0, slot]),
                pltpu.make_async_copy(wd_hbm.at[e], wd_f32.at[slot], sems.at[1, slot]))

    @pl.when(i < nused_ref[0])
    def _():
        @pl.when(start_ref[i] == 1)
        def _():
            slot = slot_ref[i]

            @pl.when(i == 0)
            def _():
                for cp in weight_copies(be_ref[0], 0):
                    cp.start()

            for cp in weight_copies(be_ref[i], slot):
                cp.wait()

            @pl.when(next_ref[i] >= 0)
            def _():
                for cp in weight_copies(next_ref[i], 1 - slot):
                    cp.start()

            wd_bf[...] = wd_f32[slot].astype(bf16)
            half = DEINT // 2
            for j in range(2 * D_FF // DEINT):
                w = jnp.dot(wgu_f32[slot, :, j * DEINT:(j + 1) * DEINT].astype(bf16), perm_ref[...],
                            preferred_element_type=f32).astype(bf16)
                wgu_bf[:, j * half:(j + 1) * half] = w[:, :half]
                wgu_bf[:, D_FF + j * half:D_FF + (j + 1) * half] = w[:, half:]

        row = lax.broadcasted_iota(i32, (MOE_BLOCK, 1), 0)
        xs = jnp.where(row < nvalid_ref[i], xs_ref[...], jnp.zeros((), u32))
        gu = jnp.dot(_unpack_bf16_pair(xs), wgu_bf[...], preferred_element_type=f32) + bgu_ref[0]
        glu = jnp.minimum(gu[:, :D_FF], SWIGLU_LIMIT)
        lin = jnp.clip(gu[:, D_FF:], -SWIGLU_LIMIT, SWIGLU_LIMIT)
        act = glu * jax.nn.sigmoid(SWIGLU_ALPHA * glu) * (lin + 1.0)
        ys_ref[...] = jnp.dot(act.astype(bf16), wd_bf[...], preferred_element_type=f32) + bd_ref[0]

    @pl.when(i >= nused_ref[0])
    def _():
        ys_ref[...] = jnp.zeros_like(ys_ref)


def _expert_segments(block_e, n_used, n_blocks):
    idx = jnp.arange(n_blocks, dtype=i32)
    prev_e = jnp.concatenate([block_e[:1], block_e[:-1]])
    start = (idx < n_used) & ((idx == 0) | (block_e != prev_e))
    slot = (jnp.cumsum(start.astype(i32)) - 1) % 2
    start_idx = jnp.where(start, idx, n_blocks)
    from_here = lax.cummin(start_idx, reverse=True)
    next_start = jnp.concatenate([from_here[1:], jnp.full((1,), n_blocks, i32)])
    next_e = jnp.sum(jnp.where(next_start[:, None] == idx[None, :], block_e[None, :], 0), axis=1)
    next_e = jnp.where(next_start < n_blocks, next_e, -1)
    return start.astype(i32), slot.astype(i32), next_e.astype(i32)


def _experts(block_e, n_used, n_valid, xs, wgu, bgu, w_down, b_down):
    n_blocks = xs.shape[0] // MOE_BLOCK
    start, slot, next_e = _expert_segments(block_e, n_used[0], n_blocks)
    blk = lambda i, be, nu, *_: (jnp.maximum(jnp.minimum(i, nu[0] - 1), 0), 0)
    exp3 = lambda i, be, *_: (be[i], 0, 0)
    c = np.arange(DEINT)
    src = np.where(c < DEINT // 2, 2 * c, 2 * (c - DEINT // 2) + 1)
    perm = jnp.asarray(np.arange(DEINT)[:, None] == src[None, :], bf16)
    return pl.pallas_call(
        _experts_body,
        grid_spec=pltpu.PrefetchScalarGridSpec(
            num_scalar_prefetch=6,
            grid=(n_blocks,),
            in_specs=[
                pl.BlockSpec((MOE_BLOCK, D_PACK), blk),
                pl.BlockSpec((1, 1, 2 * D_FF), exp3),
                pl.BlockSpec((1, 1, D_MODEL), exp3),
                pl.BlockSpec((DEINT, DEINT), lambda i, *_: (0, 0)),
                pl.BlockSpec(memory_space=pl.ANY),
                pl.BlockSpec(memory_space=pl.ANY),
            ],
            out_specs=pl.BlockSpec((MOE_BLOCK, D_MODEL), lambda i, *_: (i, 0)),
            scratch_shapes=[
                pltpu.VMEM((2, D_MODEL, 2 * D_FF), f32),
                pltpu.VMEM((2, D_FF, D_MODEL), f32),
                pltpu.VMEM((D_MODEL, 2 * D_FF), bf16),
                pltpu.VMEM((D_FF, D_MODEL), bf16),
                pltpu.SemaphoreType.DMA((2, 2)),
            ],
        ),
        out_shape=jax.ShapeDtypeStruct((xs.shape[0], D_MODEL), f32),
        compiler_params=_params("arbitrary"),
        name="moe_experts",
    )(block_e, n_used, start, slot, next_e, n_valid, xs, bgu, b_down, perm, wgu, w_down)


def _combine_body(tm, t, slot_ref, h_ref, gate_ref, nf_ref, ys_ref, o_ref, rows_ref, sems):
    s = pl.program_id(0)
    n_tiles = pl.num_programs(0) - 1

    def request():
        buf = s % 2
        for r in range(tm):
            for k in range(TOP_K):
                slot = slot_ref[k * t + s * tm + r]
                pltpu.make_async_copy(ys_ref.at[slot], rows_ref.at[buf, k, r], sems.at[buf]).start(priority=k % 2)

    def finish():
        buf = (s - 1) % 2
        for r in range(tm):
            for k in range(TOP_K):
                pltpu.make_async_copy(ys_ref.at[0], rows_ref.at[buf, k, r], sems.at[buf]).wait()
        g = gate_ref[...]
        moe = g[:, 0:1] * rows_ref[buf, 0]
        for k in range(1, TOP_K):
            moe = moe + g[:, k:k + 1] * rows_ref[buf, k]
        h = h_ref[...] + moe
        o_ref[...] = h * lax.rsqrt(jnp.mean(h * h, axis=-1, keepdims=True) + EPS) * nf_ref[...]

    @pl.when(s < n_tiles)
    def _():
        request()

    @pl.when(s > 0)
    def _():
        finish()


def _combine(slot_flat, h, gate_t, norm_final, ys, tm):
    t = h.shape[0]
    return pl.pallas_call(
        functools.partial(_combine_body, tm, t),
        grid_spec=pltpu.PrefetchScalarGridSpec(
            num_scalar_prefetch=1,
            grid=(t // tm + 1,),
            in_specs=[
                pl.BlockSpec((tm, D_MODEL), lambda i, s: (jnp.maximum(i - 1, 0), 0)),
                pl.BlockSpec((tm, TOP_K), lambda i, s: (jnp.maximum(i - 1, 0), 0)),
                pl.BlockSpec((1, D_MODEL), lambda i, s: (0, 0)),
                pl.BlockSpec(memory_space=pl.ANY),
            ],
            out_specs=pl.BlockSpec((tm, D_MODEL), lambda i, s: (jnp.maximum(i - 1, 0), 0)),
            scratch_shapes=[pltpu.VMEM((2, TOP_K, tm, D_MODEL), f32), pltpu.SemaphoreType.DMA((2,))],
        ),
        out_shape=jax.ShapeDtypeStruct((t, D_MODEL), f32),
        compiler_params=_params("arbitrary"),
        name="moe_combine",
    )(slot_flat, h, gate_t, norm_final, ys)


def kernel(x_prompt, x_sample, state_conv, state_ssm, norm_mix, w_in, conv_w, conv_b, dt_bias, a_log, d_skip, gnorm_w, v_norm_w, v_norm_b, w_spatial, b_spatial, w_branch_ssd, w_branch_mlp, w_out, norm_ffn, w_router, b_router, w_gu, b_gu, w_down, b_down, norm_final):
    assert w_in.shape[0] == 1, "single-layer trunk"
    batch, seq, _ = x_prompt.shape
    nseq, dec_seq, _ = x_sample.shape
    assert seq % CHUNK == 0 and CHUNK % dec_seq == 0 and nseq % (CHUNK // dec_seq) == 0 and dec_seq >= CONV_W - 1
    t_p, t_s = batch * seq, nseq * dec_seq
    tm = ROW_DMA_TILE
    assert t_p % tm == 0 and t_s % tm == 0
    tm_mix = MIX_TILE if t_p % MIX_TILE == 0 and t_s % MIX_TILE == 0 else tm

    wi = w_in[0]
    z0, x0, d0, u0 = D_INNER, D_INNER + CONV_DIM, D_INNER + CONV_DIM + HEADS, D_INNER + CONV_DIM + HEADS + 2 * D_GMLP
    w_segments = tuple(wi[:, a:b].astype(bf16) for a, b in ((0, z0), (d0, u0), (u0, D_IN_PROJ), (z0, x0)))
    w_dt = wi[:, x0:d0].astype(bf16)
    w_dtT = w_dt.T
    nm = norm_mix[0].reshape(1, D_MODEL)
    sc = lambda L: _ssd_consts(conv_w[0], conv_b[0], dt_bias[0], a_log[0], d_skip[0], gnorm_w[0], L)
    xp2, xs2 = x_prompt.reshape(t_p, D_MODEL), x_sample.reshape(t_s, D_MODEL)

    tril = np.tril(np.ones((CHUNK, CHUNK), bool))
    ws_p = jnp.where(tril[None], w_spatial[0], 0).astype(bf16)
    per = CHUNK // dec_seq
    blockdiag = (np.arange(CHUNK)[:, None] // dec_seq) == (np.arange(CHUNK)[None, :] // dec_seq)
    ws_s = jnp.where((tril & blockdiag)[None], jnp.tile(w_spatial[0][:, :dec_seq, :dec_seq], (1, per, per)), 0).astype(bf16)
    bs_p = b_spatial[0].T
    bs_s = jnp.tile(b_spatial[0][:, :dec_seq], (1, per)).T
    mc = dict(
        vnw=v_norm_w[0].reshape(1, D_GMLP), vnb=v_norm_b[0].reshape(1, D_GMLP),
        wbs=w_branch_ssd[0].astype(bf16), wbm=w_branch_mlp[0].astype(bf16), wo=w_out[0].astype(bf16),
        nffn=norm_ffn[0].reshape(1, D_MODEL), wrT=w_router[0].T, br=b_router[0].reshape(N_EXPERTS, 1),
        tris=jnp.asarray(np.triu(np.ones((tm_mix, tm_mix), np.float32), 1), bf16),
    )

    z_p, uv_p, g_p, xbc_p, dt_p, dtT_p = _in_proj(xp2, nm, w_segments, w_dt, w_dtT)
    yssd_p, ssm_p = _ssd_prompt(z_p, xbc_p, dt_p, dtT_p, sc(CHUNK), batch, seq)
    h_p, hf_p, te_p, gate_p, rank_p, cnt_p = _mix(uv_p, g_p, yssd_p, xp2, dict(mc, ws=ws_p, bs=bs_p), tm_mix, False)

    z_s, uv_s, g_s, xbc_s, dt_s, dtT_s = _in_proj(xs2, nm, w_segments, w_dt, w_dtT)
    pre = jnp.pad(state_conv[0], ((0, 0), (dec_seq - (CONV_W - 1), 0), (0, 0))).reshape(t_s, CONV_DIM)
    yssd_s, ssm_s = _ssd_sample(z_s, xbc_s, pre, dt_s, dtT_s, state_ssm[0].reshape(nseq, D_INNER, D_STATE),
                                sc(dec_seq), nseq, dec_seq)
    h_s, hf_s, v_s, te_s, gate_s, rank_s, cnt_s = _mix(uv_s, g_s, yssd_s, xs2, dict(mc, ws=ws_s, bs=bs_s), tm_mix, True)

    cp, cs = cnt_p[:, 0], cnt_s[:, 0]
    padded = (cp + cs + MOE_BLOCK - 1) // MOE_BLOCK * MOE_BLOCK
    pend = jnp.cumsum(padded)
    pstart = pend - padded

    def lookup(table, te):
        eid = jnp.arange(N_EXPERTS, dtype=i32)[:, None, None]
        return jnp.sum(jnp.where(te[None] == eid, table[:, None, None], 0), axis=0)

    slot_kt_p = (lookup(pstart, te_p) + rank_p).astype(i32)
    slot_kt_s = (lookup(pstart + cp, te_s) + rank_s).astype(i32)
    slot_p, slot_s = slot_kt_p.reshape(-1), slot_kt_s.reshape(-1)
    n_blocks = (t_p + t_s) * TOP_K // MOE_BLOCK + N_EXPERTS
    n_used = (pend[-1] // MOE_BLOCK).astype(i32).reshape(1)
    first_row = jnp.arange(n_blocks, dtype=i32) * MOE_BLOCK
    block_e = jnp.minimum(jnp.sum(pend[None, :] <= first_row[:, None], axis=1), N_EXPERTS - 1).astype(i32)
    last_row = jnp.sum(jnp.where(block_e[:, None] == jnp.arange(N_EXPERTS)[None, :], (pstart + cp + cs)[None, :], 0), axis=1)
    n_valid = jnp.clip(last_row - first_row, 0, MOE_BLOCK).astype(i32)

    xs_sorted = _dispatch(jnp.concatenate([slot_kt_p, slot_kt_s], axis=1), jnp.concatenate([hf_p, hf_s]),
                          n_blocks * MOE_BLOCK)
    bgu = jnp.concatenate([b_gu[0][..., 0::2], b_gu[0][..., 1::2]], axis=-1).reshape(N_EXPERTS, 1, 2 * D_FF)
    ys = _experts(block_e, n_used, n_valid, xs_sorted, w_gu[0], bgu, w_down[0], b_down[0].reshape(N_EXPERTS, 1, D_MODEL))

    nf = norm_final.reshape(1, D_MODEL)
    y_p = _combine(slot_p, h_p, gate_p.T, nf, ys, tm)
    y_s = _combine(slot_s, h_s, gate_s.T, nf, ys, tm)

    conv_p = xbc_p.reshape(batch, seq, CONV_DIM)[:, seq - (CONV_W - 1):]
    conv_s = xbc_s.reshape(nseq, dec_seq, CONV_DIM)[:, dec_seq - (CONV_W - 1):]
    st_shape = (HEADS, HEAD_DIM, D_STATE)
    return (
        y_p.reshape(batch, seq, D_MODEL),
        y_s.reshape(nseq, dec_seq, D_MODEL),
        conv_p[None],
        ssm_p.reshape(1, batch, *st_shape),
        conv_s[None],
        ssm_s.reshape(1, nseq, *st_shape),
        v_s.reshape(1, nseq, dec_seq, D_GMLP),
    )
```

```python
import functools

import numpy as np
import jax
import jax.numpy as jnp
from jax import lax
from jax.experimental import pallas as pl
from jax.experimental.pallas import tpu as pltpu
from jax.experimental.pallas import tpu_sc as plsc

f32, bf16, i32, u32 = jnp.float32, jnp.bfloat16, jnp.int32, jnp.uint32

D_MODEL = 1024
D_INNER = 2048
HEAD_DIM = 64
HEADS = 32
GROUPS = 4
HEADS_PER_GROUP = 8
GROUP_WIDTH = HEADS_PER_GROUP * HEAD_DIM
D_STATE = 128
CONV_W = 4
CONV_DIM = D_INNER + 2 * GROUPS * D_STATE
CHUNK = 128
SUBLANES = 8
D_GMLP = 1024
GMLP_HEADS = 8
N_EXPERTS = 32
TOP_K = 4
D_FF = 1024
SWIGLU_ALPHA = 1.702
SWIGLU_LIMIT = 7.0
EPS = 1e-5
LOG2_E = float(np.log2(np.e))
MOE_BLOCK = 512
DEINT = 256
ROW_DMA_TILE = 256
SC_CHUNK = 32
MIX_TILE = 512
D_IN_PROJ = 3 * D_INNER + CONV_DIM + HEADS
PROJ_TN = 512
PROMPT_SEQS_PER_STEP = 2
SAMPLE_SEQS_PER_STEP = 8
D_PACK = D_MODEL // 2
VMEM_LIMIT = 56 * 1024 * 1024

_NT = (((1,), (1,)), ((), ()))
_HI16 = np.uint32(0xFFFF0000)


def _params(*sem):
    return pltpu.CompilerParams(dimension_semantics=sem, vmem_limit_bytes=VMEM_LIMIT)


def _split(x, parts):
    out = []
    for _ in range(parts - 1):
        p = x.astype(bf16)
        out.append(p)
        x = x - p.astype(f32)
    out.append(x.astype(bf16))
    return out


def _dot_sel_rhs(x, sel, parts=3):
    return sum(jnp.dot(p, sel, preferred_element_type=f32) for p in _split(x, parts))


def _dot_sel_lhs(sel, x, parts=3):
    return sum(jnp.dot(sel, p, preferred_element_type=f32) for p in _split(x, parts))


def _silu(x):
    return x * jax.nn.sigmoid(x)


def _pack_bf16_pair(x):
    n = x.shape[1] // 2
    lo = lax.bitcast_convert_type(x[:, :n].astype(bf16).astype(f32), u32) >> 16
    hi = lax.bitcast_convert_type(x[:, n:].astype(bf16).astype(f32), u32) & _HI16
    return lo | hi


def _unpack_bf16_pair(p):
    lo = lax.bitcast_convert_type(p << 16, f32).astype(bf16)
    hi = lax.bitcast_convert_type(p & _HI16, f32).astype(bf16)
    return jnp.concatenate([lo, hi], axis=1)


def _in_proj_body(x_ref, nw_ref, wz_ref, wuv_ref, wg_ref, wxbc_ref, wdt_ref, wdtT_ref,
                  z_ref, uv_ref, g_ref, xbc_ref, dt_ref, dtT_ref):
    x = x_ref[...]
    ms = jnp.mean(x * x, axis=-1, keepdims=True)
    hn = (x * lax.rsqrt(ms + EPS) * nw_ref[...]).astype(bf16)
    dt_ref[...] = jnp.dot(hn, wdt_ref[...], preferred_element_type=f32)
    dtT_ref[...] = lax.dot_general(wdtT_ref[...], hn, _NT, preferred_element_type=f32)
    for o_ref, w_ref in ((z_ref, wz_ref), (uv_ref, wuv_ref), (g_ref, wg_ref), (xbc_ref, wxbc_ref)):
        for c in range(0, w_ref.shape[1], PROJ_TN):
            acc = jnp.dot(hn, w_ref[:, c:c + PROJ_TN], preferred_element_type=f32)
            o_ref[:, c:c + PROJ_TN] = acc.astype(o_ref.dtype)


def _in_proj(x2d, norm_w, w_segments, w_dt, w_dtT):
    t = x2d.shape[0]
    tm = min(512, t)
    row = lambda i: (i, 0)
    fixed = lambda i: (0, 0)
    return pl.pallas_call(
        _in_proj_body,
        grid=(t // tm,),
        in_specs=[
            pl.BlockSpec((tm, D_MODEL), row),
            pl.BlockSpec((1, D_MODEL), fixed),
        ] + [pl.BlockSpec(w.shape, fixed, pipeline_mode=pl.Buffered(1)) for w in w_segments] + [
            pl.BlockSpec((D_MODEL, HEADS), fixed),
            pl.BlockSpec((HEADS, D_MODEL), fixed),
        ],
        out_specs=[
            pl.BlockSpec((tm, D_INNER), row),
            pl.BlockSpec((tm, D_INNER), row),
            pl.BlockSpec((tm, D_INNER), row),
            pl.BlockSpec((tm, CONV_DIM), row),
            pl.BlockSpec((tm, HEADS), row),
            pl.BlockSpec((HEADS, tm), lambda i: (0, i)),
        ],
        out_shape=[
            jax.ShapeDtypeStruct((t, D_INNER), bf16),
            jax.ShapeDtypeStruct((t, D_INNER), bf16),
            jax.ShapeDtypeStruct((t, D_INNER), bf16),
            jax.ShapeDtypeStruct((t, CONV_DIM), f32),
            jax.ShapeDtypeStruct((t, HEADS), f32),
            jax.ShapeDtypeStruct((HEADS, t), f32),
        ],
        compiler_params=_params("arbitrary"),
        name="in_proj",
    )(x2d, norm_w, *w_segments, w_dt, w_dtT)


def _conv_taps(shifted, cw_ref, cb_ref):
    acc = cb_ref[...]
    for k in range(CONV_W):
        acc = acc + shifted(CONV_W - 1 - k) * cw_ref[k:k + 1, :]
    return _silu(acc)


def _conv_silu_one_seq(cur, prev8, cw_ref, cb_ref):
    assert CONV_W == 4
    row8 = lax.broadcasted_iota(i32, (SUBLANES, 1), 0)

    def shift_rows(x, x_prev8, j):
        xr = pltpu.roll(x, j, 0)
        top = jnp.where(row8 >= j, xr[:SUBLANES], pltpu.roll(x_prev8, j, 0))
        return jnp.concatenate([top, xr[SUBLANES:]], axis=0)

    w = [cw_ref[k:k + 1, :] for k in range(CONV_W)]
    x1 = shift_rows(cur, prev8, 1)
    near = x1 * w[2] + cur * w[3]
    far = x1 * w[0] + cur * w[1]
    far_prev8 = pltpu.roll(prev8, 1, 0) * w[0] + prev8 * w[1]
    return _silu(cb_ref[...] + shift_rows(far, far_prev8, 2) + near)


def _conv_silu_many_seq(cur, pre, cw_ref, cb_ref, seq_len):
    n = cur.shape[0]
    pos = lax.broadcasted_iota(i32, (n, 1), 0) % seq_len

    def shifted(j):
        if j == 0:
            return cur
        return jnp.where(pos >= j, pltpu.roll(cur, j, 0), pltpu.roll(pre, (j + n - seq_len) % n, 0))

    return _conv_taps(shifted, cw_ref, cb_ref)


def _ssd_block(xa, dt_raw, dtT_raw, c):
    tril = c["tril"][...]
    dt = jax.nn.softplus(dt_raw + c["dtb_row"][...])
    dtT = jax.nn.softplus(dtT_raw + c["dtb_col"][...])
    a = dt * c["aneg_row"][...]
    aT = dtT * c["aneg_col"][...]
    a_cum = _dot_sel_lhs(tril, a)
    a_cumT = _dot_sel_rhs(aT, c["triu"][...])
    a_tot = _dot_sel_lhs(c["same"][...], a)
    stack = jnp.concatenate([dt, jnp.exp(a_cum), jnp.exp(a_tot - a_cum)], axis=0)
    ex = jnp.dot(jnp.concatenate(_split(stack, 2), axis=1), c["expand"][...], preferred_element_type=f32)
    n = xa.shape[0]
    dtx, eax, dex = ex[:n], ex[n:2 * n], ex[2 * n:]

    xs = xa[:, :D_INNER]
    bm = xa[:, D_INNER:D_INNER + GROUPS * D_STATE]
    cm = xa[:, D_INNER + GROUPS * D_STATE:]
    xdt = xs * dtx
    xdt_bf = xdt.astype(bf16)
    mask = tril > 0
    a_cum2, a_cumT2 = a_cum * LOG2_E, a_cumT * LOG2_E
    lane = lax.broadcasted_iota(i32, (1, 2 * HEAD_DIM), 1)
    ys = []
    for g in range(GROUPS):
        cg = cm[:, g * D_STATE:(g + 1) * D_STATE].astype(bf16)
        bg = bm[:, g * D_STATE:(g + 1) * D_STATE].astype(bf16)
        cb = lax.dot_general(cg, bg, _NT, preferred_element_type=f32)
        for pair in range(HEADS_PER_GROUP // 2):
            halves = []
            for h in (g * HEADS_PER_GROUP + 2 * pair, g * HEADS_PER_GROUP + 2 * pair + 1):
                seg = a_cum2[:, h:h + 1] - a_cumT2[h:h + 1, :]
                decay = jnp.exp2(jnp.where(mask, seg, -jnp.inf))
                m = (cb * decay).astype(bf16)
                col = (h // 2) * 2 * HEAD_DIM
                halves.append(jnp.dot(m, xdt_bf[:, col:col + 2 * HEAD_DIM], preferred_element_type=f32))
            ys.append(jnp.where(lane < HEAD_DIM, halves[0], halves[1]))
    y_diag = jnp.concatenate(ys, axis=1)
    return dict(xs=xs, bm=bm, cm=cm, xdt=xdt, xd=xdt * dex, eax=eax, aT=aT, y_diag=y_diag)


def _gated_norm(y, z, gw):
    g = y * _silu(z.astype(f32))
    ms = jnp.mean(g * g, axis=-1, keepdims=True)
    return (g * lax.rsqrt(ms + EPS) * gw).astype(bf16)


_SSD_CONST_NAMES = ("cw", "cb", "dtb_row", "dtb_col", "aneg_row", "aneg_col", "dskipx", "gw",
                    "tril", "triu", "same", "expand")


def _ssd_consts(conv_w, conv_b, dt_bias, a_log, d_skip, gnorm_w, seq_len):
    r = np.arange(CHUNK)
    same = (r[:, None] // seq_len) == (r[None, :] // seq_len)
    tril = same & (r[None, :] <= r[:, None])
    expand = np.tile(np.repeat(np.eye(HEADS, dtype=np.float32), HEAD_DIM, axis=1), (2, 1))
    aneg = -jnp.exp(a_log.astype(f32))
    return dict(
        cw=conv_w, cb=conv_b.reshape(1, CONV_DIM),
        dtb_row=dt_bias.reshape(1, HEADS), dtb_col=dt_bias.reshape(HEADS, 1),
        aneg_row=aneg.reshape(1, HEADS), aneg_col=aneg.reshape(HEADS, 1),
        dskipx=jnp.repeat(d_skip.astype(f32), HEAD_DIM).reshape(1, D_INNER),
        gw=gnorm_w.reshape(1, D_INNER),
        tril=jnp.asarray(tril, bf16), triu=jnp.asarray(tril.T, bf16), same=jnp.asarray(same, bf16),
        expand=jnp.asarray(expand, bf16),
    )


def _const_specs(consts):
    zero = (lambda *_: (0, 0))
    return [pl.BlockSpec(consts[k].shape, zero) for k in _SSD_CONST_NAMES]


def _ssd_prompt_body(nb, z_ref, xbc_ref, dt_ref, dtT_ref, *rest):
    nc = len(_SSD_CONST_NAMES)
    c = dict(zip(_SSD_CONST_NAMES, rest[:nc]))
    y_ref, state_ref, prev_ref, st_ref = rest[nc:]
    ci = pl.program_id(1)

    @pl.when(ci == 0)
    def _():
        prev_ref[...] = jnp.zeros_like(prev_ref)
        st_ref[...] = jnp.zeros_like(st_ref)

    for s in range(nb):
        cur = xbc_ref[s]
        xa = _conv_silu_one_seq(cur, prev_ref[s], c["cw"], c["cb"])
        prev_ref[s] = cur[CHUNK - SUBLANES:]
        b = _ssd_block(xa, dt_ref[s], dtT_ref[s], c)

        y_off = []
        for g in range(GROUPS):
            sl = slice(g * GROUP_WIDTH, (g + 1) * GROUP_WIDTH)
            st = st_ref[s * GROUPS + g]
            cg = b["cm"][:, g * D_STATE:(g + 1) * D_STATE].astype(bf16)
            y_off.append(jnp.dot(cg, st.astype(bf16), preferred_element_type=f32))
            bgT = b["bm"][:, g * D_STATE:(g + 1) * D_STATE].T.astype(bf16)
            upd = jnp.dot(bgT, b["xd"][:, sl].astype(bf16), preferred_element_type=f32)
            st_ref[s * GROUPS + g] = st * b["eax"][CHUNK - 1:CHUNK, sl] + upd
        y = b["y_diag"] + jnp.concatenate(y_off, axis=1) * b["eax"] + c["dskipx"][...] * b["xs"]
        y_ref[s] = _gated_norm(y, z_ref[s], c["gw"][...])

    @pl.when(ci == pl.num_programs(1) - 1)
    def _():
        for s in range(nb):
            for g in range(GROUPS):
                state_ref[s, 0, g * GROUP_WIDTH:(g + 1) * GROUP_WIDTH, :] = st_ref[s * GROUPS + g].T


def _ssd_prompt(z, xbc, dt, dtT, consts, batch, seq):
    nb = PROMPT_SEQS_PER_STEP if batch % PROMPT_SEQS_PER_STEP == 0 else 1
    nchunk = seq // CHUNK
    part = batch // nb * seq
    split = lambda a: a.reshape(nb, part, a.shape[-1])
    dtT3 = dtT.reshape(HEADS, nb, part).transpose(1, 0, 2)
    row = lambda b, ci: (0, b * nchunk + ci, 0)
    y, state = pl.pallas_call(
        functools.partial(_ssd_prompt_body, nb),
        grid=(batch // nb, nchunk),
        in_specs=[
            pl.BlockSpec((nb, CHUNK, D_INNER), row),
            pl.BlockSpec((nb, CHUNK, CONV_DIM), row),
            pl.BlockSpec((nb, CHUNK, HEADS), row),
            pl.BlockSpec((nb, HEADS, CHUNK), lambda b, ci: (0, 0, b * nchunk + ci)),
        ] + _const_specs(consts),
        out_specs=[
            pl.BlockSpec((nb, CHUNK, D_INNER), row),
            pl.BlockSpec((nb, 1, D_INNER, D_STATE), lambda b, ci: (0, b, 0, 0)),
        ],
        out_shape=[
            jax.ShapeDtypeStruct((nb, part, D_INNER), bf16),
            jax.ShapeDtypeStruct((nb, batch // nb, D_INNER, D_STATE), f32),
        ],
        scratch_shapes=[pltpu.VMEM((nb, SUBLANES, CONV_DIM), f32),
                        pltpu.VMEM((nb * GROUPS, D_STATE, GROUP_WIDTH), f32)],
        compiler_params=_params("arbitrary", "arbitrary"),
        name="ssd_prompt",
    )(split(z), split(xbc), split(dt), dtT3, *[consts[k] for k in _SSD_CONST_NAMES])
    return y.reshape(batch * seq, D_INNER), state.reshape(batch, D_INNER, D_STATE)


def _ssd_sample_body(seq_len, z_ref, xbc_ref, pre_ref, dt_ref, dtT_ref, state_ref, *rest):
    nc = len(_SSD_CONST_NAMES)
    c = dict(zip(_SSD_CONST_NAMES, rest[:nc]))
    y_ref, state_out_ref, c_ref, b_ref, xdT_ref, yacc_ref, eax_ref, eatT_ref = rest[nc:]
    step = pl.program_id(1)
    per_step = state_ref.shape[0]

    @pl.when(step == 0)
    def _():
        xa = _conv_silu_many_seq(xbc_ref[...], pre_ref[...], c["cw"], c["cb"], seq_len)
        b = _ssd_block(xa, dt_ref[...], dtT_ref[...], c)
        c_ref[...] = b["cm"]
        b_ref[...] = b["bm"]
        xdT_ref[...] = b["xd"].T.astype(bf16)
        yacc_ref[...] = b["y_diag"] + c["dskipx"][...] * b["xs"]
        eax_ref[...] = b["eax"]
        eatT_ref[...] = jnp.exp(_dot_sel_rhs(b["aT"], c["same"][...]))

    def one_sequence(q, carry):
        s = step * per_step + q
        r0 = pl.multiple_of(s * seq_len, seq_len)
        rows = pl.ds(r0, seq_len)
        lane = lax.broadcasted_iota(i32, (1, CHUNK), 1)
        arep = jnp.broadcast_to(jnp.sum(jnp.where(lane == r0, eatT_ref[...], 0.0), axis=1, keepdims=True), (HEADS, D_STATE))
        rmask = (lax.broadcasted_iota(i32, (CHUNK, 1), 0) // seq_len) == s
        for g in range(GROUPS):
            gs = slice(g * GROUP_WIDTH, (g + 1) * GROUP_WIDTH)
            ns = slice(g * D_STATE, (g + 1) * D_STATE)
            s0 = state_ref[q, gs, :]
            cg = c_ref[rows, ns].astype(bf16)
            yo = lax.dot_general(cg, s0.astype(bf16), _NT, preferred_element_type=f32)
            yacc_ref[rows, gs] = yacc_ref[rows, gs] + yo * eax_ref[rows, gs]
            bmask = jnp.where(rmask, b_ref[:, ns], 0.0).astype(bf16)
            upd = jnp.dot(xdT_ref[gs, :], bmask, preferred_element_type=f32)
            for r in range(HEADS_PER_GROUP):
                h = g * HEADS_PER_GROUP + r
                hs = slice(r * HEAD_DIM, (r + 1) * HEAD_DIM)
                state_out_ref[q, h * HEAD_DIM:(h + 1) * HEAD_DIM, :] = s0[hs] * arep[h:h + 1, :] + upd[hs]
        return carry

    lax.fori_loop(0, per_step, one_sequence, 0)

    @pl.when(step == pl.num_programs(1) - 1)
    def _():
        y_ref[...] = _gated_norm(yacc_ref[...], z_ref[...], c["gw"][...])


def _ssd_sample(z, xbc, pre, dt, dtT, state, consts, nseq, seq_len):
    per = CHUNK // seq_len
    nblk = nseq // per
    t = nseq * seq_len
    q = SAMPLE_SEQS_PER_STEP if per % SAMPLE_SEQS_PER_STEP == 0 else 1
    blk = lambda i, s: (i, 0)
    seq3 = lambda i, s: (i * (per // q) + s, 0, 0)
    return pl.pallas_call(
        functools.partial(_ssd_sample_body, seq_len),
        grid=(nblk, per // q),
        in_specs=[
            pl.BlockSpec((CHUNK, D_INNER), blk),
            pl.BlockSpec((CHUNK, CONV_DIM), blk),
            pl.BlockSpec((CHUNK, CONV_DIM), blk),
            pl.BlockSpec((CHUNK, HEADS), blk),
            pl.BlockSpec((HEADS, CHUNK), lambda i, s: (0, i)),
            pl.BlockSpec((q, D_INNER, D_STATE), seq3),
        ] + _const_specs(consts),
        out_specs=[
            pl.BlockSpec((CHUNK, D_INNER), blk),
            pl.BlockSpec((q, D_INNER, D_STATE), seq3),
        ],
        out_shape=[
            jax.ShapeDtypeStruct((t, D_INNER), bf16),
            jax.ShapeDtypeStruct((nseq, D_INNER, D_STATE), f32),
        ],
        scratch_shapes=[
            pltpu.VMEM((CHUNK, GROUPS * D_STATE), f32),
            pltpu.VMEM((CHUNK, GROUPS * D_STATE), f32),
            pltpu.VMEM((D_INNER, CHUNK), bf16),
            pltpu.VMEM((CHUNK, D_INNER), f32),
            pltpu.VMEM((CHUNK, D_INNER), f32),
            pltpu.VMEM((HEADS, CHUNK), f32),
        ],
        compiler_params=_params("arbitrary", "arbitrary"),
        name="ssd_sample",
    )(z, xbc, pre, dt, dtT, state, *[consts[k] for k in _SSD_CONST_NAMES])


_MIX_CONST_NAMES = ("ws", "bs", "vnw", "vnb", "wbs", "wbm", "wo", "nffn", "wrT", "br", "tris")


def _mix_body(emit_v, uv_ref, gates_ref, yssd_ref, x_ref, *rest):
    nc = len(_MIX_CONST_NAMES)
    c = dict(zip(_MIX_CONST_NAMES, rest[:nc]))
    outs = list(rest[nc:])
    h_ref, hfp_ref = outs[:2]
    v_ref = outs[2] if emit_v else None
    te_ref, gate_ref, rank_ref, cnt_ref, cnt_acc = outs[-5:]
    tm = uv_ref.shape[0]

    @pl.when(pl.program_id(0) == 0)
    def _():
        cnt_acc[...] = jnp.zeros_like(cnt_acc)

    uv = uv_ref[...].astype(f32)
    uv = 0.5 * uv * (1.0 + lax.erf(uv * np.float32(np.sqrt(0.5))))
    u, v = uv[:, :D_GMLP], uv[:, D_GMLP:]
    mu = jnp.mean(v, axis=-1, keepdims=True)
    vc = v - mu
    vn = vc * lax.rsqrt(jnp.mean(vc * vc, axis=-1, keepdims=True) + EPS) * c["vnw"][...] + c["vnb"][...]
    if emit_v:
        v_ref[...] = vn
    vn_bf = vn.astype(bf16)
    gd = D_GMLP // GMLP_HEADS
    rows = []
    for ck in range(tm // CHUNK):
        rs = slice(ck * CHUNK, (ck + 1) * CHUNK)
        heads = []
        for g in range(GMLP_HEADS):
            mixed = jnp.dot(c["ws"][g], vn_bf[rs, g * gd:(g + 1) * gd], preferred_element_type=f32)
            heads.append(mixed + c["bs"][:, g:g + 1])
        rows.append(jnp.concatenate(heads, axis=1))
    y_mlp = u * jnp.concatenate(rows, axis=0)

    a = jnp.dot(yssd_ref[...], c["wbs"][...], preferred_element_type=f32)
    b = jnp.dot(y_mlp.astype(bf16), c["wbm"][...], preferred_element_type=f32)
    gs = jax.nn.sigmoid(gates_ref[...].astype(f32))
    merged = gs[:, :D_MODEL] * a + gs[:, D_MODEL:] * b
    h = x_ref[...] + jnp.dot(merged.astype(bf16), c["wo"][...], preferred_element_type=f32)
    h_ref[...] = h
    hf = h * lax.rsqrt(jnp.mean(h * h, axis=-1, keepdims=True) + EPS) * c["nffn"][...]
    hfp_ref[...] = _pack_bf16_pair(hf)

    lg = lax.dot_general(c["wrT"][...], hf, _NT, precision=lax.Precision.HIGHEST, preferred_element_type=f32) + c["br"][...]
    sub = lax.broadcasted_iota(i32, lg.shape, 0)
    idxs, vals = [], []
    for _ in range(TOP_K):
        m = jnp.max(lg, axis=0, keepdims=True)
        idx = jnp.min(jnp.where(lg == m, sub, N_EXPERTS), axis=0, keepdims=True)
        idxs.append(idx)
        vals.append(m)
        lg = jnp.where(sub == idx, -jnp.inf, lg)
    p = jnp.exp(jnp.concatenate(vals, axis=0) - vals[0])
    gate_ref[...] = p / jnp.sum(p, axis=0, keepdims=True)
    te_ref[...] = jnp.concatenate(idxs, axis=0)

    onehots = [sub == idx for idx in idxs]
    member = functools.reduce(jnp.logical_or, onehots).astype(f32)
    before = jnp.dot(member.astype(bf16), c["tris"][...], preferred_element_type=f32) + cnt_acc[:, 0:1]
    rank_ref[...] = jnp.concatenate(
        [jnp.sum(jnp.where(oh, before, 0.0), axis=0, keepdims=True) for oh in onehots], axis=0).astype(i32)
    cnt_acc[...] = cnt_acc[...] + jnp.sum(member, axis=1, keepdims=True)
    cnt_ref[...] = cnt_acc[...].astype(i32)


def _mix(uv, gates, yssd, x2d, mc, tm, emit_v):
    t = x2d.shape[0]
    row = lambda i: (i, 0)
    col = lambda i: (0, i)
    const = lambda a: pl.BlockSpec(a.shape, lambda i: (0,) * a.ndim, pipeline_mode=pl.Buffered(1))
    v_spec = [pl.BlockSpec((tm, D_GMLP), row)] if emit_v else []
    v_shape = [jax.ShapeDtypeStruct((t, D_GMLP), f32)] if emit_v else []
    return pl.pallas_call(
        functools.partial(_mix_body, emit_v),
        grid=(t // tm,),
        in_specs=[
            pl.BlockSpec((tm, D_INNER), row),
            pl.BlockSpec((tm, D_INNER), row),
            pl.BlockSpec((tm, D_INNER), row),
            pl.BlockSpec((tm, D_MODEL), row),
        ] + [const(mc[k]) for k in _MIX_CONST_NAMES],
        out_specs=[pl.BlockSpec((tm, D_MODEL), row), pl.BlockSpec((tm, D_PACK), row)] + v_spec + [
            pl.BlockSpec((TOP_K, tm), col),
            pl.BlockSpec((TOP_K, tm), col),
            pl.BlockSpec((TOP_K, tm), col),
            pl.BlockSpec((N_EXPERTS, 128), lambda i: (0, 0)),
        ],
        out_shape=[
            jax.ShapeDtypeStruct((t, D_MODEL), f32),
            jax.ShapeDtypeStruct((t, D_PACK), u32),
        ] + v_shape + [
            jax.ShapeDtypeStruct((TOP_K, t), i32),
            jax.ShapeDtypeStruct((TOP_K, t), f32),
            jax.ShapeDtypeStruct((TOP_K, t), i32),
            jax.ShapeDtypeStruct((N_EXPERTS, 128), i32),
        ],
        scratch_shapes=[pltpu.VMEM((N_EXPERTS, 128), f32)],
        compiler_params=_params("arbitrary"),
        name="mix_route",
    )(uv, gates, yssd, x2d, *[mc[k] for k in _MIX_CONST_NAMES])


def _dispatch(slot_groups, hf_groups, n_slots):
    info = plsc.get_sparse_core_info()
    workers = info.num_cores * info.num_subcores
    chunks = []
    slots = []
    for g, (slot_kt, hf) in enumerate(zip(slot_groups, hf_groups)):
        per_w, rem = divmod(hf.shape[0], workers)
        n, rem2 = divmod(per_w, SC_CHUNK)
        assert rem == 0 and rem2 == 0
        chunks += [(g, per_w, j) for j in range(n)]
        slots.append(slot_kt.reshape(TOP_K, workers, n, SC_CHUNK).transpose(1, 0, 2, 3))
    slots = jnp.concatenate(slots, axis=2)
    nch = len(chunks)
    mesh = plsc.VectorSubcoreMesh(core_axis_name="c", subcore_axis_name="s")

    @functools.partial(
        pl.kernel, mesh=mesh,
        out_type=jax.ShapeDtypeStruct((n_slots, D_PACK), u32),
        scratch_types=[pltpu.VMEM((TOP_K, nch, SC_CHUNK), i32), pltpu.VMEM((2, SC_CHUNK, D_PACK), u32)]
        + [pltpu.SemaphoreType.DMA] * 4,
    )
    def scatter_rows(*refs):
        hf_hbm, (slot_hbm, out_hbm, idx_v, rows_v, sin0, sin1, sout0, sout1) = refs[:len(hf_groups)], refs[len(hf_groups):]
        sin, sout = (sin0, sin1), (sout0, sout1)
        wid = lax.axis_index("s") * info.num_cores + lax.axis_index("c")
        pltpu.sync_copy(slot_hbm.at[wid], idx_v)

        def load(j):
            g, per_w, jg = chunks[j]
            rows = pl.ds(pl.multiple_of(wid * per_w + jg * SC_CHUNK, SUBLANES), SC_CHUNK)
            return pltpu.make_async_copy(hf_hbm[g].at[rows], rows_v.at[j % 2], sin[j % 2])

        def scatters(j):
            return [pltpu.make_async_copy(rows_v.at[j % 2], out_hbm.at[idx_v.at[k, j]], sout[j % 2]) for k in range(TOP_K)]

        load(0).start()
        for j in range(nch):
            load(j).wait()
            if j >= 1:
                for cp in scatters(j - 1):
                    cp.wait()
            if j + 1 < nch:
                load(j + 1).start()
            for cp in scatters(j):
                cp.start()
        for cp in scatters(nch - 1):
            cp.wait()

    return scatter_rows(*hf_groups, slots)


def _experts_body(be_ref, nused_ref, start_ref, slot_ref, next_ref, nvalid_ref, xs_ref, bgu_ref, bd_ref, perm_ref, wgu_hbm, wd_hbm,
                  ys_ref, wgu_f32, wd_f32, wgu_bf, wd_bf, sems):
    i = pl.program_id(0)

    def weight_copies(e, slot):
        return (pltpu.make_async_copy(wgu_hbm.at[e], wgu_f32.at[slot], sems.at[0, slot]),
                pltpu.make_async_copy(wd_hbm.at[e], wd_f32.at[slot], sems.at[1, slot]))

    @pl.when(i < nused_ref[0])
    def _():
        @pl.when(start_ref[i] == 1)
        def _():
            slot = slot_ref[i]

            @pl.when(i == 0)
            def _():
                for cp in weight_copies(be_ref[0], 0):
                    cp.start()

            for cp in weight_copies(be_ref[i], slot):
                cp.wait()

            @pl.when(next_ref[i] >= 0)
            def _():
                for cp in weight_copies(next_ref[i], 1 - slot):
                    cp.start()

            wd_bf[...] = wd_f32[slot].astype(bf16)
            half = DEINT // 2
            for j in range(2 * D_FF // DEINT):
                w = jnp.dot(wgu_f32[slot, :, j * DEINT:(j + 1) * DEINT].astype(bf16), perm_ref[...],
                            preferred_element_type=f32).astype(bf16)
                wgu_bf[:, j * half:(j + 1) * half] = w[:, :half]
                wgu_bf[:, D_FF + j * half:D_FF + (j + 1) * half] = w[:, half:]

        row = lax.broadcasted_iota(i32, (MOE_BLOCK, 1), 0)
        xs = jnp.where(row < nvalid_ref[i], xs_ref[...], jnp.zeros((), u32))
        gu = jnp.dot(_unpack_bf16_pair(xs), wgu_bf[...], preferred_element_type=f32) + bgu_ref[0]
        glu = jnp.minimum(gu[:, :D_FF], SWIGLU_LIMIT)
        lin = jnp.clip(gu[:, D_FF:], -SWIGLU_LIMIT, SWIGLU_LIMIT)
        act = glu * jax.nn.sigmoid(SWIGLU_ALPHA * glu) * (lin + 1.0)
        ys_ref[...] = jnp.dot(act.astype(bf16), wd_bf[...], preferred_element_type=f32) + bd_ref[0]

    @pl.when(i >= nused_ref[0])
    def _():
        ys_ref[...] = jnp.zeros_like(ys_ref)


def _expert_segments(block_e, n_used, n_blocks):
    idx = jnp.arange(n_blocks, dtype=i32)
    prev_e = jnp.concatenate([block_e[:1], block_e[:-1]])
    start = (idx < n_used) & ((idx == 0) | (block_e != prev_e))
    slot = (jnp.cumsum(start.astype(i32)) - 1) % 2
    start_idx = jnp.where(start, idx, n_blocks)
    from_here = lax.cummin(start_idx, reverse=True)
    next_start = jnp.concatenate([from_here[1:], jnp.full((1,), n_blocks, i32)])
    next_e = jnp.sum(jnp.where(next_start[:, None] == idx[None, :], block_e[None, :], 0), axis=1)
    next_e = jnp.where(next_start < n_blocks, next_e, -1)
    return start.astype(i32), slot.astype(i32), next_e.astype(i32)


def _experts(block_e, n_used, n_valid, xs, wgu, bgu, w_down, b_down):
    n_blocks = xs.shape[0] // MOE_BLOCK
    start, slot, next_e = _expert_segments(block_e, n_used[0], n_blocks)
    blk = lambda i, be, nu, *_: (jnp.maximum(jnp.minimum(i, nu[0] - 1), 0), 0)
    exp3 = lambda i, be, *_: (be[i], 0, 0)
    c = np.arange(DEINT)
    src = np.where(c < DEINT // 2, 2 * c, 2 * (c - DEINT // 2) + 1)
    perm = jnp.asarray(np.arange(DEINT)[:, None] == src[None, :], bf16)
    return pl.pallas_call(
        _experts_body,
        grid_spec=pltpu.PrefetchScalarGridSpec(
            num_scalar_prefetch=6,
            grid=(n_blocks,),
            in_specs=[
                pl.BlockSpec((MOE_BLOCK, D_PACK), blk),
                pl.BlockSpec((1, 1, 2 * D_FF), exp3),
                pl.BlockSpec((1, 1, D_MODEL), exp3),
                pl.BlockSpec((DEINT, DEINT), lambda i, *_: (0, 0)),
                pl.BlockSpec(memory_space=pl.ANY),
                pl.BlockSpec(memory_space=pl.ANY),
            ],
            out_specs=pl.BlockSpec((MOE_BLOCK, D_MODEL), lambda i, *_: (i, 0)),
            scratch_shapes=[
                pltpu.VMEM((2, D_MODEL, 2 * D_FF), f32),
                pltpu.VMEM((2, D_FF, D_MODEL), f32),
                pltpu.VMEM((D_MODEL, 2 * D_FF), bf16),
                pltpu.VMEM((D_FF, D_MODEL), bf16),
                pltpu.SemaphoreType.DMA((2, 2)),
            ],
        ),
        out_shape=jax.ShapeDtypeStruct((xs.shape[0], D_MODEL), f32),
        compiler_params=_params("arbitrary"),
        name="moe_experts",
    )(block_e, n_used, start, slot, next_e, n_valid, xs, bgu, b_down, perm, wgu, w_down)


def _combine_body(tm, t, slot_ref, h_ref, gate_ref, nf_ref, ys_ref, o_ref, rows_ref, sems):
    s = pl.program_id(0)
    n_tiles = pl.num_programs(0) - 1

    def request():
        buf = s % 2
        for r in range(tm):
            for k in range(TOP_K):
                slot = slot_ref[k * t + s * tm + r]
                pltpu.make_async_copy(ys_ref.at[slot], rows_ref.at[buf, k, r], sems.at[buf]).start(priority=k % 2)

    def finish():
        buf = (s - 1) % 2
        for r in range(tm):
            for k in range(TOP_K):
                pltpu.make_async_copy(ys_ref.at[0], rows_ref.at[buf, k, r], sems.at[buf]).wait()
        g = gate_ref[...]
        moe = g[:, 0:1] * rows_ref[buf, 0]
        for k in range(1, TOP_K):
            moe = moe + g[:, k:k + 1] * rows_ref[buf, k]
        h = h_ref[...] + moe
        o_ref[...] = h * lax.rsqrt(jnp.mean(h * h, axis=-1, keepdims=True) + EPS) * nf_ref[...]

    @pl.when(s < n_tiles)
    def _():
        request()

    @pl.when(s > 0)
    def _():
        finish()


def _combine(slot_flat, h, gate_t, norm_final, ys, tm):
    t = h.shape[0]
    return pl.pallas_call(
        functools.partial(_combine_body, tm, t),
        grid_spec=pltpu.PrefetchScalarGridSpec(
            num_scalar_prefetch=1,
            grid=(t // tm + 1,),
            in_specs=[
                pl.BlockSpec((tm, D_MODEL), lambda i, s: (jnp.maximum(i - 1, 0), 0)),
                pl.BlockSpec((tm, TOP_K), lambda i, s: (jnp.maximum(i - 1, 0), 0)),
                pl.BlockSpec((1, D_MODEL), lambda i, s: (0, 0)),
                pl.BlockSpec(memory_space=pl.ANY),
            ],
            out_specs=pl.BlockSpec((tm, D_MODEL), lambda i, s: (jnp.maximum(i - 1, 0), 0)),
            scratch_shapes=[pltpu.VMEM((2, TOP_K, tm, D_MODEL), f32), pltpu.SemaphoreType.DMA((2,))],
        ),
        out_shape=jax.ShapeDtypeStruct((t, D_MODEL), f32),
        compiler_params=_params("arbitrary"),
        name="moe_combine",
    )(slot_flat, h, gate_t, norm_final, ys)


def kernel(x_prompt, x_sample, state_conv, state_ssm, norm_mix, w_in, conv_w, conv_b, dt_bias, a_log, d_skip, gnorm_w, v_norm_w, v_norm_b, w_spatial, b_spatial, w_branch_ssd, w_branch_mlp, w_out, norm_ffn, w_router, b_router, w_gu, b_gu, w_down, b_down, norm_final):
    assert w_in.shape[0] == 1, "single-layer trunk"
    batch, seq, _ = x_prompt.shape
    nseq, dec_seq, _ = x_sample.shape
    assert seq % CHUNK == 0 and CHUNK % dec_seq == 0 and nseq % (CHUNK // dec_seq) == 0 and dec_seq >= CONV_W - 1
    t_p, t_s = batch * seq, nseq * dec_seq
    tm = ROW_DMA_TILE
    assert t_p % tm == 0 and t_s % tm == 0
    tm_mix = MIX_TILE if t_p % MIX_TILE == 0 and t_s % MIX_TILE == 0 else tm

    wi = w_in[0]
    z0, x0, d0, u0 = D_INNER, D_INNER + CONV_DIM, D_INNER + CONV_DIM + HEADS, D_INNER + CONV_DIM + HEADS + 2 * D_GMLP
    w_segments = tuple(wi[:, a:b].astype(bf16) for a, b in ((0, z0), (d0, u0), (u0, D_IN_PROJ), (z0, x0)))
    w_dt = wi[:, x0:d0].astype(bf16)
    w_dtT = w_dt.T
    nm = norm_mix[0].reshape(1, D_MODEL)
    sc = lambda L: _ssd_consts(conv_w[0], conv_b[0], dt_bias[0], a_log[0], d_skip[0], gnorm_w[0], L)
    xp2, xs2 = x_prompt.reshape(t_p, D_MODEL), x_sample.reshape(t_s, D_MODEL)

    tril = np.tril(np.ones((CHUNK, CHUNK), bool))
    ws_p = jnp.where(tril[None], w_spatial[0], 0).astype(bf16)
    per = CHUNK // dec_seq
    blockdiag = (np.arange(CHUNK)[:, None] // dec_seq) == (np.arange(CHUNK)[None, :] // dec_seq)
    ws_s = jnp.where((tril & blockdiag)[None], jnp.tile(w_spatial[0][:, :dec_seq, :dec_seq], (1, per, per)), 0).astype(bf16)
    bs_p = b_spatial[0].T
    bs_s = jnp.tile(b_spatial[0][:, :dec_seq], (1, per)).T
    mc = dict(
        vnw=v_norm_w[0].reshape(1, D_GMLP), vnb=v_norm_b[0].reshape(1, D_GMLP),
        wbs=w_branch_ssd[0].astype(bf16), wbm=w_branch_mlp[0].astype(bf16), wo=w_out[0].astype(bf16),
        nffn=norm_ffn[0].reshape(1, D_MODEL), wrT=w_router[0].T, br=b_router[0].reshape(N_EXPERTS, 1),
        tris=jnp.asarray(np.triu(np.ones((tm_mix, tm_mix), np.float32), 1), bf16),
    )

    z_p, uv_p, g_p, xbc_p, dt_p, dtT_p = _in_proj(xp2, nm, w_segments, w_dt, w_dtT)
    yssd_p, ssm_p = _ssd_prompt(z_p, xbc_p, dt_p, dtT_p, sc(CHUNK), batch, seq)
    h_p, hf_p, te_p, gate_p, rank_p, cnt_p = _mix(uv_p, g_p, yssd_p, xp2, dict(mc, ws=ws_p, bs=bs_p), tm_mix, False)

    z_s, uv_s, g_s, xbc_s, dt_s, dtT_s = _in_proj(xs2, nm, w_segments, w_dt, w_dtT)
    pre = jnp.pad(state_conv[0], ((0, 0), (dec_seq - (CONV_W - 1), 0), (0, 0))).reshape(t_s, CONV_DIM)
    yssd_s, ssm_s = _ssd_sample(z_s, xbc_s, pre, dt_s, dtT_s, state_ssm[0].reshape(nseq, D_INNER, D_STATE),
                                sc(dec_seq), nseq, dec_seq)
    h_s, hf_s, v_s, te_s, gate_s, rank_s, cnt_s = _mix(uv_s, g_s, yssd_s, xs2, dict(mc, ws=ws_s, bs=bs_s), tm_mix, True)

    cp, cs = cnt_p[:, 0], cnt_s[:, 0]
    padded = (cp + cs + MOE_BLOCK - 1) // MOE_BLOCK * MOE_BLOCK
    pend = jnp.cumsum(padded)
    pstart = pend - padded

    def lookup(table, te):
        eid = jnp.arange(N_EXPERTS, dtype=i32)[:, None, None]
        return jnp.sum(jnp.where(te[None] == eid, table[:, None, None], 0), axis=0)

    slot_kt_p = (lookup(pstart, te_p) + rank_p).astype(i32)
    slot_kt_s = (lookup(pstart + cp, te_s) + rank_s).astype(i32)
    slot_p, slot_s = slot_kt_p.reshape(-1), slot_kt_s.reshape(-1)
    n_blocks = (t_p + t_s) * TOP_K // MOE_BLOCK + N_EXPERTS
    n_used = (pend[-1] // MOE_BLOCK).astype(i32).reshape(1)
    first_row = jnp.arange(n_blocks, dtype=i32) * MOE_BLOCK
    block_e = jnp.minimum(jnp.sum(pend[None, :] <= first_row[:, None], axis=1), N_EXPERTS - 1).astype(i32)
    last_row = jnp.sum(jnp.where(block_e[:, None] == jnp.arange(N_EXPERTS)[None, :], (pstart + cp + cs)[None, :], 0), axis=1)
    n_valid = jnp.clip(last_row - first_row, 0, MOE_BLOCK).astype(i32)

    xs_sorted = _dispatch((slot_kt_p, slot_kt_s), (hf_p, hf_s), n_blocks * MOE_BLOCK)
    bgu = jnp.concatenate([b_gu[0][..., 0::2], b_gu[0][..., 1::2]], axis=-1).reshape(N_EXPERTS, 1, 2 * D_FF)
    ys = _experts(block_e, n_used, n_valid, xs_sorted, w_gu[0], bgu, w_down[0], b_down[0].reshape(N_EXPERTS, 1, D_MODEL))

    nf = norm_final.reshape(1, D_MODEL)
    y_p = _combine(slot_p, h_p, gate_p.T, nf, ys, tm)
    y_s = _combine(slot_s, h_s, gate_s.T, nf, ys, tm)

    conv_p = xbc_p.reshape(batch, seq, CONV_DIM)[:, seq - (CONV_W - 1):]
    conv_s = xbc_s.reshape(nseq, dec_seq, CONV_DIM)[:, dec_seq - (CONV_W - 1):]
    st_shape = (HEADS, HEAD_DIM, D_STATE)
    return (
        y_p.reshape(batch, seq, D_MODEL),
        y_s.reshape(nseq, dec_seq, D_MODEL),
        conv_p[None],
        ssm_p.reshape(1, batch, *st_shape),
        conv_s[None],
        ssm_s.reshape(1, nseq, *st_shape),
        v_s.reshape(1, nseq, dec_seq, D_GMLP),
    )
```

```python
import functools

import numpy as np
import jax
import jax.numpy as jnp
from jax import lax
from jax.experimental import pallas as pl
from jax.experimental.pallas import tpu as pltpu
from jax.experimental.pallas import tpu_sc as plsc

f32, bf16, i32, u32 = jnp.float32, jnp.bfloat16, jnp.int32, jnp.uint32

D_MODEL = 1024
D_INNER = 2048
HEAD_DIM = 64
HEADS = 32
GROUPS = 4
HEADS_PER_GROUP = 8
GROUP_WIDTH = HEADS_PER_GROUP * HEAD_DIM
D_STATE = 128
CONV_W = 4
CONV_DIM = D_INNER + 2 * GROUPS * D_STATE
CHUNK = 128
SUBLANES = 8
D_GMLP = 1024
GMLP_HEADS = 8
N_EXPERTS = 32
TOP_K = 4
D_FF = 1024
SWIGLU_ALPHA = 1.702
SWIGLU_LIMIT = 7.0
EPS = 1e-5
LOG2_E = float(np.log2(np.e))
MOE_BLOCK = 512
DEINT = 256
ROW_DMA_TILE = 256
SC_CHUNK = 32
MIX_TILE = 512
D_IN_PROJ = 3 * D_INNER + CONV_DIM + HEADS
PROJ_TN = 512
PROMPT_SEQS_PER_STEP = 2
SAMPLE_SEQS_PER_STEP = 8
D_PACK = D_MODEL // 2
VMEM_LIMIT = 56 * 1024 * 1024

_NT = (((1,), (1,)), ((), ()))
_HI16 = np.uint32(0xFFFF0000)


def _params(*sem):
    return pltpu.CompilerParams(dimension_semantics=sem, vmem_limit_bytes=VMEM_LIMIT)


def _split(x, parts):
    out = []
    for _ in range(parts - 1):
        p = x.astype(bf16)
        out.append(p)
        x = x - p.astype(f32)
    out.append(x.astype(bf16))
    return out


def _dot_sel_rhs(x, sel, parts=3):
    return sum(jnp.dot(p, sel, preferred_element_type=f32) for p in _split(x, parts))


def _dot_sel_lhs(sel, x, parts=3):
    return sum(jnp.dot(sel, p, preferred_element_type=f32) for p in _split(x, parts))


def _silu(x):
    return x * jax.nn.sigmoid(x)


def _pack_bf16_pair(x):
    n = x.shape[1] // 2
    lo = lax.bitcast_convert_type(x[:, :n].astype(bf16).astype(f32), u32) >> 16
    hi = lax.bitcast_convert_type(x[:, n:].astype(bf16).astype(f32), u32) & _HI16
    return lo | hi


def _unpack_bf16_pair(p):
    lo = lax.bitcast_convert_type(p << 16, f32).astype(bf16)
    hi = lax.bitcast_convert_type(p & _HI16, f32).astype(bf16)
    return jnp.concatenate([lo, hi], axis=1)


def _in_proj_body(x_ref, nw_ref, wz_ref, wuv_ref, wg_ref, wxbc_ref, wdt_ref, wdtT_ref,
                  z_ref, uv_ref, g_ref, xbc_ref, dt_ref, dtT_ref):
    x = x_ref[...]
    ms = jnp.mean(x * x, axis=-1, keepdims=True)
    hn = (x * lax.rsqrt(ms + EPS) * nw_ref[...]).astype(bf16)
    dt_ref[...] = jnp.dot(hn, wdt_ref[...], preferred_element_type=f32)
    dtT_ref[...] = lax.dot_general(wdtT_ref[...], hn, _NT, preferred_element_type=f32)
    for o_ref, w_ref in ((z_ref, wz_ref), (uv_ref, wuv_ref), (g_ref, wg_ref), (xbc_ref, wxbc_ref)):
        for c in range(0, w_ref.shape[1], PROJ_TN):
            acc = jnp.dot(hn, w_ref[:, c:c + PROJ_TN], preferred_element_type=f32)
            o_ref[:, c:c + PROJ_TN] = acc.astype(o_ref.dtype)


def _in_proj(x2d, norm_w, w_segments, w_dt, w_dtT):
    t = x2d.shape[0]
    tm = min(512, t)
    row = lambda i: (i, 0)
    fixed = lambda i: (0, 0)
    return pl.pallas_call(
        _in_proj_body,
        grid=(t // tm,),
        in_specs=[
            pl.BlockSpec((tm, D_MODEL), row),
            pl.BlockSpec((1, D_MODEL), fixed),
        ] + [pl.BlockSpec(w.shape, fixed, pipeline_mode=pl.Buffered(1)) for w in w_segments] + [
            pl.BlockSpec((D_MODEL, HEADS), fixed),
            pl.BlockSpec((HEADS, D_MODEL), fixed),
        ],
        out_specs=[
            pl.BlockSpec((tm, D_INNER), row),
            pl.BlockSpec((tm, D_INNER), row),
            pl.BlockSpec((tm, D_INNER), row),
            pl.BlockSpec((tm, CONV_DIM), row),
            pl.BlockSpec((tm, HEADS), row),
            pl.BlockSpec((HEADS, tm), lambda i: (0, i)),
        ],
        out_shape=[
            jax.ShapeDtypeStruct((t, D_INNER), bf16),
            jax.ShapeDtypeStruct((t, D_INNER), bf16),
            jax.ShapeDtypeStruct((t, D_INNER), bf16),
            jax.ShapeDtypeStruct((t, CONV_DIM), f32),
            jax.ShapeDtypeStruct((t, HEADS), f32),
            jax.ShapeDtypeStruct((HEADS, t), f32),
        ],
        compiler_params=_params("arbitrary"),
        name="in_proj",
    )(x2d, norm_w, *w_segments, w_dt, w_dtT)


def _conv_taps(shifted, cw_ref, cb_ref):
    acc = cb_ref[...]
    for k in range(CONV_W):
        acc = acc + shifted(CONV_W - 1 - k) * cw_ref[k:k + 1, :]
    return _silu(acc)


def _conv_silu_one_seq(cur, prev8, cw_ref, cb_ref):
    assert CONV_W == 4
    row8 = lax.broadcasted_iota(i32, (SUBLANES, 1), 0)

    def shift_rows(x, x_prev8, j):
        xr = pltpu.roll(x, j, 0)
        top = jnp.where(row8 >= j, xr[:SUBLANES], pltpu.roll(x_prev8, j, 0))
        return jnp.concatenate([top, xr[SUBLANES:]], axis=0)

    w = [cw_ref[k:k + 1, :] for k in range(CONV_W)]
    x1 = shift_rows(cur, prev8, 1)
    near = x1 * w[2] + cur * w[3]
    far = x1 * w[0] + cur * w[1]
    far_prev8 = pltpu.roll(prev8, 1, 0) * w[0] + prev8 * w[1]
    return _silu(cb_ref[...] + shift_rows(far, far_prev8, 2) + near)


def _conv_silu_many_seq(cur, pre, cw_ref, cb_ref, seq_len):
    n = cur.shape[0]
    pos = lax.broadcasted_iota(i32, (n, 1), 0) % seq_len

    def shifted(j):
        if j == 0:
            return cur
        return jnp.where(pos >= j, pltpu.roll(cur, j, 0), pltpu.roll(pre, (j + n - seq_len) % n, 0))

    return _conv_taps(shifted, cw_ref, cb_ref)


def _ssd_block(xa, dt_raw, dtT_raw, c):
    tril = c["tril"][...]
    dt = jax.nn.softplus(dt_raw + c["dtb_row"][...])
    dtT = jax.nn.softplus(dtT_raw + c["dtb_col"][...])
    a = dt * c["aneg_row"][...]
    aT = dtT * c["aneg_col"][...]
    a_cum = _dot_sel_lhs(tril, a)
    a_cumT = _dot_sel_rhs(aT, c["triu"][...])
    a_tot = _dot_sel_lhs(c["same"][...], a)
    stack = jnp.concatenate([dt, jnp.exp(a_cum), jnp.exp(a_tot - a_cum)], axis=0)
    ex = jnp.dot(jnp.concatenate(_split(stack, 2), axis=1), c["expand"][...], preferred_element_type=f32)
    n = xa.shape[0]
    dtx, eax, dex = ex[:n], ex[n:2 * n], ex[2 * n:]

    xs = xa[:, :D_INNER]
    bm = xa[:, D_INNER:D_INNER + GROUPS * D_STATE]
    cm = xa[:, D_INNER + GROUPS * D_STATE:]
    xdt = xs * dtx
    xdt_bf = xdt.astype(bf16)
    mask = tril > 0
    a_cum2, a_cumT2 = a_cum * LOG2_E, a_cumT * LOG2_E
    lane = lax.broadcasted_iota(i32, (1, 2 * HEAD_DIM), 1)
    ys = []
    for g in range(GROUPS):
        cg = cm[:, g * D_STATE:(g + 1) * D_STATE].astype(bf16)
        bg = bm[:, g * D_STATE:(g + 1) * D_STATE].astype(bf16)
        cb = lax.dot_general(cg, bg, _NT, preferred_element_type=f32)
        for pair in range(HEADS_PER_GROUP // 2):
            halves = []
            for h in (g * HEADS_PER_GROUP + 2 * pair, g * HEADS_PER_GROUP + 2 * pair + 1):
                seg = a_cum2[:, h:h + 1] - a_cumT2[h:h + 1, :]
                decay = jnp.exp2(jnp.where(mask, seg, -jnp.inf))
                m = (cb * decay).astype(bf16)
                col = (h // 2) * 2 * HEAD_DIM
                halves.append(jnp.dot(m, xdt_bf[:, col:col + 2 * HEAD_DIM], preferred_element_type=f32))
            ys.append(jnp.where(lane < HEAD_DIM, halves[0], halves[1]))
    y_diag = jnp.concatenate(ys, axis=1)
    return dict(xs=xs, bm=bm, cm=cm, xdt=xdt, xd=xdt * dex, eax=eax, aT=aT, y_diag=y_diag)


def _gated_norm(y, z, gw):
    g = y * _silu(z.astype(f32))
    ms = jnp.mean(g * g, axis=-1, keepdims=True)
    return (g * lax.rsqrt(ms + EPS) * gw).astype(bf16)


_SSD_CONST_NAMES = ("cw", "cb", "dtb_row", "dtb_col", "aneg_row", "aneg_col", "dskipx", "gw",
                    "tril", "triu", "same", "expand")


def _ssd_consts(conv_w, conv_b, dt_bias, a_log, d_skip, gnorm_w, seq_len):
    r = np.arange(CHUNK)
    same = (r[:, None] // seq_len) == (r[None, :] // seq_len)
    tril = same & (r[None, :] <= r[:, None])
    expand = np.tile(np.repeat(np.eye(HEADS, dtype=np.float32), HEAD_DIM, axis=1), (2, 1))
    aneg = -jnp.exp(a_log.astype(f32))
    return dict(
        cw=conv_w, cb=conv_b.reshape(1, CONV_DIM),
        dtb_row=dt_bias.reshape(1, HEADS), dtb_col=dt_bias.reshape(HEADS, 1),
        aneg_row=aneg.reshape(1, HEADS), aneg_col=aneg.reshape(HEADS, 1),
        dskipx=jnp.repeat(d_skip.astype(f32), HEAD_DIM).reshape(1, D_INNER),
        gw=gnorm_w.reshape(1, D_INNER),
        tril=jnp.asarray(tril, bf16), triu=jnp.asarray(tril.T, bf16), same=jnp.asarray(same, bf16),
        expand=jnp.asarray(expand, bf16),
    )


def _const_specs(consts):
    zero = (lambda *_: (0, 0))
    return [pl.BlockSpec(consts[k].shape, zero) for k in _SSD_CONST_NAMES]


def _ssd_prompt_body(nb, z_ref, xbc_ref, dt_ref, dtT_ref, *rest):
    nc = len(_SSD_CONST_NAMES)
    c = dict(zip(_SSD_CONST_NAMES, rest[:nc]))
    y_ref, state_ref, prev_ref, st_ref = rest[nc:]
    ci = pl.program_id(1)

    @pl.when(ci == 0)
    def _():
        prev_ref[...] = jnp.zeros_like(prev_ref)
        st_ref[...] = jnp.zeros_like(st_ref)

    for s in range(nb):
        cur = xbc_ref[s]
        xa = _conv_silu_one_seq(cur, prev_ref[s], c["cw"], c["cb"])
        prev_ref[s] = cur[CHUNK - SUBLANES:]
        b = _ssd_block(xa, dt_ref[s], dtT_ref[s], c)

        y_off = []
        for g in range(GROUPS):
            sl = slice(g * GROUP_WIDTH, (g + 1) * GROUP_WIDTH)
            st = st_ref[s * GROUPS + g]
            cg = b["cm"][:, g * D_STATE:(g + 1) * D_STATE].astype(bf16)
            y_off.append(jnp.dot(cg, st.astype(bf16), preferred_element_type=f32))
            bgT = b["bm"][:, g * D_STATE:(g + 1) * D_STATE].T.astype(bf16)
            upd = jnp.dot(bgT, b["xd"][:, sl].astype(bf16), preferred_element_type=f32)
            st_ref[s * GROUPS + g] = st * b["eax"][CHUNK - 1:CHUNK, sl] + upd
        y = b["y_diag"] + jnp.concatenate(y_off, axis=1) * b["eax"] + c["dskipx"][...] * b["xs"]
        y_ref[s] = _gated_norm(y, z_ref[s], c["gw"][...])

    @pl.when(ci == pl.num_programs(1) - 1)
    def _():
        for s in range(nb):
            for g in range(GROUPS):
                state_ref[s, 0, g * GROUP_WIDTH:(g + 1) * GROUP_WIDTH, :] = st_ref[s * GROUPS + g].T


def _ssd_prompt(z, xbc, dt, dtT, consts, batch, seq):
    nb = PROMPT_SEQS_PER_STEP if batch % PROMPT_SEQS_PER_STEP == 0 else 1
    nchunk = seq // CHUNK
    part = batch // nb * seq
    split = lambda a: a.reshape(nb, part, a.shape[-1])
    dtT3 = dtT.reshape(HEADS, nb, part).transpose(1, 0, 2)
    row = lambda b, ci: (0, b * nchunk + ci, 0)
    y, state = pl.pallas_call(
        functools.partial(_ssd_prompt_body, nb),
        grid=(batch // nb, nchunk),
        in_specs=[
            pl.BlockSpec((nb, CHUNK, D_INNER), row),
            pl.BlockSpec((nb, CHUNK, CONV_DIM), row),
            pl.BlockSpec((nb, CHUNK, HEADS), row),
            pl.BlockSpec((nb, HEADS, CHUNK), lambda b, ci: (0, 0, b * nchunk + ci)),
        ] + _const_specs(consts),
        out_specs=[
            pl.BlockSpec((nb, CHUNK, D_INNER), row),
            pl.BlockSpec((nb, 1, D_INNER, D_STATE), lambda b, ci: (0, b, 0, 0)),
        ],
        out_shape=[
            jax.ShapeDtypeStruct((nb, part, D_INNER), bf16),
            jax.ShapeDtypeStruct((nb, batch // nb, D_INNER, D_STATE), f32),
        ],
        scratch_shapes=[pltpu.VMEM((nb, SUBLANES, CONV_DIM), f32),
                        pltpu.VMEM((nb * GROUPS, D_STATE, GROUP_WIDTH), f32)],
        compiler_params=_params("arbitrary", "arbitrary"),
        name="ssd_prompt",
    )(split(z), split(xbc), split(dt), dtT3, *[consts[k] for k in _SSD_CONST_NAMES])
    return y.reshape(batch * seq, D_INNER), state.reshape(batch, D_INNER, D_STATE)


def _ssd_sample_body(seq_len, z_ref, xbc_ref, pre_ref, dt_ref, dtT_ref, state_ref, *rest):
    nc = len(_SSD_CONST_NAMES)
    c = dict(zip(_SSD_CONST_NAMES, rest[:nc]))
    y_ref, state_out_ref, c_ref, b_ref, xdT_ref, yacc_ref, eax_ref, eatT_ref = rest[nc:]
    step = pl.program_id(1)
    per_step = state_ref.shape[0]

    @pl.when(step == 0)
    def _():
        xa = _conv_silu_many_seq(xbc_ref[...], pre_ref[...], c["cw"], c["cb"], seq_len)
        b = _ssd_block(xa, dt_ref[...], dtT_ref[...], c)
        c_ref[...] = b["cm"]
        b_ref[...] = b["bm"]
        xdT_ref[...] = b["xd"].T.astype(bf16)
        yacc_ref[...] = b["y_diag"] + c["dskipx"][...] * b["xs"]
        eax_ref[...] = b["eax"]
        eatT_ref[...] = jnp.exp(_dot_sel_rhs(b["aT"], c["same"][...]))

    def one_sequence(q, carry):
        s = step * per_step + q
        r0 = pl.multiple_of(s * seq_len, seq_len)
        rows = pl.ds(r0, seq_len)
        lane = lax.broadcasted_iota(i32, (1, CHUNK), 1)
        arep = jnp.broadcast_to(jnp.sum(jnp.where(lane == r0, eatT_ref[...], 0.0), axis=1, keepdims=True), (HEADS, D_STATE))
        rmask = (lax.broadcasted_iota(i32, (CHUNK, 1), 0) // seq_len) == s
        for g in range(GROUPS):
            gs = slice(g * GROUP_WIDTH, (g + 1) * GROUP_WIDTH)
            ns = slice(g * D_STATE, (g + 1) * D_STATE)
            s0 = state_ref[q, gs, :]
            cg = c_ref[rows, ns].astype(bf16)
            yo = lax.dot_general(cg, s0.astype(bf16), _NT, preferred_element_type=f32)
            yacc_ref[rows, gs] = yacc_ref[rows, gs] + yo * eax_ref[rows, gs]
            bmask = jnp.where(rmask, b_ref[:, ns], 0.0).astype(bf16)
            upd = jnp.dot(xdT_ref[gs, :], bmask, preferred_element_type=f32)
            for r in range(HEADS_PER_GROUP):
                h = g * HEADS_PER_GROUP + r
                hs = slice(r * HEAD_DIM, (r + 1) * HEAD_DIM)
                state_out_ref[q, h * HEAD_DIM:(h + 1) * HEAD_DIM, :] = s0[hs] * arep[h:h + 1, :] + upd[hs]
        return carry

    lax.fori_loop(0, per_step, one_sequence, 0)

    @pl.when(step == pl.num_programs(1) - 1)
    def _():
        y_ref[...] = _gated_norm(yacc_ref[...], z_ref[...], c["gw"][...])


def _ssd_sample(z, xbc, pre, dt, dtT, state, consts, nseq, seq_len):
    per = CHUNK // seq_len
    nblk = nseq // per
    t = nseq * seq_len
    q = SAMPLE_SEQS_PER_STEP if per % SAMPLE_SEQS_PER_STEP == 0 else 1
    blk = lambda i, s: (i, 0)
    seq3 = lambda i, s: (i * (per // q) + s, 0, 0)
    return pl.pallas_call(
        functools.partial(_ssd_sample_body, seq_len),
        grid=(nblk, per // q),
        in_specs=[
            pl.BlockSpec((CHUNK, D_INNER), blk),
            pl.BlockSpec((CHUNK, CONV_DIM), blk),
            pl.BlockSpec((CHUNK, CONV_DIM), blk),
            pl.BlockSpec((CHUNK, HEADS), blk),
            pl.BlockSpec((HEADS, CHUNK), lambda i, s: (0, i)),
            pl.BlockSpec((q, D_INNER, D_STATE), seq3),
        ] + _const_specs(consts),
        out_specs=[
            pl.BlockSpec((CHUNK, D_INNER), blk),
            pl.BlockSpec((q, D_INNER, D_STATE), seq3),
        ],
        out_shape=[
            jax.ShapeDtypeStruct((t, D_INNER), bf16),
            jax.ShapeDtypeStruct((nseq, D_INNER, D_STATE), f32),
        ],
        scratch_shapes=[
            pltpu.VMEM((CHUNK, GROUPS * D_STATE), f32),
            pltpu.VMEM((CHUNK, GROUPS * D_STATE), f32),
            pltpu.VMEM((D_INNER, CHUNK), bf16),
            pltpu.VMEM((CHUNK, D_INNER), f32),
            pltpu.VMEM((CHUNK, D_INNER), f32),
            pltpu.VMEM((HEADS, CHUNK), f32),
        ],
        compiler_params=_params("arbitrary", "arbitrary"),
        name="ssd_sample",
    )(z, xbc, pre, dt, dtT, state, *[consts[k] for k in _SSD_CONST_NAMES])


_MIX_CONST_NAMES = ("ws", "bs", "vnw", "vnb", "wbs", "wbm", "wo", "nffn", "wrT", "br", "tris")


def _mix_body(emit_v, uv_ref, gates_ref, yssd_ref, x_ref, *rest):
    nc = len(_MIX_CONST_NAMES)
    c = dict(zip(_MIX_CONST_NAMES, rest[:nc]))
    outs = list(rest[nc:])
    h_ref, hfp_ref = outs[:2]
    v_ref = outs[2] if emit_v else None
    te_ref, gate_ref, rank_ref, cnt_ref, cnt_acc = outs[-5:]
    tm = uv_ref.shape[0]

    @pl.when(pl.program_id(0) == 0)
    def _():
        cnt_acc[...] = jnp.zeros_like(cnt_acc)

    uv = uv_ref[...].astype(f32)
    uv = 0.5 * uv * (1.0 + lax.erf(uv * np.float32(np.sqrt(0.5))))
    u, v = uv[:, :D_GMLP], uv[:, D_GMLP:]
    mu = jnp.mean(v, axis=-1, keepdims=True)
    vc = v - mu
    vn = vc * lax.rsqrt(jnp.mean(vc * vc, axis=-1, keepdims=True) + EPS) * c["vnw"][...] + c["vnb"][...]
    if emit_v:
        v_ref[...] = vn
    vn_bf = vn.astype(bf16)
    gd = D_GMLP // GMLP_HEADS
    rows = []
    for ck in range(tm // CHUNK):
        rs = slice(ck * CHUNK, (ck + 1) * CHUNK)
        heads = []
        for g in range(GMLP_HEADS):
            mixed = jnp.dot(c["ws"][g], vn_bf[rs, g * gd:(g + 1) * gd], preferred_element_type=f32)
            heads.append(mixed + c["bs"][:, g:g + 1])
        rows.append(jnp.concatenate(heads, axis=1))
    y_mlp = u * jnp.concatenate(rows, axis=0)

    a = jnp.dot(yssd_ref[...], c["wbs"][...], preferred_element_type=f32)
    b = jnp.dot(y_mlp.astype(bf16), c["wbm"][...], preferred_element_type=f32)
    gs = jax.nn.sigmoid(gates_ref[...].astype(f32))
    merged = gs[:, :D_MODEL] * a + gs[:, D_MODEL:] * b
    h = x_ref[...] + jnp.dot(merged.astype(bf16), c["wo"][...], preferred_element_type=f32)
    h_ref[...] = h
    hf = h * lax.rsqrt(jnp.mean(h * h, axis=-1, keepdims=True) + EPS) * c["nffn"][...]
    hfp_ref[...] = _pack_bf16_pair(hf)

    lg = lax.dot_general(c["wrT"][...], hf, _NT, precision=lax.Precision.HIGHEST, preferred_element_type=f32) + c["br"][...]
    sub = lax.broadcasted_iota(i32, lg.shape, 0)
    idxs, vals = [], []
    for _ in range(TOP_K):
        m = jnp.max(lg, axis=0, keepdims=True)
        idx = jnp.min(jnp.where(lg == m, sub, N_EXPERTS), axis=0, keepdims=True)
        idxs.append(idx)
        vals.append(m)
        lg = jnp.where(sub == idx, -jnp.inf, lg)
    p = jnp.exp(jnp.concatenate(vals, axis=0) - vals[0])
    gate_ref[...] = p / jnp.sum(p, axis=0, keepdims=True)
    te_ref[...] = jnp.concatenate(idxs, axis=0)

    onehots = [sub == idx for idx in idxs]
    member = functools.reduce(jnp.logical_or, onehots).astype(f32)
    before = jnp.dot(member.astype(bf16), c["tris"][...], preferred_element_type=f32) + cnt_acc[:, 0:1]
    rank_ref[...] = jnp.concatenate(
        [jnp.sum(jnp.where(oh, before, 0.0), axis=0, keepdims=True) for oh in onehots], axis=0).astype(i32)
    cnt_acc[...] = cnt_acc[...] + jnp.sum(member, axis=1, keepdims=True)
    cnt_ref[...] = cnt_acc[...].astype(i32)


def _mix(uv, gates, yssd, x2d, mc, tm, emit_v):
    t = x2d.shape[0]
    row = lambda i: (i, 0)
    col = lambda i: (0, i)
    const = lambda a: pl.BlockSpec(a.shape, lambda i: (0,) * a.ndim, pipeline_mode=pl.Buffered(1))
    v_spec = [pl.BlockSpec((tm, D_GMLP), row)] if emit_v else []
    v_shape = [jax.ShapeDtypeStruct((t, D_GMLP), f32)] if emit_v else []
    return pl.pallas_call(
        functools.partial(_mix_body, emit_v),
        grid=(t // tm,),
        in_specs=[
            pl.BlockSpec((tm, D_INNER), row),
            pl.BlockSpec((tm, D_INNER), row),
            pl.BlockSpec((tm, D_INNER), row),
            pl.BlockSpec((tm, D_MODEL), row),
        ] + [const(mc[k]) for k in _MIX_CONST_NAMES],
        out_specs=[pl.BlockSpec((tm, D_MODEL), row), pl.BlockSpec((tm, D_PACK), row)] + v_spec + [
            pl.BlockSpec((TOP_K, tm), col),
            pl.BlockSpec((TOP_K, tm), col),
            pl.BlockSpec((TOP_K, tm), col),
            pl.BlockSpec((N_EXPERTS, 128), lambda i: (0, 0)),
        ],
        out_shape=[
            jax.ShapeDtypeStruct((t, D_MODEL), f32),
            jax.ShapeDtypeStruct((t, D_PACK), u32),
        ] + v_shape + [
            jax.ShapeDtypeStruct((TOP_K, t), i32),
            jax.ShapeDtypeStruct((TOP_K, t), f32),
            jax.ShapeDtypeStruct((TOP_K, t), i32),
            jax.ShapeDtypeStruct((N_EXPERTS, 128), i32),
        ],
        scratch_shapes=[pltpu.VMEM((N_EXPERTS, 128), f32)],
        compiler_params=_params("arbitrary"),
        name="mix_route",
    )(uv, gates, yssd, x2d, *[mc[k] for k in _MIX_CONST_NAMES])


def _dispatch(slot_groups, hf_groups, n_slots):
    info = plsc.get_sparse_core_info()
    workers = info.num_cores * info.num_subcores
    chunks = []
    slots = []
    for g, (slot_kt, hf) in enumerate(zip(slot_groups, hf_groups)):
        per_w, rem = divmod(hf.shape[0], workers)
        n, rem2 = divmod(per_w, SC_CHUNK)
        assert rem == 0 and rem2 == 0
        chunks += [(g, per_w, j) for j in range(n)]
        slots.append(slot_kt.reshape(TOP_K, workers, n, SC_CHUNK).transpose(1, 0, 2, 3))
    slots = jnp.concatenate(slots, axis=2)
    nch = len(chunks)
    mesh = plsc.VectorSubcoreMesh(core_axis_name="c", subcore_axis_name="s")

    @functools.partial(
        pl.kernel, mesh=mesh,
        out_type=jax.ShapeDtypeStruct((n_slots, D_PACK), u32),
        scratch_types=[pltpu.VMEM((TOP_K, nch, SC_CHUNK), i32), pltpu.VMEM((2, SC_CHUNK, D_PACK), u32)]
        + [pltpu.SemaphoreType.DMA] * 4,
    )
    def scatter_rows(*refs):
        hf_hbm, (slot_hbm, out_hbm, idx_v, rows_v, sin0, sin1, sout0, sout1) = refs[:len(hf_groups)], refs[len(hf_groups):]
        sin, sout = (sin0, sin1), (sout0, sout1)
        wid = lax.axis_index("s") * info.num_cores + lax.axis_index("c")
        pltpu.sync_copy(slot_hbm.at[wid], idx_v)

        def load(j):
            g, per_w, jg = chunks[j]
            rows = pl.ds(pl.multiple_of(wid * per_w + jg * SC_CHUNK, SUBLANES), SC_CHUNK)
            return pltpu.make_async_copy(hf_hbm[g].at[rows], rows_v.at[j % 2], sin[j % 2])

        def scatters(j):
            return [pltpu.make_async_copy(rows_v.at[j % 2], out_hbm.at[idx_v.at[k, j]], sout[j % 2]) for k in range(TOP_K)]

        load(0).start()
        for j in range(nch):
            load(j).wait()
            if j >= 1:
                for cp in scatters(j - 1):
                    cp.wait()
            if j + 1 < nch:
                load(j + 1).start()
            for cp in scatters(j):
                cp.start()
        for cp in scatters(nch - 1):
            cp.wait()

    return scatter_rows(*hf_groups, slots)


def _experts_body(be_ref, nused_ref, start_ref, slot_ref, next_ref, nvalid_ref, xs_ref, bgu_ref, bd_ref, perm_ref, wgu_hbm, wd_hbm,
                  ys_ref, wgu_f32, wd_f32, wgu_bf, wd_bf, sems):
    i = pl.program_id(0)

    def weight_copies(e, slot):
        return (pltpu.make_async_copy(wgu_hbm.at[e], wgu_f32.at[slot], sems.at[0, slot]),
                pltpu.make_async_copy(wd_hbm.at[e], wd_f32.at[slot], sems.at[1, slot]))

    @pl.when(i < nused_ref[0])
    def _():
        @pl.when(start_ref[i] == 1)
        def _():
            slot = slot_ref[i]

            @pl.when(i == 0)
            def _():
                for cp in weight_copies(be_ref[0], 0):
                    cp.start()

            for cp in weight_copies(be_ref[i], slot):
                cp.wait()

            @pl.when(next_ref[i] >= 0)
            def _():
                for cp in weight_copies(next_ref[i], 1 - slot):
                    cp.start()

            wd_bf[...] = wd_f32[slot].astype(bf16)
            half = DEINT // 2
            for j in range(2 * D_FF // DEINT):
                w = jnp.dot(wgu_f32[slot, :, j * DEINT:(j + 1) * DEINT].astype(bf16), perm_ref[...],
                            preferred_element_type=f32).astype(bf16)
                wgu_bf[:, j * half:(j + 1) * half] = w[:, :half]
                wgu_bf[:, D_FF + j * half:D_FF + (j + 1) * half] = w[:, half:]

        row = lax.broadcasted_iota(i32, (MOE_BLOCK, 1), 0)
        xs = jnp.where(row < nvalid_ref[i], xs_ref[...], jnp.zeros((), u32))
        gu = jnp.dot(_unpack_bf16_pair(xs), wgu_bf[...], preferred_element_type=f32) + bgu_ref[0]
        glu = jnp.minimum(gu[:, :D_FF], SWIGLU_LIMIT)
        lin = jnp.clip(gu[:, D_FF:], -SWIGLU_LIMIT, SWIGLU_LIMIT)
        act = glu * jax.nn.sigmoid(SWIGLU_ALPHA * glu) * (lin + 1.0)
        ys_ref[...] = jnp.dot(act.astype(bf16), wd_bf[...], preferred_element_type=f32) + bd_ref[0]

    @pl.when(i >= nused_ref[0])
    def _():
        ys_ref[...] = jnp.zeros_like(ys_ref)


def _expert_segments(block_e, n_used, n_blocks):
    idx = jnp.arange(n_blocks, dtype=i32)
    prev_e = jnp.concatenate([block_e[:1], block_e[:-1]])
    start = (idx < n_used) & ((idx == 0) | (block_e != prev_e))
    slot = (jnp.cumsum(start.astype(i32)) - 1) % 2
    start_idx = jnp.where(start, idx, n_blocks)
    from_here = lax.cummin(start_idx, reverse=True)
    next_start = jnp.concatenate([from_here[1:], jnp.full((1,), n_blocks, i32)])
    next_e = jnp.sum(jnp.where(next_start[:, None] == idx[None, :], block_e[None, :], 0), axis=1)
    next_e = jnp.where(next_start < n_blocks, next_e, -1)
    return start.astype(i32), slot.astype(i32), next_e.astype(i32)


def _experts(block_e, n_used, n_valid, xs, wgu, bgu, w_down, b_down):
    n_blocks = xs.shape[0] // MOE_BLOCK
    start, slot, next_e = _expert_segments(block_e, n_used[0], n_blocks)
    blk = lambda i, be, nu, *_: (jnp.maximum(jnp.minimum(i, nu[0] - 1), 0), 0)
    exp3 = lambda i, be, *_: (be[i], 0, 0)
    c = np.arange(DEINT)
    src = np.where(c < DEINT // 2, 2 * c, 2 * (c - DEINT // 2) + 1)
    perm = jnp.asarray(np.arange(DEINT)[:, None] == src[None, :], bf16)
    return pl.pallas_call(
        _experts_body,
        grid_spec=pltpu.PrefetchScalarGridSpec(
            num_scalar_prefetch=6,
            grid=(n_blocks,),
            in_specs=[
                pl.BlockSpec((MOE_BLOCK, D_PACK), blk),
                pl.BlockSpec((1, 1, 2 * D_FF), exp3),
                pl.BlockSpec((1, 1, D_MODEL), exp3),
                pl.BlockSpec((DEINT, DEINT), lambda i, *_: (0, 0)),
                pl.BlockSpec(memory_space=pl.ANY),
                pl.BlockSpec(memory_space=pl.ANY),
            ],
            out_specs=pl.BlockSpec((MOE_BLOCK, D_MODEL), lambda i, *_: (i, 0)),
            scratch_shapes=[
                pltpu.VMEM((2, D_MODEL, 2 * D_FF), f32),
                pltpu.VMEM((2, D_FF, D_MODEL), f32),
                pltpu.VMEM((D_MODEL, 2 * D_FF), bf16),
                pltpu.VMEM((D_FF, D_MODEL), bf16),
                pltpu.SemaphoreType.DMA((2, 2)),
            ],
        ),
        out_shape=jax.ShapeDtypeStruct((xs.shape[0], D_MODEL), f32),
        compiler_params=_params("arbitrary"),
        name="moe_experts",
    )(block_e, n_used, start, slot, next_e, n_valid, xs, bgu, b_down, perm, wgu, w_down)


def _combine_body(tm, t, slot_ref, h_ref, gate_ref, nf_ref, ys_ref, o_ref, rows_ref, sems):
    s = pl.program_id(0)
    n_tiles = pl.num_programs(0) - 1

    def request():
        buf = s % 2
        for r in range(tm):
            for k in range(TOP_K):
                slot = slot_ref[k * t + s * tm + r]
                pltpu.make_async_copy(ys_ref.at[slot], rows_ref.at[buf, k, r], sems.at[buf]).start(priority=k % 2)

    def finish():
        buf = (s - 1) % 2
        for r in range(tm):
            for k in range(TOP_K):
                pltpu.make_async_copy(ys_ref.at[0], rows_ref.at[buf, k, r], sems.at[buf]).wait()
        g = gate_ref[...]
        moe = g[:, 0:1] * rows_ref[buf, 0]
        for k in range(1, TOP_K):
            moe = moe + g[:, k:k + 1] * rows_ref[buf, k]
        h = h_ref[...] + moe
        o_ref[...] = h * lax.rsqrt(jnp.mean(h * h, axis=-1, keepdims=True) + EPS) * nf_ref[...]

    @pl.when(s < n_tiles)
    def _():
        request()

    @pl.when(s > 0)
    def _():
        finish()


def _combine(slot_flat, h, gate_t, norm_final, ys, tm):
    t = h.shape[0]
    return pl.pallas_call(
        functools.partial(_combine_body, tm, t),
        grid_spec=pltpu.PrefetchScalarGridSpec(
            num_scalar_prefetch=1,
            grid=(t // tm + 1,),
            in_specs=[
                pl.BlockSpec((tm, D_MODEL), lambda i, s: (jnp.maximum(i - 1, 0), 0)),
                pl.BlockSpec((tm, TOP_K), lambda i, s: (jnp.maximum(i - 1, 0), 0)),
                pl.BlockSpec((1, D_MODEL), lambda i, s: (0, 0)),
                pl.BlockSpec(memory_space=pl.ANY),
            ],
            out_specs=pl.BlockSpec((tm, D_MODEL), lambda i, s: (jnp.maximum(i - 1, 0), 0)),
            scratch_shapes=[pltpu.VMEM((2, TOP_K, tm, D_MODEL), f32), pltpu.SemaphoreType.DMA((2,))],
        ),
        out_shape=jax.ShapeDtypeStruct((t, D_MODEL), f32),
        compiler_params=_params("arbitrary"),
        name="moe_combine",
    )(slot_flat, h, gate_t, norm_final, ys)


def kernel(x_prompt, x_sample, state_conv, state_ssm, norm_mix, w_in, conv_w, conv_b, dt_bias, a_log, d_skip, gnorm_w, v_norm_w, v_norm_b, w_spatial, b_spatial, w_branch_ssd, w_branch_mlp, w_out, norm_ffn, w_router, b_router, w_gu, b_gu, w_down, b_down, norm_final):
    assert w_in.shape[0] == 1, "single-layer trunk"
    batch, seq, _ = x_prompt.shape
    nseq, dec_seq, _ = x_sample.shape
    assert seq % CHUNK == 0 and CHUNK % dec_seq == 0 and nseq % (CHUNK // dec_seq) == 0 and dec_seq >= CONV_W - 1
    t_p, t_s = batch * seq, nseq * dec_seq
    tm = ROW_DMA_TILE
    assert t_p % tm == 0 and t_s % tm == 0
    tm_mix = MIX_TILE if t_p % MIX_TILE == 0 and t_s % MIX_TILE == 0 else tm

    wi = w_in[0]
    z0, x0, d0, u0 = D_INNER, D_INNER + CONV_DIM, D_INNER + CONV_DIM + HEADS, D_INNER + CONV_DIM + HEADS + 2 * D_GMLP
    w_segments = tuple(wi[:, a:b].astype(bf16) for a, b in ((0, z0), (d0, u0), (u0, D_IN_PROJ), (z0, x0)))
    w_dt = wi[:, x0:d0].astype(bf16)
    w_dtT = w_dt.T
    nm = norm_mix[0].reshape(1, D_MODEL)
    sc = lambda L: _ssd_consts(conv_w[0], conv_b[0], dt_bias[0], a_log[0], d_skip[0], gnorm_w[0], L)
    xp2, xs2 = x_prompt.reshape(t_p, D_MODEL), x_sample.reshape(t_s, D_MODEL)

    tril = np.tril(np.ones((CHUNK, CHUNK), bool))
    ws_p = jnp.where(tril[None], w_spatial[0], 0).astype(bf16)
    per = CHUNK // dec_seq
    blockdiag = (np.arange(CHUNK)[:, None] // dec_seq) == (np.arange(CHUNK)[None, :] // dec_seq)
    ws_s = jnp.where((tril & blockdiag)[None], jnp.tile(w_spatial[0][:, :dec_seq, :dec_seq], (1, per, per)), 0).astype(bf16)
    bs_p = b_spatial[0].T
    bs_s = jnp.tile(b_spatial[0][:, :dec_seq], (1, per)).T
    mc = dict(
        vnw=v_norm_w[0].reshape(1, D_GMLP), vnb=v_norm_b[0].reshape(1, D_GMLP),
        wbs=w_branch_ssd[0].astype(bf16), wbm=w_branch_mlp[0].astype(bf16), wo=w_out[0].astype(bf16),
        nffn=norm_ffn[0].reshape(1, D_MODEL), wrT=w_router[0].T, br=b_router[0].reshape(N_EXPERTS, 1),
        tris=jnp.asarray(np.triu(np.ones((tm_mix, tm_mix), np.float32), 1), bf16),
    )

    z_p, uv_p, g_p, xbc_p, dt_p, dtT_p = _in_proj(xp2, nm, w_segments, w_dt, w_dtT)
    yssd_p, ssm_p = _ssd_prompt(z_p, xbc_p, dt_p, dtT_p, sc(CHUNK), batch, seq)
    h_p, hf_p, te_p, gate_p, rank_p, cnt_p = _mix(uv_p, g_p, yssd_p, xp2, dict(mc, ws=ws_p, bs=bs_p), tm_mix, False)

    z_s, uv_s, g_s, xbc_s, dt_s, dtT_s = _in_proj(xs2, nm, w_segments, w_dt, w_dtT)
    pre = jnp.pad(state_conv[0], ((0, 0), (dec_seq - (CONV_W - 1), 0), (0, 0))).reshape(t_s, CONV_DIM)
    yssd_s, ssm_s = _ssd_sample(z_s, xbc_s, pre, dt_s, dtT_s, state_ssm[0].reshape(nseq, D_INNER, D_STATE),
                                sc(dec_seq), nseq, dec_seq)
    h_s, hf_s, v_s, te_s, gate_s, rank_s, cnt_s = _mix(uv_s, g_s, yssd_s, xs2, dict(mc, ws=ws_s, bs=bs_s), tm_mix, True)

    cp, cs = cnt_p[:, 0], cnt_s[:, 0]
    padded = (cp + cs + MOE_BLOCK - 1) // MOE_BLOCK * MOE_BLOCK
    pend = jnp.cumsum(padded)
    pstart = pend - padded

    def lookup(table, te):
        eid = jnp.arange(N_EXPERTS, dtype=i32)[:, None, None]
        return jnp.sum(jnp.where(te[None] == eid, table[:, None, None], 0), axis=0)

    slot_kt_p = (lookup(pstart, te_p) + rank_p).astype(i32)
    slot_kt_s = (lookup(pstart + cp, te_s) + rank_s).astype(i32)
    slot_p, slot_s = slot_kt_p.reshape(-1), slot_kt_s.reshape(-1)
    n_blocks = (t_p + t_s) * TOP_K // MOE_BLOCK + N_EXPERTS
    n_used = (pend[-1] // MOE_BLOCK).astype(i32).reshape(1)
    first_row = jnp.arange(n_blocks, dtype=i32) * MOE_BLOCK
    block_e = jnp.minimum(jnp.sum(pend[None, :] <= first_row[:, None], axis=1), N_EXPERTS - 1).astype(i32)
    last_row = jnp.sum(jnp.where(block_e[:, None] == jnp.arange(N_EXPERTS)[None, :], (pstart + cp + cs)[None, :], 0), axis=1)
    n_valid = jnp.clip(last_row - first_row, 0, MOE_BLOCK).astype(i32)

    xs_sorted = _dispatch((slot_kt_p, slot_kt_s), (hf_p, hf_s), n_blocks * MOE_BLOCK)
    bgu = jnp.concatenate([b_gu[0][..., 0::2], b_gu[0][..., 1::2]], axis=-1).reshape(N_EXPERTS, 1, 2 * D_FF)
    ys = _experts(block_e, n_used, n_valid, xs_sorted, w_gu[0], bgu, w_down[0], b_down[0].reshape(N_EXPERTS, 1, D_MODEL))

    nf = norm_final.reshape(1, D_MODEL)
    y_p = _combine(slot_p, h_p, gate_p.T, nf, ys, tm)
    y_s = _combine(slot_s, h_s, gate_s.T, nf, ys, tm)

    conv_p = xbc_p.reshape(batch, seq, CONV_DIM)[:, seq - (CONV_W - 1):]
    conv_s = xbc_s.reshape(nseq, dec_seq * CONV_DIM)[:, (dec_seq - (CONV_W - 1)) * CONV_DIM:].reshape(nseq, CONV_W - 1, CONV_DIM)
    st_shape = (HEADS, HEAD_DIM, D_STATE)
    return (
        y_p.reshape(batch, seq, D_MODEL),
        y_s.reshape(nseq, dec_seq, D_MODEL),
        conv_p[None],
        ssm_p.reshape(1, batch, *st_shape),
        conv_s[None],
        ssm_s.reshape(1, nseq, *st_shape),
        v_s.reshape(1, nseq, dec_seq, D_GMLP),
    )
```

```python
import functools

import numpy as np
import jax
import jax.numpy as jnp
from jax import lax
from jax.experimental import pallas as pl
from jax.experimental.pallas import tpu as pltpu
from jax.experimental.pallas import tpu_sc as plsc

f32, bf16, i32, u32 = jnp.float32, jnp.bfloat16, jnp.int32, jnp.uint32

D_MODEL = 1024
D_INNER = 2048
HEAD_DIM = 64
HEADS = 32
GROUPS = 4
HEADS_PER_GROUP = 8
GROUP_WIDTH = HEADS_PER_GROUP * HEAD_DIM
D_STATE = 128
CONV_W = 4
CONV_DIM = D_INNER + 2 * GROUPS * D_STATE
CHUNK = 128
SUBLANES = 8
D_GMLP = 1024
GMLP_HEADS = 8
N_EXPERTS = 32
TOP_K = 4
D_FF = 1024
SWIGLU_ALPHA = 1.702
SWIGLU_LIMIT = 7.0
EPS = 1e-5
LOG2_E = float(np.log2(np.e))
MOE_BLOCK = 512
DEINT = 256
ROW_DMA_TILE = 256
SC_CHUNK = 32
MIX_TILE = 512
D_IN_PROJ = 3 * D_INNER + CONV_DIM + HEADS
PROJ_TN = 512
PROMPT_SEQS_PER_STEP = 2
SAMPLE_SEQS_PER_STEP = 8
D_PACK = D_MODEL // 2
VMEM_LIMIT = 56 * 1024 * 1024

_NT = (((1,), (1,)), ((), ()))
_HI16 = np.uint32(0xFFFF0000)


def _params(*sem):
    return pltpu.CompilerParams(dimension_semantics=sem, vmem_limit_bytes=VMEM_LIMIT)


def _split(x, parts):
    out = []
    for _ in range(parts - 1):
        p = x.astype(bf16)
        out.append(p)
        x = x - p.astype(f32)
    out.append(x.astype(bf16))
    return out


def _dot_sel_rhs(x, sel, parts=3):
    return sum(jnp.dot(p, sel, preferred_element_type=f32) for p in _split(x, parts))


def _dot_sel_lhs(sel, x, parts=3):
    return sum(jnp.dot(sel, p, preferred_element_type=f32) for p in _split(x, parts))


def _silu(x):
    return x * jax.nn.sigmoid(x)


def _pack_bf16_pair(x):
    n = x.shape[1] // 2
    lo = lax.bitcast_convert_type(x[:, :n].astype(bf16).astype(f32), u32) >> 16
    hi = lax.bitcast_convert_type(x[:, n:].astype(bf16).astype(f32), u32) & _HI16
    return lo | hi


def _unpack_bf16_pair(p):
    lo = lax.bitcast_convert_type(p << 16, f32).astype(bf16)
    hi = lax.bitcast_convert_type(p & _HI16, f32).astype(bf16)
    return jnp.concatenate([lo, hi], axis=1)


def _in_proj_body(x_ref, nw_ref, wz_ref, wuv_ref, wg_ref, wxbc_ref, wdt_ref, wdtT_ref,
                  z_ref, uv_ref, g_ref, xbc_ref, dt_ref, dtT_ref):
    x = x_ref[...]
    ms = jnp.mean(x * x, axis=-1, keepdims=True)
    hn = (x * lax.rsqrt(ms + EPS) * nw_ref[...]).astype(bf16)
    dt_ref[...] = jnp.dot(hn, wdt_ref[...], preferred_element_type=f32)
    dtT_ref[...] = lax.dot_general(wdtT_ref[...], hn, _NT, preferred_element_type=f32)
    for o_ref, w_ref in ((z_ref, wz_ref), (uv_ref, wuv_ref), (g_ref, wg_ref), (xbc_ref, wxbc_ref)):
        for c in range(0, w_ref.shape[1], PROJ_TN):
            acc = jnp.dot(hn, w_ref[:, c:c + PROJ_TN], preferred_element_type=f32)
            o_ref[:, c:c + PROJ_TN] = acc.astype(o_ref.dtype)


def _in_proj(x2d, norm_w, w_segments, w_dt, w_dtT):
    t = x2d.shape[0]
    tm = min(512, t)
    row = lambda i: (i, 0)
    fixed = lambda i: (0, 0)
    return pl.pallas_call(
        _in_proj_body,
        grid=(t // tm,),
        in_specs=[
            pl.BlockSpec((tm, D_MODEL), row),
            pl.BlockSpec((1, D_MODEL), fixed),
        ] + [pl.BlockSpec(w.shape, fixed, pipeline_mode=pl.Buffered(1)) for w in w_segments] + [
            pl.BlockSpec((D_MODEL, HEADS), fixed),
            pl.BlockSpec((HEADS, D_MODEL), fixed),
        ],
        out_specs=[
            pl.BlockSpec((tm, D_INNER), row),
            pl.BlockSpec((tm, D_INNER), row),
            pl.BlockSpec((tm, D_INNER), row),
            pl.BlockSpec((tm, CONV_DIM), row),
            pl.BlockSpec((tm, HEADS), row),
            pl.BlockSpec((HEADS, tm), lambda i: (0, i)),
        ],
        out_shape=[
            jax.ShapeDtypeStruct((t, D_INNER), bf16),
            jax.ShapeDtypeStruct((t, D_INNER), bf16),
            jax.ShapeDtypeStruct((t, D_INNER), bf16),
            jax.ShapeDtypeStruct((t, CONV_DIM), f32),
            jax.ShapeDtypeStruct((t, HEADS), f32),
            jax.ShapeDtypeStruct((HEADS, t), f32),
        ],
        compiler_params=_params("arbitrary"),
        name="in_proj",
    )(x2d, norm_w, *w_segments, w_dt, w_dtT)


def _conv_taps(shifted, cw_ref, cb_ref):
    acc = cb_ref[...]
    for k in range(CONV_W):
        acc = acc + shifted(CONV_W - 1 - k) * cw_ref[k:k + 1, :]
    return _silu(acc)


def _conv_silu_one_seq(cur, prev8, cw_ref, cb_ref):
    assert CONV_W == 4
    row8 = lax.broadcasted_iota(i32, (SUBLANES, 1), 0)

    def shift_rows(x, x_prev8, j):
        xr = pltpu.roll(x, j, 0)
        top = jnp.where(row8 >= j, xr[:SUBLANES], pltpu.roll(x_prev8, j, 0))
        return jnp.concatenate([top, xr[SUBLANES:]], axis=0)

    w = [cw_ref[k:k + 1, :] for k in range(CONV_W)]
    x1 = shift_rows(cur, prev8, 1)
    near = x1 * w[2] + cur * w[3]
    far = x1 * w[0] + cur * w[1]
    far_prev8 = pltpu.roll(prev8, 1, 0) * w[0] + prev8 * w[1]
    return _silu(cb_ref[...] + shift_rows(far, far_prev8, 2) + near)


def _conv_silu_many_seq(cur, pre, cw_ref, cb_ref, seq_len):
    n = cur.shape[0]
    pos = lax.broadcasted_iota(i32, (n, 1), 0) % seq_len

    def shifted(j):
        if j == 0:
            return cur
        return jnp.where(pos >= j, pltpu.roll(cur, j, 0), pltpu.roll(pre, (j + n - seq_len) % n, 0))

    return _conv_taps(shifted, cw_ref, cb_ref)


def _ssd_block(xa, dt_raw, dtT_raw, c):
    tril = c["tril"][...]
    dt = jax.nn.softplus(dt_raw + c["dtb_row"][...])
    dtT = jax.nn.softplus(dtT_raw + c["dtb_col"][...])
    a = dt * c["aneg_row"][...]
    aT = dtT * c["aneg_col"][...]
    a_cum = _dot_sel_lhs(tril, a)
    a_cumT = _dot_sel_rhs(aT, c["triu"][...])
    a_tot = _dot_sel_lhs(c["same"][...], a)
    stack = jnp.concatenate([dt, jnp.exp(a_cum), jnp.exp(a_tot - a_cum)], axis=0)
    ex = jnp.dot(jnp.concatenate(_split(stack, 2), axis=1), c["expand"][...], preferred_element_type=f32)
    n = xa.shape[0]
    dtx, eax, dex = ex[:n], ex[n:2 * n], ex[2 * n:]

    xs = xa[:, :D_INNER]
    bm = xa[:, D_INNER:D_INNER + GROUPS * D_STATE]
    cm = xa[:, D_INNER + GROUPS * D_STATE:]
    xdt = xs * dtx
    xdt_bf = xdt.astype(bf16)
    mask = tril > 0
    a_cum2, a_cumT2 = a_cum * LOG2_E, a_cumT * LOG2_E
    lane = lax.broadcasted_iota(i32, (1, 2 * HEAD_DIM), 1)
    ys = []
    for g in range(GROUPS):
        cg = cm[:, g * D_STATE:(g + 1) * D_STATE].astype(bf16)
        bg = bm[:, g * D_STATE:(g + 1) * D_STATE].astype(bf16)
        cb = lax.dot_general(cg, bg, _NT, preferred_element_type=f32)
        for pair in range(HEADS_PER_GROUP // 2):
            halves = []
            for h in (g * HEADS_PER_GROUP + 2 * pair, g * HEADS_PER_GROUP + 2 * pair + 1):
                seg = a_cum2[:, h:h + 1] - a_cumT2[h:h + 1, :]
                decay = jnp.exp2(jnp.where(mask, seg, -jnp.inf))
                m = (cb * decay).astype(bf16)
                col = (h // 2) * 2 * HEAD_DIM
                halves.append(jnp.dot(m, xdt_bf[:, col:col + 2 * HEAD_DIM], preferred_element_type=f32))
            ys.append(jnp.where(lane < HEAD_DIM, halves[0], halves[1]))
    y_diag = jnp.concatenate(ys, axis=1)
    return dict(xs=xs, bm=bm, cm=cm, xdt=xdt, xd=xdt * dex, eax=eax, aT=aT, y_diag=y_diag)


def _gated_norm(y, z, gw):
    g = y * _silu(z.astype(f32))
    ms = jnp.mean(g * g, axis=-1, keepdims=True)
    return (g * lax.rsqrt(ms + EPS) * gw).astype(bf16)


_SSD_CONST_NAMES = ("cw", "cb", "dtb_row", "dtb_col", "aneg_row", "aneg_col", "dskipx", "gw",
                    "tril", "triu", "same", "expand")


def _ssd_consts(conv_w, conv_b, dt_bias, a_log, d_skip, gnorm_w, seq_len):
    r = np.arange(CHUNK)
    same = (r[:, None] // seq_len) == (r[None, :] // seq_len)
    tril = same & (r[None, :] <= r[:, None])
    expand = np.tile(np.repeat(np.eye(HEADS, dtype=np.float32), HEAD_DIM, axis=1), (2, 1))
    aneg = -jnp.exp(a_log.astype(f32))
    return dict(
        cw=conv_w, cb=conv_b.reshape(1, CONV_DIM),
        dtb_row=dt_bias.reshape(1, HEADS), dtb_col=dt_bias.reshape(HEADS, 1),
        aneg_row=aneg.reshape(1, HEADS), aneg_col=aneg.reshape(HEADS, 1),
        dskipx=jnp.repeat(d_skip.astype(f32), HEAD_DIM).reshape(1, D_INNER),
        gw=gnorm_w.reshape(1, D_INNER),
        tril=jnp.asarray(tril, bf16), triu=jnp.asarray(tril.T, bf16), same=jnp.asarray(same, bf16),
        expand=jnp.asarray(expand, bf16),
    )


def _const_specs(consts):
    zero = (lambda *_: (0, 0))
    return [pl.BlockSpec(consts[k].shape, zero) for k in _SSD_CONST_NAMES]


def _ssd_prompt_body(nb, z_ref, xbc_ref, dt_ref, dtT_ref, *rest):
    nc = len(_SSD_CONST_NAMES)
    c = dict(zip(_SSD_CONST_NAMES, rest[:nc]))
    y_ref, state_ref, prev_ref, st_ref = rest[nc:]
    ci = pl.program_id(1)

    @pl.when(ci == 0)
    def _():
        prev_ref[...] = jnp.zeros_like(prev_ref)
        st_ref[...] = jnp.zeros_like(st_ref)

    for s in range(nb):
        cur = xbc_ref[s]
        xa = _conv_silu_one_seq(cur, prev_ref[s], c["cw"], c["cb"])
        prev_ref[s] = cur[CHUNK - SUBLANES:]
        b = _ssd_block(xa, dt_ref[s], dtT_ref[s], c)

        y_off = []
        for g in range(GROUPS):
            sl = slice(g * GROUP_WIDTH, (g + 1) * GROUP_WIDTH)
            st = st_ref[s * GROUPS + g]
            cg = b["cm"][:, g * D_STATE:(g + 1) * D_STATE].astype(bf16)
            y_off.append(jnp.dot(cg, st.astype(bf16), preferred_element_type=f32))
            bgT = b["bm"][:, g * D_STATE:(g + 1) * D_STATE].T.astype(bf16)
            upd = jnp.dot(bgT, b["xd"][:, sl].astype(bf16), preferred_element_type=f32)
            st_ref[s * GROUPS + g] = st * b["eax"][CHUNK - 1:CHUNK, sl] + upd
        y = b["y_diag"] + jnp.concatenate(y_off, axis=1) * b["eax"] + c["dskipx"][...] * b["xs"]
        y_ref[s] = _gated_norm(y, z_ref[s], c["gw"][...])

    @pl.when(ci == pl.num_programs(1) - 1)
    def _():
        for s in range(nb):
            for g in range(GROUPS):
                state_ref[s, 0, g * GROUP_WIDTH:(g + 1) * GROUP_WIDTH, :] = st_ref[s * GROUPS + g].T


def _ssd_prompt(z, xbc, dt, dtT, consts, batch, seq):
    nb = PROMPT_SEQS_PER_STEP if batch % PROMPT_SEQS_PER_STEP == 0 else 1
    nchunk = seq // CHUNK
    part = batch // nb * seq
    split = lambda a: a.reshape(nb, part, a.shape[-1])
    dtT3 = dtT.reshape(HEADS, nb, part).transpose(1, 0, 2)
    row = lambda b, ci: (0, b * nchunk + ci, 0)
    y, state = pl.pallas_call(
        functools.partial(_ssd_prompt_body, nb),
        grid=(batch // nb, nchunk),
        in_specs=[
            pl.BlockSpec((nb, CHUNK, D_INNER), row),
            pl.BlockSpec((nb, CHUNK, CONV_DIM), row),
            pl.BlockSpec((nb, CHUNK, HEADS), row),
            pl.BlockSpec((nb, HEADS, CHUNK), lambda b, ci: (0, 0, b * nchunk + ci)),
        ] + _const_specs(consts),
        out_specs=[
            pl.BlockSpec((nb, CHUNK, D_INNER), row),
            pl.BlockSpec((nb, 1, D_INNER, D_STATE), lambda b, ci: (0, b, 0, 0)),
        ],
        out_shape=[
            jax.ShapeDtypeStruct((nb, part, D_INNER), bf16),
            jax.ShapeDtypeStruct((nb, batch // nb, D_INNER, D_STATE), f32),
        ],
        scratch_shapes=[pltpu.VMEM((nb, SUBLANES, CONV_DIM), f32),
                        pltpu.VMEM((nb * GROUPS, D_STATE, GROUP_WIDTH), f32)],
        compiler_params=_params("arbitrary", "arbitrary"),
        name="ssd_prompt",
    )(split(z), split(xbc), split(dt), dtT3, *[consts[k] for k in _SSD_CONST_NAMES])
    return y.reshape(batch * seq, D_INNER), state.reshape(batch, D_INNER, D_STATE)


def _ssd_sample_body(seq_len, z_ref, xbc_ref, pre_ref, dt_ref, dtT_ref, state_ref, *rest):
    nc = len(_SSD_CONST_NAMES)
    c = dict(zip(_SSD_CONST_NAMES, rest[:nc]))
    y_ref, state_out_ref, c_ref, b_ref, xdT_ref, yacc_ref, eax_ref, eatT_ref = rest[nc:]
    step = pl.program_id(1)
    per_step = state_ref.shape[0]

    @pl.when(step == 0)
    def _():
        xa = _conv_silu_many_seq(xbc_ref[...], pre_ref[...], c["cw"], c["cb"], seq_len)
        b = _ssd_block(xa, dt_ref[...], dtT_ref[...], c)
        c_ref[...] = b["cm"]
        b_ref[...] = b["bm"]
        xdT_ref[...] = b["xd"].T.astype(bf16)
        yacc_ref[...] = b["y_diag"] + c["dskipx"][...] * b["xs"]
        eax_ref[...] = b["eax"]
        eatT_ref[...] = jnp.exp(_dot_sel_rhs(b["aT"], c["same"][...]))

    def one_sequence(q, carry):
        s = step * per_step + q
        r0 = pl.multiple_of(s * seq_len, seq_len)
        rows = pl.ds(r0, seq_len)
        lane = lax.broadcasted_iota(i32, (1, CHUNK), 1)
        arep = jnp.broadcast_to(jnp.sum(jnp.where(lane == r0, eatT_ref[...], 0.0), axis=1, keepdims=True), (HEADS, D_STATE))
        rmask = (lax.broadcasted_iota(i32, (CHUNK, 1), 0) // seq_len) == s
        for g in range(GROUPS):
            gs = slice(g * GROUP_WIDTH, (g + 1) * GROUP_WIDTH)
            ns = slice(g * D_STATE, (g + 1) * D_STATE)
            s0 = state_ref[q, gs, :]
            cg = c_ref[rows, ns].astype(bf16)
            yo = lax.dot_general(cg, s0.astype(bf16), _NT, preferred_element_type=f32)
            yacc_ref[rows, gs] = yacc_ref[rows, gs] + yo * eax_ref[rows, gs]
            bmask = jnp.where(rmask, b_ref[:, ns], 0.0).astype(bf16)
            upd = jnp.dot(xdT_ref[gs, :], bmask, preferred_element_type=f32)
            for r in range(HEADS_PER_GROUP):
                h = g * HEADS_PER_GROUP + r
                hs = slice(r * HEAD_DIM, (r + 1) * HEAD_DIM)
                state_out_ref[q, h * HEAD_DIM:(h + 1) * HEAD_DIM, :] = s0[hs] * arep[h:h + 1, :] + upd[hs]
        return carry

    lax.fori_loop(0, per_step, one_sequence, 0)

    @pl.when(step == pl.num_programs(1) - 1)
    def _():
        y_ref[...] = _gated_norm(yacc_ref[...], z_ref[...], c["gw"][...])


def _ssd_sample(z, xbc, pre, dt, dtT, state, consts, nseq, seq_len):
    per = CHUNK // seq_len
    nblk = nseq // per
    t = nseq * seq_len
    q = SAMPLE_SEQS_PER_STEP if per % SAMPLE_SEQS_PER_STEP == 0 else 1
    blk = lambda i, s: (i, 0)
    seq3 = lambda i, s: (i * (per // q) + s, 0, 0)
    return pl.pallas_call(
        functools.partial(_ssd_sample_body, seq_len),
        grid=(nblk, per // q),
        in_specs=[
            pl.BlockSpec((CHUNK, D_INNER), blk),
            pl.BlockSpec((CHUNK, CONV_DIM), blk),
            pl.BlockSpec((CHUNK, CONV_DIM), blk),
            pl.BlockSpec((CHUNK, HEADS), blk),
            pl.BlockSpec((HEADS, CHUNK), lambda i, s: (0, i)),
            pl.BlockSpec((q, D_INNER, D_STATE), seq3),
        ] + _const_specs(consts),
        out_specs=[
            pl.BlockSpec((CHUNK, D_INNER), blk),
            pl.BlockSpec((q, D_INNER, D_STATE), seq3),
        ],
        out_shape=[
            jax.ShapeDtypeStruct((t, D_INNER), bf16),
            jax.ShapeDtypeStruct((nseq, D_INNER, D_STATE), f32),
        ],
        scratch_shapes=[
            pltpu.VMEM((CHUNK, GROUPS * D_STATE), f32),
            pltpu.VMEM((CHUNK, GROUPS * D_STATE), f32),
            pltpu.VMEM((D_INNER, CHUNK), bf16),
            pltpu.VMEM((CHUNK, D_INNER), f32),
            pltpu.VMEM((CHUNK, D_INNER), f32),
            pltpu.VMEM((HEADS, CHUNK), f32),
        ],
        compiler_params=_params("arbitrary", "arbitrary"),
        name="ssd_sample",
    )(z, xbc, pre, dt, dtT, state, *[consts[k] for k in _SSD_CONST_NAMES])


_MIX_CONST_NAMES = ("ws", "bs", "vnw", "vnb", "wbs", "wbm", "wo", "nffn", "wrT", "br", "tris")


def _mix_body(emit_v, uv_ref, gates_ref, yssd_ref, x_ref, *rest):
    nc = len(_MIX_CONST_NAMES)
    c = dict(zip(_MIX_CONST_NAMES, rest[:nc]))
    outs = list(rest[nc:])
    h_ref, hfp_ref = outs[:2]
    v_ref = outs[2] if emit_v else None
    te_ref, gate_ref, rank_ref, cnt_ref, cnt_acc = outs[-5:]
    tm = uv_ref.shape[0]

    @pl.when(pl.program_id(0) == 0)
    def _():
        cnt_acc[...] = jnp.zeros_like(cnt_acc)

    uv = uv_ref[...].astype(f32)
    uv = 0.5 * uv * (1.0 + lax.erf(uv * np.float32(np.sqrt(0.5))))
    u, v = uv[:, :D_GMLP], uv[:, D_GMLP:]
    mu = jnp.mean(v, axis=-1, keepdims=True)
    vc = v - mu
    vn = vc * lax.rsqrt(jnp.mean(vc * vc, axis=-1, keepdims=True) + EPS) * c["vnw"][...] + c["vnb"][...]
    if emit_v:
        v_ref[...] = vn
    vn_bf = vn.astype(bf16)
    gd = D_GMLP // GMLP_HEADS
    rows = []
    for ck in range(tm // CHUNK):
        rs = slice(ck * CHUNK, (ck + 1) * CHUNK)
        heads = []
        for g in range(GMLP_HEADS):
            mixed = jnp.dot(c["ws"][g], vn_bf[rs, g * gd:(g + 1) * gd], preferred_element_type=f32)
            heads.append(mixed + c["bs"][:, g:g + 1])
        rows.append(jnp.concatenate(heads, axis=1))
    y_mlp = u * jnp.concatenate(rows, axis=0)

    a = jnp.dot(yssd_ref[...], c["wbs"][...], preferred_element_type=f32)
    b = jnp.dot(y_mlp.astype(bf16), c["wbm"][...], preferred_element_type=f32)
    gs = jax.nn.sigmoid(gates_ref[...].astype(f32))
    merged = gs[:, :D_MODEL] * a + gs[:, D_MODEL:] * b
    h = x_ref[...] + jnp.dot(merged.astype(bf16), c["wo"][...], preferred_element_type=f32)
    h_ref[...] = h
    hf = h * lax.rsqrt(jnp.mean(h * h, axis=-1, keepdims=True) + EPS) * c["nffn"][...]
    hfp_ref[...] = _pack_bf16_pair(hf)

    lg = lax.dot_general(c["wrT"][...], hf, _NT, precision=lax.Precision.HIGHEST, preferred_element_type=f32) + c["br"][...]
    sub = lax.broadcasted_iota(i32, lg.shape, 0)
    idxs, vals = [], []
    for _ in range(TOP_K):
        m = jnp.max(lg, axis=0, keepdims=True)
        idx = jnp.min(jnp.where(lg == m, sub, N_EXPERTS), axis=0, keepdims=True)
        idxs.append(idx)
        vals.append(m)
        lg = jnp.where(sub == idx, -jnp.inf, lg)
    p = jnp.exp(jnp.concatenate(vals, axis=0) - vals[0])
    gate_ref[...] = p / jnp.sum(p, axis=0, keepdims=True)
    te_ref[...] = jnp.concatenate(idxs, axis=0)

    onehots = [sub == idx for idx in idxs]
    member = functools.reduce(jnp.logical_or, onehots).astype(f32)
    before = jnp.dot(member.astype(bf16), c["tris"][...], preferred_element_type=f32) + cnt_acc[:, 0:1]
    rank_ref[...] = jnp.concatenate(
        [jnp.sum(jnp.where(oh, before, 0.0), axis=0, keepdims=True) for oh in onehots], axis=0).astype(i32)
    cnt_acc[...] = cnt_acc[...] + jnp.sum(member, axis=1, keepdims=True)
    cnt_ref[...] = cnt_acc[...].astype(i32)


def _mix(uv, gates, yssd, x2d, mc, tm, emit_v):
    t = x2d.shape[0]
    row = lambda i: (i, 0)
    col = lambda i: (0, i)
    const = lambda a: pl.BlockSpec(a.shape, lambda i: (0,) * a.ndim, pipeline_mode=pl.Buffered(1))
    v_spec = [pl.BlockSpec((tm, D_GMLP), row)] if emit_v else []
    v_shape = [jax.ShapeDtypeStruct((t, D_GMLP), f32)] if emit_v else []
    return pl.pallas_call(
        functools.partial(_mix_body, emit_v),
        grid=(t // tm,),
        in_specs=[
            pl.BlockSpec((tm, D_INNER), row),
            pl.BlockSpec((tm, D_INNER), row),
            pl.BlockSpec((tm, D_INNER), row),
            pl.BlockSpec((tm, D_MODEL), row),
        ] + [const(mc[k]) for k in _MIX_CONST_NAMES],
        out_specs=[pl.BlockSpec((tm, D_MODEL), row), pl.BlockSpec((tm, D_PACK), row)] + v_spec + [
            pl.BlockSpec((TOP_K, tm), col),
            pl.BlockSpec((TOP_K, tm), col),
            pl.BlockSpec((TOP_K, tm), col),
            pl.BlockSpec((N_EXPERTS, 128), lambda i: (0, 0)),
        ],
        out_shape=[
            jax.ShapeDtypeStruct((t, D_MODEL), f32),
            jax.ShapeDtypeStruct((t, D_PACK), u32),
        ] + v_shape + [
            jax.ShapeDtypeStruct((TOP_K, t), i32),
            jax.ShapeDtypeStruct((TOP_K, t), f32),
            jax.ShapeDtypeStruct((TOP_K, t), i32),
            jax.ShapeDtypeStruct((N_EXPERTS, 128), i32),
        ],
        scratch_shapes=[pltpu.VMEM((N_EXPERTS, 128), f32)],
        compiler_params=_params("arbitrary"),
        name="mix_route",
    )(uv, gates, yssd, x2d, *[mc[k] for k in _MIX_CONST_NAMES])


def _dispatch(slot_groups, hf_groups, n_slots):
    info = plsc.get_sparse_core_info()
    workers = info.num_cores * info.num_subcores
    chunks = []
    slots = []
    for g, (slot_kt, hf) in enumerate(zip(slot_groups, hf_groups)):
        per_w, rem = divmod(hf.shape[0], workers)
        n, rem2 = divmod(per_w, SC_CHUNK)
        assert rem == 0 and rem2 == 0
        chunks += [(g, per_w, j) for j in range(n)]
        slots.append(slot_kt.reshape(TOP_K, workers, n, SC_CHUNK).transpose(1, 0, 2, 3))
    slots = jnp.concatenate(slots, axis=2)
    nch = len(chunks)
    mesh = plsc.VectorSubcoreMesh(core_axis_name="c", subcore_axis_name="s")

    @functools.partial(
        pl.kernel, mesh=mesh,
        out_type=jax.ShapeDtypeStruct((n_slots, D_PACK), u32),
        scratch_types=[pltpu.VMEM((TOP_K, nch, SC_CHUNK), i32), pltpu.VMEM((2, SC_CHUNK, D_PACK), u32)]
        + [pltpu.SemaphoreType.DMA] * 4,
    )
    def scatter_rows(*refs):
        hf_hbm, (slot_hbm, out_hbm, idx_v, rows_v, sin0, sin1, sout0, sout1) = refs[:len(hf_groups)], refs[len(hf_groups):]
        sin, sout = (sin0, sin1), (sout0, sout1)
        wid = lax.axis_index("s") * info.num_cores + lax.axis_index("c")
        pltpu.sync_copy(slot_hbm.at[wid], idx_v)

        def load(j):
            g, per_w, jg = chunks[j]
            rows = pl.ds(pl.multiple_of(wid * per_w + jg * SC_CHUNK, SUBLANES), SC_CHUNK)
            return pltpu.make_async_copy(hf_hbm[g].at[rows], rows_v.at[j % 2], sin[j % 2])

        def scatters(j):
            return [pltpu.make_async_copy(rows_v.at[j % 2], out_hbm.at[idx_v.at[k, j]], sout[j % 2]) for k in range(TOP_K)]

        load(0).start()
        for j in range(nch):
            load(j).wait()
            if j >= 1:
                for cp in scatters(j - 1):
                    cp.wait()
            if j + 1 < nch:
                load(j + 1).start()
            for cp in scatters(j):
                cp.start()
        for cp in scatters(nch - 1):
            cp.wait()

    return scatter_rows(*hf_groups, slots)


def _experts_body(be_ref, nused_ref, start_ref, slot_ref, next_ref, nvalid_ref, xs_ref, bgu_ref, bd_ref, perm_ref, wgu_hbm, wd_hbm,
                  ys_ref, wgu_f32, wd_f32, wgu_bf, wd_bf, sems):
    i = pl.program_id(0)

    def weight_copies(e, slot):
        return (pltpu.make_async_copy(wgu_hbm.at[e], wgu_f32.at[slot], sems.at[0, slot]),
                pltpu.make_async_copy(wd_hbm.at[e], wd_f32.at[slot], sems.at[1, slot]))

    @pl.when(i < nused_ref[0])
    def _():
        @pl.when(start_ref[i] == 1)
        def _():
            slot = slot_ref[i]

            @pl.when(i == 0)
            def _():
                for cp in weight_copies(be_ref[0], 0):
                    cp.start()

            for cp in weight_copies(be_ref[i], slot):
                cp.wait()

            @pl.when(next_ref[i] >= 0)
            def _():
                for cp in weight_copies(next_ref[i], 1 - slot):
                    cp.start()

            wd_bf[...] = wd_f32[slot].astype(bf16)
            half = DEINT // 2
            for j in range(2 * D_FF // DEINT):
                w = jnp.dot(wgu_f32[slot, :, j * DEINT:(j + 1) * DEINT].astype(bf16), perm_ref[...],
                            preferred_element_type=f32).astype(bf16)
                wgu_bf[:, j * half:(j + 1) * half] = w[:, :half]
                wgu_bf[:, D_FF + j * half:D_FF + (j + 1) * half] = w[:, half:]

        row = lax.broadcasted_iota(i32, (MOE_BLOCK, 1), 0)
        xs = jnp.where(row < nvalid_ref[i], xs_ref[...], jnp.zeros((), u32))
        gu = jnp.dot(_unpack_bf16_pair(xs), wgu_bf[...], preferred_element_type=f32) + bgu_ref[0]
        glu = jnp.minimum(gu[:, :D_FF], SWIGLU_LIMIT)
        lin = jnp.clip(gu[:, D_FF:], -SWIGLU_LIMIT, SWIGLU_LIMIT)
        act = glu * jax.nn.sigmoid(SWIGLU_ALPHA * glu) * (lin + 1.0)
        ys_ref[...] = jnp.dot(act.astype(bf16), wd_bf[...], preferred_element_type=f32) + bd_ref[0]

    @pl.when(i >= nused_ref[0])
    def _():
        ys_ref[...] = jnp.zeros_like(ys_ref)


def _expert_segments(block_e, n_used, n_blocks):
    idx = jnp.arange(n_blocks, dtype=i32)
    prev_e = jnp.concatenate([block_e[:1], block_e[:-1]])
    start = (idx < n_used) & ((idx == 0) | (block_e != prev_e))
    slot = (jnp.cumsum(start.astype(i32)) - 1) % 2
    start_idx = jnp.where(start, idx, n_blocks)
    from_here = lax.cummin(start_idx, reverse=True)
    next_start = jnp.concatenate([from_here[1:], jnp.full((1,), n_blocks, i32)])
    next_e = jnp.sum(jnp.where(next_start[:, None] == idx[None, :], block_e[None, :], 0), axis=1)
    next_e = jnp.where(next_start < n_blocks, next_e, -1)
    return start.astype(i32), slot.astype(i32), next_e.astype(i32)


def _experts(block_e, n_used, n_valid, xs, wgu, bgu, w_down, b_down):
    n_blocks = xs.shape[0] // MOE_BLOCK
    start, slot, next_e = _expert_segments(block_e, n_used[0], n_blocks)
    blk = lambda i, be, nu, *_: (jnp.maximum(jnp.minimum(i, nu[0] - 1), 0), 0)
    exp3 = lambda i, be, *_: (be[i], 0, 0)
    c = np.arange(DEINT)
    src = np.where(c < DEINT // 2, 2 * c, 2 * (c - DEINT // 2) + 1)
    perm = jnp.asarray(np.arange(DEINT)[:, None] == src[None, :], bf16)
    return pl.pallas_call(
        _experts_body,
        grid_spec=pltpu.PrefetchScalarGridSpec(
            num_scalar_prefetch=6,
            grid=(n_blocks,),
            in_specs=[
                pl.BlockSpec((MOE_BLOCK, D_PACK), blk),
                pl.BlockSpec((1, 1, 2 * D_FF), exp3),
                pl.BlockSpec((1, 1, D_MODEL), exp3),
                pl.BlockSpec((DEINT, DEINT), lambda i, *_: (0, 0)),
                pl.BlockSpec(memory_space=pl.ANY),
                pl.BlockSpec(memory_space=pl.ANY),
            ],
            out_specs=pl.BlockSpec((MOE_BLOCK, D_MODEL), lambda i, *_: (i, 0)),
            scratch_shapes=[
                pltpu.VMEM((2, D_MODEL, 2 * D_FF), f32),
                pltpu.VMEM((2, D_FF, D_MODEL), f32),
                pltpu.VMEM((D_MODEL, 2 * D_FF), bf16),
                pltpu.VMEM((D_FF, D_MODEL), bf16),
                pltpu.SemaphoreType.DMA((2, 2)),
            ],
        ),
        out_shape=jax.ShapeDtypeStruct((xs.shape[0], D_MODEL), f32),
        compiler_params=_params("arbitrary"),
        name="moe_experts",
    )(block_e, n_used, start, slot, next_e, n_valid, xs, bgu, b_down, perm, wgu, w_down)


def _combine_body(tm, t, slot_ref, h_ref, gate_ref, nf_ref, ys_ref, o_ref, rows_ref, sems):
    s = pl.program_id(0)
    n_tiles = pl.num_programs(0) - 1

    def request():
        buf = s % 2
        for r in range(tm):
            for k in range(TOP_K):
                slot = slot_ref[k * t + s * tm + r]
                pltpu.make_async_copy(ys_ref.at[slot], rows_ref.at[buf, k, r], sems.at[buf]).start(priority=k % 2)

    def finish():
        buf = (s - 1) % 2
        for r in range(tm):
            for k in range(TOP_K):
                pltpu.make_async_copy(ys_ref.at[0], rows_ref.at[buf, k, r], sems.at[buf]).wait()
        g = gate_ref[...]
        moe = g[:, 0:1] * rows_ref[buf, 0]
        for k in range(1, TOP_K):
            moe = moe + g[:, k:k + 1] * rows_ref[buf, k]
        h = h_ref[...] + moe
        o_ref[...] = h * lax.rsqrt(jnp.mean(h * h, axis=-1, keepdims=True) + EPS) * nf_ref[...]

    @pl.when(s < n_tiles)
    def _():
        request()

    @pl.when(s > 0)
    def _():
        finish()


def _combine(slot_flat, h, gate_t, norm_final, ys, tm):
    t = h.shape[0]
    return pl.pallas_call(
        functools.partial(_combine_body, tm, t),
        grid_spec=pltpu.PrefetchScalarGridSpec(
            num_scalar_prefetch=1,
            grid=(t // tm + 1,),
            in_specs=[
                pl.BlockSpec((tm, D_MODEL), lambda i, s: (jnp.maximum(i - 1, 0), 0)),
                pl.BlockSpec((tm, TOP_K), lambda i, s: (jnp.maximum(i - 1, 0), 0)),
                pl.BlockSpec((1, D_MODEL), lambda i, s: (0, 0)),
                pl.BlockSpec(memory_space=pl.ANY),
            ],
            out_specs=pl.BlockSpec((tm, D_MODEL), lambda i, s: (jnp.maximum(i - 1, 0), 0)),
            scratch_shapes=[pltpu.VMEM((2, TOP_K, tm, D_MODEL), f32), pltpu.SemaphoreType.DMA((2,))],
        ),
        out_shape=jax.ShapeDtypeStruct((t, D_MODEL), f32),
        compiler_params=_params("arbitrary"),
        name="moe_combine",
    )(slot_flat, h, gate_t, norm_final, ys)


def kernel(x_prompt, x_sample, state_conv, state_ssm, norm_mix, w_in, conv_w, conv_b, dt_bias, a_log, d_skip, gnorm_w, v_norm_w, v_norm_b, w_spatial, b_spatial, w_branch_ssd, w_branch_mlp, w_out, norm_ffn, w_router, b_router, w_gu, b_gu, w_down, b_down, norm_final):
    assert w_in.shape[0] == 1, "single-layer trunk"
    batch, seq, _ = x_prompt.shape
    nseq, dec_seq, _ = x_sample.shape
    assert seq % CHUNK == 0 and CHUNK % dec_seq == 0 and nseq % (CHUNK // dec_seq) == 0 and dec_seq >= CONV_W - 1
    t_p, t_s = batch * seq, nseq * dec_seq
    tm = ROW_DMA_TILE
    assert t_p % tm == 0 and t_s % tm == 0
    tm_mix = MIX_TILE if t_p % MIX_TILE == 0 and t_s % MIX_TILE == 0 else tm

    wi = w_in[0]
    z0, x0, d0, u0 = D_INNER, D_INNER + CONV_DIM, D_INNER + CONV_DIM + HEADS, D_INNER + CONV_DIM + HEADS + 2 * D_GMLP
    w_segments = tuple(wi[:, a:b].astype(bf16) for a, b in ((0, z0), (d0, u0), (u0, D_IN_PROJ), (z0, x0)))
    w_dt = wi[:, x0:d0].astype(bf16)
    w_dtT = w_dt.T
    nm = norm_mix[0].reshape(1, D_MODEL)
    sc = lambda L: _ssd_consts(conv_w[0], conv_b[0], dt_bias[0], a_log[0], d_skip[0], gnorm_w[0], L)
    xp2, xs2 = x_prompt.reshape(t_p, D_MODEL), x_sample.reshape(t_s, D_MODEL)

    tril = np.tril(np.ones((CHUNK, CHUNK), bool))
    ws_p = jnp.where(tril[None], w_spatial[0], 0).astype(bf16)
    per = CHUNK // dec_seq
    blockdiag = (np.arange(CHUNK)[:, None] // dec_seq) == (np.arange(CHUNK)[None, :] // dec_seq)
    ws_s = jnp.where((tril & blockdiag)[None], jnp.tile(w_spatial[0][:, :dec_seq, :dec_seq], (1, per, per)), 0).astype(bf16)
    bs_p = b_spatial[0].T
    bs_s = jnp.tile(b_spatial[0][:, :dec_seq], (1, per)).T
    mc = dict(
        vnw=v_norm_w[0].reshape(1, D_GMLP), vnb=v_norm_b[0].reshape(1, D_GMLP),
        wbs=w_branch_ssd[0].astype(bf16), wbm=w_branch_mlp[0].astype(bf16), wo=w_out[0].astype(bf16),
        nffn=norm_ffn[0].reshape(1, D_MODEL), wrT=w_router[0].T, br=b_router[0].reshape(N_EXPERTS, 1),
        tris=jnp.asarray(np.triu(np.ones((tm_mix, tm_mix), np.float32), 1), bf16),
    )

    z_p, uv_p, g_p, xbc_p, dt_p, dtT_p = _in_proj(xp2, nm, w_segments, w_dt, w_dtT)
    yssd_p, ssm_p = _ssd_prompt(z_p, xbc_p, dt_p, dtT_p, sc(CHUNK), batch, seq)
    h_p, hf_p, te_p, gate_p, rank_p, cnt_p = _mix(uv_p, g_p, yssd_p, xp2, dict(mc, ws=ws_p, bs=bs_p), tm_mix, False)

    z_s, uv_s, g_s, xbc_s, dt_s, dtT_s = _in_proj(xs2, nm, w_segments, w_dt, w_dtT)
    pre = jnp.pad(state_conv[0], ((0, 0), (dec_seq - (CONV_W - 1), 0), (0, 0))).reshape(t_s, CONV_DIM)
    yssd_s, ssm_s = _ssd_sample(z_s, xbc_s, pre, dt_s, dtT_s, state_ssm[0].reshape(nseq, D_INNER, D_STATE),
                                sc(dec_seq), nseq, dec_seq)
    h_s, hf_s, v_s, te_s, gate_s, rank_s, cnt_s = _mix(uv_s, g_s, yssd_s, xs2, dict(mc, ws=ws_s, bs=bs_s), tm_mix, True)

    cp, cs = cnt_p[:, 0], cnt_s[:, 0]
    padded = (cp + cs + MOE_BLOCK - 1) // MOE_BLOCK * MOE_BLOCK
    pend = jnp.cumsum(padded)
    pstart = pend - padded

    def lookup(table, te):
        eid = jnp.arange(N_EXPERTS, dtype=i32)[:, None, None]
        return jnp.sum(jnp.where(te[None] == eid, table[:, None, None], 0), axis=0)

    slot_kt_p = (lookup(pstart, te_p) + rank_p).astype(i32)
    slot_kt_s = (lookup(pstart + cp, te_s) + rank_s).astype(i32)
    slot_p, slot_s = slot_kt_p.reshape(-1), slot_kt_s.reshape(-1)
    n_blocks = (t_p + t_s) * TOP_K // MOE_BLOCK + N_EXPERTS
    n_used = (pend[-1] // MOE_BLOCK).astype(i32).reshape(1)
    first_row = jnp.arange(n_blocks, dtype=i32) * MOE_BLOCK
    block_e = jnp.minimum(jnp.sum(pend[None, :] <= first_row[:, None], axis=1), N_EXPERTS - 1).astype(i32)
    last_row = jnp.sum(jnp.where(block_e[:, None] == jnp.arange(N_EXPERTS)[None, :], (pstart + cp + cs)[None, :], 0), axis=1)
    n_valid = jnp.clip(last_row - first_row, 0, MOE_BLOCK).astype(i32)

    xs_sorted = _dispatch((slot_kt_p, slot_kt_s), (hf_p, hf_s), n_blocks * MOE_BLOCK)
    bgu = jnp.concatenate([b_gu[0][..., 0::2], b_gu[0][..., 1::2]], axis=-1).reshape(N_EXPERTS, 1, 2 * D_FF)
    ys = _experts(block_e, n_used, n_valid, xs_sorted, w_gu[0], bgu, w_down[0], b_down[0].reshape(N_EXPERTS, 1, D_MODEL))

    nf = norm_final.reshape(1, D_MODEL)
    y_p = _combine(slot_p, h_p, gate_p.T, nf, ys, tm)
    y_s = _combine(slot_s, h_s, gate_s.T, nf, ys, tm)

    conv_p = xbc_p.reshape(batch, seq, CONV_DIM)[:, seq - (CONV_W - 1):]
    rows_s = xbc_s.reshape(nseq, dec_seq * CONV_DIM)
    conv_s = jnp.stack([rows_s[:, r * CONV_DIM:(r + 1) * CONV_DIM] for r in range(dec_seq - (CONV_W - 1), dec_seq)], axis=1)
    st_shape = (HEADS, HEAD_DIM, D_STATE)
    return (
        y_p.reshape(batch, seq, D_MODEL),
        y_s.reshape(nseq, dec_seq, D_MODEL),
        conv_p[None],
        ssm_p.reshape(1, batch, *st_shape),
        conv_s[None],
        ssm_s.reshape(1, nseq, *st_shape),
        v_s.reshape(1, nseq, dec_seq, D_GMLP),
    )
```

```python
import functools

import numpy as np
import jax
import jax.numpy as jnp
from jax import lax
from jax.experimental import pallas as pl
from jax.experimental.pallas import tpu as pltpu
from jax.experimental.pallas import tpu_sc as plsc

f32, bf16, i32, u32 = jnp.float32, jnp.bfloat16, jnp.int32, jnp.uint32

D_MODEL = 1024
D_INNER = 2048
HEAD_DIM = 64
HEADS = 32
GROUPS = 4
HEADS_PER_GROUP = 8
GROUP_WIDTH = HEADS_PER_GROUP * HEAD_DIM
D_STATE = 128
CONV_W = 4
CONV_DIM = D_INNER + 2 * GROUPS * D_STATE
CHUNK = 128
SUBLANES = 8
D_GMLP = 1024
GMLP_HEADS = 8
N_EXPERTS = 32
TOP_K = 4
D_FF = 1024
SWIGLU_ALPHA = 1.702
SWIGLU_LIMIT = 7.0
EPS = 1e-5
LOG2_E = float(np.log2(np.e))
MOE_BLOCK = 512
DEINT = 256
ROW_DMA_TILE = 512
SC_CHUNK = 32
MIX_TILE = 512
D_IN_PROJ = 3 * D_INNER + CONV_DIM + HEADS
PROJ_TN = 512
PROMPT_SEQS_PER_STEP = 2
SAMPLE_SEQS_PER_STEP = 8
D_PACK = D_MODEL // 2
VMEM_LIMIT = 56 * 1024 * 1024

_NT = (((1,), (1,)), ((), ()))
_HI16 = np.uint32(0xFFFF0000)


def _params(*sem):
    return pltpu.CompilerParams(dimension_semantics=sem, vmem_limit_bytes=VMEM_LIMIT)


def _split(x, parts):
    out = []
    for _ in range(parts - 1):
        p = x.astype(bf16)
        out.append(p)
        x = x - p.astype(f32)
    out.append(x.astype(bf16))
    return out


def _dot_sel_rhs(x, sel, parts=3):
    return sum(jnp.dot(p, sel, preferred_element_type=f32) for p in _split(x, parts))


def _dot_sel_lhs(sel, x, parts=3):
    return sum(jnp.dot(sel, p, preferred_element_type=f32) for p in _split(x, parts))


def _silu(x):
    return x * jax.nn.sigmoid(x)


def _pack_bf16_pair(x):
    n = x.shape[1] // 2
    lo = lax.bitcast_convert_type(x[:, :n].astype(bf16).astype(f32), u32) >> 16
    hi = lax.bitcast_convert_type(x[:, n:].astype(bf16).astype(f32), u32) & _HI16
    return lo | hi


def _unpack_bf16_pair(p):
    lo = lax.bitcast_convert_type(p << 16, f32).astype(bf16)
    hi = lax.bitcast_convert_type(p & _HI16, f32).astype(bf16)
    return jnp.concatenate([lo, hi], axis=1)


def _in_proj_body(x_ref, nw_ref, wz_ref, wuv_ref, wg_ref, wxbc_ref, wdt_ref, wdtT_ref,
                  z_ref, uv_ref, g_ref, xbc_ref, dt_ref, dtT_ref):
    x = x_ref[...]
    ms = jnp.mean(x * x, axis=-1, keepdims=True)
    hn = (x * lax.rsqrt(ms + EPS) * nw_ref[...]).astype(bf16)
    dt_ref[...] = jnp.dot(hn, wdt_ref[...], preferred_element_type=f32)
    dtT_ref[...] = lax.dot_general(wdtT_ref[...], hn, _NT, preferred_element_type=f32)
    for o_ref, w_ref in ((z_ref, wz_ref), (uv_ref, wuv_ref), (g_ref, wg_ref), (xbc_ref, wxbc_ref)):
        for c in range(0, w_ref.shape[1], PROJ_TN):
            acc = jnp.dot(hn, w_ref[:, c:c + PROJ_TN], preferred_element_type=f32)
            o_ref[:, c:c + PROJ_TN] = acc.astype(o_ref.dtype)


def _in_proj(x2d, norm_w, w_segments, w_dt, w_dtT):
    t = x2d.shape[0]
    tm = min(512, t)
    row = lambda i: (i, 0)
    fixed = lambda i: (0, 0)
    return pl.pallas_call(
        _in_proj_body,
        grid=(t // tm,),
        in_specs=[
            pl.BlockSpec((tm, D_MODEL), row),
            pl.BlockSpec((1, D_MODEL), fixed),
        ] + [pl.BlockSpec(w.shape, fixed, pipeline_mode=pl.Buffered(1)) for w in w_segments] + [
            pl.BlockSpec((D_MODEL, HEADS), fixed),
            pl.BlockSpec((HEADS, D_MODEL), fixed),
        ],
        out_specs=[
            pl.BlockSpec((tm, D_INNER), row),
            pl.BlockSpec((tm, D_INNER), row),
            pl.BlockSpec((tm, D_INNER), row),
            pl.BlockSpec((tm, CONV_DIM), row),
            pl.BlockSpec((tm, HEADS), row),
            pl.BlockSpec((HEADS, tm), lambda i: (0, i)),
        ],
        out_shape=[
            jax.ShapeDtypeStruct((t, D_INNER), bf16),
            jax.ShapeDtypeStruct((t, D_INNER), bf16),
            jax.ShapeDtypeStruct((t, D_INNER), bf16),
            jax.ShapeDtypeStruct((t, CONV_DIM), f32),
            jax.ShapeDtypeStruct((t, HEADS), f32),
            jax.ShapeDtypeStruct((HEADS, t), f32),
        ],
        compiler_params=_params("arbitrary"),
        name="in_proj",
    )(x2d, norm_w, *w_segments, w_dt, w_dtT)


def _conv_taps(shifted, cw_ref, cb_ref):
    acc = cb_ref[...]
    for k in range(CONV_W):
        acc = acc + shifted(CONV_W - 1 - k) * cw_ref[k:k + 1, :]
    return _silu(acc)


def _conv_silu_one_seq(cur, prev8, cw_ref, cb_ref):
    assert CONV_W == 4
    row8 = lax.broadcasted_iota(i32, (SUBLANES, 1), 0)

    def shift_rows(x, x_prev8, j):
        xr = pltpu.roll(x, j, 0)
        top = jnp.where(row8 >= j, xr[:SUBLANES], pltpu.roll(x_prev8, j, 0))
        return jnp.concatenate([top, xr[SUBLANES:]], axis=0)

    w = [cw_ref[k:k + 1, :] for k in range(CONV_W)]
    x1 = shift_rows(cur, prev8, 1)
    near = x1 * w[2] + cur * w[3]
    far = x1 * w[0] + cur * w[1]
    far_prev8 = pltpu.roll(prev8, 1, 0) * w[0] + prev8 * w[1]
    return _silu(cb_ref[...] + shift_rows(far, far_prev8, 2) + near)


def _conv_silu_many_seq(cur, pre, cw_ref, cb_ref, seq_len):
    n = cur.shape[0]
    pos = lax.broadcasted_iota(i32, (n, 1), 0) % seq_len

    def shifted(j):
        if j == 0:
            return cur
        return jnp.where(pos >= j, pltpu.roll(cur, j, 0), pltpu.roll(pre, (j + n - seq_len) % n, 0))

    return _conv_taps(shifted, cw_ref, cb_ref)


def _ssd_block(xa, dt_raw, dtT_raw, c):
    tril = c["tril"][...]
    dt = jax.nn.softplus(dt_raw + c["dtb_row"][...])
    dtT = jax.nn.softplus(dtT_raw + c["dtb_col"][...])
    a = dt * c["aneg_row"][...]
    aT = dtT * c["aneg_col"][...]
    a_cum = _dot_sel_lhs(tril, a)
    a_cumT = _dot_sel_rhs(aT, c["triu"][...])
    a_tot = _dot_sel_lhs(c["same"][...], a)
    stack = jnp.concatenate([dt, jnp.exp(a_cum), jnp.exp(a_tot - a_cum)], axis=0)
    ex = jnp.dot(jnp.concatenate(_split(stack, 2), axis=1), c["expand"][...], preferred_element_type=f32)
    n = xa.shape[0]
    dtx, eax, dex = ex[:n], ex[n:2 * n], ex[2 * n:]

    xs = xa[:, :D_INNER]
    bm = xa[:, D_INNER:D_INNER + GROUPS * D_STATE]
    cm = xa[:, D_INNER + GROUPS * D_STATE:]
    xdt = xs * dtx
    xdt_bf = xdt.astype(bf16)
    mask = tril > 0
    a_cum2, a_cumT2 = a_cum * LOG2_E, a_cumT * LOG2_E
    lane = lax.broadcasted_iota(i32, (1, 2 * HEAD_DIM), 1)
    ys = []
    for g in range(GROUPS):
        cg = cm[:, g * D_STATE:(g + 1) * D_STATE].astype(bf16)
        bg = bm[:, g * D_STATE:(g + 1) * D_STATE].astype(bf16)
        cb = lax.dot_general(cg, bg, _NT, preferred_element_type=f32)
        for pair in range(HEADS_PER_GROUP // 2):
            halves = []
            for h in (g * HEADS_PER_GROUP + 2 * pair, g * HEADS_PER_GROUP + 2 * pair + 1):
                seg = a_cum2[:, h:h + 1] - a_cumT2[h:h + 1, :]
                decay = jnp.exp2(jnp.where(mask, seg, -jnp.inf))
                m = (cb * decay).astype(bf16)
                col = (h // 2) * 2 * HEAD_DIM
                halves.append(jnp.dot(m, xdt_bf[:, col:col + 2 * HEAD_DIM], preferred_element_type=f32))
            ys.append(jnp.where(lane < HEAD_DIM, halves[0], halves[1]))
    y_diag = jnp.concatenate(ys, axis=1)
    return dict(xs=xs, bm=bm, cm=cm, xdt=xdt, xd=xdt * dex, eax=eax, aT=aT, y_diag=y_diag)


def _gated_norm(y, z, gw):
    g = y * _silu(z.astype(f32))
    ms = jnp.mean(g * g, axis=-1, keepdims=True)
    return (g * lax.rsqrt(ms + EPS) * gw).astype(bf16)


_SSD_CONST_NAMES = ("cw", "cb", "dtb_row", "dtb_col", "aneg_row", "aneg_col", "dskipx", "gw",
                    "tril", "triu", "same", "expand")


def _ssd_consts(conv_w, conv_b, dt_bias, a_log, d_skip, gnorm_w, seq_len):
    r = np.arange(CHUNK)
    same = (r[:, None] // seq_len) == (r[None, :] // seq_len)
    tril = same & (r[None, :] <= r[:, None])
    expand = np.tile(np.repeat(np.eye(HEADS, dtype=np.float32), HEAD_DIM, axis=1), (2, 1))
    aneg = -jnp.exp(a_log.astype(f32))
    return dict(
        cw=conv_w, cb=conv_b.reshape(1, CONV_DIM),
        dtb_row=dt_bias.reshape(1, HEADS), dtb_col=dt_bias.reshape(HEADS, 1),
        aneg_row=aneg.reshape(1, HEADS), aneg_col=aneg.reshape(HEADS, 1),
        dskipx=jnp.repeat(d_skip.astype(f32), HEAD_DIM).reshape(1, D_INNER),
        gw=gnorm_w.reshape(1, D_INNER),
        tril=jnp.asarray(tril, bf16), triu=jnp.asarray(tril.T, bf16), same=jnp.asarray(same, bf16),
        expand=jnp.asarray(expand, bf16),
    )


def _const_specs(consts):
    zero = (lambda *_: (0, 0))
    return [pl.BlockSpec(consts[k].shape, zero) for k in _SSD_CONST_NAMES]


def _ssd_prompt_body(nb, z_ref, xbc_ref, dt_ref, dtT_ref, *rest):
    nc = len(_SSD_CONST_NAMES)
    c = dict(zip(_SSD_CONST_NAMES, rest[:nc]))
    y_ref, state_ref, prev_ref, st_ref = rest[nc:]
    ci = pl.program_id(1)

    @pl.when(ci == 0)
    def _():
        prev_ref[...] = jnp.zeros_like(prev_ref)
        st_ref[...] = jnp.zeros_like(st_ref)

    for s in range(nb):
        cur = xbc_ref[s]
        xa = _conv_silu_one_seq(cur, prev_ref[s], c["cw"], c["cb"])
        prev_ref[s] = cur[CHUNK - SUBLANES:]
        b = _ssd_block(xa, dt_ref[s], dtT_ref[s], c)

        y_off = []
        for g in range(GROUPS):
            sl = slice(g * GROUP_WIDTH, (g + 1) * GROUP_WIDTH)
            st = st_ref[s * GROUPS + g]
            cg = b["cm"][:, g * D_STATE:(g + 1) * D_STATE].astype(bf16)
            y_off.append(jnp.dot(cg, st.astype(bf16), preferred_element_type=f32))
            bgT = b["bm"][:, g * D_STATE:(g + 1) * D_STATE].T.astype(bf16)
            upd = jnp.dot(bgT, b["xd"][:, sl].astype(bf16), preferred_element_type=f32)
            st_ref[s * GROUPS + g] = st * b["eax"][CHUNK - 1:CHUNK, sl] + upd
        y = b["y_diag"] + jnp.concatenate(y_off, axis=1) * b["eax"] + c["dskipx"][...] * b["xs"]
        y_ref[s] = _gated_norm(y, z_ref[s], c["gw"][...])

    @pl.when(ci == pl.num_programs(1) - 1)
    def _():
        for s in range(nb):
            for g in range(GROUPS):
                state_ref[s, 0, g * GROUP_WIDTH:(g + 1) * GROUP_WIDTH, :] = st_ref[s * GROUPS + g].T


def _ssd_prompt(z, xbc, dt, dtT, consts, batch, seq):
    nb = PROMPT_SEQS_PER_STEP if batch % PROMPT_SEQS_PER_STEP == 0 else 1
    nchunk = seq // CHUNK
    part = batch // nb * seq
    split = lambda a: a.reshape(nb, part, a.shape[-1])
    dtT3 = dtT.reshape(HEADS, nb, part).transpose(1, 0, 2)
    row = lambda b, ci: (0, b * nchunk + ci, 0)
    y, state = pl.pallas_call(
        functools.partial(_ssd_prompt_body, nb),
        grid=(batch // nb, nchunk),
        in_specs=[
            pl.BlockSpec((nb, CHUNK, D_INNER), row),
            pl.BlockSpec((nb, CHUNK, CONV_DIM), row),
            pl.BlockSpec((nb, CHUNK, HEADS), row),
            pl.BlockSpec((nb, HEADS, CHUNK), lambda b, ci: (0, 0, b * nchunk + ci)),
        ] + _const_specs(consts),
        out_specs=[
            pl.BlockSpec((nb, CHUNK, D_INNER), row),
            pl.BlockSpec((nb, 1, D_INNER, D_STATE), lambda b, ci: (0, b, 0, 0)),
        ],
        out_shape=[
            jax.ShapeDtypeStruct((nb, part, D_INNER), bf16),
            jax.ShapeDtypeStruct((nb, batch // nb, D_INNER, D_STATE), f32),
        ],
        scratch_shapes=[pltpu.VMEM((nb, SUBLANES, CONV_DIM), f32),
                        pltpu.VMEM((nb * GROUPS, D_STATE, GROUP_WIDTH), f32)],
        compiler_params=_params("arbitrary", "arbitrary"),
        name="ssd_prompt",
    )(split(z), split(xbc), split(dt), dtT3, *[consts[k] for k in _SSD_CONST_NAMES])
    return y.reshape(batch * seq, D_INNER), state.reshape(batch, D_INNER, D_STATE)


def _ssd_sample_body(seq_len, z_ref, xbc_ref, pre_ref, dt_ref, dtT_ref, state_ref, *rest):
    nc = len(_SSD_CONST_NAMES)
    c = dict(zip(_SSD_CONST_NAMES, rest[:nc]))
    y_ref, state_out_ref, c_ref, b_ref, xdT_ref, yacc_ref, eax_ref, eatT_ref = rest[nc:]
    step = pl.program_id(1)
    per_step = state_ref.shape[0]

    @pl.when(step == 0)
    def _():
        xa = _conv_silu_many_seq(xbc_ref[...], pre_ref[...], c["cw"], c["cb"], seq_len)
        b = _ssd_block(xa, dt_ref[...], dtT_ref[...], c)
        c_ref[...] = b["cm"]
        b_ref[...] = b["bm"]
        xdT_ref[...] = b["xd"].T.astype(bf16)
        yacc_ref[...] = b["y_diag"] + c["dskipx"][...] * b["xs"]
        eax_ref[...] = b["eax"]
        eatT_ref[...] = jnp.exp(_dot_sel_rhs(b["aT"], c["same"][...]))

    def one_sequence(q, carry):
        s = step * per_step + q
        r0 = pl.multiple_of(s * seq_len, seq_len)
        rows = pl.ds(r0, seq_len)
        lane = lax.broadcasted_iota(i32, (1, CHUNK), 1)
        arep = jnp.broadcast_to(jnp.sum(jnp.where(lane == r0, eatT_ref[...], 0.0), axis=1, keepdims=True), (HEADS, D_STATE))
        rmask = (lax.broadcasted_iota(i32, (CHUNK, 1), 0) // seq_len) == s
        for g in range(GROUPS):
            gs = slice(g * GROUP_WIDTH, (g + 1) * GROUP_WIDTH)
            ns = slice(g * D_STATE, (g + 1) * D_STATE)
            s0 = state_ref[q, gs, :]
            cg = c_ref[rows, ns].astype(bf16)
            yo = lax.dot_general(cg, s0.astype(bf16), _NT, preferred_element_type=f32)
            yacc_ref[rows, gs] = yacc_ref[rows, gs] + yo * eax_ref[rows, gs]
            bmask = jnp.where(rmask, b_ref[:, ns], 0.0).astype(bf16)
            upd = jnp.dot(xdT_ref[gs, :], bmask, preferred_element_type=f32)
            for r in range(HEADS_PER_GROUP):
                h = g * HEADS_PER_GROUP + r
                hs = slice(r * HEAD_DIM, (r + 1) * HEAD_DIM)
                state_out_ref[q, h * HEAD_DIM:(h + 1) * HEAD_DIM, :] = s0[hs] * arep[h:h + 1, :] + upd[hs]
        return carry

    lax.fori_loop(0, per_step, one_sequence, 0)

    @pl.when(step == pl.num_programs(1) - 1)
    def _():
        y_ref[...] = _gated_norm(yacc_ref[...], z_ref[...], c["gw"][...])


def _ssd_sample(z, xbc, pre, dt, dtT, state, consts, nseq, seq_len):
    per = CHUNK // seq_len
    nblk = nseq // per
    t = nseq * seq_len
    q = SAMPLE_SEQS_PER_STEP if per % SAMPLE_SEQS_PER_STEP == 0 else 1
    blk = lambda i, s: (i, 0)
    seq3 = lambda i, s: (i * (per // q) + s, 0, 0)
    return pl.pallas_call(
        functools.partial(_ssd_sample_body, seq_len),
        grid=(nblk, per // q),
        in_specs=[
            pl.BlockSpec((CHUNK, D_INNER), blk),
            pl.BlockSpec((CHUNK, CONV_DIM), blk),
            pl.BlockSpec((CHUNK, CONV_DIM), blk),
            pl.BlockSpec((CHUNK, HEADS), blk),
            pl.BlockSpec((HEADS, CHUNK), lambda i, s: (0, i)),
            pl.BlockSpec((q, D_INNER, D_STATE), seq3),
        ] + _const_specs(consts),
        out_specs=[
            pl.BlockSpec((CHUNK, D_INNER), blk),
            pl.BlockSpec((q, D_INNER, D_STATE), seq3),
        ],
        out_shape=[
            jax.ShapeDtypeStruct((t, D_INNER), bf16),
            jax.ShapeDtypeStruct((nseq, D_INNER, D_STATE), f32),
        ],
        scratch_shapes=[
            pltpu.VMEM((CHUNK, GROUPS * D_STATE), f32),
            pltpu.VMEM((CHUNK, GROUPS * D_STATE), f32),
            pltpu.VMEM((D_INNER, CHUNK), bf16),
            pltpu.VMEM((CHUNK, D_INNER), f32),
            pltpu.VMEM((CHUNK, D_INNER), f32),
            pltpu.VMEM((HEADS, CHUNK), f32),
        ],
        compiler_params=_params("arbitrary", "arbitrary"),
        name="ssd_sample",
    )(z, xbc, pre, dt, dtT, state, *[consts[k] for k in _SSD_CONST_NAMES])


_MIX_CONST_NAMES = ("ws", "bs", "vnw", "vnb", "wbs", "wbm", "wo", "nffn", "wrT", "br", "tris")


def _mix_body(emit_v, uv_ref, gates_ref, yssd_ref, x_ref, *rest):
    nc = len(_MIX_CONST_NAMES)
    c = dict(zip(_MIX_CONST_NAMES, rest[:nc]))
    outs = list(rest[nc:])
    h_ref, hfp_ref = outs[:2]
    v_ref = outs[2] if emit_v else None
    te_ref, gate_ref, rank_ref, cnt_ref, cnt_acc = outs[-5:]
    tm = uv_ref.shape[0]

    @pl.when(pl.program_id(0) == 0)
    def _():
        cnt_acc[...] = jnp.zeros_like(cnt_acc)

    uv = uv_ref[...].astype(f32)
    uv = 0.5 * uv * (1.0 + lax.erf(uv * np.float32(np.sqrt(0.5))))
    u, v = uv[:, :D_GMLP], uv[:, D_GMLP:]
    mu = jnp.mean(v, axis=-1, keepdims=True)
    vc = v - mu
    vn = vc * lax.rsqrt(jnp.mean(vc * vc, axis=-1, keepdims=True) + EPS) * c["vnw"][...] + c["vnb"][...]
    if emit_v:
        v_ref[...] = vn
    vn_bf = vn.astype(bf16)
    gd = D_GMLP // GMLP_HEADS
    rows = []
    for ck in range(tm // CHUNK):
        rs = slice(ck * CHUNK, (ck + 1) * CHUNK)
        heads = []
        for g in range(GMLP_HEADS):
            mixed = jnp.dot(c["ws"][g], vn_bf[rs, g * gd:(g + 1) * gd], preferred_element_type=f32)
            heads.append(mixed + c["bs"][:, g:g + 1])
        rows.append(jnp.concatenate(heads, axis=1))
    y_mlp = u * jnp.concatenate(rows, axis=0)

    a = jnp.dot(yssd_ref[...], c["wbs"][...], preferred_element_type=f32)
    b = jnp.dot(y_mlp.astype(bf16), c["wbm"][...], preferred_element_type=f32)
    gs = jax.nn.sigmoid(gates_ref[...].astype(f32))
    merged = gs[:, :D_MODEL] * a + gs[:, D_MODEL:] * b
    h = x_ref[...] + jnp.dot(merged.astype(bf16), c["wo"][...], preferred_element_type=f32)
    h_ref[...] = h
    hf = h * lax.rsqrt(jnp.mean(h * h, axis=-1, keepdims=True) + EPS) * c["nffn"][...]
    hfp_ref[...] = _pack_bf16_pair(hf)

    lg = lax.dot_general(c["wrT"][...], hf, _NT, precision=lax.Precision.HIGHEST, preferred_element_type=f32) + c["br"][...]
    sub = lax.broadcasted_iota(i32, lg.shape, 0)
    idxs, vals = [], []
    for _ in range(TOP_K):
        m = jnp.max(lg, axis=0, keepdims=True)
        idx = jnp.min(jnp.where(lg == m, sub, N_EXPERTS), axis=0, keepdims=True)
        idxs.append(idx)
        vals.append(m)
        lg = jnp.where(sub == idx, -jnp.inf, lg)
    p = jnp.exp(jnp.concatenate(vals, axis=0) - vals[0])
    gate_ref[...] = p / jnp.sum(p, axis=0, keepdims=True)
    te_ref[...] = jnp.concatenate(idxs, axis=0)

    onehots = [sub == idx for idx in idxs]
    member = functools.reduce(jnp.logical_or, onehots).astype(f32)
    before = jnp.dot(member.astype(bf16), c["tris"][...], preferred_element_type=f32) + cnt_acc[:, 0:1]
    rank_ref[...] = jnp.concatenate(
        [jnp.sum(jnp.where(oh, before, 0.0), axis=0, keepdims=True) for oh in onehots], axis=0).astype(i32)
    cnt_acc[...] = cnt_acc[...] + jnp.sum(member, axis=1, keepdims=True)
    cnt_ref[...] = cnt_acc[...].astype(i32)


def _mix(uv, gates, yssd, x2d, mc, tm, emit_v):
    t = x2d.shape[0]
    row = lambda i: (i, 0)
    col = lambda i: (0, i)
    const = lambda a: pl.BlockSpec(a.shape, lambda i: (0,) * a.ndim, pipeline_mode=pl.Buffered(1))
    v_spec = [pl.BlockSpec((tm, D_GMLP), row)] if emit_v else []
    v_shape = [jax.ShapeDtypeStruct((t, D_GMLP), f32)] if emit_v else []
    return pl.pallas_call(
        functools.partial(_mix_body, emit_v),
        grid=(t // tm,),
        in_specs=[
            pl.BlockSpec((tm, D_INNER), row),
            pl.BlockSpec((tm, D_INNER), row),
            pl.BlockSpec((tm, D_INNER), row),
            pl.BlockSpec((tm, D_MODEL), row),
        ] + [const(mc[k]) for k in _MIX_CONST_NAMES],
        out_specs=[pl.BlockSpec((tm, D_MODEL), row), pl.BlockSpec((tm, D_PACK), row)] + v_spec + [
            pl.BlockSpec((TOP_K, tm), col),
            pl.BlockSpec((TOP_K, tm), col),
            pl.BlockSpec((TOP_K, tm), col),
            pl.BlockSpec((N_EXPERTS, 128), lambda i: (0, 0)),
        ],
        out_shape=[
            jax.ShapeDtypeStruct((t, D_MODEL), f32),
            jax.ShapeDtypeStruct((t, D_PACK), u32),
        ] + v_shape + [
            jax.ShapeDtypeStruct((TOP_K, t), i32),
            jax.ShapeDtypeStruct((TOP_K, t), f32),
            jax.ShapeDtypeStruct((TOP_K, t), i32),
            jax.ShapeDtypeStruct((N_EXPERTS, 128), i32),
        ],
        scratch_shapes=[pltpu.VMEM((N_EXPERTS, 128), f32)],
        compiler_params=_params("arbitrary"),
        name="mix_route",
    )(uv, gates, yssd, x2d, *[mc[k] for k in _MIX_CONST_NAMES])


def _dispatch(slot_groups, hf_groups, n_slots):
    info = plsc.get_sparse_core_info()
    workers = info.num_cores * info.num_subcores
    chunks = []
    slots = []
    for g, (slot_kt, hf) in enumerate(zip(slot_groups, hf_groups)):
        per_w, rem = divmod(hf.shape[0], workers)
        n, rem2 = divmod(per_w, SC_CHUNK)
        assert rem == 0 and rem2 == 0
        chunks += [(g, per_w, j) for j in range(n)]
        slots.append(slot_kt.reshape(TOP_K, workers, n, SC_CHUNK).transpose(1, 0, 2, 3))
    slots = jnp.concatenate(slots, axis=2)
    nch = len(chunks)
    mesh = plsc.VectorSubcoreMesh(core_axis_name="c", subcore_axis_name="s")

    @functools.partial(
        pl.kernel, mesh=mesh,
        out_type=jax.ShapeDtypeStruct((n_slots, D_PACK), u32),
        scratch_types=[pltpu.VMEM((TOP_K, nch, SC_CHUNK), i32), pltpu.VMEM((2, SC_CHUNK, D_PACK), u32)]
        + [pltpu.SemaphoreType.DMA] * 4,
    )
    def scatter_rows(*refs):
        hf_hbm, (slot_hbm, out_hbm, idx_v, rows_v, sin0, sin1, sout0, sout1) = refs[:len(hf_groups)], refs[len(hf_groups):]
        sin, sout = (sin0, sin1), (sout0, sout1)
        wid = lax.axis_index("s") * info.num_cores + lax.axis_index("c")
        pltpu.sync_copy(slot_hbm.at[wid], idx_v)

        def load(j):
            g, per_w, jg = chunks[j]
            rows = pl.ds(pl.multiple_of(wid * per_w + jg * SC_CHUNK, SUBLANES), SC_CHUNK)
            return pltpu.make_async_copy(hf_hbm[g].at[rows], rows_v.at[j % 2], sin[j % 2])

        def scatters(j):
            return [pltpu.make_async_copy(rows_v.at[j % 2], out_hbm.at[idx_v.at[k, j]], sout[j % 2]) for k in range(TOP_K)]

        load(0).start()
        for j in range(nch):
            load(j).wait()
            if j >= 1:
                for cp in scatters(j - 1):
                    cp.wait()
            if j + 1 < nch:
                load(j + 1).start()
            for cp in scatters(j):
                cp.start()
        for cp in scatters(nch - 1):
            cp.wait()

    return scatter_rows(*hf_groups, slots)


def _experts_body(be_ref, nused_ref, start_ref, slot_ref, next_ref, nvalid_ref, xs_ref, bgu_ref, bd_ref, perm_ref, wgu_hbm, wd_hbm,
                  ys_ref, wgu_f32, wd_f32, wgu_bf, wd_bf, sems):
    i = pl.program_id(0)

    def weight_copies(e, slot):
        return (pltpu.make_async_copy(wgu_hbm.at[e], wgu_f32.at[slot], sems.at[0, slot]),
                pltpu.make_async_copy(wd_hbm.at[e], wd_f32.at[slot], sems.at[1, slot]))

    @pl.when(i < nused_ref[0])
    def _():
        @pl.when(start_ref[i] == 1)
        def _():
            slot = slot_ref[i]

            @pl.when(i == 0)
            def _():
                for cp in weight_copies(be_ref[0], 0):
                    cp.start()

            for cp in weight_copies(be_ref[i], slot):
                cp.wait()

            @pl.when(next_ref[i] >= 0)
            def _():
                for cp in weight_copies(next_ref[i], 1 - slot):
                    cp.start()

            wd_bf[...] = wd_f32[slot].astype(bf16)
            half = DEINT // 2
            for j in range(2 * D_FF // DEINT):
                w = jnp.dot(wgu_f32[slot, :, j * DEINT:(j + 1) * DEINT].astype(bf16), perm_ref[...],
                            preferred_element_type=f32).astype(bf16)
                wgu_bf[:, j * half:(j + 1) * half] = w[:, :half]
                wgu_bf[:, D_FF + j * half:D_FF + (j + 1) * half] = w[:, half:]

        row = lax.broadcasted_iota(i32, (MOE_BLOCK, 1), 0)
        xs = jnp.where(row < nvalid_ref[i], xs_ref[...], jnp.zeros((), u32))
        gu = jnp.dot(_unpack_bf16_pair(xs), wgu_bf[...], preferred_element_type=f32) + bgu_ref[0]
        glu = jnp.minimum(gu[:, :D_FF], SWIGLU_LIMIT)
        lin = jnp.clip(gu[:, D_FF:], -SWIGLU_LIMIT, SWIGLU_LIMIT)
        act = glu * jax.nn.sigmoid(SWIGLU_ALPHA * glu) * (lin + 1.0)
        ys_ref[...] = jnp.dot(act.astype(bf16), wd_bf[...], preferred_element_type=f32) + bd_ref[0]

    @pl.when(i >= nused_ref[0])
    def _():
        ys_ref[...] = jnp.zeros_like(ys_ref)


def _expert_segments(block_e, n_used, n_blocks):
    idx = jnp.arange(n_blocks, dtype=i32)
    prev_e = jnp.concatenate([block_e[:1], block_e[:-1]])
    start = (idx < n_used) & ((idx == 0) | (block_e != prev_e))
    slot = (jnp.cumsum(start.astype(i32)) - 1) % 2
    start_idx = jnp.where(start, idx, n_blocks)
    from_here = lax.cummin(start_idx, reverse=True)
    next_start = jnp.concatenate([from_here[1:], jnp.full((1,), n_blocks, i32)])
    next_e = jnp.sum(jnp.where(next_start[:, None] == idx[None, :], block_e[None, :], 0), axis=1)
    next_e = jnp.where(next_start < n_blocks, next_e, -1)
    return start.astype(i32), slot.astype(i32), next_e.astype(i32)


def _experts(block_e, n_used, n_valid, xs, wgu, bgu, w_down, b_down):
    n_blocks = xs.shape[0] // MOE_BLOCK
    start, slot, next_e = _expert_segments(block_e, n_used[0], n_blocks)
    blk = lambda i, be, nu, *_: (jnp.maximum(jnp.minimum(i, nu[0] - 1), 0), 0)
    exp3 = lambda i, be, *_: (be[i], 0, 0)
    c = np.arange(DEINT)
    src = np.where(c < DEINT // 2, 2 * c, 2 * (c - DEINT // 2) + 1)
    perm = jnp.asarray(np.arange(DEINT)[:, None] == src[None, :], bf16)
    return pl.pallas_call(
        _experts_body,
        grid_spec=pltpu.PrefetchScalarGridSpec(
            num_scalar_prefetch=6,
            grid=(n_blocks,),
            in_specs=[
                pl.BlockSpec((MOE_BLOCK, D_PACK), blk),
                pl.BlockSpec((1, 1, 2 * D_FF), exp3),
                pl.BlockSpec((1, 1, D_MODEL), exp3),
                pl.BlockSpec((DEINT, DEINT), lambda i, *_: (0, 0)),
                pl.BlockSpec(memory_space=pl.ANY),
                pl.BlockSpec(memory_space=pl.ANY),
            ],
            out_specs=pl.BlockSpec((MOE_BLOCK, D_MODEL), lambda i, *_: (i, 0)),
            scratch_shapes=[
                pltpu.VMEM((2, D_MODEL, 2 * D_FF), f32),
                pltpu.VMEM((2, D_FF, D_MODEL), f32),
                pltpu.VMEM((D_MODEL, 2 * D_FF), bf16),
                pltpu.VMEM((D_FF, D_MODEL), bf16),
                pltpu.SemaphoreType.DMA((2, 2)),
            ],
        ),
        out_shape=jax.ShapeDtypeStruct((xs.shape[0], D_MODEL), f32),
        compiler_params=_params("arbitrary"),
        name="moe_experts",
    )(block_e, n_used, start, slot, next_e, n_valid, xs, bgu, b_down, perm, wgu, w_down)


def _combine_body(tm, t, slot_ref, h_ref, gate_ref, nf_ref, ys_ref, o_ref, rows_ref, sems):
    s = pl.program_id(0)
    n_tiles = pl.num_programs(0) - 1

    def request():
        buf = s % 2
        for r in range(tm):
            for k in range(TOP_K):
                slot = slot_ref[k * t + s * tm + r]
                pltpu.make_async_copy(ys_ref.at[slot], rows_ref.at[buf, k, r], sems.at[buf]).start(priority=k % 2)

    def finish():
        buf = (s - 1) % 2
        for r in range(tm):
            for k in range(TOP_K):
                pltpu.make_async_copy(ys_ref.at[0], rows_ref.at[buf, k, r], sems.at[buf]).wait()
        g = gate_ref[...]
        moe = g[:, 0:1] * rows_ref[buf, 0]
        for k in range(1, TOP_K):
            moe = moe + g[:, k:k + 1] * rows_ref[buf, k]
        h = h_ref[...] + moe
        o_ref[...] = h * lax.rsqrt(jnp.mean(h * h, axis=-1, keepdims=True) + EPS) * nf_ref[...]

    @pl.when(s < n_tiles)
    def _():
        request()

    @pl.when(s > 0)
    def _():
        finish()


def _combine(slot_flat, h, gate_t, norm_final, ys, tm):
    t = h.shape[0]
    return pl.pallas_call(
        functools.partial(_combine_body, tm, t),
        grid_spec=pltpu.PrefetchScalarGridSpec(
            num_scalar_prefetch=1,
            grid=(t // tm + 1,),
            in_specs=[
                pl.BlockSpec((tm, D_MODEL), lambda i, s: (jnp.maximum(i - 1, 0), 0)),
                pl.BlockSpec((tm, TOP_K), lambda i, s: (jnp.maximum(i - 1, 0), 0)),
                pl.BlockSpec((1, D_MODEL), lambda i, s: (0, 0)),
                pl.BlockSpec(memory_space=pl.ANY),
            ],
            out_specs=pl.BlockSpec((tm, D_MODEL), lambda i, s: (jnp.maximum(i - 1, 0), 0)),
            scratch_shapes=[pltpu.VMEM((2, TOP_K, tm, D_MODEL), f32), pltpu.SemaphoreType.DMA((2,))],
        ),
        out_shape=jax.ShapeDtypeStruct((t, D_MODEL), f32),
        compiler_params=_params("arbitrary"),
        name="moe_combine",
    )(slot_flat, h, gate_t, norm_final, ys)


def kernel(x_prompt, x_sample, state_conv, state_ssm, norm_mix, w_in, conv_w, conv_b, dt_bias, a_log, d_skip, gnorm_w, v_norm_w, v_norm_b, w_spatial, b_spatial, w_branch_ssd, w_branch_mlp, w_out, norm_ffn, w_router, b_router, w_gu, b_gu, w_down, b_down, norm_final):
    assert w_in.shape[0] == 1, "single-layer trunk"
    batch, seq, _ = x_prompt.shape
    nseq, dec_seq, _ = x_sample.shape
    assert seq % CHUNK == 0 and CHUNK % dec_seq == 0 and nseq % (CHUNK // dec_seq) == 0 and dec_seq >= CONV_W - 1
    t_p, t_s = batch * seq, nseq * dec_seq
    tm = ROW_DMA_TILE
    assert t_p % tm == 0 and t_s % tm == 0
    tm_mix = MIX_TILE if t_p % MIX_TILE == 0 and t_s % MIX_TILE == 0 else tm

    wi = w_in[0]
    z0, x0, d0, u0 = D_INNER, D_INNER + CONV_DIM, D_INNER + CONV_DIM + HEADS, D_INNER + CONV_DIM + HEADS + 2 * D_GMLP
    w_segments = tuple(wi[:, a:b].astype(bf16) for a, b in ((0, z0), (d0, u0), (u0, D_IN_PROJ), (z0, x0)))
    w_dt = wi[:, x0:d0].astype(bf16)
    w_dtT = w_dt.T
    nm = norm_mix[0].reshape(1, D_MODEL)
    sc = lambda L: _ssd_consts(conv_w[0], conv_b[0], dt_bias[0], a_log[0], d_skip[0], gnorm_w[0], L)
    xp2, xs2 = x_prompt.reshape(t_p, D_MODEL), x_sample.reshape(t_s, D_MODEL)

    tril = np.tril(np.ones((CHUNK, CHUNK), bool))
    ws_p = jnp.where(tril[None], w_spatial[0], 0).astype(bf16)
    per = CHUNK // dec_seq
    blockdiag = (np.arange(CHUNK)[:, None] // dec_seq) == (np.arange(CHUNK)[None, :] // dec_seq)
    ws_s = jnp.where((tril & blockdiag)[None], jnp.tile(w_spatial[0][:, :dec_seq, :dec_seq], (1, per, per)), 0).astype(bf16)
    bs_p = b_spatial[0].T
    bs_s = jnp.tile(b_spatial[0][:, :dec_seq], (1, per)).T
    mc = dict(
        vnw=v_norm_w[0].reshape(1, D_GMLP), vnb=v_norm_b[0].reshape(1, D_GMLP),
        wbs=w_branch_ssd[0].astype(bf16), wbm=w_branch_mlp[0].astype(bf16), wo=w_out[0].astype(bf16),
        nffn=norm_ffn[0].reshape(1, D_MODEL), wrT=w_router[0].T, br=b_router[0].reshape(N_EXPERTS, 1),
        tris=jnp.asarray(np.triu(np.ones((tm_mix, tm_mix), np.float32), 1), bf16),
    )

    z_p, uv_p, g_p, xbc_p, dt_p, dtT_p = _in_proj(xp2, nm, w_segments, w_dt, w_dtT)
    yssd_p, ssm_p = _ssd_prompt(z_p, xbc_p, dt_p, dtT_p, sc(CHUNK), batch, seq)
    h_p, hf_p, te_p, gate_p, rank_p, cnt_p = _mix(uv_p, g_p, yssd_p, xp2, dict(mc, ws=ws_p, bs=bs_p), tm_mix, False)

    z_s, uv_s, g_s, xbc_s, dt_s, dtT_s = _in_proj(xs2, nm, w_segments, w_dt, w_dtT)
    pre = jnp.pad(state_conv[0], ((0, 0), (dec_seq - (CONV_W - 1), 0), (0, 0))).reshape(t_s, CONV_DIM)
    yssd_s, ssm_s = _ssd_sample(z_s, xbc_s, pre, dt_s, dtT_s, state_ssm[0].reshape(nseq, D_INNER, D_STATE),
                                sc(dec_seq), nseq, dec_seq)
    h_s, hf_s, v_s, te_s, gate_s, rank_s, cnt_s = _mix(uv_s, g_s, yssd_s, xs2, dict(mc, ws=ws_s, bs=bs_s), tm_mix, True)

    cp, cs = cnt_p[:, 0], cnt_s[:, 0]
    padded = (cp + cs + MOE_BLOCK - 1) // MOE_BLOCK * MOE_BLOCK
    pend = jnp.cumsum(padded)
    pstart = pend - padded

    def lookup(table, te):
        eid = jnp.arange(N_EXPERTS, dtype=i32)[:, None, None]
        return jnp.sum(jnp.where(te[None] == eid, table[:, None, None], 0), axis=0)

    slot_kt_p = (lookup(pstart, te_p) + rank_p).astype(i32)
    slot_kt_s = (lookup(pstart + cp, te_s) + rank_s).astype(i32)
    slot_p, slot_s = slot_kt_p.reshape(-1), slot_kt_s.reshape(-1)
    n_blocks = (t_p + t_s) * TOP_K // MOE_BLOCK + N_EXPERTS
    n_used = (pend[-1] // MOE_BLOCK).astype(i32).reshape(1)
    first_row = jnp.arange(n_blocks, dtype=i32) * MOE_BLOCK
    block_e = jnp.minimum(jnp.sum(pend[None, :] <= first_row[:, None], axis=1), N_EXPERTS - 1).astype(i32)
    last_row = jnp.sum(jnp.where(block_e[:, None] == jnp.arange(N_EXPERTS)[None, :], (pstart + cp + cs)[None, :], 0), axis=1)
    n_valid = jnp.clip(last_row - first_row, 0, MOE_BLOCK).astype(i32)

    xs_sorted = _dispatch((slot_kt_p, slot_kt_s), (hf_p, hf_s), n_blocks * MOE_BLOCK)
    bgu = jnp.concatenate([b_gu[0][..., 0::2], b_gu[0][..., 1::2]], axis=-1).reshape(N_EXPERTS, 1, 2 * D_FF)
    ys = _experts(block_e, n_used, n_valid, xs_sorted, w_gu[0], bgu, w_down[0], b_down[0].reshape(N_EXPERTS, 1, D_MODEL))

    nf = norm_final.reshape(1, D_MODEL)
    y_p = _combine(slot_p, h_p, gate_p.T, nf, ys, tm)
    y_s = _combine(slot_s, h_s, gate_s.T, nf, ys, tm)

    conv_p = xbc_p.reshape(batch, seq, CONV_DIM)[:, seq - (CONV_W - 1):]
    rows_s = xbc_s.reshape(nseq, dec_seq * CONV_DIM)
    conv_s = jnp.stack([rows_s[:, r * CONV_DIM:(r + 1) * CONV_DIM] for r in range(dec_seq - (CONV_W - 1), dec_seq)], axis=1)
    st_shape = (HEADS, HEAD_DIM, D_STATE)
    return (
        y_p.reshape(batch, seq, D_MODEL),
        y_s.reshape(nseq, dec_seq, D_MODEL),
        conv_p[None],
        ssm_p.reshape(1, batch, *st_shape),
        conv_s[None],
        ssm_s.reshape(1, nseq, *st_shape),
        v_s.reshape(1, nseq, dec_seq, D_GMLP),
    )
```
